```python
import math
import jax, jax.numpy as jnp
from jax import lax
import numpy as np

D_MODEL = 1024
BATCH = 8
SEQ = 16384
DEPTH = 4

N_MIXERS = 2
EXPAND = 2
D_INNER = EXPAND * D_MODEL
CONF_KERNEL = 31
HEADDIM = 64
SSD_HEADS = D_INNER // HEADDIM
SSD_GROUPS = 4
D_STATE = 128
SSD_CONV = 5
CHUNK = 128
SSD_CONV_DIM = D_INNER + 2 * SSD_GROUPS * D_STATE
SSD_IN = D_INNER + SSD_CONV_DIM + 2 * SSD_HEADS
N_CONF_LAYERS = (DEPTH + 1) // 2
N_SSD_LAYERS = DEPTH // 2
EPS = 1e-5

kernel_name = "bidir_conformer_ssd_hybrid"


def rmsnorm(x, w):
    xf = x.astype(jnp.float32)
    y = xf * lax.rsqrt(jnp.mean(xf * xf, axis=-1, keepdims=True) + EPS)
    return (y * w.astype(jnp.float32)).astype(x.dtype)


def layernorm(x, w, b):
    xf = x.astype(jnp.float32)
    mu = jnp.mean(xf, axis=-1, keepdims=True)
    xc = xf - mu
    var = jnp.mean(xc * xc, axis=-1, keepdims=True)
    y = xc * lax.rsqrt(var + EPS)
    return (y * w.astype(jnp.float32) + b.astype(jnp.float32)).astype(x.dtype)


def dwconv_centred(x, w, b):
    width, ch = w.shape
    pad = (width - 1) // 2
    y = lax.conv_general_dilated(
        x, w[:, None, :].astype(x.dtype), window_strides=(1,), padding=[(pad, pad)],
        dimension_numbers=("NWC", "WIO", "NWC"), feature_group_count=ch)
    return y + b.astype(x.dtype)


def conformer_mixer(h, w_in, dw_w, dw_b, ln_w, ln_b, w_out):
    proj = h @ w_in
    v, g, z = jnp.split(proj, 3, axis=-1)
    u = v * jax.nn.sigmoid(g)
    u = dwconv_centred(u, dw_w, dw_b)
    u = layernorm(u, ln_w, ln_b)
    u = jax.nn.silu(u) * jax.nn.silu(z)
    return u @ w_out


def ssd_chunked(x, dt, A, B, C):
    in_dtype = x.dtype
    f32 = jnp.float32
    x, dt, A, B, C = (t.astype(f32) for t in (x, dt, A, B, C))
    b, s, H, P = x.shape
    G, N = B.shape[2], B.shape[3]
    hpg = H // G
    nc = s // CHUNK
    xdt = (x * dt[..., None]).reshape(b, nc, CHUNK, G, hpg, P)
    a = (dt * A).reshape(b, nc, CHUNK, G, hpg)
    a = jnp.transpose(a, (0, 3, 4, 1, 2))
    Bc = B.reshape(b, nc, CHUNK, G, N)
    Cc = C.reshape(b, nc, CHUNK, G, N)
    acs = jnp.cumsum(a, axis=-1)

    diff = acs[..., :, None] - acs[..., None, :]
    mask = jnp.tril(jnp.ones((CHUNK, CHUNK), dtype=bool))
    Lmat = jnp.where(mask, jnp.exp(jnp.where(mask, diff, 0.0)), 0.0)
    CB = jnp.einsum("bclgn,bcsgn->bgcls", Cc, Bc)
    M = CB[:, :, None] * Lmat
    y_diag = jnp.einsum("bghcls,bcsghp->bclghp", M, xdt)

    decay_states = jnp.exp(acs[..., -1:] - acs)
    states = jnp.einsum("bclgn,bghcl,bclghp->bcghpn", Bc, decay_states, xdt)

    chunk_decay = jnp.exp(acs[..., -1])

    def step(hstate, inp):
        dec, st = inp
        return hstate * dec[..., None, None] + st, hstate

    h0 = jnp.zeros((b, G, hpg, P, N), f32)
    _, state_in = lax.scan(step, h0, (jnp.moveaxis(chunk_decay, 3, 0), jnp.moveaxis(states, 1, 0)))
    state_in = jnp.moveaxis(state_in, 0, 1)

    y_off = jnp.einsum("bclgn,bcghpn,bghcl->bclghp", Cc, state_in, jnp.exp(acs))
    y = (y_diag + y_off).reshape(b, s, H, P)
    return y.astype(in_dtype)


def ssd_mixer(h, w_in, conv_w, conv_b, dt_bias, A_log, D_skip, norm_w, w_out):
    b, s, _ = h.shape
    proj = h @ w_in
    z = proj[..., :D_INNER]
    xBC = proj[..., D_INNER:D_INNER + SSD_CONV_DIM]
    dt_raw = proj[..., D_INNER + SSD_CONV_DIM:]
    xBC = jax.nn.silu(dwconv_centred(xBC, conv_w, conv_b))
    xs = xBC[..., :D_INNER].reshape(b, s, SSD_HEADS, HEADDIM)
    Bm = xBC[..., D_INNER:D_INNER + SSD_GROUPS * D_STATE].reshape(b, s, SSD_GROUPS, D_STATE)
    Cm = xBC[..., D_INNER + SSD_GROUPS * D_STATE:].reshape(b, s, SSD_GROUPS, D_STATE)
    dt = jax.nn.softplus(dt_raw.reshape(b, s, 2, SSD_HEADS).astype(jnp.float32)
                         + dt_bias.astype(jnp.float32))
    A = -jnp.exp(A_log.astype(jnp.float32))
    y_fwd = ssd_chunked(xs, dt[:, :, 0], A[0], Bm, Cm)
    flip = lambda t: jnp.flip(t, axis=1)
    y_bwd = flip(ssd_chunked(flip(xs), flip(dt[:, :, 1]), A[1], flip(Bm), flip(Cm)))
    y = y_fwd + y_bwd + xs * D_skip[:, None].astype(xs.dtype)
    y = y.reshape(b, s, D_INNER)
    y = rmsnorm(y * jax.nn.silu(z), norm_w)
    return y @ w_out


def _fwd_setup_inputs(seed: int = 0) -> dict:
    key = jax.random.key(seed)
    ks = jax.random.split(key, 20)
    f32 = jnp.float32
    NC, NS = N_CONF_LAYERS, N_SSD_LAYERS
    x = jax.random.normal(ks[0], (BATCH, SEQ, D_MODEL), f32)
    norm_w = 1.0 + 0.02 * jax.random.normal(ks[1], (DEPTH, D_MODEL), f32)
    final_norm_w = 1.0 + 0.02 * jax.random.normal(ks[2], (D_MODEL,), f32)
    cm_w_in = jax.random.normal(ks[3], (NC, D_MODEL, 3 * D_INNER), f32) * D_MODEL ** -0.5
    cm_dw_w = jax.random.normal(ks[4], (NC, CONF_KERNEL, D_INNER), f32) * CONF_KERNEL ** -0.5
    cm_dw_b = 0.02 * jax.random.normal(ks[5], (NC, D_INNER), f32)
    cm_ln_w = 1.0 + 0.02 * jax.random.normal(ks[6], (NC, D_INNER), f32)
    cm_ln_b = 0.02 * jax.random.normal(ks[7], (NC, D_INNER), f32)
    cm_w_out = jax.random.normal(ks[8], (NC, D_INNER, D_MODEL), f32) * D_INNER ** -0.5
    ssd_w_in = jax.random.normal(ks[9], (NS, D_MODEL, SSD_IN), f32) * D_MODEL ** -0.5
    ssd_conv_w = jax.random.normal(ks[10], (NS, SSD_CONV, SSD_CONV_DIM), f32) * SSD_CONV ** -0.5
    ssd_conv_b = 0.02 * jax.random.normal(ks[11], (NS, SSD_CONV_DIM), f32)
    u = jax.random.uniform(ks[12], (NS, 2, SSD_HEADS), f32)
    dt0 = jnp.exp(u * (math.log(0.1) - math.log(0.001)) + math.log(0.001))
    ssd_dt_bias = dt0 + jnp.log(-jnp.expm1(-dt0))
    ssd_A_log = jnp.log(jax.random.uniform(ks[13], (NS, 2, SSD_HEADS), f32, 1.0, 16.0))
    ssd_D = 1.0 + 0.1 * jax.random.normal(ks[14], (NS, SSD_HEADS), f32)
    ssd_norm_w = 1.0 + 0.02 * jax.random.normal(ks[15], (NS, D_INNER), f32)
    ssd_w_out = jax.random.normal(ks[16], (NS, D_INNER, D_MODEL), f32) * D_INNER ** -0.5
    return {"x": x, "norm_w": norm_w, "final_norm_w": final_norm_w,
            "cm_w_in": cm_w_in, "cm_dw_w": cm_dw_w, "cm_dw_b": cm_dw_b,
            "cm_ln_w": cm_ln_w, "cm_ln_b": cm_ln_b, "cm_w_out": cm_w_out,
            "ssd_w_in": ssd_w_in, "ssd_conv_w": ssd_conv_w, "ssd_conv_b": ssd_conv_b,
            "ssd_dt_bias": ssd_dt_bias, "ssd_A_log": ssd_A_log, "ssd_D": ssd_D,
            "ssd_norm_w": ssd_norm_w, "ssd_w_out": ssd_w_out}


def _fwd_reference(x, norm_w, final_norm_w, cm_w_in, cm_dw_w, cm_dw_b, cm_ln_w, cm_ln_b, cm_w_out,
              ssd_w_in, ssd_conv_w, ssd_conv_b, ssd_dt_bias, ssd_A_log, ssd_D, ssd_norm_w, ssd_w_out):
    h = x
    for i in range(DEPTH):
        hn = rmsnorm(h, norm_w[i])
        j = i // N_MIXERS
        if i % N_MIXERS == 0:
            out = conformer_mixer(hn, cm_w_in[j], cm_dw_w[j], cm_dw_b[j], cm_ln_w[j], cm_ln_b[j], cm_w_out[j])
        else:
            out = ssd_mixer(hn, ssd_w_in[j], ssd_conv_w[j], ssd_conv_b[j], ssd_dt_bias[j], ssd_A_log[j],
                            ssd_D[j], ssd_norm_w[j], ssd_w_out[j])
        h = h + out
    return rmsnorm(h, final_norm_w)


import jax as _jax
import jax.numpy as _jnp

TWIN_FORMAT = 'train_step'
FWD_PARAMS = ['x', 'norm_w', 'final_norm_w', 'cm_w_in', 'cm_dw_w', 'cm_dw_b', 'cm_ln_w', 'cm_ln_b', 'cm_w_out', 'ssd_w_in', 'ssd_conv_w', 'ssd_conv_b', 'ssd_dt_bias', 'ssd_A_log', 'ssd_D', 'ssd_norm_w', 'ssd_w_out']
TWIN_WEIGHTS = ['norm_w', 'final_norm_w', 'cm_w_in', 'cm_dw_w', 'cm_dw_b', 'cm_ln_w', 'cm_ln_b', 'cm_w_out', 'ssd_w_in', 'ssd_conv_w', 'ssd_conv_b', 'ssd_dt_bias', 'ssd_A_log', 'ssd_D', 'ssd_norm_w', 'ssd_w_out']
TWIN_DIFF_INPUT = 'x'
TWIN_INPUTS = ['x', 'norm_w', 'final_norm_w', 'cm_w_in', 'cm_dw_w', 'cm_dw_b', 'cm_ln_w', 'cm_ln_b', 'cm_w_out', 'ssd_w_in', 'ssd_conv_w', 'ssd_conv_b', 'ssd_dt_bias', 'ssd_A_log', 'ssd_D', 'ssd_norm_w', 'ssd_w_out', 'loss_target', 'm_norm_w', 'm_final_norm_w', 'm_cm_w_in', 'm_cm_dw_w', 'm_cm_dw_b', 'm_cm_ln_w', 'm_cm_ln_b', 'm_cm_w_out', 'm_ssd_w_in', 'm_ssd_conv_w', 'm_ssd_conv_b', 'm_ssd_dt_bias', 'm_ssd_A_log', 'm_ssd_D', 'm_ssd_norm_w', 'm_ssd_w_out', 'v_norm_w', 'v_final_norm_w', 'v_cm_w_in', 'v_cm_dw_w', 'v_cm_dw_b', 'v_cm_ln_w', 'v_cm_ln_b', 'v_cm_w_out', 'v_ssd_w_in', 'v_ssd_conv_w', 'v_ssd_conv_b', 'v_ssd_dt_bias', 'v_ssd_A_log', 'v_ssd_D', 'v_ssd_norm_w', 'v_ssd_w_out']
TWIN_OUTPUTS = ['loss', 'grad_x', 'grad_norm_w', 'grad_final_norm_w', 'grad_cm_w_in', 'grad_cm_dw_w', 'grad_cm_dw_b', 'grad_cm_ln_w', 'grad_cm_ln_b', 'grad_cm_w_out', 'grad_ssd_w_in', 'grad_ssd_conv_w', 'grad_ssd_conv_b', 'grad_ssd_dt_bias', 'grad_ssd_A_log', 'grad_ssd_D', 'grad_ssd_norm_w', 'grad_ssd_w_out', 'delta_norm_w', 'delta_final_norm_w', 'delta_cm_w_in', 'delta_cm_dw_w', 'delta_cm_dw_b', 'delta_cm_ln_w', 'delta_cm_ln_b', 'delta_cm_w_out', 'delta_ssd_w_in', 'delta_ssd_conv_w', 'delta_ssd_conv_b', 'delta_ssd_dt_bias', 'delta_ssd_A_log', 'delta_ssd_D', 'delta_ssd_norm_w', 'delta_ssd_w_out', 'new_m_norm_w', 'new_m_final_norm_w', 'new_m_cm_w_in', 'new_m_cm_dw_w', 'new_m_cm_dw_b', 'new_m_cm_ln_w', 'new_m_cm_ln_b', 'new_m_cm_w_out', 'new_m_ssd_w_in', 'new_m_ssd_conv_w', 'new_m_ssd_conv_b', 'new_m_ssd_dt_bias', 'new_m_ssd_A_log', 'new_m_ssd_D', 'new_m_ssd_norm_w', 'new_m_ssd_w_out', 'new_v_norm_w', 'new_v_final_norm_w', 'new_v_cm_w_in', 'new_v_cm_dw_w', 'new_v_cm_dw_b', 'new_v_cm_ln_w', 'new_v_cm_ln_b', 'new_v_cm_w_out', 'new_v_ssd_w_in', 'new_v_ssd_conv_w', 'new_v_ssd_conv_b', 'new_v_ssd_dt_bias', 'new_v_ssd_A_log', 'new_v_ssd_D', 'new_v_ssd_norm_w', 'new_v_ssd_w_out']
TWIN_LEAF_KINDS = {'loss': 'loss', 'grad_x': 'grad_x', 'grad_norm_w': 'grad_w', 'grad_final_norm_w': 'grad_w', 'grad_cm_w_in': 'grad_w', 'grad_cm_dw_w': 'grad_w', 'grad_cm_dw_b': 'grad_w', 'grad_cm_ln_w': 'grad_w', 'grad_cm_ln_b': 'grad_w', 'grad_cm_w_out': 'grad_w', 'grad_ssd_w_in': 'grad_w', 'grad_ssd_conv_w': 'grad_w', 'grad_ssd_conv_b': 'grad_w', 'grad_ssd_dt_bias': 'grad_w', 'grad_ssd_A_log': 'grad_w', 'grad_ssd_D': 'grad_w', 'grad_ssd_norm_w': 'grad_w', 'grad_ssd_w_out': 'grad_w', 'delta_norm_w': 'delta_w', 'delta_final_norm_w': 'delta_w', 'delta_cm_w_in': 'delta_w', 'delta_cm_dw_w': 'delta_w', 'delta_cm_dw_b': 'delta_w', 'delta_cm_ln_w': 'delta_w', 'delta_cm_ln_b': 'delta_w', 'delta_cm_w_out': 'delta_w', 'delta_ssd_w_in': 'delta_w', 'delta_ssd_conv_w': 'delta_w', 'delta_ssd_conv_b': 'delta_w', 'delta_ssd_dt_bias': 'delta_w', 'delta_ssd_A_log': 'delta_w', 'delta_ssd_D': 'delta_w', 'delta_ssd_norm_w': 'delta_w', 'delta_ssd_w_out': 'delta_w', 'new_m_norm_w': 'new_m', 'new_m_final_norm_w': 'new_m', 'new_m_cm_w_in': 'new_m', 'new_m_cm_dw_w': 'new_m', 'new_m_cm_dw_b': 'new_m', 'new_m_cm_ln_w': 'new_m', 'new_m_cm_ln_b': 'new_m', 'new_m_cm_w_out': 'new_m', 'new_m_ssd_w_in': 'new_m', 'new_m_ssd_conv_w': 'new_m', 'new_m_ssd_conv_b': 'new_m', 'new_m_ssd_dt_bias': 'new_m', 'new_m_ssd_A_log': 'new_m', 'new_m_ssd_D': 'new_m', 'new_m_ssd_norm_w': 'new_m', 'new_m_ssd_w_out': 'new_m', 'new_v_norm_w': 'new_v', 'new_v_final_norm_w': 'new_v', 'new_v_cm_w_in': 'new_v', 'new_v_cm_dw_w': 'new_v', 'new_v_cm_dw_b': 'new_v', 'new_v_cm_ln_w': 'new_v', 'new_v_cm_ln_b': 'new_v', 'new_v_cm_w_out': 'new_v', 'new_v_ssd_w_in': 'new_v', 'new_v_ssd_conv_w': 'new_v', 'new_v_ssd_conv_b': 'new_v', 'new_v_ssd_dt_bias': 'new_v', 'new_v_ssd_A_log': 'new_v', 'new_v_ssd_D': 'new_v', 'new_v_ssd_norm_w': 'new_v', 'new_v_ssd_w_out': 'new_v'}


def _forward(args):
    return _fwd_reference(*[args[k] for k in FWD_PARAMS])


def _output_shape():
    def fwd():
        inp = _fwd_setup_inputs(0)
        return _fwd_reference(*[inp[k] for k in FWD_PARAMS])
    out = _jax.eval_shape(fwd)
    return out.shape, out.dtype

N_MICROBATCH = 1
ADAM_LR = 0.001
ADAM_B1 = 0.9
ADAM_B2 = 0.999
ADAM_EPS = 1e-08
ADAM_WD = 0.01
ADAM_STEP = 10
PER_EXAMPLE_BATCH_AXIS = {'x': 0, 'loss_target': 0}
SHARED_INPUTS = []
_WEIGHT_DTYPES = {'norm_w': _jnp.float32, 'final_norm_w': _jnp.float32, 'cm_w_in': _jnp.float32, 'cm_dw_w': _jnp.float32, 'cm_dw_b': _jnp.float32, 'cm_ln_w': _jnp.float32, 'cm_ln_b': _jnp.float32, 'cm_w_out': _jnp.float32, 'ssd_w_in': _jnp.float32, 'ssd_conv_w': _jnp.float32, 'ssd_conv_b': _jnp.float32, 'ssd_dt_bias': _jnp.float32, 'ssd_A_log': _jnp.float32, 'ssd_D': _jnp.float32, 'ssd_norm_w': _jnp.float32, 'ssd_w_out': _jnp.float32}
MOMENT_SCALE = {'norm_w': 3.162429e-01, 'final_norm_w': 1.281228e+02, 'cm_w_in': 9.238122e-02, 'cm_dw_w': 1.086039e-01, 'cm_dw_b': 2.086282e-01, 'cm_ln_w': 1.287215e-01, 'cm_ln_b': 1.106745e-01, 'cm_w_out': 1.496563e-01, 'ssd_w_in': 1.704549e-01, 'ssd_conv_w': 1.552605e-01, 'ssd_conv_b': 2.409217e-01, 'ssd_dt_bias': 5.213085e-01, 'ssd_A_log': 4.067954e-01, 'ssd_D': 1.284114e+00, 'ssd_norm_w': 1.785866e-01, 'ssd_w_out': 2.578564e-01}


def _to_microbatches(a, axis):
    t = _jnp.moveaxis(a, axis, 0)
    t = t.reshape((N_MICROBATCH, t.shape[0] // N_MICROBATCH) + t.shape[1:])
    return _jnp.moveaxis(t, 1, axis + 1)


def setup_inputs(seed: int = 0) -> dict:
    inp = _fwd_setup_inputs(seed)
    key = _jax.random.fold_in(_jax.random.key(seed), 7919)
    shape, _ = _output_shape()
    out = dict(inp)
    out["loss_target"] = _jax.random.normal(_jax.random.fold_in(key, 0), shape, _jnp.float32)
    for i, name in enumerate(TWIN_WEIGHTS):
        w = inp[name].astype(_jnp.float32)
        if MOMENT_SCALE is None:
            s = _jnp.sqrt(_jnp.mean(_jnp.square(w)) + 1e-30)
        else:
            s = MOMENT_SCALE[name]
        km, kv = _jax.random.split(_jax.random.fold_in(key, i + 1))
        out[name] = w
        out["m_" + name] = s * _jax.random.normal(km, w.shape, _jnp.float32)
        out["v_" + name] = (s * s) * _jax.random.uniform(kv, w.shape, _jnp.float32, 0.5, 1.5)
    if N_MICROBATCH > 1:
        for name, axis in PER_EXAMPLE_BATCH_AXIS.items():
            out[name] = _to_microbatches(out[name], axis)
    return {'x': out['x'], 'norm_w': out['norm_w'], 'final_norm_w': out['final_norm_w'], 'cm_w_in': out['cm_w_in'], 'cm_dw_w': out['cm_dw_w'], 'cm_dw_b': out['cm_dw_b'], 'cm_ln_w': out['cm_ln_w'], 'cm_ln_b': out['cm_ln_b'], 'cm_w_out': out['cm_w_out'], 'ssd_w_in': out['ssd_w_in'], 'ssd_conv_w': out['ssd_conv_w'], 'ssd_conv_b': out['ssd_conv_b'], 'ssd_dt_bias': out['ssd_dt_bias'], 'ssd_A_log': out['ssd_A_log'], 'ssd_D': out['ssd_D'], 'ssd_norm_w': out['ssd_norm_w'], 'ssd_w_out': out['ssd_w_out'], 'loss_target': out['loss_target'], 'm_norm_w': out['m_norm_w'], 'm_final_norm_w': out['m_final_norm_w'], 'm_cm_w_in': out['m_cm_w_in'], 'm_cm_dw_w': out['m_cm_dw_w'], 'm_cm_dw_b': out['m_cm_dw_b'], 'm_cm_ln_w': out['m_cm_ln_w'], 'm_cm_ln_b': out['m_cm_ln_b'], 'm_cm_w_out': out['m_cm_w_out'], 'm_ssd_w_in': out['m_ssd_w_in'], 'm_ssd_conv_w': out['m_ssd_conv_w'], 'm_ssd_conv_b': out['m_ssd_conv_b'], 'm_ssd_dt_bias': out['m_ssd_dt_bias'], 'm_ssd_A_log': out['m_ssd_A_log'], 'm_ssd_D': out['m_ssd_D'], 'm_ssd_norm_w': out['m_ssd_norm_w'], 'm_ssd_w_out': out['m_ssd_w_out'], 'v_norm_w': out['v_norm_w'], 'v_final_norm_w': out['v_final_norm_w'], 'v_cm_w_in': out['v_cm_w_in'], 'v_cm_dw_w': out['v_cm_dw_w'], 'v_cm_dw_b': out['v_cm_dw_b'], 'v_cm_ln_w': out['v_cm_ln_w'], 'v_cm_ln_b': out['v_cm_ln_b'], 'v_cm_w_out': out['v_cm_w_out'], 'v_ssd_w_in': out['v_ssd_w_in'], 'v_ssd_conv_w': out['v_ssd_conv_w'], 'v_ssd_conv_b': out['v_ssd_conv_b'], 'v_ssd_dt_bias': out['v_ssd_dt_bias'], 'v_ssd_A_log': out['v_ssd_A_log'], 'v_ssd_D': out['v_ssd_D'], 'v_ssd_norm_w': out['v_ssd_norm_w'], 'v_ssd_w_out': out['v_ssd_w_out']}


def _loss(weights, diff, rest, loss_target):
    with _jax.named_scope("forward"):
        args = {**rest, TWIN_DIFF_INPUT: diff, **{k: w.astype(_WEIGHT_DTYPES[k]) for k, w in weights.items()}}
        y = _forward(args)
    with _jax.named_scope("loss_head"):
        err = _jnp.square(y.astype(_jnp.float32) - loss_target)
        return 0.5 * _jnp.sum(_jnp.mean(err, axis=-1)) if err.ndim else 0.5 * err


def _adamw(w, g, m, v):
    m = ADAM_B1 * m + (1.0 - ADAM_B1) * g
    v = ADAM_B2 * v + (1.0 - ADAM_B2) * _jnp.square(g)
    m_hat = m / (1.0 - ADAM_B1 ** ADAM_STEP)
    v_hat = v / (1.0 - ADAM_B2 ** ADAM_STEP)
    delta = -ADAM_LR * (m_hat / (_jnp.sqrt(v_hat) + ADAM_EPS) + ADAM_WD * w)
    return delta, m, v


def reference(x, norm_w, final_norm_w, cm_w_in, cm_dw_w, cm_dw_b, cm_ln_w, cm_ln_b, cm_w_out, ssd_w_in, ssd_conv_w, ssd_conv_b, ssd_dt_bias, ssd_A_log, ssd_D, ssd_norm_w, ssd_w_out, loss_target, m_norm_w, m_final_norm_w, m_cm_w_in, m_cm_dw_w, m_cm_dw_b, m_cm_ln_w, m_cm_ln_b, m_cm_w_out, m_ssd_w_in, m_ssd_conv_w, m_ssd_conv_b, m_ssd_dt_bias, m_ssd_A_log, m_ssd_D, m_ssd_norm_w, m_ssd_w_out, v_norm_w, v_final_norm_w, v_cm_w_in, v_cm_dw_w, v_cm_dw_b, v_cm_ln_w, v_cm_ln_b, v_cm_w_out, v_ssd_w_in, v_ssd_conv_w, v_ssd_conv_b, v_ssd_dt_bias, v_ssd_A_log, v_ssd_D, v_ssd_norm_w, v_ssd_w_out):
    given = dict(x=x, norm_w=norm_w, final_norm_w=final_norm_w, cm_w_in=cm_w_in, cm_dw_w=cm_dw_w, cm_dw_b=cm_dw_b, cm_ln_w=cm_ln_w, cm_ln_b=cm_ln_b, cm_w_out=cm_w_out, ssd_w_in=ssd_w_in, ssd_conv_w=ssd_conv_w, ssd_conv_b=ssd_conv_b, ssd_dt_bias=ssd_dt_bias, ssd_A_log=ssd_A_log, ssd_D=ssd_D, ssd_norm_w=ssd_norm_w, ssd_w_out=ssd_w_out, loss_target=loss_target, m_norm_w=m_norm_w, m_final_norm_w=m_final_norm_w, m_cm_w_in=m_cm_w_in, m_cm_dw_w=m_cm_dw_w, m_cm_dw_b=m_cm_dw_b, m_cm_ln_w=m_cm_ln_w, m_cm_ln_b=m_cm_ln_b, m_cm_w_out=m_cm_w_out, m_ssd_w_in=m_ssd_w_in, m_ssd_conv_w=m_ssd_conv_w, m_ssd_conv_b=m_ssd_conv_b, m_ssd_dt_bias=m_ssd_dt_bias, m_ssd_A_log=m_ssd_A_log, m_ssd_D=m_ssd_D, m_ssd_norm_w=m_ssd_norm_w, m_ssd_w_out=m_ssd_w_out, v_norm_w=v_norm_w, v_final_norm_w=v_final_norm_w, v_cm_w_in=v_cm_w_in, v_cm_dw_w=v_cm_dw_w, v_cm_dw_b=v_cm_dw_b, v_cm_ln_w=v_cm_ln_w, v_cm_ln_b=v_cm_ln_b, v_cm_w_out=v_cm_w_out, v_ssd_w_in=v_ssd_w_in, v_ssd_conv_w=v_ssd_conv_w, v_ssd_conv_b=v_ssd_conv_b, v_ssd_dt_bias=v_ssd_dt_bias, v_ssd_A_log=v_ssd_A_log, v_ssd_D=v_ssd_D, v_ssd_norm_w=v_ssd_norm_w, v_ssd_w_out=v_ssd_w_out)
    weights = {n: given[n] for n in TWIN_WEIGHTS}
    shared = {n: given[n] for n in SHARED_INPUTS}
    per_example = {n: given[n] for n in ['x']}
    grad_fn = _jax.value_and_grad(_loss, argnums=(0, 1))

    def one_microbatch(ex, loss_target):
        ex = dict(ex)
        diff = ex.pop(TWIN_DIFF_INPUT)
        return grad_fn(weights, diff, {**shared, **ex}, loss_target)

    if N_MICROBATCH == 1:
        loss, (grad_w, grad_x) = one_microbatch(per_example, given["loss_target"])
    else:
        def body(carry, xs):
            loss_sum, grad_sum = carry
            l_k, (gw_k, gx_k) = one_microbatch(xs[0], xs[1])
            with _jax.named_scope("update"):
                return (loss_sum + l_k, _jax.tree.map(_jnp.add, grad_sum, gw_k)), gx_k

        init = (_jnp.zeros((), _jnp.float32), _jax.tree.map(_jnp.zeros_like, weights))
        (loss, grad_w), grad_x = _jax.lax.scan(body, init, (per_example, given["loss_target"]))
    with _jax.named_scope("update"):
        delta_w, new_m, new_v = {}, {}, {}
        for n in TWIN_WEIGHTS:
            delta_w[n], new_m[n], new_v[n] = _adamw(weights[n], grad_w[n], given["m_" + n], given["v_" + n])
    return (loss, grad_x, *[grad_w[n] for n in TWIN_WEIGHTS], *[delta_w[n] for n in TWIN_WEIGHTS],
            *[new_m[n] for n in TWIN_WEIGHTS], *[new_v[n] for n in TWIN_WEIGHTS])
```

```python
import jax
import jax.numpy as jnp
from jax import lax
from jax.experimental import pallas as pl
from jax.experimental.pallas import tpu as pltpu

F32 = jnp.float32
BF16 = jnp.bfloat16
MESH = pl.DeviceIdType.MESH

EPS = 1e-5
HEADDIM = 64
HEADS = 32
GROUPS = 4
HPG = HEADS // GROUPS
D_STATE = 128
CHUNK = 128
GW = HPG * HEADDIM
XCG = GW + 2 * D_STATE
HALO = 16
LANES = 128
N_CHIPS = 4
N_DEV = 8

ADAM_LR = 0.001
ADAM_B1 = 0.9
ADAM_B2 = 0.999
ADAM_EPS = 1e-08
ADAM_WD = 0.01
ADAM_STEP = 10

VMEM_LIMIT = 52 * 1024 * 1024


def _params(n_axes):
    return pltpu.CompilerParams(dimension_semantics=("arbitrary",) * n_axes, vmem_limit_bytes=VMEM_LIMIT)


def _sigmoid(x):
    return 1.0 / (1.0 + jnp.exp(-x))


def _softplus(x):
    return jnp.maximum(x, 0.0) + jnp.log(1.0 + jnp.exp(-jnp.abs(x)))


def _dot(a, b):
    return jnp.dot(a, b, preferred_element_type=F32)


def _dot_nt(a, b):
    return lax.dot_general(a, b, (((1,), (1,)), ((), ())), preferred_element_type=F32)


def _dot_tn(a, b):
    return lax.dot_general(a, b, (((0,), (0,)), ((), ())), preferred_element_type=F32)


def _pick(n, pref):
    for t in pref:
        if n % t == 0:
            return t
    return n


def mm_nn(a, b, *, out_dtype, name, res=None, a2=None, b2=None):
    M, K = a.shape
    N = b.shape[1]
    tm = _pick(M, (1024, 512, 256, 128))
    tn = _pick(N, (1024, 512, 256, 128))
    tk = _pick(K, (1024, 512, 256, 128))
    nk = K // tk
    has2, has_res = a2 is not None, res is not None

    def body(*refs):
        a_ref, b_ref = refs[0], refs[1]
        pos = 2
        if has2:
            a2_ref, b2_ref = refs[pos], refs[pos + 1]
            pos += 2
        if has_res:
            r_ref = refs[pos]
            pos += 1
        o_ref, acc_ref = refs[pos], refs[pos + 1]
        k = pl.program_id(2)

        @pl.when(k == 0)
        def _():
            if has2:
                acc_ref[...] = _dot(a2_ref[...].astype(BF16), b2_ref[...])
            else:
                acc_ref[...] = jnp.zeros_like(acc_ref)

        acc_ref[...] += _dot(a_ref[...].astype(BF16), b_ref[...])

        @pl.when(k == nk - 1)
        def _():
            r = acc_ref[...]
            if has_res:
                r = r + r_ref[...]
            o_ref[...] = r.astype(out_dtype)

    in_specs = [pl.BlockSpec((tm, tk), lambda i, j, k: (i, k)), pl.BlockSpec((tk, tn), lambda i, j, k: (k, j))]
    args = [a, b]
    if has2:
        k2 = a2.shape[1]
        in_specs += [pl.BlockSpec((tm, k2), lambda i, j, k: (i, 0)), pl.BlockSpec((k2, tn), lambda i, j, k: (0, j))]
        args += [a2, b2]
    if has_res:
        in_specs.append(pl.BlockSpec((tm, tn), lambda i, j, k: (i, j)))
        args.append(res)
    return pl.pallas_call(
        body, name=name, grid=(M // tm, N // tn, nk), in_specs=in_specs,
        out_specs=pl.BlockSpec((tm, tn), lambda i, j, k: (i, j)),
        out_shape=jax.ShapeDtypeStruct((M, N), out_dtype),
        scratch_shapes=[pltpu.VMEM((tm, tn), F32)], compiler_params=_params(3),
    )(*args)


def mm_tn(a, b, *, name):
    T, M = a.shape
    N = b.shape[1]
    tm = _pick(M, (1024, 512, 256, 128))
    tn = _pick(N, (1024, 512, 256, 128))
    tt = _pick(T, (1024, 512, 256, 128))

    def body(a_ref, b_ref, o_ref):
        @pl.when(pl.program_id(2) == 0)
        def _():
            o_ref[...] = jnp.zeros_like(o_ref)

        o_ref[...] += _dot_tn(a_ref[...].astype(BF16), b_ref[...].astype(BF16))

    return pl.pallas_call(
        body, name=name, grid=(M // tm, N // tn, T // tt),
        in_specs=[pl.BlockSpec((tt, tm), lambda i, j, t: (t, i)), pl.BlockSpec((tt, tn), lambda i, j, t: (t, j))],
        out_specs=pl.BlockSpec((tm, tn), lambda i, j, t: (i, j)),
        out_shape=jax.ShapeDtypeStruct((M, N), F32), compiler_params=_params(3),
    )(a, b)


def rmsnorm_fwd(h, w, *, name):
    T, D = h.shape
    tm = _pick(T, (512, 256, 128))

    def body(h_ref, w_ref, o_ref):
        x = h_ref[...]
        rstd = lax.rsqrt(jnp.mean(x * x, axis=-1, keepdims=True) + EPS)
        o_ref[...] = (x * rstd * w_ref[...]).astype(BF16)

    return pl.pallas_call(
        body, name=name, grid=(T // tm,),
        in_specs=[pl.BlockSpec((tm, D), lambda i: (i, 0)), pl.BlockSpec((1, D), lambda i: (0, 0))],
        out_specs=pl.BlockSpec((tm, D), lambda i: (i, 0)),
        out_shape=jax.ShapeDtypeStruct((T, D), BF16), compiler_params=_params(1),
    )(h, w)


def rmsnorm_bwd(dhn, h, w, dh, *, name):
    T, D = h.shape
    tm = _pick(T, (512, 256, 128))

    def body(dhn_ref, h_ref, w_ref, dh_ref, o_ref, dw_ref):
        @pl.when(pl.program_id(0) == 0)
        def _():
            dw_ref[...] = jnp.zeros_like(dw_ref)

        x = h_ref[...]
        g = dhn_ref[...]
        rstd = lax.rsqrt(jnp.mean(x * x, axis=-1, keepdims=True) + EPS)
        xhat = x * rstd
        dxh = g * w_ref[...]
        o_ref[...] = dh_ref[...] + rstd * (dxh - xhat * jnp.mean(dxh * xhat, axis=-1, keepdims=True))
        dw_ref[...] += jnp.sum(g * xhat, axis=0, keepdims=True)

    row = pl.BlockSpec((tm, D), lambda i: (i, 0))
    vec = pl.BlockSpec((1, D), lambda i: (0, 0))
    return pl.pallas_call(
        body, name=name, grid=(T // tm,), in_specs=[row, row, vec, row], out_specs=(row, vec),
        out_shape=(jax.ShapeDtypeStruct((T, D), F32), jax.ShapeDtypeStruct((1, D), F32)),
        compiler_params=_params(1),
    )(dhn, h, w, dh)


def loss_head(h, target, w, *, name):
    T, D = h.shape
    tm = _pick(T, (512, 256, 128))

    def body(h_ref, t_ref, w_ref, dh_ref, loss_ref, dw_ref):
        @pl.when(pl.program_id(0) == 0)
        def _():
            loss_ref[...] = jnp.zeros_like(loss_ref)
            dw_ref[...] = jnp.zeros_like(dw_ref)

        x = h_ref[...]
        rstd = lax.rsqrt(jnp.mean(x * x, axis=-1, keepdims=True) + EPS)
        xhat = x * rstd
        err = xhat * w_ref[...] - t_ref[...]
        rows = jnp.sum(err * err, axis=-1, keepdims=True)
        loss_ref[...] += (0.5 / D) * jnp.sum(rows, axis=0, keepdims=True)
        dy = err * (1.0 / D)
        dxh = dy * w_ref[...]
        dh_ref[...] = rstd * (dxh - xhat * jnp.mean(dxh * xhat, axis=-1, keepdims=True))
        dw_ref[...] += jnp.sum(dy * xhat, axis=0, keepdims=True)

    row = pl.BlockSpec((tm, D), lambda i: (i, 0))
    vec = pl.BlockSpec((1, D), lambda i: (0, 0))
    return pl.pallas_call(
        body, name=name, grid=(T // tm,), in_specs=[row, row, vec],
        out_specs=(row, pl.BlockSpec((1, 1), lambda i: (0, 0)), vec),
        out_shape=(jax.ShapeDtypeStruct((T, D), F32), jax.ShapeDtypeStruct((1, 1), F32),
                   jax.ShapeDtypeStruct((1, D), F32)),
        compiler_params=_params(1),
    )(h, target, w)


CONV_TM = 256
CONV_TC = 512
CONV_RB = 32


def _conv_specs(T, tm, sw, col0):
    hb = tm // HALO
    last = T // HALO - 1
    main = pl.BlockSpec((tm, sw), lambda j, i: (i, col0 + j))
    prev = pl.BlockSpec((HALO, sw), lambda j, i: (jnp.maximum(i * hb - 1, 0), col0 + j))
    nxt = pl.BlockSpec((HALO, sw), lambda j, i: (jnp.minimum((i + 1) * hb, last), col0 + j))
    return main, prev, nxt


def _conv_input(blk, glu, tc):
    x = blk.astype(F32)
    if glu:
        return x[:, :tc] * _sigmoid(x[:, tc:])
    return x


def _fill_padded(pad_ref, main, prev, nxt, first, last, tm):
    pad_ref[0:HALO, :] = jnp.where(first, 0.0, prev)
    pad_ref[HALO:HALO + tm, :] = main
    pad_ref[HALO + tm:HALO + tm + HALO, :] = jnp.where(last, 0.0, nxt)


def dwconv_fwd(src, w, b, *, width, glu, silu, col0, name):
    T = src.shape[0]
    C = w.shape[1]
    tm, tc = min(CONV_TM, T), CONV_TC
    sw = 2 * tc if glu else tc
    n_i = T // tm
    p = (width - 1) // 2
    rb = CONV_RB

    def body(m_ref, p_ref, n_ref, w_ref, b_ref, o_ref, pad_ref):
        i = pl.program_id(1)
        _fill_padded(pad_ref, _conv_input(m_ref[...], glu, tc), _conv_input(p_ref[...], glu, tc),
                     _conv_input(n_ref[...], glu, tc), i == 0, i == n_i - 1, tm)
        for r in range(tm // rb):
            acc = jnp.zeros((rb, tc), F32)
            for k in range(width):
                off = HALO - p + k + r * rb
                acc = acc + pad_ref[off:off + rb, :] * w_ref[k:k + 1, :]
            acc = acc + b_ref[...]
            if silu:
                acc = acc * _sigmoid(acc)
            o_ref[r * rb:(r + 1) * rb, :] = acc.astype(BF16)

    main, prev, nxt = _conv_specs(T, tm, sw, col0)
    return pl.pallas_call(
        body, name=name, grid=(C // tc, n_i),
        in_specs=[main, prev, nxt, pl.BlockSpec((w.shape[0], tc), lambda j, i: (0, j)),
                  pl.BlockSpec((1, tc), lambda j, i: (0, j))],
        out_specs=pl.BlockSpec((tm, tc), lambda j, i: (i, j)),
        out_shape=jax.ShapeDtypeStruct((T, C), BF16),
        scratch_shapes=[pltpu.VMEM((tm + 2 * HALO, tc), F32)], compiler_params=_params(2),
    )(src, src, src, w, b)


def dwconv_bwd(dout, src, w, b, dsrc, *, width, glu, silu, col0, dcol0, name):
    T = src.shape[0]
    C = w.shape[1]
    kp = w.shape[0]
    tm, tc = min(CONV_TM, T), CONV_TC
    sw = 2 * tc if glu else tc
    n_i = T // tm
    p = (width - 1) // 2
    rb = CONV_RB
    edge = 8
    assert p <= edge or not silu

    def body(dm_ref, dp_ref, dn_ref, m_ref, p_ref, n_ref, w_ref, b_ref, _, o_ref, dw_ref, db_ref, pad_ref, dpre_ref):
        i = pl.program_id(1)

        @pl.when(i == 0)
        def _():
            dw_ref[...] = jnp.zeros_like(dw_ref)
            db_ref[...] = jnp.zeros_like(db_ref)

        first, last = i == 0, i == n_i - 1
        _fill_padded(pad_ref, _conv_input(m_ref[...], glu, tc), _conv_input(p_ref[...], glu, tc),
                     _conv_input(n_ref[...], glu, tc), first, last, tm)
        _fill_padded(dpre_ref, dm_ref[...].astype(F32), dp_ref[...].astype(F32), dn_ref[...].astype(F32),
                     first, last, tm)
        if silu:
            for r0 in range(HALO - edge, HALO + tm + edge, HALO):
                pre = jnp.zeros((HALO, tc), F32)
                for k in range(width):
                    pre = pre + pad_ref[r0 - p + k:r0 - p + k + HALO, :] * w_ref[k:k + 1, :]
                pre = pre + b_ref[...]
                s = _sigmoid(pre)
                dpre_ref[r0:r0 + HALO, :] = dpre_ref[r0:r0 + HALO, :] * (s * (1.0 + pre * (1.0 - s)))

        for r in range(tm // rb):
            acc = jnp.zeros((rb, tc), F32)
            for k in range(width):
                off = HALO + p - k + r * rb
                acc = acc + dpre_ref[off:off + rb, :] * w_ref[k:k + 1, :]
            if glu:
                blk = m_ref[r * rb:(r + 1) * rb, :].astype(F32)
                v, s = blk[:, :tc], _sigmoid(blk[:, tc:])
                o_ref[r * rb:(r + 1) * rb, :tc] = (acc * s).astype(BF16)
                o_ref[r * rb:(r + 1) * rb, tc:] = (acc * v * s * (1.0 - s)).astype(BF16)
            else:
                o_ref[r * rb:(r + 1) * rb, :] = acc.astype(BF16)

        dmain = dpre_ref[HALO:HALO + tm, :]
        for k in range(width):
            off = HALO - p + k
            dw_ref[k:k + 1, :] += jnp.sum(dmain * pad_ref[off:off + tm, :], axis=0, keepdims=True)
        db_ref[...] += jnp.sum(dmain, axis=0, keepdims=True)

    dmain_s, dprev_s, dnext_s = _conv_specs(T, tm, tc, 0)
    main, prev, nxt = _conv_specs(T, tm, sw, col0)
    wspec = pl.BlockSpec((kp, tc), lambda j, i: (0, j))
    bspec = pl.BlockSpec((1, tc), lambda j, i: (0, j))
    return pl.pallas_call(
        body, name=name, grid=(C // tc, n_i),
        in_specs=[dmain_s, dprev_s, dnext_s, main, prev, nxt, wspec, bspec, pl.BlockSpec(memory_space=pl.ANY)],
        out_specs=(pl.BlockSpec((tm, sw), lambda j, i: (i, dcol0 + j)), wspec, bspec),
        out_shape=(jax.ShapeDtypeStruct(dsrc.shape, dsrc.dtype), jax.ShapeDtypeStruct((kp, C), F32),
                   jax.ShapeDtypeStruct((1, C), F32)),
        input_output_aliases={8: 0},
        scratch_shapes=[pltpu.VMEM((tm + 2 * HALO, tc), F32), pltpu.VMEM((tm + 2 * HALO, tc), F32)],
        compiler_params=_params(2),
    )(dout, dout, dout, src, src, src, w, b, dsrc)


def _silu_grad(x, s):
    return s * (1.0 + x * (1.0 - s))


def conf_ln_fwd(u2, proj, ln_w, ln_b, *, name):
    T, E = u2.shape
    zc = proj.shape[1] // E - 1
    tm = _pick(T, (256, 128))

    def body(u_ref, z_ref, w_ref, b_ref, o_ref):
        x = u_ref[...].astype(F32)
        xc = x - jnp.mean(x, axis=-1, keepdims=True)
        rstd = lax.rsqrt(jnp.mean(xc * xc, axis=-1, keepdims=True) + EPS)
        u3 = xc * rstd * w_ref[...] + b_ref[...]
        z = z_ref[...].astype(F32)
        o_ref[...] = (u3 * _sigmoid(u3) * z * _sigmoid(z)).astype(BF16)

    row = pl.BlockSpec((tm, E), lambda i: (i, 0))
    vec = pl.BlockSpec((1, E), lambda i: (0, 0))
    return pl.pallas_call(
        body, name=name, grid=(T // tm,),
        in_specs=[row, pl.BlockSpec((tm, E), lambda i: (i, zc)), vec, vec], out_specs=row,
        out_shape=jax.ShapeDtypeStruct((T, E), BF16), compiler_params=_params(1),
    )(u2, proj, ln_w, ln_b)


def conf_ln_bwd(du4, u2, proj, ln_w, ln_b, *, name):
    T, E = u2.shape
    ncol = proj.shape[1] // E
    zc = ncol - 1
    tm = _pick(T, (256, 128))

    def body(d_ref, u_ref, z_ref, w_ref, b_ref, du_ref, dz_ref, dw_ref, db_ref):
        @pl.when(pl.program_id(0) == 0)
        def _():
            dw_ref[...] = jnp.zeros_like(dw_ref)
            db_ref[...] = jnp.zeros_like(db_ref)

        x = u_ref[...].astype(F32)
        xc = x - jnp.mean(x, axis=-1, keepdims=True)
        rstd = lax.rsqrt(jnp.mean(xc * xc, axis=-1, keepdims=True) + EPS)
        xhat = xc * rstd
        u3 = xhat * w_ref[...] + b_ref[...]
        z = z_ref[...].astype(F32)
        s3, sz = _sigmoid(u3), _sigmoid(z)
        d4 = d_ref[...].astype(F32)
        du3 = d4 * (z * sz) * _silu_grad(u3, s3)
        dz_ref[...] = (d4 * (u3 * s3) * _silu_grad(z, sz)).astype(BF16)
        dw_ref[...] += jnp.sum(du3 * xhat, axis=0, keepdims=True)
        db_ref[...] += jnp.sum(du3, axis=0, keepdims=True)
        dxh = du3 * w_ref[...]
        du = rstd * (dxh - jnp.mean(dxh, axis=-1, keepdims=True) - xhat * jnp.mean(dxh * xhat, axis=-1, keepdims=True))
        du_ref[...] = du.astype(BF16)

    row = pl.BlockSpec((tm, E), lambda i: (i, 0))
    zrow = pl.BlockSpec((tm, E), lambda i: (i, zc))
    vec = pl.BlockSpec((1, E), lambda i: (0, 0))
    return pl.pallas_call(
        body, name=name, grid=(T // tm,), in_specs=[row, row, zrow, vec, vec], out_specs=(row, zrow, vec, vec),
        out_shape=(jax.ShapeDtypeStruct((T, E), BF16), jax.ShapeDtypeStruct(proj.shape, BF16),
                   jax.ShapeDtypeStruct((1, E), F32), jax.ShapeDtypeStruct((1, E), F32)),
        compiler_params=_params(1),
    )(du4, u2, proj, ln_w, ln_b)


def ssd_gate_fwd(y, zx, norm_w, *, name):
    T, E = y.shape
    tm = _pick(T, (256, 128))

    def body(y_ref, z_ref, w_ref, o_ref):
        z = z_ref[...].astype(F32)
        yz = y_ref[...].astype(F32) * (z * _sigmoid(z))
        rstd = lax.rsqrt(jnp.mean(yz * yz, axis=-1, keepdims=True) + EPS)
        o_ref[...] = (yz * rstd * w_ref[...]).astype(BF16)

    row = pl.BlockSpec((tm, E), lambda i: (i, 0))
    vec = pl.BlockSpec((1, E), lambda i: (0, 0))
    return pl.pallas_call(
        body, name=name, grid=(T // tm,), in_specs=[row, row, vec], out_specs=row,
        out_shape=jax.ShapeDtypeStruct((T, E), BF16), compiler_params=_params(1),
    )(y, zx, norm_w)


def ssd_gate_bwd(dyn, y, zx, norm_w, *, name):
    T, E = y.shape
    tm = _pick(T, (256, 128))

    def body(d_ref, y_ref, z_ref, w_ref, dy_ref, dz_ref, dw_ref):
        @pl.when(pl.program_id(0) == 0)
        def _():
            dw_ref[...] = jnp.zeros_like(dw_ref)

        z = z_ref[...].astype(F32)
        sz = _sigmoid(z)
        gate = z * sz
        yv = y_ref[...].astype(F32)
        yz = yv * gate
        rstd = lax.rsqrt(jnp.mean(yz * yz, axis=-1, keepdims=True) + EPS)
        yhat = yz * rstd
        d = d_ref[...].astype(F32)
        dw_ref[...] += jnp.sum(d * yhat, axis=0, keepdims=True)
        dxh = d * w_ref[...]
        dyz = rstd * (dxh - yhat * jnp.mean(dxh * yhat, axis=-1, keepdims=True))
        dy_ref[...] = (dyz * gate).astype(BF16)
        dz_ref[...] = (dyz * yv * _silu_grad(z, sz)).astype(BF16)

    row = pl.BlockSpec((tm, E), lambda i: (i, 0))
    vec = pl.BlockSpec((1, E), lambda i: (0, 0))
    return pl.pallas_call(
        body, name=name, grid=(T // tm,), in_specs=[row, row, row, vec], out_specs=(row, row, vec),
        out_shape=(jax.ShapeDtypeStruct((T, E), BF16), jax.ShapeDtypeStruct(zx.shape, BF16),
                   jax.ShapeDtypeStruct((1, E), F32)),
        compiler_params=_params(1),
    )(dyn, y, zx, norm_w)


def _cumsum_mm(mask, a):
    hi = a.astype(BF16)
    r1 = a - hi.astype(F32)
    mid = r1.astype(BF16)
    lo = (r1 - mid.astype(F32)).astype(BF16)
    out = _dot(jnp.where(mask, 1.0, 0.0).astype(BF16), jnp.concatenate([hi, mid, lo], axis=1))
    return out[:, :LANES] + out[:, LANES:2 * LANES] + out[:, 2 * LANES:]


def _chunk_terms(xcb, dt_raw, bias, alog, rev):
    L = CHUNK
    xs = xcb[:, :GW].astype(F32)
    Bm = xcb[:, GW:GW + D_STATE]
    Cm = xcb[:, GW + D_STATE:]
    pre = dt_raw + bias
    dt = _softplus(pre)
    A = -jnp.exp(alog)
    row = lax.broadcasted_iota(jnp.int32, (L, L), 0)
    col = lax.broadcasted_iota(jnp.int32, (L, L), 1)
    mask = (col >= row) if rev else (col <= row)
    mask_t = (col <= row) if rev else (col >= row)
    cs = _cumsum_mm(mask, dt * A)
    tot = cs[0:1, :] if rev else cs[L - 1:L, :]
    return xs, Bm, Cm, pre, dt, A, mask, mask_t, cs, cs.T, tot


def _decay(cs, cs_t, ln, mask):
    d = cs[:, ln:ln + 1] - cs_t[ln:ln + 1, :]
    return jnp.where(mask, jnp.exp(jnp.where(mask, d, 0.0)), 0.0)


def _pair(v, ln0, lo):
    return jnp.where(lo[:v.shape[0]], v[:, ln0:ln0 + 1], v[:, ln0 + 1:ln0 + 2])


def _scan_specs(nc, rev_order):
    ci = (lambda c: nc - 1 - c) if rev_order else (lambda c: c)
    xc = pl.BlockSpec((CHUNK, XCG), lambda g, c: (ci(c), g))
    dt = pl.BlockSpec((CHUNK, LANES), lambda g, c: (ci(c), g))
    vec = pl.BlockSpec((1, LANES), lambda g, c: (0, g))
    wide = pl.BlockSpec((CHUNK, GW), lambda g, c: (ci(c), g))
    wvec = pl.BlockSpec((1, GW), lambda g, c: (0, g))
    st = pl.BlockSpec((1, D_STATE, GW), lambda g, c: (ci(c), 0, g))
    return xc, dt, vec, wide, wvec, st


def ssd_scan_fwd(xc, dt4, bias4, alog4, *, rev, name, prev=None, dvec=None):
    T = xc.shape[0]
    nc = T // CHUNK
    E = GROUPS * GW
    r = 1 if rev else 0
    skip = prev is not None

    def body(*refs):
        xc_ref, dt_ref, bias_ref, alog_ref = refs[:4]
        pos = 4
        if skip:
            prev_ref, dvec_ref = refs[4], refs[5]
            pos = 6
        y_ref, st_ref, s_ref = refs[pos:pos + 3]

        @pl.when(pl.program_id(1) == 0)
        def _():
            s_ref[...] = jnp.zeros_like(s_ref)

        xs, Bm, Cm, _, dt, _, mask, _, cs, cs_t, tot = _chunk_terms(xc_ref[...], dt_ref[...], bias_ref[...],
                                                                   alog_ref[...], rev)
        e, d, et = jnp.exp(cs), jnp.exp(tot - cs), jnp.exp(tot)
        cb = _dot_nt(Cm, Bm)
        sb = s_ref[...].astype(BF16)
        st_ref[0] = sb
        c_s = _dot(Cm, sb)
        lo = lax.broadcasted_iota(jnp.int32, (CHUNK, LANES), 1) < HEADDIM
        xd_parts, et_parts = [], []
        for p in range(HPG // 2):
            ln0 = r * HPG + 2 * p
            sl = slice(p * LANES, (p + 1) * LANES)
            dtp, ep, dp = _pair(dt, ln0, lo), _pair(e, ln0, lo), _pair(d, ln0, lo)
            xp = xs[:, sl] * dtp
            mcat = jnp.concatenate([cb * _decay(cs, cs_t, ln0, mask), cb * _decay(cs, cs_t, ln0 + 1, mask)],
                                   axis=1).astype(BF16)
            xbd = jnp.concatenate([jnp.where(lo, xp, 0.0), jnp.where(lo, 0.0, xp)], axis=0).astype(BF16)
            yp = _dot(mcat, xbd) + c_s[:, sl] * ep
            if skip:
                yp = yp + prev_ref[:, sl].astype(F32) + xs[:, sl] * dvec_ref[:, sl]
            y_ref[:, sl] = yp.astype(BF16)
            xd_parts.append((xp * dp).astype(BF16))
            et_parts.append(_pair(et, ln0, lo))
        s_ref[...] = s_ref[...] * jnp.concatenate(et_parts, axis=1) + _dot_tn(Bm, jnp.concatenate(xd_parts, axis=1))

    s_xc, s_dt, s_vec, s_wide, s_wvec, s_st = _scan_specs(nc, rev)
    in_specs = [s_xc, s_dt, s_vec, s_vec]
    args = [xc, dt4, bias4, alog4]
    if skip:
        in_specs += [s_wide, s_wvec]
        args += [prev, dvec]
    return pl.pallas_call(
        body, name=name, grid=(GROUPS, nc), in_specs=in_specs, out_specs=(s_wide, s_st),
        out_shape=(jax.ShapeDtypeStruct((T, E), BF16), jax.ShapeDtypeStruct((nc, D_STATE, E), BF16)),
        scratch_shapes=[pltpu.VMEM((D_STATE, GW), F32)], compiler_params=_params(2),
    )(*args)


def ssd_scan_bwd(xc, dt4, bias4, alog4, dy, states, *, rev, name, prev=None, dvec=None):
    T = xc.shape[0]
    nc = T // CHUNK
    E = GROUPS * GW
    L = CHUNK
    r = 1 if rev else 0
    skip = prev is not None

    def body(*refs):
        xc_ref, dt_ref, bias_ref, alog_ref, dy_ref, st_ref = refs[:6]
        pos = 6
        if skip:
            pdxc_ref, pddt_ref, dvec_ref = refs[6:9]
            pos = 9
        dxc_ref, ddt_ref, dalog_ref, dbias_ref = refs[pos:pos + 4]
        pos += 4
        if skip:
            dd_ref = refs[pos]
            pos += 1
        g_ref = refs[pos]

        @pl.when(pl.program_id(1) == 0)
        def _():
            g_ref[...] = jnp.zeros_like(g_ref)
            dalog_ref[...] = jnp.zeros_like(dalog_ref)
            dbias_ref[...] = jnp.zeros_like(dbias_ref)
            if skip:
                dd_ref[...] = jnp.zeros_like(dd_ref)

        xs, Bm, Cm, pre, dt, A, mask, mask_t, cs, cs_t, tot = _chunk_terms(
            xc_ref[...], dt_ref[...], bias_ref[...], alog_ref[...], rev)
        e, d, et = jnp.exp(cs), jnp.exp(tot - cs), jnp.exp(tot)
        cb = _dot_nt(Cm, Bm)
        s_in = st_ref[0]
        dy_all = dy_ref[...].astype(F32)
        g_f = g_ref[...]
        g_b = g_f.astype(BF16)
        c_s = _dot(Cm, s_in)
        b_g = _dot(Bm, g_b)
        lane = lax.broadcasted_iota(jnp.int32, (L, LANES), 1)
        lane1 = lax.broadcasted_iota(jnp.int32, (1, LANES), 1)
        sub = lax.broadcasted_iota(jnp.int32, (LANES, L), 0)
        lo = lane < HEADDIM
        dcs = jnp.zeros((L, LANES), F32)
        dcs_t = jnp.zeros((LANES, L), F32)
        ddt = jnp.zeros((L, LANES), F32)
        dtot = jnp.zeros((1, LANES), F32)
        dcb = jnp.zeros((L, L), F32)
        dye_parts, xd_parts, et_parts = [], [], []
        for p in range(HPG // 2):
            ln0 = r * HPG + 2 * p
            sl = slice(p * LANES, (p + 1) * LANES)
            dtp, ep, dp = _pair(dt, ln0, lo), _pair(e, ln0, lo), _pair(d, ln0, lo)
            xs_p, dy_p = xs[:, sl], dy_all[:, sl]
            xp = xs_p * dtp
            dye = dy_p * ep
            w2 = xp * b_g[:, sl] * dp
            u = dye * c_s[:, sl] - w2
            lam0, lam1 = _decay(cs, cs_t, ln0, mask), _decay(cs, cs_t, ln0 + 1, mask)
            m0, m1 = cb * lam0, cb * lam1
            dybd = jnp.concatenate([jnp.where(lo, dy_p, 0.0), jnp.where(lo, 0.0, dy_p)], axis=0).astype(BF16)
            dm = _dot_nt(dybd, xp.astype(BF16))
            dm0, dm1 = dm[:L], dm[L:]
            dcb = dcb + dm0 * lam0 + dm1 * lam1
            dx = _dot_tn(jnp.concatenate([m0, m1], axis=0).astype(BF16), dybd) + b_g[:, sl] * dp
            dxx = dx * xs_p
            for ln, half, q in ((ln0, lo, dm0 * m0), (ln0 + 1, jnp.logical_not(lo), dm1 * m1)):
                r_u = jnp.sum(jnp.where(half, u, 0.0), axis=1, keepdims=True)
                r_q = jnp.sum(q, axis=1, keepdims=True)
                c_q = jnp.sum(q, axis=0, keepdims=True)
                r_x = jnp.sum(jnp.where(half, dxx, 0.0), axis=1, keepdims=True)
                r_w = jnp.sum(jnp.where(half, w2, 0.0), axis=1, keepdims=True)
                dcs = dcs + jnp.where(lane == ln, r_u + r_q, 0.0)
                dcs_t = dcs_t - jnp.where(sub == ln, c_q, 0.0)
                ddt = ddt + jnp.where(lane == ln, r_x, 0.0)
                dtot = dtot + jnp.where(lane1 == ln, jnp.sum(r_w, axis=0, keepdims=True), 0.0)
            dxs = dx * dtp
            if skip:
                dxs = dxs + dy_p * dvec_ref[:, sl] + pdxc_ref[:, sl].astype(F32)
                dd_ref[:, sl] += jnp.sum(dy_p * xs_p, axis=0, keepdims=True)
            dxc_ref[:, sl] = dxs.astype(BF16)
            dye_parts.append(dye.astype(BF16))
            xd_parts.append((xp * dp).astype(BF16))
            et_parts.append(_pair(et, ln0, lo))
        dye_all = jnp.concatenate(dye_parts, axis=1)
        xd = jnp.concatenate(xd_parts, axis=1)
        etx = jnp.concatenate(et_parts, axis=1)
        dcb_b = dcb.astype(BF16)
        d_b = _dot_nt(xd, g_b) + _dot_tn(dcb_b, Cm)
        d_c = _dot_nt(dye_all, s_in) + _dot(dcb_b, Bm)
        if skip:
            d_b = d_b + pdxc_ref[:, GW:GW + D_STATE].astype(F32)
            d_c = d_c + pdxc_ref[:, GW + D_STATE:].astype(F32)
        dxc_ref[:, GW:GW + D_STATE] = d_b.astype(BF16)
        dxc_ref[:, GW + D_STATE:] = d_c.astype(BF16)
        gs = jnp.sum(g_f * s_in.astype(F32), axis=0, keepdims=True) * etx
        for j in range(HPG):
            val = jnp.sum(gs[:, j * HEADDIM:(j + 1) * HEADDIM], axis=1, keepdims=True)
            dtot = dtot + jnp.where(lane1 == r * HPG + j, val, 0.0)
        g_ref[...] = g_f * etx + _dot_tn(Cm, dye_all)
        rowi = lax.broadcasted_iota(jnp.int32, (L, LANES), 0)
        dcs = dcs + dcs_t.T + jnp.where(rowi == (0 if rev else L - 1), dtot, 0.0)
        da = _cumsum_mm(mask_t, dcs)
        keep = jnp.logical_and(lane >= r * HPG, lane < (r + 1) * HPG)
        ddr = jnp.where(keep, (da * A + ddt) * _sigmoid(pre), 0.0)
        dbias_ref[...] += jnp.sum(ddr, axis=0, keepdims=True)
        dalog_ref[...] += jnp.sum(jnp.where(keep, da * dt * A, 0.0), axis=0, keepdims=True)
        if skip:
            ddr = ddr + pddt_ref[...]
        ddt_ref[...] = ddr

    s_xc, s_dt, s_vec, s_wide, s_wvec, s_st = _scan_specs(nc, not rev)
    in_specs = [s_xc, s_dt, s_vec, s_vec, s_wide, s_st]
    args = [xc, dt4, bias4, alog4, dy, states]
    out_specs = [s_xc, s_dt, s_vec, s_vec]
    out_shape = [jax.ShapeDtypeStruct((T, GROUPS * XCG), BF16), jax.ShapeDtypeStruct((T, GROUPS * LANES), F32),
                 jax.ShapeDtypeStruct((1, GROUPS * LANES), F32), jax.ShapeDtypeStruct((1, GROUPS * LANES), F32)]
    if skip:
        in_specs += [s_xc, s_dt, s_wvec]
        args += [prev[0], prev[1], dvec]
        out_specs.append(s_wvec)
        out_shape.append(jax.ShapeDtypeStruct((1, E), F32))
    return pl.pallas_call(
        body, name=name, grid=(GROUPS, nc), in_specs=in_specs, out_specs=tuple(out_specs),
        out_shape=tuple(out_shape), scratch_shapes=[pltpu.VMEM((D_STATE, GW), F32)], compiler_params=_params(2),
    )(*args)


def _conf_cols(w):
    e = w.shape[-1] // 3
    lead = w.shape[:-1]
    vg = w[..., :2 * e].reshape(*lead, 2, e // CONV_TC, CONV_TC)
    vg = jnp.swapaxes(vg, -3, -2).reshape(*lead, 2 * e)
    return jnp.concatenate([vg, w[..., 2 * e:]], axis=-1)


def _conf_cols_inv(w):
    e = w.shape[-1] // 3
    lead = w.shape[:-1]
    vg = w[..., :2 * e].reshape(*lead, e // CONV_TC, 2, CONV_TC)
    vg = jnp.swapaxes(vg, -3, -2).reshape(*lead, 2 * e)
    return jnp.concatenate([vg, w[..., 2 * e:]], axis=-1)


def _xbc_cols(w):
    lead = w.shape[:-1]
    e = GROUPS * GW
    gn = GROUPS * D_STATE
    parts = [w[..., :e].reshape(*lead, GROUPS, GW), w[..., e:e + gn].reshape(*lead, GROUPS, D_STATE),
             w[..., e + gn:].reshape(*lead, GROUPS, D_STATE)]
    return jnp.concatenate(parts, axis=-1).reshape(*lead, GROUPS * XCG)


def _xbc_cols_inv(w):
    lead = w.shape[:-1]
    g = w.reshape(*lead, GROUPS, XCG)
    parts = [g[..., :GW].reshape(*lead, GROUPS * GW), g[..., GW:GW + D_STATE].reshape(*lead, GROUPS * D_STATE),
             g[..., GW + D_STATE:].reshape(*lead, GROUPS * D_STATE)]
    return jnp.concatenate(parts, axis=-1)


def _dt_cols(w):
    lead = w.shape[:-1]
    t = jnp.swapaxes(w.reshape(*lead, 2, GROUPS, HPG), -3, -2).reshape(*lead, GROUPS, 2 * HPG)
    pad = [(0, 0)] * (t.ndim - 1) + [(0, LANES - 2 * HPG)]
    return jnp.pad(t, pad).reshape(*lead, GROUPS * LANES)


def _dt_cols_inv(w):
    lead = w.shape[:-1]
    t = w.reshape(*lead, GROUPS, LANES)[..., :2 * HPG].reshape(*lead, GROUPS, 2, HPG)
    return jnp.swapaxes(t, -3, -2).reshape(*lead, 2 * HEADS)


def _pad_rows(w, rows):
    return jnp.pad(w, ((0, rows - w.shape[0]), (0, 0)))


def conf_weights(w_in, dw_w, dw_b, ln_w, ln_b, w_out):
    w_in_p = _conf_cols(w_in)
    return dict(w_in=w_in_p, w_in_t=w_in_p.T, w_out=w_out, w_out_t=w_out.T,
                dw_w=_pad_rows(dw_w, 32), dw_b=dw_b.reshape(1, -1), ln_w=ln_w.reshape(1, -1), ln_b=ln_b.reshape(1, -1))


def ssd_weights(w_in, conv_w, conv_b, dt_bias, a_log, d_skip, norm_w, w_out):
    e = GROUPS * GW
    xbc = e + 2 * GROUPS * D_STATE
    w_zx = jnp.concatenate([w_in[:, :e], _xbc_cols(w_in[:, e:e + xbc])], axis=-1)
    w_dt = _dt_cols(w_in[:, e + xbc:])
    return dict(w_zx=w_zx, w_zx_t=w_zx.T, w_dt=w_dt, w_dt_t=w_dt.T, w_out=w_out, w_out_t=w_out.T,
                conv_w=_pad_rows(_xbc_cols(conv_w), 8), conv_b=_xbc_cols(conv_b.reshape(1, -1)),
                bias4=_dt_cols(dt_bias.reshape(1, -1)), alog4=_dt_cols(a_log.reshape(1, -1)),
                dvec=jnp.repeat(d_skip, HEADDIM).reshape(1, -1), norm_w=norm_w.reshape(1, -1))


def conf_layer_fwd(h, nw, p, tag):
    hn = rmsnorm_fwd(h, nw, name=f"{tag}_norm")
    proj = mm_nn(hn, p["w_in"], out_dtype=BF16, name=f"{tag}_proj")
    u2 = dwconv_fwd(proj, p["dw_w"], p["dw_b"], width=31, glu=True, silu=False, col0=0, name=f"{tag}_conv")
    u4 = conf_ln_fwd(u2, proj, p["ln_w"], p["ln_b"], name=f"{tag}_ln")
    h2 = mm_nn(u4, p["w_out"], out_dtype=F32, res=h, name=f"{tag}_out")
    return h2, (h, hn, proj, u2, u4)


def conf_layer_bwd(dh, saved, nw, p, tag):
    h, hn, proj, u2, u4 = saved
    du4 = mm_nn(dh, p["w_out_t"], out_dtype=BF16, name=f"{tag}_d_u4")
    dw_out = mm_tn(u4, dh, name=f"{tag}_dw_out")
    du2, dproj, dln_w, dln_b = conf_ln_bwd(du4, u2, proj, p["ln_w"], p["ln_b"], name=f"{tag}_d_ln")
    dproj, ddw_w, ddw_b = dwconv_bwd(du2, proj, p["dw_w"], p["dw_b"], dproj, width=31, glu=True, silu=False,
                                     col0=0, dcol0=0, name=f"{tag}_d_conv")
    dhn = mm_nn(dproj, p["w_in_t"], out_dtype=F32, name=f"{tag}_d_hn")
    dw_in = mm_tn(hn, dproj, name=f"{tag}_dw_in")
    dh_prev, dnw = rmsnorm_bwd(dhn, h, nw, dh, name=f"{tag}_d_norm")
    grads = dict(w_in=_conf_cols_inv(dw_in), dw_w=ddw_w[:31], dw_b=ddw_b[0], ln_w=dln_w[0], ln_b=dln_b[0],
                 w_out=dw_out, norm=dnw[0])
    return dh_prev, grads


def ssd_layer_fwd(h, nw, p, tag):
    e = GROUPS * GW
    hn = rmsnorm_fwd(h, nw, name=f"{tag}_norm")
    zx = mm_nn(hn, p["w_zx"], out_dtype=BF16, name=f"{tag}_proj")
    dt4 = mm_nn(hn, p["w_dt"], out_dtype=F32, name=f"{tag}_proj_dt")
    xc = dwconv_fwd(zx, p["conv_w"], p["conv_b"], width=5, glu=False, silu=True, col0=e // CONV_TC, name=f"{tag}_conv")
    y0, st0 = ssd_scan_fwd(xc, dt4, p["bias4"], p["alog4"], rev=False, name=f"{tag}_scan_f")
    y, st1 = ssd_scan_fwd(xc, dt4, p["bias4"], p["alog4"], rev=True, prev=y0, dvec=p["dvec"], name=f"{tag}_scan_b")
    yn = ssd_gate_fwd(y, zx, p["norm_w"], name=f"{tag}_gate")
    h2 = mm_nn(yn, p["w_out"], out_dtype=F32, res=h, name=f"{tag}_out")
    return h2, (h, hn, zx, dt4, xc, st0, st1, y, yn)


def ssd_layer_bwd(dh, saved, nw, p, tag):
    e = GROUPS * GW
    h, hn, zx, dt4, xc, st0, st1, y, yn = saved
    dyn = mm_nn(dh, p["w_out_t"], out_dtype=BF16, name=f"{tag}_d_yn")
    dw_out = mm_tn(yn, dh, name=f"{tag}_dw_out")
    dy, dzx, dnorm_w = ssd_gate_bwd(dyn, y, zx, p["norm_w"], name=f"{tag}_d_gate")
    dxc0, ddt0, dalog0, dbias0 = ssd_scan_bwd(xc, dt4, p["bias4"], p["alog4"], dy, st0, rev=False,
                                              name=f"{tag}_d_scan_f")
    dxc, ddt4, dalog1, dbias1, ddvec = ssd_scan_bwd(xc, dt4, p["bias4"], p["alog4"], dy, st1, rev=True,
                                                    prev=(dxc0, ddt0), dvec=p["dvec"], name=f"{tag}_d_scan_b")
    dzx, dconv_w, dconv_b = dwconv_bwd(dxc, zx, p["conv_w"], p["conv_b"], dzx, width=5, glu=False, silu=True,
                                       col0=e // CONV_TC, dcol0=e // CONV_TC, name=f"{tag}_d_conv")
    dhn = mm_nn(dzx, p["w_zx_t"], out_dtype=F32, a2=ddt4, b2=p["w_dt_t"], name=f"{tag}_d_hn")
    dw_zx = mm_tn(hn, dzx, name=f"{tag}_dw_zx")
    dw_dt = mm_tn(hn, ddt4, name=f"{tag}_dw_dt")
    dh_prev, dnw = rmsnorm_bwd(dhn, h, nw, dh, name=f"{tag}_d_norm")
    dw_in = jnp.concatenate([dw_zx[:, :e], _xbc_cols_inv(dw_zx[:, e:]), _dt_cols_inv(dw_dt)], axis=-1)
    grads = dict(w_in=dw_in, conv_w=_xbc_cols_inv(dconv_w[:5]), conv_b=_xbc_cols_inv(dconv_b)[0],
                 dt_bias=_dt_cols_inv(dbias0 + dbias1).reshape(2, HEADS),
                 a_log=_dt_cols_inv(dalog0 + dalog1).reshape(2, HEADS),
                 d_skip=jnp.sum(ddvec.reshape(HEADS, HEADDIM), axis=-1), norm_w=dnorm_w[0], w_out=dw_out,
                 norm=dnw[0])
    return dh_prev, grads


ANY = pl.BlockSpec(memory_space=pl.ANY)


def _place():
    return lax.axis_index("x"), lax.axis_index("y"), lax.axis_index("c")


def gather_chips(bufs, *, name):
    n = len(bufs)

    def body(*refs):
        ins, outs = refs[:n], refs[n:2 * n]
        send_sems, recv_sems, local_sems = refs[2 * n:]
        x, y, c = _place()
        k_me = 2 * x + y
        chips = [(1 - x, y), (x, 1 - y), (1 - x, 1 - y)]
        copies = []
        for t in range(n):
            own = pltpu.make_async_copy(ins[t], outs[t].at[k_me], local_sems.at[t])
            own.start()
            copies.append(own)
            for j, (px, py) in enumerate(chips):
                cp = pltpu.make_async_remote_copy(
                    src_ref=ins[t], dst_ref=outs[t].at[k_me], send_sem=send_sems.at[3 * t + j],
                    recv_sem=recv_sems.at[3 * t + j], device_id=(px, py, c), device_id_type=MESH)
                cp.start()
                copies.append(cp)
        for cp in copies:
            cp.wait()

    return pl.pallas_call(
        body, name=name, in_specs=[ANY] * n, out_specs=tuple([ANY] * n),
        out_shape=tuple(jax.ShapeDtypeStruct((N_CHIPS,) + b.shape, b.dtype) for b in bufs),
        scratch_shapes=[pltpu.SemaphoreType.DMA((3 * n,)), pltpu.SemaphoreType.DMA((3 * n,)),
                        pltpu.SemaphoreType.DMA((n,))],
    )(*bufs)


def swap_other_half(g2, *, name):
    def body(g_ref, o_ref, send_sem, recv_sem):
        x, y, c = _place()
        cp = pltpu.make_async_remote_copy(src_ref=g_ref.at[1 - c], dst_ref=o_ref, send_sem=send_sem, recv_sem=recv_sem,
                                          device_id=(x, y, 1 - c), device_id_type=MESH)
        cp.start()
        cp.wait()

    return pl.pallas_call(
        body, name=name, in_specs=[ANY], out_specs=ANY, out_shape=jax.ShapeDtypeStruct(g2.shape[1:], g2.dtype),
        scratch_shapes=[pltpu.SemaphoreType.DMA, pltpu.SemaphoreType.DMA],
    )(g2)


def exchange_chips(p, *, name):
    def body(p_ref, o_ref, send_sems, recv_sems, local_sem):
        x, y, c = _place()
        k_me = 2 * x + y
        own = pltpu.make_async_copy(p_ref.at[k_me], o_ref.at[k_me], local_sem)
        own.start()
        copies = [own]
        for j, (px, py) in enumerate([(1 - x, y), (x, 1 - y), (1 - x, 1 - y)]):
            cp = pltpu.make_async_remote_copy(
                src_ref=p_ref.at[2 * px + py], dst_ref=o_ref.at[k_me], send_sem=send_sems.at[j],
                recv_sem=recv_sems.at[j], device_id=(px, py, c), device_id_type=MESH)
            cp.start()
            copies.append(cp)
        for cp in copies:
            cp.wait()

    return pl.pallas_call(
        body, name=name, in_specs=[ANY], out_specs=ANY, out_shape=jax.ShapeDtypeStruct(p.shape, p.dtype),
        scratch_shapes=[pltpu.SemaphoreType.DMA((3,)), pltpu.SemaphoreType.DMA((3,)), pltpu.SemaphoreType.DMA],
    )(p)


def share_half(full, *, name):
    def body(_, f_ref, send_sem, recv_sem):
        x, y, c = _place()
        cp = pltpu.make_async_remote_copy(src_ref=f_ref.at[c], dst_ref=f_ref.at[c], send_sem=send_sem,
                                          recv_sem=recv_sem, device_id=(x, y, 1 - c), device_id_type=MESH)
        cp.start()
        cp.wait()

    return pl.pallas_call(
        body, name=name, in_specs=[ANY], out_specs=ANY, out_shape=jax.ShapeDtypeStruct(full.shape, full.dtype),
        input_output_aliases={0: 0},
        scratch_shapes=[pltpu.SemaphoreType.DMA, pltpu.SemaphoreType.DMA],
    )(full)


def gather_all(v, *, name):
    def body(v_ref, o_ref, send_sems, recv_sems, local_sem):
        x, y, c = _place()
        me = 4 * x + 2 * y + c
        own = pltpu.make_async_copy(v_ref, o_ref.at[me], local_sem)
        own.start()
        copies = [own]
        idx = 0
        for fx in (0, 1):
            for fy in (0, 1):
                for fc in (0, 1):
                    if not (fx or fy or fc):
                        continue
                    peer = (1 - x if fx else x, 1 - y if fy else y, 1 - c if fc else c)
                    cp = pltpu.make_async_remote_copy(src_ref=v_ref, dst_ref=o_ref.at[me], send_sem=send_sems.at[idx],
                                                      recv_sem=recv_sems.at[idx], device_id=peer, device_id_type=MESH)
                    cp.start()
                    copies.append(cp)
                    idx += 1
        for cp in copies:
            cp.wait()

    return pl.pallas_call(
        body, name=name, in_specs=[ANY], out_specs=ANY, out_shape=jax.ShapeDtypeStruct((N_DEV,) + v.shape, v.dtype),
        scratch_shapes=[pltpu.SemaphoreType.DMA((N_DEV - 1,)), pltpu.SemaphoreType.DMA((N_DEV - 1,)),
                        pltpu.SemaphoreType.DMA],
    )(v)


RED_TR = 432


def pair_sum(g2, recv, cidx, *, name):
    _, K, R, C = g2.shape
    tr = _pick(R, (RED_TR, 8))

    def body(c_ref, a_ref, b_ref, o_ref):
        o_ref[...] = (a_ref[0] + b_ref[...]).astype(BF16)

    blk = pl.BlockSpec((1, tr, C), lambda k, i, c: (k, i, 0))
    return pl.pallas_call(
        body, name=name,
        grid_spec=pltpu.PrefetchScalarGridSpec(
            num_scalar_prefetch=1, grid=(K, R // tr),
            in_specs=[pl.BlockSpec((1, 1, tr, C), lambda k, i, c: (c[0], k, i, 0)), blk], out_specs=blk),
        out_shape=jax.ShapeDtypeStruct((K, R, C), BF16), compiler_params=_params(2),
    )(cidx, g2, recv)


def sum_lead(a, *, name, slot=None, nslots=1):
    K, R, C = a.shape
    tr = _pick(R, (RED_TR, 8))

    def body(s_ref, a_ref, o_ref):
        acc = a_ref[0].astype(F32)
        for k in range(1, K):
            acc = acc + a_ref[k].astype(F32)
        o_ref[0] = acc

    if slot is None:
        slot = jnp.zeros((1,), jnp.int32)
    return pl.pallas_call(
        body, name=name,
        grid_spec=pltpu.PrefetchScalarGridSpec(
            num_scalar_prefetch=1, grid=(R // tr,),
            in_specs=[pl.BlockSpec((K, tr, C), lambda i, s: (0, i, 0))],
            out_specs=pl.BlockSpec((1, tr, C), lambda i, s: (s[0], i, 0))),
        out_shape=jax.ShapeDtypeStruct((nslots, R, C), F32), compiler_params=_params(1),
    )(slot, a)


def adamw(g, w, m, v, *, name):
    R, C = w.shape
    tr = _pick(R, (256, 128, 64, 32, 16, 8))

    def body(g_ref, w_ref, m_ref, v_ref, d_ref, nm_ref, nv_ref):
        gv = g_ref[...]
        m_new = ADAM_B1 * m_ref[...] + (1.0 - ADAM_B1) * gv
        v_new = ADAM_B2 * v_ref[...] + (1.0 - ADAM_B2) * (gv * gv)
        m_hat = m_new / (1.0 - ADAM_B1 ** ADAM_STEP)
        v_hat = v_new / (1.0 - ADAM_B2 ** ADAM_STEP)
        d_ref[...] = -ADAM_LR * (m_hat / (jnp.sqrt(v_hat) + ADAM_EPS) + ADAM_WD * w_ref[...])
        nm_ref[...] = m_new
        nv_ref[...] = v_new

    blk = pl.BlockSpec((tr, C), lambda i: (i, 0))
    sds = jax.ShapeDtypeStruct((R, C), F32)
    return pl.pallas_call(
        body, name=name, grid=(R // tr,), in_specs=[blk] * 4, out_specs=(blk,) * 3, out_shape=(sds,) * 3,
        compiler_params=_params(1),
    )(g, w, m, v)


WEIGHTS = ("norm_w", "final_norm_w", "cm_w_in", "cm_dw_w", "cm_dw_b", "cm_ln_w", "cm_ln_b", "cm_w_out", "ssd_w_in",
           "ssd_conv_w", "ssd_conv_b", "ssd_dt_bias", "ssd_A_log", "ssd_D", "ssd_norm_w", "ssd_w_out")
BIG = (("cm_w_in", 2), ("cm_w_out", 1), ("ssd_w_in", 2), ("ssd_w_out", 1))
SMALL_SHARDED = (("cm_dw_w", 2), ("ssd_conv_w", 2), ("ssd_conv_b", 1), ("ssd_norm_w", 1))
REPLICATED = ("norm_w", "final_norm_w", "cm_dw_b", "cm_ln_w", "cm_ln_b", "ssd_dt_bias", "ssd_A_log", "ssd_D")
ROW = 1024


def _to_shards(g, axis):
    n = g.shape[axis]
    s = g.reshape(g.shape[:axis] + (N_CHIPS, n // N_CHIPS) + g.shape[axis + 1:])
    return jnp.moveaxis(s, axis, 0).reshape(N_CHIPS, -1)


def _from_shards(x4, local_shape, axis):
    local_shape = tuple(local_shape)
    s = jnp.moveaxis(x4.reshape((N_CHIPS,) + local_shape), 0, axis)
    return s.reshape(local_shape[:axis] + (N_CHIPS * local_shape[axis],) + local_shape[axis + 1:])


def _flat_pad(parts, multiple):
    v = jnp.concatenate([p.reshape(-1) for p in parts])
    return jnp.pad(v, (0, (-v.size) % multiple))


def _split(flat, like, names):
    out, off = {}, 0
    for n in names:
        out[n] = flat[off:off + like[n].size].reshape(like[n].shape)
        off += like[n].size
    return out


def kernel(x, norm_w, final_norm_w, cm_w_in, cm_dw_w, cm_dw_b, cm_ln_w, cm_ln_b, cm_w_out, ssd_w_in, ssd_conv_w, ssd_conv_b, ssd_dt_bias, ssd_A_log, ssd_D, ssd_norm_w, ssd_w_out, loss_target, m_norm_w, m_final_norm_w, m_cm_w_in, m_cm_dw_w, m_cm_dw_b, m_cm_ln_w, m_cm_ln_b, m_cm_w_out, m_ssd_w_in, m_ssd_conv_w, m_ssd_conv_b, m_ssd_dt_bias, m_ssd_A_log, m_ssd_D, m_ssd_norm_w, m_ssd_w_out, v_norm_w, v_final_norm_w, v_cm_w_in, v_cm_dw_w, v_cm_dw_b, v_cm_ln_w, v_cm_ln_b, v_cm_w_out, v_ssd_w_in, v_ssd_conv_w, v_ssd_conv_b, v_ssd_dt_bias, v_ssd_A_log, v_ssd_D, v_ssd_norm_w, v_ssd_w_out):
    a = dict(locals())
    w = {n: a[n] for n in WEIGHTS}
    m = {n: a["m_" + n] for n in WEIGHTS}
    v = {n: a["v_" + n] for n in WEIGHTS}
    _, T, D = x.shape
    cidx = lax.axis_index("c").astype(jnp.int32).reshape(1)
    big_names = [n for n, _ in BIG]
    small_names = [n for n, _ in SMALL_SHARDED]

    big = _flat_pad([w[n] for n in big_names], 16 * ROW).astype(BF16).reshape(-1, ROW)
    small = _flat_pad([w[n] for n in small_names], 8 * ROW).reshape(-1, ROW)
    g_big, g_small = gather_chips([big, small], name="gather_weights")
    g_big, g_small = g_big.reshape(N_CHIPS, -1), g_small.reshape(N_CHIPS, -1)
    full, off = {}, 0
    for n, ax in BIG:
        full[n] = _from_shards(g_big[:, off:off + w[n].size], w[n].shape, ax)
        off += w[n].size
    off = 0
    for n, ax in SMALL_SHARDED:
        full[n] = _from_shards(g_small[:, off:off + w[n].size], w[n].shape, ax)
        off += w[n].size
    n_layers = norm_w.shape[0]
    lw = []
    for i in range(n_layers):
        j = i // 2
        if i % 2 == 0:
            lw.append(conf_weights(full["cm_w_in"][j], full["cm_dw_w"][j], cm_dw_b[j], cm_ln_w[j], cm_ln_b[j],
                                   full["cm_w_out"][j]))
        else:
            lw.append(ssd_weights(full["ssd_w_in"][j], full["ssd_conv_w"][j], full["ssd_conv_b"][j], ssd_dt_bias[j],
                                  ssd_A_log[j], ssd_D[j], full["ssd_norm_w"][j], full["ssd_w_out"][j]))

    h = x[0]
    saved = []
    for i in range(n_layers):
        fwd = conf_layer_fwd if i % 2 == 0 else ssd_layer_fwd
        h, s = fwd(h, norm_w[i].reshape(1, -1), lw[i], f"l{i}")
        saved.append(s)
    dh, loss_local, d_final = loss_head(h, loss_target[0], final_norm_w.reshape(1, -1), name="loss_head")
    lg = [None] * n_layers
    for i in reversed(range(n_layers)):
        bwd = conf_layer_bwd if i % 2 == 0 else ssd_layer_bwd
        dh, lg[i] = bwd(dh, saved[i], norm_w[i].reshape(1, -1), lw[i], f"l{i}")
    conf_g, ssd_g = lg[0::2], lg[1::2]
    local = {
        "norm_w": jnp.stack([g["norm"] for g in lg]), "final_norm_w": d_final[0],
        "cm_w_in": jnp.stack([g["w_in"] for g in conf_g]), "cm_dw_w": jnp.stack([g["dw_w"] for g in conf_g]),
        "cm_dw_b": jnp.stack([g["dw_b"] for g in conf_g]), "cm_ln_w": jnp.stack([g["ln_w"] for g in conf_g]),
        "cm_ln_b": jnp.stack([g["ln_b"] for g in conf_g]), "cm_w_out": jnp.stack([g["w_out"] for g in conf_g]),
        "ssd_w_in": jnp.stack([g["w_in"] for g in ssd_g]), "ssd_conv_w": jnp.stack([g["conv_w"] for g in ssd_g]),
        "ssd_conv_b": jnp.stack([g["conv_b"] for g in ssd_g]), "ssd_dt_bias": jnp.stack([g["dt_bias"] for g in ssd_g]),
        "ssd_A_log": jnp.stack([g["a_log"] for g in ssd_g]), "ssd_D": jnp.stack([g["d_skip"] for g in ssd_g]),
        "ssd_norm_w": jnp.stack([g["norm_w"] for g in ssd_g]), "ssd_w_out": jnp.stack([g["w_out"] for g in ssd_g]),
    }

    flat4 = jnp.concatenate([_to_shards(local[n], ax) for n, ax in BIG + SMALL_SHARDED], axis=1)
    flat4 = jnp.pad(flat4, ((0, 0), (0, (-flat4.shape[1]) % (2 * RED_TR * ROW))))
    g2 = jnp.swapaxes(flat4.reshape(N_CHIPS, 2, -1, ROW), 0, 1)
    theirs = swap_other_half(g2, name="grad_pair_swap")
    part = pair_sum(g2, theirs, cidx, name="grad_pair_sum")
    got = exchange_chips(part, name="grad_chip_exchange")
    half = sum_lead(got, slot=cidx, nslots=2, name="grad_chip_sum")
    shard_flat = share_half(half, name="grad_pair_share").reshape(-1)
    grads = _split(shard_flat, w, big_names + small_names)

    rep = _flat_pad([local[n] for n in REPLICATED], 8 * LANES).reshape(-1, LANES)
    rep_sum = sum_lead(gather_all(rep, name="grad_small_gather"), name="grad_small_sum")
    grads.update(_split(rep_sum.reshape(-1), w, REPLICATED))

    delta, new_m, new_v = {}, {}, {}
    for n in big_names:
        two_d = (-1, w[n].shape[-1])
        d_, m_, v_ = adamw(grads[n].reshape(two_d), w[n].reshape(two_d), m[n].reshape(two_d), v[n].reshape(two_d),
                           name="adamw_" + n)
        delta[n], new_m[n], new_v[n] = d_.reshape(w[n].shape), m_.reshape(w[n].shape), v_.reshape(w[n].shape)
    rest = list(REPLICATED) + small_names
    packed = [_flat_pad([t[n] for n in rest], 8 * LANES).reshape(-1, LANES) for t in (grads, w, m, v)]
    for out, res in zip((delta, new_m, new_v), adamw(*packed, name="adamw_small")):
        out.update(_split(res.reshape(-1), w, rest))

    loss = lax.psum(loss_local[0, 0], ("x", "y", "c"))
    return (loss, dh.reshape(x.shape), *[grads[n] for n in WEIGHTS], *[delta[n] for n in WEIGHTS],
            *[new_m[n] for n in WEIGHTS], *[new_v[n] for n in WEIGHTS])
```

```python
import jax
import jax.numpy as jnp
from jax import lax
from jax.experimental import pallas as pl
from jax.experimental.pallas import tpu as pltpu

F32 = jnp.float32
BF16 = jnp.bfloat16
MESH = pl.DeviceIdType.MESH

EPS = 1e-5
HEADDIM = 64
HEADS = 32
GROUPS = 4
HPG = HEADS // GROUPS
D_STATE = 128
CHUNK = 128
GW = HPG * HEADDIM
XCG = GW + 2 * D_STATE
HALO = 16
LANES = 128
N_CHIPS = 4
N_DEV = 8

ADAM_LR = 0.001
ADAM_B1 = 0.9
ADAM_B2 = 0.999
ADAM_EPS = 1e-08
ADAM_WD = 0.01
ADAM_STEP = 10

VMEM_LIMIT = 52 * 1024 * 1024


def _params(n_axes):
    return pltpu.CompilerParams(dimension_semantics=("arbitrary",) * n_axes, vmem_limit_bytes=VMEM_LIMIT)


def _sigmoid(x):
    return 1.0 / (1.0 + jnp.exp(-x))


def _softplus(x):
    return jnp.maximum(x, 0.0) + jnp.log(1.0 + jnp.exp(-jnp.abs(x)))


def _dot(a, b):
    return jnp.dot(a, b, preferred_element_type=F32)


def _dot_nt(a, b):
    return lax.dot_general(a, b, (((1,), (1,)), ((), ())), preferred_element_type=F32)


def _dot_tn(a, b):
    return lax.dot_general(a, b, (((0,), (0,)), ((), ())), preferred_element_type=F32)


def _pick(n, pref):
    for t in pref:
        if n % t == 0:
            return t
    return n


def mm_nn(a, b, *, out_dtype, name, res=None, a2=None, b2=None, b_rows_are_n=False):
    M, K = a.shape
    N = b.shape[0] if b_rows_are_n else b.shape[1]
    tm = _pick(M, (1024, 512, 256, 128))
    tn = _pick(N, (1024, 512, 256, 128))
    tk = _pick(K, (1024, 512, 256, 128))
    nk = K // tk
    has2, has_res = a2 is not None, res is not None

    def body(*refs):
        a_ref, b_ref = refs[0], refs[1]
        pos = 2
        if has2:
            a2_ref, b2_ref = refs[pos], refs[pos + 1]
            pos += 2
        if has_res:
            r_ref = refs[pos]
            pos += 1
        o_ref, acc_ref = refs[pos], refs[pos + 1]
        k = pl.program_id(2)

        @pl.when(k == 0)
        def _():
            if has2:
                acc_ref[...] = _dot(a2_ref[...].astype(BF16), b2_ref[...])
            else:
                acc_ref[...] = jnp.zeros_like(acc_ref)

        acc_ref[...] += (_dot_nt if b_rows_are_n else _dot)(a_ref[...].astype(BF16), b_ref[...])

        @pl.when(k == nk - 1)
        def _():
            r = acc_ref[...]
            if has_res:
                r = r + r_ref[...]
            o_ref[...] = r.astype(out_dtype)

    b_spec = pl.BlockSpec((tn, tk), lambda i, j, k: (j, k)) if b_rows_are_n else pl.BlockSpec((tk, tn), lambda i, j, k: (k, j))
    in_specs = [pl.BlockSpec((tm, tk), lambda i, j, k: (i, k)), b_spec]
    args = [a, b]
    if has2:
        k2 = a2.shape[1]
        in_specs += [pl.BlockSpec((tm, k2), lambda i, j, k: (i, 0)), pl.BlockSpec((k2, tn), lambda i, j, k: (0, j))]
        args += [a2, b2]
    if has_res:
        in_specs.append(pl.BlockSpec((tm, tn), lambda i, j, k: (i, j)))
        args.append(res)
    return pl.pallas_call(
        body, name=name, grid=(M // tm, N // tn, nk), in_specs=in_specs,
        out_specs=pl.BlockSpec((tm, tn), lambda i, j, k: (i, j)),
        out_shape=jax.ShapeDtypeStruct((M, N), out_dtype),
        scratch_shapes=[pltpu.VMEM((tm, tn), F32)], compiler_params=_params(3),
    )(*args)


def mm_tn(a, b, *, name):
    T, M = a.shape
    N = b.shape[1]
    tm = _pick(M, (1024, 512, 256, 128))
    tn = _pick(N, (1024, 512, 256, 128))
    tt = _pick(T, (1024, 512, 256, 128))

    def body(a_ref, b_ref, o_ref):
        @pl.when(pl.program_id(2) == 0)
        def _():
            o_ref[...] = jnp.zeros_like(o_ref)

        o_ref[...] += _dot_tn(a_ref[...].astype(BF16), b_ref[...].astype(BF16))

    return pl.pallas_call(
        body, name=name, grid=(M // tm, N // tn, T // tt),
        in_specs=[pl.BlockSpec((tt, tm), lambda i, j, t: (t, i)), pl.BlockSpec((tt, tn), lambda i, j, t: (t, j))],
        out_specs=pl.BlockSpec((tm, tn), lambda i, j, t: (i, j)),
        out_shape=jax.ShapeDtypeStruct((M, N), F32), compiler_params=_params(3),
    )(a, b)


def rmsnorm_fwd(h, w, *, name):
    T, D = h.shape
    tm = _pick(T, (512, 256, 128))

    def body(h_ref, w_ref, o_ref):
        x = h_ref[...]
        rstd = lax.rsqrt(jnp.mean(x * x, axis=-1, keepdims=True) + EPS)
        o_ref[...] = (x * rstd * w_ref[...]).astype(BF16)

    return pl.pallas_call(
        body, name=name, grid=(T // tm,),
        in_specs=[pl.BlockSpec((tm, D), lambda i: (i, 0)), pl.BlockSpec((1, D), lambda i: (0, 0))],
        out_specs=pl.BlockSpec((tm, D), lambda i: (i, 0)),
        out_shape=jax.ShapeDtypeStruct((T, D), BF16), compiler_params=_params(1),
    )(h, w)


def rmsnorm_bwd(dhn, h, w, dh, *, name):
    T, D = h.shape
    tm = _pick(T, (512, 256, 128))

    def body(dhn_ref, h_ref, w_ref, dh_ref, o_ref, dw_ref):
        @pl.when(pl.program_id(0) == 0)
        def _():
            dw_ref[...] = jnp.zeros_like(dw_ref)

        x = h_ref[...]
        g = dhn_ref[...]
        rstd = lax.rsqrt(jnp.mean(x * x, axis=-1, keepdims=True) + EPS)
        xhat = x * rstd
        dxh = g * w_ref[...]
        o_ref[...] = dh_ref[...] + rstd * (dxh - xhat * jnp.mean(dxh * xhat, axis=-1, keepdims=True))
        dw_ref[...] += jnp.sum(g * xhat, axis=0, keepdims=True)

    row = pl.BlockSpec((tm, D), lambda i: (i, 0))
    vec = pl.BlockSpec((1, D), lambda i: (0, 0))
    return pl.pallas_call(
        body, name=name, grid=(T // tm,), in_specs=[row, row, vec, row], out_specs=(row, vec),
        out_shape=(jax.ShapeDtypeStruct((T, D), F32), jax.ShapeDtypeStruct((1, D), F32)),
        compiler_params=_params(1),
    )(dhn, h, w, dh)


def loss_head(h, target, w, *, name):
    T, D = h.shape
    tm = _pick(T, (512, 256, 128))

    def body(h_ref, t_ref, w_ref, dh_ref, loss_ref, dw_ref):
        @pl.when(pl.program_id(0) == 0)
        def _():
            loss_ref[...] = jnp.zeros_like(loss_ref)
            dw_ref[...] = jnp.zeros_like(dw_ref)

        x = h_ref[...]
        rstd = lax.rsqrt(jnp.mean(x * x, axis=-1, keepdims=True) + EPS)
        xhat = x * rstd
        err = xhat * w_ref[...] - t_ref[...]
        rows = jnp.sum(err * err, axis=-1, keepdims=True)
        loss_ref[...] += (0.5 / D) * jnp.sum(rows, axis=0, keepdims=True)
        dy = err * (1.0 / D)
        dxh = dy * w_ref[...]
        dh_ref[...] = rstd * (dxh - xhat * jnp.mean(dxh * xhat, axis=-1, keepdims=True))
        dw_ref[...] += jnp.sum(dy * xhat, axis=0, keepdims=True)

    row = pl.BlockSpec((tm, D), lambda i: (i, 0))
    vec = pl.BlockSpec((1, D), lambda i: (0, 0))
    return pl.pallas_call(
        body, name=name, grid=(T // tm,), in_specs=[row, row, vec],
        out_specs=(row, pl.BlockSpec((1, 1), lambda i: (0, 0)), vec),
        out_shape=(jax.ShapeDtypeStruct((T, D), F32), jax.ShapeDtypeStruct((1, 1), F32),
                   jax.ShapeDtypeStruct((1, D), F32)),
        compiler_params=_params(1),
    )(h, target, w)


CONV_TM = 256
CONV_TC = 512
CONV_RB = 32


def _conv_specs(T, tm, sw, col0):
    hb = tm // HALO
    last = T // HALO - 1
    main = pl.BlockSpec((tm, sw), lambda j, i: (i, col0 + j))
    prev = pl.BlockSpec((HALO, sw), lambda j, i: (jnp.maximum(i * hb - 1, 0), col0 + j))
    nxt = pl.BlockSpec((HALO, sw), lambda j, i: (jnp.minimum((i + 1) * hb, last), col0 + j))
    return main, prev, nxt


def _conv_input(blk, glu, tc):
    x = blk.astype(F32)
    if glu:
        return x[:, :tc] * _sigmoid(x[:, tc:])
    return x


def _fill_padded(pad_ref, main, prev, nxt, first, last, tm):
    pad_ref[0:HALO, :] = jnp.where(first, 0.0, prev)
    pad_ref[HALO:HALO + tm, :] = main
    pad_ref[HALO + tm:HALO + tm + HALO, :] = jnp.where(last, 0.0, nxt)


def dwconv_fwd(src, w, b, *, width, glu, silu, col0, name):
    T = src.shape[0]
    C = w.shape[1]
    tm, tc = min(CONV_TM, T), CONV_TC
    sw = 2 * tc if glu else tc
    n_i = T // tm
    p = (width - 1) // 2
    rb = CONV_RB

    def body(m_ref, p_ref, n_ref, w_ref, b_ref, o_ref, pad_ref):
        i = pl.program_id(1)
        _fill_padded(pad_ref, _conv_input(m_ref[...], glu, tc), _conv_input(p_ref[...], glu, tc),
                     _conv_input(n_ref[...], glu, tc), i == 0, i == n_i - 1, tm)
        for r in range(tm // rb):
            acc = jnp.zeros((rb, tc), F32)
            for k in range(width):
                off = HALO - p + k + r * rb
                acc = acc + pad_ref[off:off + rb, :] * w_ref[k:k + 1, :]
            acc = acc + b_ref[...]
            if silu:
                acc = acc * _sigmoid(acc)
            o_ref[r * rb:(r + 1) * rb, :] = acc.astype(BF16)

    main, prev, nxt = _conv_specs(T, tm, sw, col0)
    return pl.pallas_call(
        body, name=name, grid=(C // tc, n_i),
        in_specs=[main, prev, nxt, pl.BlockSpec((w.shape[0], tc), lambda j, i: (0, j)),
                  pl.BlockSpec((1, tc), lambda j, i: (0, j))],
        out_specs=pl.BlockSpec((tm, tc), lambda j, i: (i, j)),
        out_shape=jax.ShapeDtypeStruct((T, C), BF16),
        scratch_shapes=[pltpu.VMEM((tm + 2 * HALO, tc), F32)], compiler_params=_params(2),
    )(src, src, src, w, b)


def dwconv_bwd(dout, src, w, b, dsrc, *, width, glu, silu, col0, dcol0, name):
    T = src.shape[0]
    C = w.shape[1]
    kp = w.shape[0]
    tm, tc = min(CONV_TM, T), CONV_TC
    sw = 2 * tc if glu else tc
    n_i = T // tm
    p = (width - 1) // 2
    rb = CONV_RB
    edge = 8
    assert p <= edge or not silu

    def body(dm_ref, dp_ref, dn_ref, m_ref, p_ref, n_ref, w_ref, b_ref, _, o_ref, dw_ref, db_ref, pad_ref, dpre_ref):
        i = pl.program_id(1)

        @pl.when(i == 0)
        def _():
            dw_ref[...] = jnp.zeros_like(dw_ref)
            db_ref[...] = jnp.zeros_like(db_ref)

        first, last = i == 0, i == n_i - 1
        _fill_padded(pad_ref, _conv_input(m_ref[...], glu, tc), _conv_input(p_ref[...], glu, tc),
                     _conv_input(n_ref[...], glu, tc), first, last, tm)
        _fill_padded(dpre_ref, dm_ref[...].astype(F32), dp_ref[...].astype(F32), dn_ref[...].astype(F32),
                     first, last, tm)
        if silu:
            for r0 in range(HALO - edge, HALO + tm + edge, HALO):
                pre = jnp.zeros((HALO, tc), F32)
                for k in range(width):
                    pre = pre + pad_ref[r0 - p + k:r0 - p + k + HALO, :] * w_ref[k:k + 1, :]
                pre = pre + b_ref[...]
                s = _sigmoid(pre)
                dpre_ref[r0:r0 + HALO, :] = dpre_ref[r0:r0 + HALO, :] * (s * (1.0 + pre * (1.0 - s)))

        for r in range(tm // rb):
            acc = jnp.zeros((rb, tc), F32)
            for k in range(width):
                off = HALO + p - k + r * rb
                acc = acc + dpre_ref[off:off + rb, :] * w_ref[k:k + 1, :]
            if glu:
                blk = m_ref[r * rb:(r + 1) * rb, :].astype(F32)
                v, s = blk[:, :tc], _sigmoid(blk[:, tc:])
                o_ref[r * rb:(r + 1) * rb, :tc] = (acc * s).astype(BF16)
                o_ref[r * rb:(r + 1) * rb, tc:] = (acc * v * s * (1.0 - s)).astype(BF16)
            else:
                o_ref[r * rb:(r + 1) * rb, :] = acc.astype(BF16)

        dmain = dpre_ref[HALO:HALO + tm, :]
        for k in range(width):
            off = HALO - p + k
            dw_ref[k:k + 1, :] += jnp.sum(dmain * pad_ref[off:off + tm, :], axis=0, keepdims=True)
        db_ref[...] += jnp.sum(dmain, axis=0, keepdims=True)

    dmain_s, dprev_s, dnext_s = _conv_specs(T, tm, tc, 0)
    main, prev, nxt = _conv_specs(T, tm, sw, col0)
    wspec = pl.BlockSpec((kp, tc), lambda j, i: (0, j))
    bspec = pl.BlockSpec((1, tc), lambda j, i: (0, j))
    return pl.pallas_call(
        body, name=name, grid=(C // tc, n_i),
        in_specs=[dmain_s, dprev_s, dnext_s, main, prev, nxt, wspec, bspec, pl.BlockSpec(memory_space=pl.ANY)],
        out_specs=(pl.BlockSpec((tm, sw), lambda j, i: (i, dcol0 + j)), wspec, bspec),
        out_shape=(jax.ShapeDtypeStruct(dsrc.shape, dsrc.dtype), jax.ShapeDtypeStruct((kp, C), F32),
                   jax.ShapeDtypeStruct((1, C), F32)),
        input_output_aliases={8: 0},
        scratch_shapes=[pltpu.VMEM((tm + 2 * HALO, tc), F32), pltpu.VMEM((tm + 2 * HALO, tc), F32)],
        compiler_params=_params(2),
    )(dout, dout, dout, src, src, src, w, b, dsrc)


def _silu_grad(x, s):
    return s * (1.0 + x * (1.0 - s))


def conf_ln_fwd(u2, proj, ln_w, ln_b, *, name):
    T, E = u2.shape
    zc = proj.shape[1] // E - 1
    tm = _pick(T, (256, 128))

    def body(u_ref, z_ref, w_ref, b_ref, o_ref):
        x = u_ref[...].astype(F32)
        xc = x - jnp.mean(x, axis=-1, keepdims=True)
        rstd = lax.rsqrt(jnp.mean(xc * xc, axis=-1, keepdims=True) + EPS)
        u3 = xc * rstd * w_ref[...] + b_ref[...]
        z = z_ref[...].astype(F32)
        o_ref[...] = (u3 * _sigmoid(u3) * z * _sigmoid(z)).astype(BF16)

    row = pl.BlockSpec((tm, E), lambda i: (i, 0))
    vec = pl.BlockSpec((1, E), lambda i: (0, 0))
    return pl.pallas_call(
        body, name=name, grid=(T // tm,),
        in_specs=[row, pl.BlockSpec((tm, E), lambda i: (i, zc)), vec, vec], out_specs=row,
        out_shape=jax.ShapeDtypeStruct((T, E), BF16), compiler_params=_params(1),
    )(u2, proj, ln_w, ln_b)


def conf_ln_bwd(du4, u2, proj, ln_w, ln_b, *, name):
    T, E = u2.shape
    ncol = proj.shape[1] // E
    zc = ncol - 1
    tm = _pick(T, (256, 128))

    def body(d_ref, u_ref, z_ref, w_ref, b_ref, du_ref, dz_ref, dw_ref, db_ref):
        @pl.when(pl.program_id(0) == 0)
        def _():
            dw_ref[...] = jnp.zeros_like(dw_ref)
            db_ref[...] = jnp.zeros_like(db_ref)

        x = u_ref[...].astype(F32)
        xc = x - jnp.mean(x, axis=-1, keepdims=True)
        rstd = lax.rsqrt(jnp.mean(xc * xc, axis=-1, keepdims=True) + EPS)
        xhat = xc * rstd
        u3 = xhat * w_ref[...] + b_ref[...]
        z = z_ref[...].astype(F32)
        s3, sz = _sigmoid(u3), _sigmoid(z)
        d4 = d_ref[...].astype(F32)
        du3 = d4 * (z * sz) * _silu_grad(u3, s3)
        dz_ref[...] = (d4 * (u3 * s3) * _silu_grad(z, sz)).astype(BF16)
        dw_ref[...] += jnp.sum(du3 * xhat, axis=0, keepdims=True)
        db_ref[...] += jnp.sum(du3, axis=0, keepdims=True)
        dxh = du3 * w_ref[...]
        du = rstd * (dxh - jnp.mean(dxh, axis=-1, keepdims=True) - xhat * jnp.mean(dxh * xhat, axis=-1, keepdims=True))
        du_ref[...] = du.astype(BF16)

    row = pl.BlockSpec((tm, E), lambda i: (i, 0))
    zrow = pl.BlockSpec((tm, E), lambda i: (i, zc))
    vec = pl.BlockSpec((1, E), lambda i: (0, 0))
    return pl.pallas_call(
        body, name=name, grid=(T // tm,), in_specs=[row, row, zrow, vec, vec], out_specs=(row, zrow, vec, vec),
        out_shape=(jax.ShapeDtypeStruct((T, E), BF16), jax.ShapeDtypeStruct(proj.shape, BF16),
                   jax.ShapeDtypeStruct((1, E), F32), jax.ShapeDtypeStruct((1, E), F32)),
        compiler_params=_params(1),
    )(du4, u2, proj, ln_w, ln_b)


def ssd_gate_fwd(y, zx, norm_w, *, name):
    T, E = y.shape
    tm = _pick(T, (256, 128))

    def body(y_ref, z_ref, w_ref, o_ref):
        z = z_ref[...].astype(F32)
        yz = y_ref[...].astype(F32) * (z * _sigmoid(z))
        rstd = lax.rsqrt(jnp.mean(yz * yz, axis=-1, keepdims=True) + EPS)
        o_ref[...] = (yz * rstd * w_ref[...]).astype(BF16)

    row = pl.BlockSpec((tm, E), lambda i: (i, 0))
    vec = pl.BlockSpec((1, E), lambda i: (0, 0))
    return pl.pallas_call(
        body, name=name, grid=(T // tm,), in_specs=[row, row, vec], out_specs=row,
        out_shape=jax.ShapeDtypeStruct((T, E), BF16), compiler_params=_params(1),
    )(y, zx, norm_w)


def ssd_gate_bwd(dyn, y, zx, norm_w, *, name):
    T, E = y.shape
    tm = _pick(T, (256, 128))

    def body(d_ref, y_ref, z_ref, w_ref, dy_ref, dz_ref, dw_ref):
        @pl.when(pl.program_id(0) == 0)
        def _():
            dw_ref[...] = jnp.zeros_like(dw_ref)

        z = z_ref[...].astype(F32)
        sz = _sigmoid(z)
        gate = z * sz
        yv = y_ref[...].astype(F32)
        yz = yv * gate
        rstd = lax.rsqrt(jnp.mean(yz * yz, axis=-1, keepdims=True) + EPS)
        yhat = yz * rstd
        d = d_ref[...].astype(F32)
        dw_ref[...] += jnp.sum(d * yhat, axis=0, keepdims=True)
        dxh = d * w_ref[...]
        dyz = rstd * (dxh - yhat * jnp.mean(dxh * yhat, axis=-1, keepdims=True))
        dy_ref[...] = (dyz * gate).astype(BF16)
        dz_ref[...] = (dyz * yv * _silu_grad(z, sz)).astype(BF16)

    row = pl.BlockSpec((tm, E), lambda i: (i, 0))
    vec = pl.BlockSpec((1, E), lambda i: (0, 0))
    return pl.pallas_call(
        body, name=name, grid=(T // tm,), in_specs=[row, row, row, vec], out_specs=(row, row, vec),
        out_shape=(jax.ShapeDtypeStruct((T, E), BF16), jax.ShapeDtypeStruct(zx.shape, BF16),
                   jax.ShapeDtypeStruct((1, E), F32)),
        compiler_params=_params(1),
    )(dyn, y, zx, norm_w)


def _cumsum_mm(mask, a):
    hi = a.astype(BF16)
    r1 = a - hi.astype(F32)
    mid = r1.astype(BF16)
    lo = (r1 - mid.astype(F32)).astype(BF16)
    out = _dot(jnp.where(mask, 1.0, 0.0).astype(BF16), jnp.concatenate([hi, mid, lo], axis=1))
    return out[:, :LANES] + out[:, LANES:2 * LANES] + out[:, 2 * LANES:]


def _chunk_terms(xcb, dt_raw, bias, alog, rev):
    L = CHUNK
    xs = xcb[:, :GW].astype(F32)
    Bm = xcb[:, GW:GW + D_STATE]
    Cm = xcb[:, GW + D_STATE:]
    pre = dt_raw + bias
    dt = _softplus(pre)
    A = -jnp.exp(alog)
    row = lax.broadcasted_iota(jnp.int32, (L, L), 0)
    col = lax.broadcasted_iota(jnp.int32, (L, L), 1)
    mask = (col >= row) if rev else (col <= row)
    mask_t = (col <= row) if rev else (col >= row)
    cs = _cumsum_mm(mask, dt * A)
    tot = cs[0:1, :] if rev else cs[L - 1:L, :]
    return xs, Bm, Cm, pre, dt, A, mask, mask_t, cs, cs.T, tot


def _decay(cs, cs_t, ln, mask):
    d = cs[:, ln:ln + 1] - cs_t[ln:ln + 1, :]
    return jnp.where(mask, jnp.exp(jnp.where(mask, d, 0.0)), 0.0)


def _pair(v, ln0, lo):
    return jnp.where(lo[:v.shape[0]], v[:, ln0:ln0 + 1], v[:, ln0 + 1:ln0 + 2])


def _scan_specs(nc, rev_order):
    ci = (lambda c: nc - 1 - c) if rev_order else (lambda c: c)
    xc = pl.BlockSpec((CHUNK, GROUPS * XCG), lambda c: (ci(c), 0))
    dt = pl.BlockSpec((CHUNK, GROUPS * LANES), lambda c: (ci(c), 0))
    vec = pl.BlockSpec((1, GROUPS * LANES), lambda c: (0, 0))
    wide = pl.BlockSpec((CHUNK, GROUPS * GW), lambda c: (ci(c), 0))
    wvec = pl.BlockSpec((1, GROUPS * GW), lambda c: (0, 0))
    st = pl.BlockSpec((1, D_STATE, GROUPS * GW), lambda c: (ci(c), 0, 0))
    return xc, dt, vec, wide, wvec, st


def _cols(ref, g, width):
    return ref.at[:, pl.ds(g * width, width)]


def ssd_scan_fwd(xc, dt4, bias4, alog4, *, rev, name, prev=None, dvec=None):
    T = xc.shape[0]
    nc = T // CHUNK
    E = GROUPS * GW
    r = 1 if rev else 0
    skip = prev is not None

    def one_group(xc_ref, dt_ref, bias_ref, alog_ref, prev_ref, dvec_ref, y_ref, st_ref, s_ref):
        xs, Bm, Cm, _, dt, _, mask, _, cs, cs_t, tot = _chunk_terms(xc_ref[...], dt_ref[...], bias_ref[...],
                                                                   alog_ref[...], rev)
        e, d, et = jnp.exp(cs), jnp.exp(tot - cs), jnp.exp(tot)
        cb = _dot_nt(Cm, Bm)
        sb = s_ref[...].astype(BF16)
        st_ref[...] = sb
        c_s = _dot(Cm, sb)
        lo = lax.broadcasted_iota(jnp.int32, (CHUNK, LANES), 1) < HEADDIM
        xd_parts, et_parts = [], []
        for p in range(HPG // 2):
            ln0 = r * HPG + 2 * p
            sl = slice(p * LANES, (p + 1) * LANES)
            dtp, ep, dp = _pair(dt, ln0, lo), _pair(e, ln0, lo), _pair(d, ln0, lo)
            xp = xs[:, sl] * dtp
            mcat = jnp.concatenate([cb * _decay(cs, cs_t, ln0, mask), cb * _decay(cs, cs_t, ln0 + 1, mask)],
                                   axis=1).astype(BF16)
            xbd = jnp.concatenate([jnp.where(lo, xp, 0.0), jnp.where(lo, 0.0, xp)], axis=0).astype(BF16)
            yp = _dot(mcat, xbd) + c_s[:, sl] * ep
            if skip:
                yp = yp + prev_ref[:, sl].astype(F32) + xs[:, sl] * dvec_ref[:, sl]
            y_ref[:, sl] = yp.astype(BF16)
            xd_parts.append((xp * dp).astype(BF16))
            et_parts.append(_pair(et, ln0, lo))
        s_ref[...] = s_ref[...] * jnp.concatenate(et_parts, axis=1) + _dot_tn(Bm, jnp.concatenate(xd_parts, axis=1))

    def body(*refs):
        xc_ref, dt_ref, bias_ref, alog_ref = refs[:4]
        prev_ref, dvec_ref = (refs[4], refs[5]) if skip else (None, None)
        y_ref, st_ref, s_ref = refs[-3:]

        @pl.when(pl.program_id(0) == 0)
        def _():
            s_ref[...] = jnp.zeros_like(s_ref)

        for g in range(GROUPS):
            one_group(_cols(xc_ref, g, XCG), _cols(dt_ref, g, LANES), _cols(bias_ref, g, LANES),
                      _cols(alog_ref, g, LANES), _cols(prev_ref, g, GW) if skip else None,
                      _cols(dvec_ref, g, GW) if skip else None, _cols(y_ref, g, GW),
                      st_ref.at[0, :, pl.ds(g * GW, GW)], _cols(s_ref, g, GW))

    s_xc, s_dt, s_vec, s_wide, s_wvec, s_st = _scan_specs(nc, rev)
    in_specs = [s_xc, s_dt, s_vec, s_vec]
    args = [xc, dt4, bias4, alog4]
    if skip:
        in_specs += [s_wide, s_wvec]
        args += [prev, dvec]
    return pl.pallas_call(
        body, name=name, grid=(nc,), in_specs=in_specs, out_specs=(s_wide, s_st),
        out_shape=(jax.ShapeDtypeStruct((T, E), BF16), jax.ShapeDtypeStruct((nc, D_STATE, E), BF16)),
        scratch_shapes=[pltpu.VMEM((D_STATE, E), F32)], compiler_params=_params(1),
    )(*args)


def ssd_scan_bwd(xc, dt4, bias4, alog4, dy, states, *, rev, name, prev=None, dvec=None):
    T = xc.shape[0]
    nc = T // CHUNK
    E = GROUPS * GW
    L = CHUNK
    r = 1 if rev else 0
    skip = prev is not None

    def one_group(xc_ref, dt_ref, bias_ref, alog_ref, dy_ref, st_ref, pdxc_ref, pddt_ref, dvec_ref,
                  dxc_ref, ddt_ref, dalog_ref, dbias_ref, dd_ref, g_ref):
        xs, Bm, Cm, pre, dt, A, mask, mask_t, cs, cs_t, tot = _chunk_terms(
            xc_ref[...], dt_ref[...], bias_ref[...], alog_ref[...], rev)
        e, d, et = jnp.exp(cs), jnp.exp(tot - cs), jnp.exp(tot)
        cb = _dot_nt(Cm, Bm)
        s_in = st_ref[...]
        dy_all = dy_ref[...].astype(F32)
        g_f = g_ref[...]
        g_b = g_f.astype(BF16)
        c_s = _dot(Cm, s_in)
        b_g = _dot(Bm, g_b)
        lane = lax.broadcasted_iota(jnp.int32, (L, LANES), 1)
        lane1 = lax.broadcasted_iota(jnp.int32, (1, LANES), 1)
        sub = lax.broadcasted_iota(jnp.int32, (LANES, L), 0)
        lo = lane < HEADDIM
        dcs = jnp.zeros((L, LANES), F32)
        dcs_t = jnp.zeros((LANES, L), F32)
        ddt = jnp.zeros((L, LANES), F32)
        dtot = jnp.zeros((1, LANES), F32)
        dcb = jnp.zeros((L, L), F32)
        dye_parts, xd_parts, et_parts = [], [], []
        for p in range(HPG // 2):
            ln0 = r * HPG + 2 * p
            sl = slice(p * LANES, (p + 1) * LANES)
            dtp, ep, dp = _pair(dt, ln0, lo), _pair(e, ln0, lo), _pair(d, ln0, lo)
            xs_p, dy_p = xs[:, sl], dy_all[:, sl]
            xp = xs_p * dtp
            dye = dy_p * ep
            w2 = xp * b_g[:, sl] * dp
            u = dye * c_s[:, sl] - w2
            lam0, lam1 = _decay(cs, cs_t, ln0, mask), _decay(cs, cs_t, ln0 + 1, mask)
            m0, m1 = cb * lam0, cb * lam1
            dybd = jnp.concatenate([jnp.where(lo, dy_p, 0.0), jnp.where(lo, 0.0, dy_p)], axis=0).astype(BF16)
            dm = _dot_nt(dybd, xp.astype(BF16))
            dm0, dm1 = dm[:L], dm[L:]
            dcb = dcb + dm0 * lam0 + dm1 * lam1
            dx = _dot_tn(jnp.concatenate([m0, m1], axis=0).astype(BF16), dybd) + b_g[:, sl] * dp
            dxx = dx * xs_p
            for ln, half, q in ((ln0, lo, dm0 * m0), (ln0 + 1, jnp.logical_not(lo), dm1 * m1)):
                r_u = jnp.sum(jnp.where(half, u, 0.0), axis=1, keepdims=True)
                r_q = jnp.sum(q, axis=1, keepdims=True)
                c_q = jnp.sum(q, axis=0, keepdims=True)
                r_x = jnp.sum(jnp.where(half, dxx, 0.0), axis=1, keepdims=True)
                r_w = jnp.sum(jnp.where(half, w2, 0.0), axis=1, keepdims=True)
                dcs = dcs + jnp.where(lane == ln, r_u + r_q, 0.0)
                dcs_t = dcs_t - jnp.where(sub == ln, c_q, 0.0)
                ddt = ddt + jnp.where(lane == ln, r_x, 0.0)
                dtot = dtot + jnp.where(lane1 == ln, jnp.sum(r_w, axis=0, keepdims=True), 0.0)
            dxs = dx * dtp
            if skip:
                dxs = dxs + dy_p * dvec_ref[:, sl] + pdxc_ref[:, sl].astype(F32)
                dd_ref[:, sl] += jnp.sum(dy_p * xs_p, axis=0, keepdims=True)
            dxc_ref[:, sl] = dxs.astype(BF16)
            dye_parts.append(dye.astype(BF16))
            xd_parts.append((xp * dp).astype(BF16))
            et_parts.append(_pair(et, ln0, lo))
        dye_all = jnp.concatenate(dye_parts, axis=1)
        xd = jnp.concatenate(xd_parts, axis=1)
        etx = jnp.concatenate(et_parts, axis=1)
        dcb_b = dcb.astype(BF16)
        d_b = _dot_nt(xd, g_b) + _dot_tn(dcb_b, Cm)
        d_c = _dot_nt(dye_all, s_in) + _dot(dcb_b, Bm)
        if skip:
            d_b = d_b + pdxc_ref[:, GW:GW + D_STATE].astype(F32)
            d_c = d_c + pdxc_ref[:, GW + D_STATE:].astype(F32)
        dxc_ref[:, GW:GW + D_STATE] = d_b.astype(BF16)
        dxc_ref[:, GW + D_STATE:] = d_c.astype(BF16)
        gs = jnp.sum(g_f * s_in.astype(F32), axis=0, keepdims=True) * etx
        for j in range(HPG):
            val = jnp.sum(gs[:, j * HEADDIM:(j + 1) * HEADDIM], axis=1, keepdims=True)
            dtot = dtot + jnp.where(lane1 == r * HPG + j, val, 0.0)
        g_ref[...] = g_f * etx + _dot_tn(Cm, dye_all)
        rowi = lax.broadcasted_iota(jnp.int32, (L, LANES), 0)
        dcs = dcs + dcs_t.T + jnp.where(rowi == (0 if rev else L - 1), dtot, 0.0)
        da = _cumsum_mm(mask_t, dcs)
        keep = jnp.logical_and(lane >= r * HPG, lane < (r + 1) * HPG)
        ddr = jnp.where(keep, (da * A + ddt) * _sigmoid(pre), 0.0)
        dbias_ref[...] += jnp.sum(ddr, axis=0, keepdims=True)
        dalog_ref[...] += jnp.sum(jnp.where(keep, da * dt * A, 0.0), axis=0, keepdims=True)
        if skip:
            ddr = ddr + pddt_ref[...]
        ddt_ref[...] = ddr

    def body(*refs):
        xc_ref, dt_ref, bias_ref, alog_ref, dy_ref, st_ref = refs[:6]
        pdxc_ref, pddt_ref, dvec_ref = refs[6:9] if skip else (None, None, None)
        pos = 9 if skip else 6
        dxc_ref, ddt_ref, dalog_ref, dbias_ref = refs[pos:pos + 4]
        dd_ref = refs[pos + 4] if skip else None
        g_ref = refs[-1]

        @pl.when(pl.program_id(0) == 0)
        def _():
            g_ref[...] = jnp.zeros_like(g_ref)
            dalog_ref[...] = jnp.zeros_like(dalog_ref)
            dbias_ref[...] = jnp.zeros_like(dbias_ref)
            if skip:
                dd_ref[...] = jnp.zeros_like(dd_ref)

        for g in range(GROUPS):
            one_group(_cols(xc_ref, g, XCG), _cols(dt_ref, g, LANES), _cols(bias_ref, g, LANES),
                      _cols(alog_ref, g, LANES), _cols(dy_ref, g, GW), st_ref.at[0, :, pl.ds(g * GW, GW)],
                      _cols(pdxc_ref, g, XCG) if skip else None, _cols(pddt_ref, g, LANES) if skip else None,
                      _cols(dvec_ref, g, GW) if skip else None, _cols(dxc_ref, g, XCG), _cols(ddt_ref, g, LANES),
                      _cols(dalog_ref, g, LANES), _cols(dbias_ref, g, LANES),
                      _cols(dd_ref, g, GW) if skip else None, _cols(g_ref, g, GW))

    s_xc, s_dt, s_vec, s_wide, s_wvec, s_st = _scan_specs(nc, not rev)
    in_specs = [s_xc, s_dt, s_vec, s_vec, s_wide, s_st]
    args = [xc, dt4, bias4, alog4, dy, states]
    out_specs = [s_xc, s_dt, s_vec, s_vec]
    out_shape = [jax.ShapeDtypeStruct((T, GROUPS * XCG), BF16), jax.ShapeDtypeStruct((T, GROUPS * LANES), F32),
                 jax.ShapeDtypeStruct((1, GROUPS * LANES), F32), jax.ShapeDtypeStruct((1, GROUPS * LANES), F32)]
    if skip:
        in_specs += [s_xc, s_dt, s_wvec]
        args += [prev[0], prev[1], dvec]
        out_specs.append(s_wvec)
        out_shape.append(jax.ShapeDtypeStruct((1, E), F32))
    return pl.pallas_call(
        body, name=name, grid=(nc,), in_specs=in_specs, out_specs=tuple(out_specs),
        out_shape=tuple(out_shape), scratch_shapes=[pltpu.VMEM((D_STATE, E), F32)], compiler_params=_params(1),
    )(*args)


def _conf_cols(w):
    e = w.shape[-1] // 3
    lead = w.shape[:-1]
    vg = w[..., :2 * e].reshape(*lead, 2, e // CONV_TC, CONV_TC)
    vg = jnp.swapaxes(vg, -3, -2).reshape(*lead, 2 * e)
    return jnp.concatenate([vg, w[..., 2 * e:]], axis=-1)


def _conf_cols_inv(w):
    e = w.shape[-1] // 3
    lead = w.shape[:-1]
    vg = w[..., :2 * e].reshape(*lead, e // CONV_TC, 2, CONV_TC)
    vg = jnp.swapaxes(vg, -3, -2).reshape(*lead, 2 * e)
    return jnp.concatenate([vg, w[..., 2 * e:]], axis=-1)


def _xbc_cols(w):
    lead = w.shape[:-1]
    e = GROUPS * GW
    gn = GROUPS * D_STATE
    parts = [w[..., :e].reshape(*lead, GROUPS, GW), w[..., e:e + gn].reshape(*lead, GROUPS, D_STATE),
             w[..., e + gn:].reshape(*lead, GROUPS, D_STATE)]
    return jnp.concatenate(parts, axis=-1).reshape(*lead, GROUPS * XCG)


def _xbc_cols_inv(w):
    lead = w.shape[:-1]
    g = w.reshape(*lead, GROUPS, XCG)
    parts = [g[..., :GW].reshape(*lead, GROUPS * GW), g[..., GW:GW + D_STATE].reshape(*lead, GROUPS * D_STATE),
             g[..., GW + D_STATE:].reshape(*lead, GROUPS * D_STATE)]
    return jnp.concatenate(parts, axis=-1)


def _dt_cols(w):
    lead = w.shape[:-1]
    t = jnp.swapaxes(w.reshape(*lead, 2, GROUPS, HPG), -3, -2).reshape(*lead, GROUPS, 2 * HPG)
    pad = [(0, 0)] * (t.ndim - 1) + [(0, LANES - 2 * HPG)]
    return jnp.pad(t, pad).reshape(*lead, GROUPS * LANES)


def _dt_cols_inv(w):
    lead = w.shape[:-1]
    t = w.reshape(*lead, GROUPS, LANES)[..., :2 * HPG].reshape(*lead, GROUPS, 2, HPG)
    return jnp.swapaxes(t, -3, -2).reshape(*lead, 2 * HEADS)


def _pad_rows(w, rows):
    return jnp.pad(w, ((0, rows - w.shape[0]), (0, 0)))


def conf_weights(w_in, dw_w, dw_b, ln_w, ln_b, w_out):
    w_in_p = _conf_cols(w_in)
    return dict(w_in=w_in_p, w_in_t=w_in_p.T, w_out=w_out, w_out_t=w_out.T,
                dw_w=_pad_rows(dw_w, 32), dw_b=dw_b.reshape(1, -1), ln_w=ln_w.reshape(1, -1), ln_b=ln_b.reshape(1, -1))


def _xbc_rows(w):
    e, gn, c = GROUPS * GW, GROUPS * D_STATE, w.shape[1]
    parts = [w[:e].reshape(GROUPS, GW, c), w[e:e + gn].reshape(GROUPS, D_STATE, c),
             w[e + gn:].reshape(GROUPS, D_STATE, c)]
    return jnp.concatenate(parts, axis=1).reshape(GROUPS * XCG, c)


def _xbc_rows_inv(w):
    c = w.shape[1]
    g = w.reshape(GROUPS, XCG, c)
    parts = [g[:, :GW].reshape(GROUPS * GW, c), g[:, GW:GW + D_STATE].reshape(GROUPS * D_STATE, c),
             g[:, GW + D_STATE:].reshape(GROUPS * D_STATE, c)]
    return jnp.concatenate(parts, axis=0)


def _dt_rows(w):
    c = w.shape[1]
    t = jnp.swapaxes(w.reshape(2, GROUPS, HPG, c), 0, 1).reshape(GROUPS, 2 * HPG, c)
    return jnp.pad(t, ((0, 0), (0, LANES - 2 * HPG), (0, 0))).reshape(GROUPS * LANES, c)


def _dt_rows_inv(w):
    c = w.shape[1]
    t = w.reshape(GROUPS, LANES, c)[:, :2 * HPG].reshape(GROUPS, 2, HPG, c)
    return jnp.swapaxes(t, 0, 1).reshape(2 * HEADS, c)


def ssd_weights(w_in_t, conv_w, conv_b, dt_bias, a_log, d_skip, norm_w, w_out):
    e = GROUPS * GW
    xbc = e + 2 * GROUPS * D_STATE
    w_zx_t = jnp.concatenate([w_in_t[:e], _xbc_rows(w_in_t[e:e + xbc])], axis=0)
    return dict(w_zx_t=w_zx_t, w_dt_t=_dt_rows(w_in_t[e + xbc:]), w_out=w_out, w_out_t=w_out.T,
                conv_w=_pad_rows(_xbc_cols(conv_w), 8), conv_b=_xbc_cols(conv_b.reshape(1, -1)),
                bias4=_dt_cols(dt_bias.reshape(1, -1)), alog4=_dt_cols(a_log.reshape(1, -1)),
                dvec=jnp.repeat(d_skip, HEADDIM).reshape(1, -1), norm_w=norm_w.reshape(1, -1))


def conf_layer_fwd(h, nw, p, tag):
    hn = rmsnorm_fwd(h, nw, name=f"{tag}_norm")
    proj = mm_nn(hn, p["w_in"], out_dtype=BF16, name=f"{tag}_proj")
    u2 = dwconv_fwd(proj, p["dw_w"], p["dw_b"], width=31, glu=True, silu=False, col0=0, name=f"{tag}_conv")
    u4 = conf_ln_fwd(u2, proj, p["ln_w"], p["ln_b"], name=f"{tag}_ln")
    h2 = mm_nn(u4, p["w_out"], out_dtype=F32, res=h, name=f"{tag}_out")
    return h2, (h, hn, proj, u2, u4)


def conf_layer_bwd(dh, saved, nw, p, tag):
    h, hn, proj, u2, u4 = saved
    du4 = mm_nn(dh, p["w_out_t"], out_dtype=BF16, name=f"{tag}_d_u4")
    dw_out = mm_tn(u4, dh, name=f"{tag}_dw_out")
    du2, dproj, dln_w, dln_b = conf_ln_bwd(du4, u2, proj, p["ln_w"], p["ln_b"], name=f"{tag}_d_ln")
    dproj, ddw_w, ddw_b = dwconv_bwd(du2, proj, p["dw_w"], p["dw_b"], dproj, width=31, glu=True, silu=False,
                                     col0=0, dcol0=0, name=f"{tag}_d_conv")
    dhn = mm_nn(dproj, p["w_in_t"], out_dtype=F32, name=f"{tag}_d_hn")
    dw_in = mm_tn(hn, dproj, name=f"{tag}_dw_in")
    dh_prev, dnw = rmsnorm_bwd(dhn, h, nw, dh, name=f"{tag}_d_norm")
    grads = dict(w_in=_conf_cols_inv(dw_in), dw_w=ddw_w[:31], dw_b=ddw_b[0], ln_w=dln_w[0], ln_b=dln_b[0],
                 w_out=dw_out, norm=dnw[0])
    return dh_prev, grads


def ssd_layer_fwd(h, nw, p, tag):
    e = GROUPS * GW
    hn = rmsnorm_fwd(h, nw, name=f"{tag}_norm")
    zx = mm_nn(hn, p["w_zx_t"], out_dtype=BF16, b_rows_are_n=True, name=f"{tag}_proj")
    dt4 = mm_nn(hn, p["w_dt_t"], out_dtype=F32, b_rows_are_n=True, name=f"{tag}_proj_dt")
    xc = dwconv_fwd(zx, p["conv_w"], p["conv_b"], width=5, glu=False, silu=True, col0=e // CONV_TC, name=f"{tag}_conv")
    y0, st0 = ssd_scan_fwd(xc, dt4, p["bias4"], p["alog4"], rev=False, name=f"{tag}_scan_f")
    y, st1 = ssd_scan_fwd(xc, dt4, p["bias4"], p["alog4"], rev=True, prev=y0, dvec=p["dvec"], name=f"{tag}_scan_b")
    yn = ssd_gate_fwd(y, zx, p["norm_w"], name=f"{tag}_gate")
    h2 = mm_nn(yn, p["w_out"], out_dtype=F32, res=h, name=f"{tag}_out")
    return h2, (h, hn, zx, dt4, xc, st0, st1, y, yn)


def ssd_layer_bwd(dh, saved, nw, p, tag):
    e = GROUPS * GW
    h, hn, zx, dt4, xc, st0, st1, y, yn = saved
    dyn = mm_nn(dh, p["w_out_t"], out_dtype=BF16, name=f"{tag}_d_yn")
    dw_out = mm_tn(yn, dh, name=f"{tag}_dw_out")
    dy, dzx, dnorm_w = ssd_gate_bwd(dyn, y, zx, p["norm_w"], name=f"{tag}_d_gate")
    dxc0, ddt0, dalog0, dbias0 = ssd_scan_bwd(xc, dt4, p["bias4"], p["alog4"], dy, st0, rev=False,
                                              name=f"{tag}_d_scan_f")
    dxc, ddt4, dalog1, dbias1, ddvec = ssd_scan_bwd(xc, dt4, p["bias4"], p["alog4"], dy, st1, rev=True,
                                                    prev=(dxc0, ddt0), dvec=p["dvec"], name=f"{tag}_d_scan_b")
    dzx, dconv_w, dconv_b = dwconv_bwd(dxc, zx, p["conv_w"], p["conv_b"], dzx, width=5, glu=False, silu=True,
                                       col0=e // CONV_TC, dcol0=e // CONV_TC, name=f"{tag}_d_conv")
    dhn = mm_nn(dzx, p["w_zx_t"], out_dtype=F32, a2=ddt4, b2=p["w_dt_t"], name=f"{tag}_d_hn")
    dw_zx_t = mm_tn(dzx, hn, name=f"{tag}_dw_zx")
    dw_dt_t = mm_tn(ddt4, hn, name=f"{tag}_dw_dt")
    dh_prev, dnw = rmsnorm_bwd(dhn, h, nw, dh, name=f"{tag}_d_norm")
    dw_in_t = jnp.concatenate([dw_zx_t[:e], _xbc_rows_inv(dw_zx_t[e:]), _dt_rows_inv(dw_dt_t)], axis=0)
    grads = dict(w_in_t=dw_in_t, conv_w=_xbc_cols_inv(dconv_w[:5]), conv_b=_xbc_cols_inv(dconv_b)[0],
                 dt_bias=_dt_cols_inv(dbias0 + dbias1).reshape(2, HEADS),
                 a_log=_dt_cols_inv(dalog0 + dalog1).reshape(2, HEADS),
                 d_skip=jnp.sum(ddvec.reshape(HEADS, HEADDIM), axis=-1), norm_w=dnorm_w[0], w_out=dw_out,
                 norm=dnw[0])
    return dh_prev, grads


ANY = pl.BlockSpec(memory_space=pl.ANY)


def _place():
    return lax.axis_index("x"), lax.axis_index("y"), lax.axis_index("c")


def gather_chips(bufs, *, name):
    n = len(bufs)

    def body(*refs):
        ins, outs = refs[:n], refs[n:2 * n]
        send_sems, recv_sems, local_sems = refs[2 * n:]
        x, y, c = _place()
        k_me = 2 * x + y
        chips = [(1 - x, y), (x, 1 - y), (1 - x, 1 - y)]
        copies = []
        for t in range(n):
            own = pltpu.make_async_copy(ins[t], outs[t].at[k_me], local_sems.at[t])
            own.start()
            copies.append(own)
            for j, (px, py) in enumerate(chips):
                cp = pltpu.make_async_remote_copy(
                    src_ref=ins[t], dst_ref=outs[t].at[k_me], send_sem=send_sems.at[3 * t + j],
                    recv_sem=recv_sems.at[3 * t + j], device_id=(px, py, c), device_id_type=MESH)
                cp.start()
                copies.append(cp)
        for cp in copies:
            cp.wait()

    return pl.pallas_call(
        body, name=name, in_specs=[ANY] * n, out_specs=tuple([ANY] * n),
        out_shape=tuple(jax.ShapeDtypeStruct((N_CHIPS,) + b.shape, b.dtype) for b in bufs),
        scratch_shapes=[pltpu.SemaphoreType.DMA((3 * n,)), pltpu.SemaphoreType.DMA((3 * n,)),
                        pltpu.SemaphoreType.DMA((n,))],
    )(*bufs)


def swap_other_half(g2, *, name):
    def body(g_ref, o_ref, send_sem, recv_sem):
        x, y, c = _place()
        cp = pltpu.make_async_remote_copy(src_ref=g_ref.at[1 - c], dst_ref=o_ref, send_sem=send_sem, recv_sem=recv_sem,
                                          device_id=(x, y, 1 - c), device_id_type=MESH)
        cp.start()
        cp.wait()

    return pl.pallas_call(
        body, name=name, in_specs=[ANY], out_specs=ANY, out_shape=jax.ShapeDtypeStruct(g2.shape[1:], g2.dtype),
        scratch_shapes=[pltpu.SemaphoreType.DMA, pltpu.SemaphoreType.DMA],
    )(g2)


def exchange_chips(p, *, name):
    def body(p_ref, o_ref, send_sems, recv_sems, local_sem):
        x, y, c = _place()
        k_me = 2 * x + y
        own = pltpu.make_async_copy(p_ref.at[k_me], o_ref.at[k_me], local_sem)
        own.start()
        copies = [own]
        for j, (px, py) in enumerate([(1 - x, y), (x, 1 - y), (1 - x, 1 - y)]):
            cp = pltpu.make_async_remote_copy(
                src_ref=p_ref.at[2 * px + py], dst_ref=o_ref.at[k_me], send_sem=send_sems.at[j],
                recv_sem=recv_sems.at[j], device_id=(px, py, c), device_id_type=MESH)
            cp.start()
            copies.append(cp)
        for cp in copies:
            cp.wait()

    return pl.pallas_call(
        body, name=name, in_specs=[ANY], out_specs=ANY, out_shape=jax.ShapeDtypeStruct(p.shape, p.dtype),
        scratch_shapes=[pltpu.SemaphoreType.DMA((3,)), pltpu.SemaphoreType.DMA((3,)), pltpu.SemaphoreType.DMA],
    )(p)


def share_half(full, *, name):
    def body(_, f_ref, send_sem, recv_sem):
        x, y, c = _place()
        cp = pltpu.make_async_remote_copy(src_ref=f_ref.at[c], dst_ref=f_ref.at[c], send_sem=send_sem,
                                          recv_sem=recv_sem, device_id=(x, y, 1 - c), device_id_type=MESH)
        cp.start()
        cp.wait()

    return pl.pallas_call(
        body, name=name, in_specs=[ANY], out_specs=ANY, out_shape=jax.ShapeDtypeStruct(full.shape, full.dtype),
        input_output_aliases={0: 0},
        scratch_shapes=[pltpu.SemaphoreType.DMA, pltpu.SemaphoreType.DMA],
    )(full)


def gather_all(v, *, name):
    def body(v_ref, o_ref, send_sems, recv_sems, local_sem):
        x, y, c = _place()
        me = 4 * x + 2 * y + c
        own = pltpu.make_async_copy(v_ref, o_ref.at[me], local_sem)
        own.start()
        copies = [own]
        idx = 0
        for fx in (0, 1):
            for fy in (0, 1):
                for fc in (0, 1):
                    if not (fx or fy or fc):
                        continue
                    peer = (1 - x if fx else x, 1 - y if fy else y, 1 - c if fc else c)
                    cp = pltpu.make_async_remote_copy(src_ref=v_ref, dst_ref=o_ref.at[me], send_sem=send_sems.at[idx],
                                                      recv_sem=recv_sems.at[idx], device_id=peer, device_id_type=MESH)
                    cp.start()
                    copies.append(cp)
                    idx += 1
        for cp in copies:
            cp.wait()

    return pl.pallas_call(
        body, name=name, in_specs=[ANY], out_specs=ANY, out_shape=jax.ShapeDtypeStruct((N_DEV,) + v.shape, v.dtype),
        scratch_shapes=[pltpu.SemaphoreType.DMA((N_DEV - 1,)), pltpu.SemaphoreType.DMA((N_DEV - 1,)),
                        pltpu.SemaphoreType.DMA],
    )(v)


RED_TR = 432


def pair_sum(g2, recv, cidx, *, name):
    _, K, R, C = g2.shape
    tr = _pick(R, (RED_TR, 8))

    def body(c_ref, a_ref, b_ref, o_ref):
        o_ref[...] = (a_ref[0] + b_ref[...]).astype(BF16)

    blk = pl.BlockSpec((1, tr, C), lambda k, i, c: (k, i, 0))
    return pl.pallas_call(
        body, name=name,
        grid_spec=pltpu.PrefetchScalarGridSpec(
            num_scalar_prefetch=1, grid=(K, R // tr),
            in_specs=[pl.BlockSpec((1, 1, tr, C), lambda k, i, c: (c[0], k, i, 0)), blk], out_specs=blk),
        out_shape=jax.ShapeDtypeStruct((K, R, C), BF16), compiler_params=_params(2),
    )(cidx, g2, recv)


def sum_lead(a, *, name, slot=None, nslots=1):
    K, R, C = a.shape
    tr = _pick(R, (RED_TR, 8))

    def body(s_ref, a_ref, o_ref):
        acc = a_ref[0].astype(F32)
        for k in range(1, K):
            acc = acc + a_ref[k].astype(F32)
        o_ref[0] = acc

    if slot is None:
        slot = jnp.zeros((1,), jnp.int32)
    return pl.pallas_call(
        body, name=name,
        grid_spec=pltpu.PrefetchScalarGridSpec(
            num_scalar_prefetch=1, grid=(R // tr,),
            in_specs=[pl.BlockSpec((K, tr, C), lambda i, s: (0, i, 0))],
            out_specs=pl.BlockSpec((1, tr, C), lambda i, s: (s[0], i, 0))),
        out_shape=jax.ShapeDtypeStruct((nslots, R, C), F32), compiler_params=_params(1),
    )(slot, a)


def adamw(g, w, m, v, *, name):
    R, C = w.shape
    tr = _pick(R, (256, 128, 64, 32, 16, 8))

    def body(g_ref, w_ref, m_ref, v_ref, d_ref, nm_ref, nv_ref):
        gv = g_ref[...]
        m_new = ADAM_B1 * m_ref[...] + (1.0 - ADAM_B1) * gv
        v_new = ADAM_B2 * v_ref[...] + (1.0 - ADAM_B2) * (gv * gv)
        m_hat = m_new / (1.0 - ADAM_B1 ** ADAM_STEP)
        v_hat = v_new / (1.0 - ADAM_B2 ** ADAM_STEP)
        d_ref[...] = -ADAM_LR * (m_hat / (jnp.sqrt(v_hat) + ADAM_EPS) + ADAM_WD * w_ref[...])
        nm_ref[...] = m_new
        nv_ref[...] = v_new

    blk = pl.BlockSpec((tr, C), lambda i: (i, 0))
    sds = jax.ShapeDtypeStruct((R, C), F32)
    return pl.pallas_call(
        body, name=name, grid=(R // tr,), in_specs=[blk] * 4, out_specs=(blk,) * 3, out_shape=(sds,) * 3,
        compiler_params=_params(1),
    )(g, w, m, v)


WEIGHTS = ("norm_w", "final_norm_w", "cm_w_in", "cm_dw_w", "cm_dw_b", "cm_ln_w", "cm_ln_b", "cm_w_out", "ssd_w_in",
           "ssd_conv_w", "ssd_conv_b", "ssd_dt_bias", "ssd_A_log", "ssd_D", "ssd_norm_w", "ssd_w_out")
BIG = (("cm_w_in", 2), ("cm_w_out", 1), ("ssd_w_in", 1), ("ssd_w_out", 1))
TRANSPOSED = ("ssd_w_in",)
SMALL_SHARDED = (("cm_dw_w", 2), ("ssd_conv_w", 2), ("ssd_conv_b", 1), ("ssd_norm_w", 1))
REPLICATED = ("norm_w", "final_norm_w", "cm_dw_b", "cm_ln_w", "cm_ln_b", "ssd_dt_bias", "ssd_A_log", "ssd_D")
ROW = 1024


def _to_shards(g, axis):
    n = g.shape[axis]
    s = g.reshape(g.shape[:axis] + (N_CHIPS, n // N_CHIPS) + g.shape[axis + 1:])
    return jnp.moveaxis(s, axis, 0).reshape(N_CHIPS, -1)


def _from_shards(x4, local_shape, axis):
    local_shape = tuple(local_shape)
    s = jnp.moveaxis(x4.reshape((N_CHIPS,) + local_shape), 0, axis)
    return s.reshape(local_shape[:axis] + (N_CHIPS * local_shape[axis],) + local_shape[axis + 1:])


def _flat_pad(parts, multiple):
    v = jnp.concatenate([p.reshape(-1) for p in parts])
    return jnp.pad(v, (0, (-v.size) % multiple))


def _split(flat, like, names):
    out, off = {}, 0
    for n in names:
        out[n] = flat[off:off + like[n].size].reshape(like[n].shape)
        off += like[n].size
    return out


def kernel(x, norm_w, final_norm_w, cm_w_in, cm_dw_w, cm_dw_b, cm_ln_w, cm_ln_b, cm_w_out, ssd_w_in, ssd_conv_w, ssd_conv_b, ssd_dt_bias, ssd_A_log, ssd_D, ssd_norm_w, ssd_w_out, loss_target, m_norm_w, m_final_norm_w, m_cm_w_in, m_cm_dw_w, m_cm_dw_b, m_cm_ln_w, m_cm_ln_b, m_cm_w_out, m_ssd_w_in, m_ssd_conv_w, m_ssd_conv_b, m_ssd_dt_bias, m_ssd_A_log, m_ssd_D, m_ssd_norm_w, m_ssd_w_out, v_norm_w, v_final_norm_w, v_cm_w_in, v_cm_dw_w, v_cm_dw_b, v_cm_ln_w, v_cm_ln_b, v_cm_w_out, v_ssd_w_in, v_ssd_conv_w, v_ssd_conv_b, v_ssd_dt_bias, v_ssd_A_log, v_ssd_D, v_ssd_norm_w, v_ssd_w_out):
    a = dict(locals())
    w = {n: a[n] for n in WEIGHTS}
    m = {n: a["m_" + n] for n in WEIGHTS}
    v = {n: a["v_" + n] for n in WEIGHTS}
    _, T, D = x.shape
    cidx = lax.axis_index("c").astype(jnp.int32).reshape(1)
    big_names = [n for n, _ in BIG]
    small_names = [n for n, _ in SMALL_SHARDED]

    wx = {n: (jnp.swapaxes(w[n], 1, 2) if n in TRANSPOSED else w[n]) for n in big_names + small_names}
    big = _flat_pad([wx[n] for n in big_names], 16 * ROW).astype(BF16).reshape(-1, ROW)
    small = _flat_pad([wx[n] for n in small_names], 8 * ROW).reshape(-1, ROW)
    g_big, g_small = gather_chips([big, small], name="gather_weights")
    g_big, g_small = g_big.reshape(N_CHIPS, -1), g_small.reshape(N_CHIPS, -1)
    full, off = {}, 0
    for n, ax in BIG:
        full[n] = _from_shards(g_big[:, off:off + wx[n].size], wx[n].shape, ax)
        off += wx[n].size
    off = 0
    for n, ax in SMALL_SHARDED:
        full[n] = _from_shards(g_small[:, off:off + wx[n].size], wx[n].shape, ax)
        off += wx[n].size
    n_layers = norm_w.shape[0]
    lw = []
    for i in range(n_layers):
        j = i // 2
        if i % 2 == 0:
            lw.append(conf_weights(full["cm_w_in"][j], full["cm_dw_w"][j], cm_dw_b[j], cm_ln_w[j], cm_ln_b[j],
                                   full["cm_w_out"][j]))
        else:
            lw.append(ssd_weights(full["ssd_w_in"][j], full["ssd_conv_w"][j], full["ssd_conv_b"][j], ssd_dt_bias[j],
                                  ssd_A_log[j], ssd_D[j], full["ssd_norm_w"][j], full["ssd_w_out"][j]))

    h = x[0]
    saved = []
    for i in range(n_layers):
        fwd = conf_layer_fwd if i % 2 == 0 else ssd_layer_fwd
        h, s = fwd(h, norm_w[i].reshape(1, -1), lw[i], f"l{i}")
        saved.append(s)
    dh, loss_local, d_final = loss_head(h, loss_target[0], final_norm_w.reshape(1, -1), name="loss_head")
    lg = [None] * n_layers
    for i in reversed(range(n_layers)):
        bwd = conf_layer_bwd if i % 2 == 0 else ssd_layer_bwd
        dh, lg[i] = bwd(dh, saved[i], norm_w[i].reshape(1, -1), lw[i], f"l{i}")
    conf_g, ssd_g = lg[0::2], lg[1::2]
    local = {
        "norm_w": jnp.stack([g["norm"] for g in lg]), "final_norm_w": d_final[0],
        "cm_w_in": jnp.stack([g["w_in"] for g in conf_g]), "cm_dw_w": jnp.stack([g["dw_w"] for g in conf_g]),
        "cm_dw_b": jnp.stack([g["dw_b"] for g in conf_g]), "cm_ln_w": jnp.stack([g["ln_w"] for g in conf_g]),
        "cm_ln_b": jnp.stack([g["ln_b"] for g in conf_g]), "cm_w_out": jnp.stack([g["w_out"] for g in conf_g]),
        "ssd_w_in": jnp.stack([g["w_in_t"] for g in ssd_g]), "ssd_conv_w": jnp.stack([g["conv_w"] for g in ssd_g]),
        "ssd_conv_b": jnp.stack([g["conv_b"] for g in ssd_g]), "ssd_dt_bias": jnp.stack([g["dt_bias"] for g in ssd_g]),
        "ssd_A_log": jnp.stack([g["a_log"] for g in ssd_g]), "ssd_D": jnp.stack([g["d_skip"] for g in ssd_g]),
        "ssd_norm_w": jnp.stack([g["norm_w"] for g in ssd_g]), "ssd_w_out": jnp.stack([g["w_out"] for g in ssd_g]),
    }

    flat4 = jnp.concatenate([_to_shards(local[n], ax) for n, ax in BIG + SMALL_SHARDED], axis=1)
    flat4 = jnp.pad(flat4, ((0, 0), (0, (-flat4.shape[1]) % (2 * RED_TR * ROW))))
    g2 = jnp.swapaxes(flat4.reshape(N_CHIPS, 2, -1, ROW), 0, 1)
    theirs = swap_other_half(g2, name="grad_pair_swap")
    part = pair_sum(g2, theirs, cidx, name="grad_pair_sum")
    got = exchange_chips(part, name="grad_chip_exchange")
    half = sum_lead(got, slot=cidx, nslots=2, name="grad_chip_sum")
    shard_flat = share_half(half, name="grad_pair_share").reshape(-1)
    grads = _split(shard_flat, wx, big_names + small_names)
    for n in TRANSPOSED:
        grads[n] = jnp.swapaxes(grads[n], 1, 2)

    rep = _flat_pad([local[n] for n in REPLICATED], 8 * LANES).reshape(-1, LANES)
    rep_sum = sum_lead(gather_all(rep, name="grad_small_gather"), name="grad_small_sum")
    grads.update(_split(rep_sum.reshape(-1), w, REPLICATED))

    delta, new_m, new_v = {}, {}, {}
    for n in big_names:
        two_d = (-1, w[n].shape[-1])
        d_, m_, v_ = adamw(grads[n].reshape(two_d), w[n].reshape(two_d), m[n].reshape(two_d), v[n].reshape(two_d),
                           name="adamw_" + n)
        delta[n], new_m[n], new_v[n] = d_.reshape(w[n].shape), m_.reshape(w[n].shape), v_.reshape(w[n].shape)
    rest = list(REPLICATED) + small_names
    packed = [_flat_pad([t[n] for n in rest], 8 * LANES).reshape(-1, LANES) for t in (grads, w, m, v)]
    for out, res in zip((delta, new_m, new_v), adamw(*packed, name="adamw_small")):
        out.update(_split(res.reshape(-1), w, rest))

    loss = lax.psum(loss_local[0, 0], ("x", "y", "c"))
    return (loss, dh.reshape(x.shape), *[grads[n] for n in WEIGHTS], *[delta[n] for n in WEIGHTS],
            *[new_m[n] for n in WEIGHTS], *[new_v[n] for n in WEIGHTS])
```

```python
import jax
import jax.numpy as jnp
from jax import lax
from jax.experimental import pallas as pl
from jax.experimental.pallas import tpu as pltpu

F32 = jnp.float32
BF16 = jnp.bfloat16
MESH = pl.DeviceIdType.MESH

EPS = 1e-5
HEADDIM = 64
HEADS = 32
GROUPS = 4
HPG = HEADS // GROUPS
D_STATE = 128
CHUNK = 128
GW = HPG * HEADDIM
XCG = GW + 2 * D_STATE
HALO = 16
LANES = 128
N_CHIPS = 4
N_DEV = 8

ADAM_LR = 0.001
ADAM_B1 = 0.9
ADAM_B2 = 0.999
ADAM_EPS = 1e-08
ADAM_WD = 0.01
ADAM_STEP = 10

VMEM_LIMIT = 52 * 1024 * 1024


def _params(n_axes):
    return pltpu.CompilerParams(dimension_semantics=("arbitrary",) * n_axes, vmem_limit_bytes=VMEM_LIMIT)


def _sigmoid(x):
    return 1.0 / (1.0 + jnp.exp(-x))


def _softplus(x):
    return jnp.maximum(x, 0.0) + jnp.log(1.0 + jnp.exp(-jnp.abs(x)))


def _dot(a, b):
    return jnp.dot(a, b, preferred_element_type=F32)


def _dot_nt(a, b):
    return lax.dot_general(a, b, (((1,), (1,)), ((), ())), preferred_element_type=F32)


def _dot_tn(a, b):
    return lax.dot_general(a, b, (((0,), (0,)), ((), ())), preferred_element_type=F32)


def _pick(n, pref):
    for t in pref:
        if n % t == 0:
            return t
    return n


def mm_nn(a, b, *, out_dtype, name, res=None, a2=None, b2=None, b_rows_are_n=False):
    M, K = a.shape
    N = b.shape[0] if b_rows_are_n else b.shape[1]
    tm = _pick(M, (1024, 512, 256, 128))
    tn = _pick(N, (1024, 512, 256, 128))
    tk = _pick(K, (1024, 512, 256, 128))
    nk = K // tk
    has2, has_res = a2 is not None, res is not None

    def body(*refs):
        a_ref, b_ref = refs[0], refs[1]
        pos = 2
        if has2:
            a2_ref, b2_ref = refs[pos], refs[pos + 1]
            pos += 2
        if has_res:
            r_ref = refs[pos]
            pos += 1
        o_ref, acc_ref = refs[pos], refs[pos + 1]
        k = pl.program_id(2)

        @pl.when(k == 0)
        def _():
            if has2:
                acc_ref[...] = _dot(a2_ref[...].astype(BF16), b2_ref[...])
            else:
                acc_ref[...] = jnp.zeros_like(acc_ref)

        acc_ref[...] += (_dot_nt if b_rows_are_n else _dot)(a_ref[...].astype(BF16), b_ref[...])

        @pl.when(k == nk - 1)
        def _():
            r = acc_ref[...]
            if has_res:
                r = r + r_ref[...]
            o_ref[...] = r.astype(out_dtype)

    b_spec = pl.BlockSpec((tn, tk), lambda i, j, k: (j, k)) if b_rows_are_n else pl.BlockSpec((tk, tn), lambda i, j, k: (k, j))
    in_specs = [pl.BlockSpec((tm, tk), lambda i, j, k: (i, k)), b_spec]
    args = [a, b]
    if has2:
        k2 = a2.shape[1]
        in_specs += [pl.BlockSpec((tm, k2), lambda i, j, k: (i, 0)), pl.BlockSpec((k2, tn), lambda i, j, k: (0, j))]
        args += [a2, b2]
    if has_res:
        in_specs.append(pl.BlockSpec((tm, tn), lambda i, j, k: (i, j)))
        args.append(res)
    return pl.pallas_call(
        body, name=name, grid=(M // tm, N // tn, nk), in_specs=in_specs,
        out_specs=pl.BlockSpec((tm, tn), lambda i, j, k: (i, j)),
        out_shape=jax.ShapeDtypeStruct((M, N), out_dtype),
        scratch_shapes=[pltpu.VMEM((tm, tn), F32)], compiler_params=_params(3),
    )(*args)


def mm_tn(a, b, *, name):
    T, M = a.shape
    N = b.shape[1]
    tm = _pick(M, (1024, 512, 256, 128))
    tn = _pick(N, (1024, 512, 256, 128))
    tt = _pick(T, (1024, 512, 256, 128))

    def body(a_ref, b_ref, o_ref):
        @pl.when(pl.program_id(2) == 0)
        def _():
            o_ref[...] = jnp.zeros_like(o_ref)

        o_ref[...] += _dot_tn(a_ref[...].astype(BF16), b_ref[...].astype(BF16))

    return pl.pallas_call(
        body, name=name, grid=(M // tm, N // tn, T // tt),
        in_specs=[pl.BlockSpec((tt, tm), lambda i, j, t: (t, i)), pl.BlockSpec((tt, tn), lambda i, j, t: (t, j))],
        out_specs=pl.BlockSpec((tm, tn), lambda i, j, t: (i, j)),
        out_shape=jax.ShapeDtypeStruct((M, N), F32), compiler_params=_params(3),
    )(a, b)


def rmsnorm_fwd(h, w, *, name):
    T, D = h.shape
    tm = _pick(T, (512, 256, 128))

    def body(h_ref, w_ref, o_ref):
        x = h_ref[...]
        rstd = lax.rsqrt(jnp.mean(x * x, axis=-1, keepdims=True) + EPS)
        o_ref[...] = (x * rstd * w_ref[...]).astype(BF16)

    return pl.pallas_call(
        body, name=name, grid=(T // tm,),
        in_specs=[pl.BlockSpec((tm, D), lambda i: (i, 0)), pl.BlockSpec((1, D), lambda i: (0, 0))],
        out_specs=pl.BlockSpec((tm, D), lambda i: (i, 0)),
        out_shape=jax.ShapeDtypeStruct((T, D), BF16), compiler_params=_params(1),
    )(h, w)


def rmsnorm_bwd(dhn, h, w, dh, *, name):
    T, D = h.shape
    tm = _pick(T, (512, 256, 128))

    def body(dhn_ref, h_ref, w_ref, dh_ref, o_ref, dw_ref):
        @pl.when(pl.program_id(0) == 0)
        def _():
            dw_ref[...] = jnp.zeros_like(dw_ref)

        x = h_ref[...]
        g = dhn_ref[...]
        rstd = lax.rsqrt(jnp.mean(x * x, axis=-1, keepdims=True) + EPS)
        xhat = x * rstd
        dxh = g * w_ref[...]
        o_ref[...] = dh_ref[...] + rstd * (dxh - xhat * jnp.mean(dxh * xhat, axis=-1, keepdims=True))
        dw_ref[...] += jnp.sum(g * xhat, axis=0, keepdims=True)

    row = pl.BlockSpec((tm, D), lambda i: (i, 0))
    vec = pl.BlockSpec((1, D), lambda i: (0, 0))
    return pl.pallas_call(
        body, name=name, grid=(T // tm,), in_specs=[row, row, vec, row], out_specs=(row, vec),
        out_shape=(jax.ShapeDtypeStruct((T, D), F32), jax.ShapeDtypeStruct((1, D), F32)),
        compiler_params=_params(1),
    )(dhn, h, w, dh)


def loss_head(h, target, w, *, name):
    T, D = h.shape
    tm = _pick(T, (512, 256, 128))

    def body(h_ref, t_ref, w_ref, dh_ref, loss_ref, dw_ref):
        @pl.when(pl.program_id(0) == 0)
        def _():
            loss_ref[...] = jnp.zeros_like(loss_ref)
            dw_ref[...] = jnp.zeros_like(dw_ref)

        x = h_ref[...]
        rstd = lax.rsqrt(jnp.mean(x * x, axis=-1, keepdims=True) + EPS)
        xhat = x * rstd
        err = xhat * w_ref[...] - t_ref[...]
        rows = jnp.sum(err * err, axis=-1, keepdims=True)
        loss_ref[...] += (0.5 / D) * jnp.sum(rows, axis=0, keepdims=True)
        dy = err * (1.0 / D)
        dxh = dy * w_ref[...]
        dh_ref[...] = rstd * (dxh - xhat * jnp.mean(dxh * xhat, axis=-1, keepdims=True))
        dw_ref[...] += jnp.sum(dy * xhat, axis=0, keepdims=True)

    row = pl.BlockSpec((tm, D), lambda i: (i, 0))
    vec = pl.BlockSpec((1, D), lambda i: (0, 0))
    return pl.pallas_call(
        body, name=name, grid=(T // tm,), in_specs=[row, row, vec],
        out_specs=(row, pl.BlockSpec((1, 1), lambda i: (0, 0)), vec),
        out_shape=(jax.ShapeDtypeStruct((T, D), F32), jax.ShapeDtypeStruct((1, 1), F32),
                   jax.ShapeDtypeStruct((1, D), F32)),
        compiler_params=_params(1),
    )(h, target, w)


CONV_TM = 256
CONV_TC = 512
CONV_RB = 32


def _conv_specs(T, tm, sw, col0):
    hb = tm // HALO
    last = T // HALO - 1
    main = pl.BlockSpec((tm, sw), lambda j, i: (i, col0 + j))
    prev = pl.BlockSpec((HALO, sw), lambda j, i: (jnp.maximum(i * hb - 1, 0), col0 + j))
    nxt = pl.BlockSpec((HALO, sw), lambda j, i: (jnp.minimum((i + 1) * hb, last), col0 + j))
    return main, prev, nxt


def _conv_input(blk, glu, tc):
    x = blk.astype(F32)
    if glu:
        return x[:, :tc] * _sigmoid(x[:, tc:])
    return x


def _fill_padded(pad_ref, main, prev, nxt, first, last, tm):
    pad_ref[0:HALO, :] = jnp.where(first, 0.0, prev)
    pad_ref[HALO:HALO + tm, :] = main
    pad_ref[HALO + tm:HALO + tm + HALO, :] = jnp.where(last, 0.0, nxt)


SH_ROWS = 24


def _tap_plan(offsets):
    plan = [(o % 8, o - o % 8) for o in offsets]
    return plan, sorted({b for b, _ in plan if b})


def _fill_shifted(sh_ref, pad_ref, shifts, tm):
    for b in shifts:
        sh_ref[b] = pad_ref[b:b + tm + SH_ROWS, :]


def _tap_rows(pad_ref, sh_ref, b, start, rows):
    return pad_ref[start:start + rows, :] if b == 0 else sh_ref[b, start:start + rows, :]


def dwconv_fwd(src, w, b, *, width, glu, silu, col0, name):
    T = src.shape[0]
    C = w.shape[1]
    tm, tc = min(CONV_TM, T), CONV_TC
    sw = 2 * tc if glu else tc
    n_i = T // tm
    p = (width - 1) // 2
    rb = CONV_RB
    plan, shifts = _tap_plan([HALO - p + k for k in range(width)])

    def body(m_ref, p_ref, n_ref, w_ref, b_ref, o_ref, pad_ref, sh_ref):
        i = pl.program_id(1)
        _fill_padded(pad_ref, _conv_input(m_ref[...], glu, tc), _conv_input(p_ref[...], glu, tc),
                     _conv_input(n_ref[...], glu, tc), i == 0, i == n_i - 1, tm)
        _fill_shifted(sh_ref, pad_ref, shifts, tm)
        for r in range(tm // rb):
            acc = jnp.zeros((rb, tc), F32)
            for k, (sb, start) in enumerate(plan):
                acc = acc + _tap_rows(pad_ref, sh_ref, sb, start + r * rb, rb) * w_ref[k:k + 1, :]
            acc = acc + b_ref[...]
            if silu:
                acc = acc * _sigmoid(acc)
            o_ref[r * rb:(r + 1) * rb, :] = acc.astype(BF16)

    main, prev, nxt = _conv_specs(T, tm, sw, col0)
    return pl.pallas_call(
        body, name=name, grid=(C // tc, n_i),
        in_specs=[main, prev, nxt, pl.BlockSpec((w.shape[0], tc), lambda j, i: (0, j)),
                  pl.BlockSpec((1, tc), lambda j, i: (0, j))],
        out_specs=pl.BlockSpec((tm, tc), lambda j, i: (i, j)),
        out_shape=jax.ShapeDtypeStruct((T, C), BF16),
        scratch_shapes=[pltpu.VMEM((tm + 2 * HALO, tc), F32), pltpu.VMEM((8, tm + SH_ROWS, tc), F32)],
        compiler_params=_params(2),
    )(src, src, src, w, b)


def dwconv_bwd(dout, src, w, b, dsrc, *, width, glu, silu, col0, dcol0, name):
    T = src.shape[0]
    C = w.shape[1]
    kp = w.shape[0]
    tm, tc = min(CONV_TM, T), CONV_TC
    sw = 2 * tc if glu else tc
    n_i = T // tm
    p = (width - 1) // 2
    rb = CONV_RB
    edge = 8
    assert p <= edge or not silu
    plan, shifts = _tap_plan([HALO - p + k for k in range(width)])
    dplan, dshifts = _tap_plan([HALO + p - k for k in range(width)])

    def body(dm_ref, dp_ref, dn_ref, m_ref, p_ref, n_ref, w_ref, b_ref, _, o_ref, dw_ref, db_ref, pad_ref, dpre_ref,
             sh_ref, dsh_ref):
        i = pl.program_id(1)

        @pl.when(i == 0)
        def _():
            dw_ref[...] = jnp.zeros_like(dw_ref)
            db_ref[...] = jnp.zeros_like(db_ref)

        first, last = i == 0, i == n_i - 1
        _fill_padded(pad_ref, _conv_input(m_ref[...], glu, tc), _conv_input(p_ref[...], glu, tc),
                     _conv_input(n_ref[...], glu, tc), first, last, tm)
        _fill_padded(dpre_ref, dm_ref[...].astype(F32), dp_ref[...].astype(F32), dn_ref[...].astype(F32),
                     first, last, tm)
        _fill_shifted(sh_ref, pad_ref, shifts, tm)
        if silu:
            for r0 in range(HALO - edge, HALO + tm + edge, HALO):
                pre = jnp.zeros((HALO, tc), F32)
                for k, (sb, start) in enumerate(plan):
                    pre = pre + _tap_rows(pad_ref, sh_ref, sb, start + r0 - HALO, HALO) * w_ref[k:k + 1, :]
                pre = pre + b_ref[...]
                s = _sigmoid(pre)
                dpre_ref[r0:r0 + HALO, :] = dpre_ref[r0:r0 + HALO, :] * (s * (1.0 + pre * (1.0 - s)))
        _fill_shifted(dsh_ref, dpre_ref, dshifts, tm)

        for r in range(tm // rb):
            acc = jnp.zeros((rb, tc), F32)
            for k, (sb, start) in enumerate(dplan):
                acc = acc + _tap_rows(dpre_ref, dsh_ref, sb, start + r * rb, rb) * w_ref[k:k + 1, :]
            if glu:
                blk = m_ref[r * rb:(r + 1) * rb, :].astype(F32)
                v, s = blk[:, :tc], _sigmoid(blk[:, tc:])
                o_ref[r * rb:(r + 1) * rb, :tc] = (acc * s).astype(BF16)
                o_ref[r * rb:(r + 1) * rb, tc:] = (acc * v * s * (1.0 - s)).astype(BF16)
            else:
                o_ref[r * rb:(r + 1) * rb, :] = acc.astype(BF16)

        dmain = dpre_ref[HALO:HALO + tm, :]
        for k, (sb, start) in enumerate(plan):
            dw_ref[k:k + 1, :] += jnp.sum(dmain * _tap_rows(pad_ref, sh_ref, sb, start, tm), axis=0, keepdims=True)
        db_ref[...] += jnp.sum(dmain, axis=0, keepdims=True)

    dmain_s, dprev_s, dnext_s = _conv_specs(T, tm, tc, 0)
    main, prev, nxt = _conv_specs(T, tm, sw, col0)
    wspec = pl.BlockSpec((kp, tc), lambda j, i: (0, j))
    bspec = pl.BlockSpec((1, tc), lambda j, i: (0, j))
    return pl.pallas_call(
        body, name=name, grid=(C // tc, n_i),
        in_specs=[dmain_s, dprev_s, dnext_s, main, prev, nxt, wspec, bspec, pl.BlockSpec(memory_space=pl.ANY)],
        out_specs=(pl.BlockSpec((tm, sw), lambda j, i: (i, dcol0 + j)), wspec, bspec),
        out_shape=(jax.ShapeDtypeStruct(dsrc.shape, dsrc.dtype), jax.ShapeDtypeStruct((kp, C), F32),
                   jax.ShapeDtypeStruct((1, C), F32)),
        input_output_aliases={8: 0},
        scratch_shapes=[pltpu.VMEM((tm + 2 * HALO, tc), F32), pltpu.VMEM((tm + 2 * HALO, tc), F32),
                        pltpu.VMEM((8, tm + SH_ROWS, tc), F32), pltpu.VMEM((8, tm + SH_ROWS, tc), F32)],
        compiler_params=_params(2),
    )(dout, dout, dout, src, src, src, w, b, dsrc)


def _silu_grad(x, s):
    return s * (1.0 + x * (1.0 - s))


def conf_ln_fwd(u2, proj, ln_w, ln_b, *, name):
    T, E = u2.shape
    zc = proj.shape[1] // E - 1
    tm = _pick(T, (256, 128))

    def body(u_ref, z_ref, w_ref, b_ref, o_ref):
        x = u_ref[...].astype(F32)
        xc = x - jnp.mean(x, axis=-1, keepdims=True)
        rstd = lax.rsqrt(jnp.mean(xc * xc, axis=-1, keepdims=True) + EPS)
        u3 = xc * rstd * w_ref[...] + b_ref[...]
        z = z_ref[...].astype(F32)
        o_ref[...] = (u3 * _sigmoid(u3) * z * _sigmoid(z)).astype(BF16)

    row = pl.BlockSpec((tm, E), lambda i: (i, 0))
    vec = pl.BlockSpec((1, E), lambda i: (0, 0))
    return pl.pallas_call(
        body, name=name, grid=(T // tm,),
        in_specs=[row, pl.BlockSpec((tm, E), lambda i: (i, zc)), vec, vec], out_specs=row,
        out_shape=jax.ShapeDtypeStruct((T, E), BF16), compiler_params=_params(1),
    )(u2, proj, ln_w, ln_b)


def conf_ln_bwd(du4, u2, proj, ln_w, ln_b, *, name):
    T, E = u2.shape
    ncol = proj.shape[1] // E
    zc = ncol - 1
    tm = _pick(T, (256, 128))

    def body(d_ref, u_ref, z_ref, w_ref, b_ref, du_ref, dz_ref, dw_ref, db_ref):
        @pl.when(pl.program_id(0) == 0)
        def _():
            dw_ref[...] = jnp.zeros_like(dw_ref)
            db_ref[...] = jnp.zeros_like(db_ref)

        x = u_ref[...].astype(F32)
        xc = x - jnp.mean(x, axis=-1, keepdims=True)
        rstd = lax.rsqrt(jnp.mean(xc * xc, axis=-1, keepdims=True) + EPS)
        xhat = xc * rstd
        u3 = xhat * w_ref[...] + b_ref[...]
        z = z_ref[...].astype(F32)
        s3, sz = _sigmoid(u3), _sigmoid(z)
        d4 = d_ref[...].astype(F32)
        du3 = d4 * (z * sz) * _silu_grad(u3, s3)
        dz_ref[...] = (d4 * (u3 * s3) * _silu_grad(z, sz)).astype(BF16)
        dw_ref[...] += jnp.sum(du3 * xhat, axis=0, keepdims=True)
        db_ref[...] += jnp.sum(du3, axis=0, keepdims=True)
        dxh = du3 * w_ref[...]
        du = rstd * (dxh - jnp.mean(dxh, axis=-1, keepdims=True) - xhat * jnp.mean(dxh * xhat, axis=-1, keepdims=True))
        du_ref[...] = du.astype(BF16)

    row = pl.BlockSpec((tm, E), lambda i: (i, 0))
    zrow = pl.BlockSpec((tm, E), lambda i: (i, zc))
    vec = pl.BlockSpec((1, E), lambda i: (0, 0))
    return pl.pallas_call(
        body, name=name, grid=(T // tm,), in_specs=[row, row, zrow, vec, vec], out_specs=(row, zrow, vec, vec),
        out_shape=(jax.ShapeDtypeStruct((T, E), BF16), jax.ShapeDtypeStruct(proj.shape, BF16),
                   jax.ShapeDtypeStruct((1, E), F32), jax.ShapeDtypeStruct((1, E), F32)),
        compiler_params=_params(1),
    )(du4, u2, proj, ln_w, ln_b)


def ssd_gate_fwd(y, zx, norm_w, *, name):
    T, E = y.shape
    tm = _pick(T, (256, 128))

    def body(y_ref, z_ref, w_ref, o_ref):
        z = z_ref[...].astype(F32)
        yz = y_ref[...].astype(F32) * (z * _sigmoid(z))
        rstd = lax.rsqrt(jnp.mean(yz * yz, axis=-1, keepdims=True) + EPS)
        o_ref[...] = (yz * rstd * w_ref[...]).astype(BF16)

    row = pl.BlockSpec((tm, E), lambda i: (i, 0))
    vec = pl.BlockSpec((1, E), lambda i: (0, 0))
    return pl.pallas_call(
        body, name=name, grid=(T // tm,), in_specs=[row, row, vec], out_specs=row,
        out_shape=jax.ShapeDtypeStruct((T, E), BF16), compiler_params=_params(1),
    )(y, zx, norm_w)


def ssd_gate_bwd(dyn, y, zx, norm_w, *, name):
    T, E = y.shape
    tm = _pick(T, (256, 128))

    def body(d_ref, y_ref, z_ref, w_ref, dy_ref, dz_ref, dw_ref):
        @pl.when(pl.program_id(0) == 0)
        def _():
            dw_ref[...] = jnp.zeros_like(dw_ref)

        z = z_ref[...].astype(F32)
        sz = _sigmoid(z)
        gate = z * sz
        yv = y_ref[...].astype(F32)
        yz = yv * gate
        rstd = lax.rsqrt(jnp.mean(yz * yz, axis=-1, keepdims=True) + EPS)
        yhat = yz * rstd
        d = d_ref[...].astype(F32)
        dw_ref[...] += jnp.sum(d * yhat, axis=0, keepdims=True)
        dxh = d * w_ref[...]
        dyz = rstd * (dxh - yhat * jnp.mean(dxh * yhat, axis=-1, keepdims=True))
        dy_ref[...] = (dyz * gate).astype(BF16)
        dz_ref[...] = (dyz * yv * _silu_grad(z, sz)).astype(BF16)

    row = pl.BlockSpec((tm, E), lambda i: (i, 0))
    vec = pl.BlockSpec((1, E), lambda i: (0, 0))
    return pl.pallas_call(
        body, name=name, grid=(T // tm,), in_specs=[row, row, row, vec], out_specs=(row, row, vec),
        out_shape=(jax.ShapeDtypeStruct((T, E), BF16), jax.ShapeDtypeStruct(zx.shape, BF16),
                   jax.ShapeDtypeStruct((1, E), F32)),
        compiler_params=_params(1),
    )(dyn, y, zx, norm_w)


def _cumsum_mm(mask, a):
    hi = a.astype(BF16)
    r1 = a - hi.astype(F32)
    mid = r1.astype(BF16)
    lo = (r1 - mid.astype(F32)).astype(BF16)
    out = _dot(jnp.where(mask, 1.0, 0.0).astype(BF16), jnp.concatenate([hi, mid, lo], axis=1))
    return out[:, :LANES] + out[:, LANES:2 * LANES] + out[:, 2 * LANES:]


def _chunk_terms(xcb, dt_raw, bias, alog, rev):
    L = CHUNK
    xs = xcb[:, :GW].astype(F32)
    Bm = xcb[:, GW:GW + D_STATE]
    Cm = xcb[:, GW + D_STATE:]
    pre = dt_raw + bias
    dt = _softplus(pre)
    A = -jnp.exp(alog)
    row = lax.broadcasted_iota(jnp.int32, (L, L), 0)
    col = lax.broadcasted_iota(jnp.int32, (L, L), 1)
    mask = (col >= row) if rev else (col <= row)
    mask_t = (col <= row) if rev else (col >= row)
    cs = _cumsum_mm(mask, dt * A)
    tot = cs[0:1, :] if rev else cs[L - 1:L, :]
    return xs, Bm, Cm, pre, dt, A, mask, mask_t, cs, cs.T, tot


def _decay(cs, cs_t, ln, mask):
    d = cs[:, ln:ln + 1] - cs_t[ln:ln + 1, :]
    return jnp.where(mask, jnp.exp(jnp.where(mask, d, 0.0)), 0.0)


def _pair(v, ln0, lo):
    return jnp.where(lo[:v.shape[0]], v[:, ln0:ln0 + 1], v[:, ln0 + 1:ln0 + 2])


def _scan_specs(nc, rev_order):
    ci = (lambda c: nc - 1 - c) if rev_order else (lambda c: c)
    xc = pl.BlockSpec((CHUNK, GROUPS * XCG), lambda c: (ci(c), 0))
    dt = pl.BlockSpec((CHUNK, GROUPS * LANES), lambda c: (ci(c), 0))
    vec = pl.BlockSpec((1, GROUPS * LANES), lambda c: (0, 0))
    wide = pl.BlockSpec((CHUNK, GROUPS * GW), lambda c: (ci(c), 0))
    wvec = pl.BlockSpec((1, GROUPS * GW), lambda c: (0, 0))
    st = pl.BlockSpec((1, D_STATE, GROUPS * GW), lambda c: (ci(c), 0, 0))
    return xc, dt, vec, wide, wvec, st


def _cols(ref, g, width):
    return ref.at[:, pl.ds(g * width, width)]


def _head_expand(r):
    row = lax.broadcasted_iota(jnp.int32, (LANES, GW), 0)
    col = lax.broadcasted_iota(jnp.int32, (LANES, GW), 1)
    first = (row - r * HPG) * HEADDIM
    return jnp.where(jnp.logical_and(col >= first, col < first + HEADDIM), 1.0, 0.0).astype(BF16)


def _head_collect(r):
    row = lax.broadcasted_iota(jnp.int32, (GW, LANES), 0)
    first = (lax.broadcasted_iota(jnp.int32, (GW, LANES), 1) - r * HPG) * HEADDIM
    return jnp.where(jnp.logical_and(row >= first, row < first + HEADDIM), 1.0, 0.0).astype(BF16)


def _expand(parts, sel):
    n = parts[0].shape[0]
    out = _dot(jnp.concatenate(parts, axis=0).astype(BF16), sel)
    return [out[i * n:(i + 1) * n] for i in range(len(parts))]


def ssd_scan_fwd(xc, dt4, bias4, alog4, *, rev, name, prev=None, dvec=None):
    T = xc.shape[0]
    nc = T // CHUNK
    E = GROUPS * GW
    r = 1 if rev else 0
    skip = prev is not None

    def one_group(sel, xc_ref, dt_ref, bias_ref, alog_ref, prev_ref, dvec_ref, y_ref, st_ref, s_ref):
        xs, Bm, Cm, _, dt, _, mask, _, cs, cs_t, tot = _chunk_terms(xc_ref[...], dt_ref[...], bias_ref[...],
                                                                   alog_ref[...], rev)
        dtx, ex, dx = _expand([dt, jnp.exp(cs), jnp.exp(tot - cs)], sel)
        et = jnp.exp(tot)
        cb = _dot_nt(Cm, Bm)
        sb = s_ref[...].astype(BF16)
        st_ref[...] = sb
        xp_all = xs * dtx
        y_off = _dot(Cm, sb) * ex
        lo = lax.broadcasted_iota(jnp.int32, (CHUNK, LANES), 1) < HEADDIM
        et_parts = []
        for p in range(HPG // 2):
            ln0 = r * HPG + 2 * p
            sl = slice(p * LANES, (p + 1) * LANES)
            xp = xp_all[:, sl]
            mcat = jnp.concatenate([cb * _decay(cs, cs_t, ln0, mask), cb * _decay(cs, cs_t, ln0 + 1, mask)],
                                   axis=1).astype(BF16)
            xbd = jnp.concatenate([jnp.where(lo, xp, 0.0), jnp.where(lo, 0.0, xp)], axis=0).astype(BF16)
            yp = _dot(mcat, xbd) + y_off[:, sl]
            if skip:
                yp = yp + prev_ref[:, sl].astype(F32) + xs[:, sl] * dvec_ref[:, sl]
            y_ref[:, sl] = yp.astype(BF16)
            et_parts.append(_pair(et, ln0, lo))
        s_ref[...] = s_ref[...] * jnp.concatenate(et_parts, axis=1) + _dot_tn(Bm, (xp_all * dx).astype(BF16))

    def body(*refs):
        xc_ref, dt_ref, bias_ref, alog_ref = refs[:4]
        prev_ref, dvec_ref = (refs[4], refs[5]) if skip else (None, None)
        y_ref, st_ref, s_ref = refs[-3:]

        @pl.when(pl.program_id(0) == 0)
        def _():
            s_ref[...] = jnp.zeros_like(s_ref)

        sel = _head_expand(r)
        for g in range(GROUPS):
            one_group(sel, _cols(xc_ref, g, XCG), _cols(dt_ref, g, LANES), _cols(bias_ref, g, LANES),
                      _cols(alog_ref, g, LANES), _cols(prev_ref, g, GW) if skip else None,
                      _cols(dvec_ref, g, GW) if skip else None, _cols(y_ref, g, GW),
                      st_ref.at[0, :, pl.ds(g * GW, GW)], _cols(s_ref, g, GW))

    s_xc, s_dt, s_vec, s_wide, s_wvec, s_st = _scan_specs(nc, rev)
    in_specs = [s_xc, s_dt, s_vec, s_vec]
    args = [xc, dt4, bias4, alog4]
    if skip:
        in_specs += [s_wide, s_wvec]
        args += [prev, dvec]
    return pl.pallas_call(
        body, name=name, grid=(nc,), in_specs=in_specs, out_specs=(s_wide, s_st),
        out_shape=(jax.ShapeDtypeStruct((T, E), BF16), jax.ShapeDtypeStruct((nc, D_STATE, E), BF16)),
        scratch_shapes=[pltpu.VMEM((D_STATE, E), F32)], compiler_params=_params(1),
    )(*args)


def ssd_scan_bwd(xc, dt4, bias4, alog4, dy, states, *, rev, name, prev=None, dvec=None):
    T = xc.shape[0]
    nc = T // CHUNK
    E = GROUPS * GW
    L = CHUNK
    r = 1 if rev else 0
    skip = prev is not None

    def one_group(sel, sel_t, xc_ref, dt_ref, bias_ref, alog_ref, dy_ref, st_ref, pdxc_ref, pddt_ref, dvec_ref,
                  dxc_ref, ddt_ref, dalog_ref, dbias_ref, dd_ref, g_ref):
        xs, Bm, Cm, pre, dt, A, mask, mask_t, cs, cs_t, tot = _chunk_terms(
            xc_ref[...], dt_ref[...], bias_ref[...], alog_ref[...], rev)
        dtx, ex, dx = _expand([dt, jnp.exp(cs), jnp.exp(tot - cs)], sel)
        et = jnp.exp(tot)
        cb = _dot_nt(Cm, Bm)
        s_in = st_ref[...]
        dy_all = dy_ref[...].astype(F32)
        g_f = g_ref[...]
        g_b = g_f.astype(BF16)
        xp_all = xs * dtx
        dye_all = dy_all * ex
        bgd = _dot(Bm, g_b) * dx
        lane = lax.broadcasted_iota(jnp.int32, (L, LANES), 1)
        lo = lane < HEADDIM
        dcb = jnp.zeros((L, L), F32)
        yd_parts, dxd_parts, et_parts = [], [], []
        for p in range(HPG // 2):
            ln0 = r * HPG + 2 * p
            sl = slice(p * LANES, (p + 1) * LANES)
            xp, dy_p = xp_all[:, sl], dy_all[:, sl]
            lam0, lam1 = _decay(cs, cs_t, ln0, mask), _decay(cs, cs_t, ln0 + 1, mask)
            m0, m1 = (cb * lam0).astype(BF16), (cb * lam1).astype(BF16)
            dybd = jnp.concatenate([jnp.where(lo, dy_p, 0.0), jnp.where(lo, 0.0, dy_p)], axis=0).astype(BF16)
            xbd = jnp.concatenate([jnp.where(lo, xp, 0.0), jnp.where(lo, 0.0, xp)], axis=0).astype(BF16)
            dm = _dot_nt(dybd, xp.astype(BF16))
            dcb = dcb + dm[:L] * lam0 + dm[L:] * lam1
            yd_parts.append(_dot(jnp.concatenate([m0, m1], axis=1), xbd))
            dxd_parts.append(_dot_tn(jnp.concatenate([m0, m1], axis=0), dybd))
            et_parts.append(_pair(et, ln0, lo))
        y_diag = jnp.concatenate(yd_parts, axis=1)
        dx_diag = jnp.concatenate(dxd_parts, axis=1)
        etx = jnp.concatenate(et_parts, axis=1)
        dxt = dx_diag + bgd
        w2 = xp_all * bgd
        dy_r, xp_r = dy_all.astype(BF16).astype(F32), xp_all.astype(BF16).astype(F32)
        u = dye_all * _dot(Cm, s_in) + dy_r * y_diag - xp_r * dx_diag - w2
        dxx = dxt * xs
        tail = jnp.broadcast_to(jnp.sum(w2, axis=0, keepdims=True)
                                + jnp.sum(g_f * s_in.astype(F32), axis=0, keepdims=True) * etx, (8, GW))
        u_hi, t_hi = u.astype(BF16), tail.astype(BF16)
        red = _dot(jnp.concatenate([u_hi, (u - u_hi.astype(F32)).astype(BF16), dxx.astype(BF16), t_hi,
                                    (tail - t_hi.astype(F32)).astype(BF16)], axis=0), sel_t)
        dcs = red[:L] + red[L:2 * L]
        ddt = red[2 * L:3 * L]
        dtot = red[3 * L:3 * L + 1] + red[3 * L + 8:3 * L + 9]
        dxs = dxt * dtx
        if skip:
            dxs = dxs + dy_all * dvec_ref[...] + pdxc_ref[:, :GW].astype(F32)
            dd_ref[...] += jnp.sum(dy_all * xs, axis=0, keepdims=True)
        dxc_ref[:, :GW] = dxs.astype(BF16)
        dye_b = dye_all.astype(BF16)
        xd = (xp_all * dx).astype(BF16)
        dcb_b = dcb.astype(BF16)
        d_b = _dot_nt(xd, g_b) + _dot_tn(dcb_b, Cm)
        d_c = _dot_nt(dye_b, s_in) + _dot(dcb_b, Bm)
        if skip:
            d_b = d_b + pdxc_ref[:, GW:GW + D_STATE].astype(F32)
            d_c = d_c + pdxc_ref[:, GW + D_STATE:].astype(F32)
        dxc_ref[:, GW:GW + D_STATE] = d_b.astype(BF16)
        dxc_ref[:, GW + D_STATE:] = d_c.astype(BF16)
        g_ref[...] = g_f * etx + _dot_tn(Cm, dye_b)
        rowi = lax.broadcasted_iota(jnp.int32, (L, LANES), 0)
        da = _cumsum_mm(mask_t, dcs + jnp.where(rowi == (0 if rev else L - 1), dtot, 0.0))
        keep = jnp.logical_and(lane >= r * HPG, lane < (r + 1) * HPG)
        ddr = jnp.where(keep, (da * A + ddt) * _sigmoid(pre), 0.0)
        dbias_ref[...] += jnp.sum(ddr, axis=0, keepdims=True)
        dalog_ref[...] += jnp.sum(jnp.where(keep, da * dt * A, 0.0), axis=0, keepdims=True)
        if skip:
            ddr = ddr + pddt_ref[...]
        ddt_ref[...] = ddr

    def body(*refs):
        xc_ref, dt_ref, bias_ref, alog_ref, dy_ref, st_ref = refs[:6]
        pdxc_ref, pddt_ref, dvec_ref = refs[6:9] if skip else (None, None, None)
        pos = 9 if skip else 6
        dxc_ref, ddt_ref, dalog_ref, dbias_ref = refs[pos:pos + 4]
        dd_ref = refs[pos + 4] if skip else None
        g_ref = refs[-1]

        @pl.when(pl.program_id(0) == 0)
        def _():
            g_ref[...] = jnp.zeros_like(g_ref)
            dalog_ref[...] = jnp.zeros_like(dalog_ref)
            dbias_ref[...] = jnp.zeros_like(dbias_ref)
            if skip:
                dd_ref[...] = jnp.zeros_like(dd_ref)

        sel, sel_t = _head_expand(r), _head_collect(r)
        for g in range(GROUPS):
            one_group(sel, sel_t, _cols(xc_ref, g, XCG), _cols(dt_ref, g, LANES), _cols(bias_ref, g, LANES),
                      _cols(alog_ref, g, LANES), _cols(dy_ref, g, GW), st_ref.at[0, :, pl.ds(g * GW, GW)],
                      _cols(pdxc_ref, g, XCG) if skip else None, _cols(pddt_ref, g, LANES) if skip else None,
                      _cols(dvec_ref, g, GW) if skip else None, _cols(dxc_ref, g, XCG), _cols(ddt_ref, g, LANES),
                      _cols(dalog_ref, g, LANES), _cols(dbias_ref, g, LANES),
                      _cols(dd_ref, g, GW) if skip else None, _cols(g_ref, g, GW))

    s_xc, s_dt, s_vec, s_wide, s_wvec, s_st = _scan_specs(nc, not rev)
    in_specs = [s_xc, s_dt, s_vec, s_vec, s_wide, s_st]
    args = [xc, dt4, bias4, alog4, dy, states]
    out_specs = [s_xc, s_dt, s_vec, s_vec]
    out_shape = [jax.ShapeDtypeStruct((T, GROUPS * XCG), BF16), jax.ShapeDtypeStruct((T, GROUPS * LANES), F32),
                 jax.ShapeDtypeStruct((1, GROUPS * LANES), F32), jax.ShapeDtypeStruct((1, GROUPS * LANES), F32)]
    if skip:
        in_specs += [s_xc, s_dt, s_wvec]
        args += [prev[0], prev[1], dvec]
        out_specs.append(s_wvec)
        out_shape.append(jax.ShapeDtypeStruct((1, E), F32))
    return pl.pallas_call(
        body, name=name, grid=(nc,), in_specs=in_specs, out_specs=tuple(out_specs),
        out_shape=tuple(out_shape), scratch_shapes=[pltpu.VMEM((D_STATE, E), F32)], compiler_params=_params(1),
    )(*args)


def _conf_cols(w):
    e = w.shape[-1] // 3
    lead = w.shape[:-1]
    vg = w[..., :2 * e].reshape(*lead, 2, e // CONV_TC, CONV_TC)
    vg = jnp.swapaxes(vg, -3, -2).reshape(*lead, 2 * e)
    return jnp.concatenate([vg, w[..., 2 * e:]], axis=-1)


def _conf_cols_inv(w):
    e = w.shape[-1] // 3
    lead = w.shape[:-1]
    vg = w[..., :2 * e].reshape(*lead, e // CONV_TC, 2, CONV_TC)
    vg = jnp.swapaxes(vg, -3, -2).reshape(*lead, 2 * e)
    return jnp.concatenate([vg, w[..., 2 * e:]], axis=-1)


def _xbc_cols(w):
    lead = w.shape[:-1]
    e = GROUPS * GW
    gn = GROUPS * D_STATE
    parts = [w[..., :e].reshape(*lead, GROUPS, GW), w[..., e:e + gn].reshape(*lead, GROUPS, D_STATE),
             w[..., e + gn:].reshape(*lead, GROUPS, D_STATE)]
    return jnp.concatenate(parts, axis=-1).reshape(*lead, GROUPS * XCG)


def _xbc_cols_inv(w):
    lead = w.shape[:-1]
    g = w.reshape(*lead, GROUPS, XCG)
    parts = [g[..., :GW].reshape(*lead, GROUPS * GW), g[..., GW:GW + D_STATE].reshape(*lead, GROUPS * D_STATE),
             g[..., GW + D_STATE:].reshape(*lead, GROUPS * D_STATE)]
    return jnp.concatenate(parts, axis=-1)


def _dt_cols(w):
    lead = w.shape[:-1]
    t = jnp.swapaxes(w.reshape(*lead, 2, GROUPS, HPG), -3, -2).reshape(*lead, GROUPS, 2 * HPG)
    pad = [(0, 0)] * (t.ndim - 1) + [(0, LANES - 2 * HPG)]
    return jnp.pad(t, pad).reshape(*lead, GROUPS * LANES)


def _dt_cols_inv(w):
    lead = w.shape[:-1]
    t = w.reshape(*lead, GROUPS, LANES)[..., :2 * HPG].reshape(*lead, GROUPS, 2, HPG)
    return jnp.swapaxes(t, -3, -2).reshape(*lead, 2 * HEADS)


def _pad_rows(w, rows):
    return jnp.pad(w, ((0, rows - w.shape[0]), (0, 0)))


def conf_weights(w_in, dw_w, dw_b, ln_w, ln_b, w_out):
    w_in_p = _conf_cols(w_in)
    return dict(w_in=w_in_p, w_in_t=w_in_p.T, w_out=w_out, w_out_t=w_out.T,
                dw_w=_pad_rows(dw_w, 32), dw_b=dw_b.reshape(1, -1), ln_w=ln_w.reshape(1, -1), ln_b=ln_b.reshape(1, -1))


def _xbc_rows(w):
    e, gn, c = GROUPS * GW, GROUPS * D_STATE, w.shape[1]
    parts = [w[:e].reshape(GROUPS, GW, c), w[e:e + gn].reshape(GROUPS, D_STATE, c),
             w[e + gn:].reshape(GROUPS, D_STATE, c)]
    return jnp.concatenate(parts, axis=1).reshape(GROUPS * XCG, c)


def _xbc_rows_inv(w):
    c = w.shape[1]
    g = w.reshape(GROUPS, XCG, c)
    parts = [g[:, :GW].reshape(GROUPS * GW, c), g[:, GW:GW + D_STATE].reshape(GROUPS * D_STATE, c),
             g[:, GW + D_STATE:].reshape(GROUPS * D_STATE, c)]
    return jnp.concatenate(parts, axis=0)


def _dt_rows(w):
    c = w.shape[1]
    t = jnp.swapaxes(w.reshape(2, GROUPS, HPG, c), 0, 1).reshape(GROUPS, 2 * HPG, c)
    return jnp.pad(t, ((0, 0), (0, LANES - 2 * HPG), (0, 0))).reshape(GROUPS * LANES, c)


def _dt_rows_inv(w):
    c = w.shape[1]
    t = w.reshape(GROUPS, LANES, c)[:, :2 * HPG].reshape(GROUPS, 2, HPG, c)
    return jnp.swapaxes(t, 0, 1).reshape(2 * HEADS, c)


def ssd_weights(w_in_t, conv_w, conv_b, dt_bias, a_log, d_skip, norm_w, w_out):
    e = GROUPS * GW
    xbc = e + 2 * GROUPS * D_STATE
    w_zx_t = jnp.concatenate([w_in_t[:e], _xbc_rows(w_in_t[e:e + xbc])], axis=0)
    return dict(w_zx_t=w_zx_t, w_dt_t=_dt_rows(w_in_t[e + xbc:]), w_out=w_out, w_out_t=w_out.T,
                conv_w=_pad_rows(_xbc_cols(conv_w), 8), conv_b=_xbc_cols(conv_b.reshape(1, -1)),
                bias4=_dt_cols(dt_bias.reshape(1, -1)), alog4=_dt_cols(a_log.reshape(1, -1)),
                dvec=jnp.repeat(d_skip, HEADDIM).reshape(1, -1), norm_w=norm_w.reshape(1, -1))


def conf_layer_fwd(h, nw, p, tag):
    hn = rmsnorm_fwd(h, nw, name=f"{tag}_norm")
    proj = mm_nn(hn, p["w_in"], out_dtype=BF16, name=f"{tag}_proj")
    u2 = dwconv_fwd(proj, p["dw_w"], p["dw_b"], width=31, glu=True, silu=False, col0=0, name=f"{tag}_conv")
    u4 = conf_ln_fwd(u2, proj, p["ln_w"], p["ln_b"], name=f"{tag}_ln")
    h2 = mm_nn(u4, p["w_out"], out_dtype=F32, res=h, name=f"{tag}_out")
    return h2, (h, hn, proj, u2, u4)


def conf_layer_bwd(dh, saved, nw, p, tag):
    h, hn, proj, u2, u4 = saved
    du4 = mm_nn(dh, p["w_out_t"], out_dtype=BF16, name=f"{tag}_d_u4")
    dw_out = mm_tn(u4, dh, name=f"{tag}_dw_out")
    du2, dproj, dln_w, dln_b = conf_ln_bwd(du4, u2, proj, p["ln_w"], p["ln_b"], name=f"{tag}_d_ln")
    dproj, ddw_w, ddw_b = dwconv_bwd(du2, proj, p["dw_w"], p["dw_b"], dproj, width=31, glu=True, silu=False,
                                     col0=0, dcol0=0, name=f"{tag}_d_conv")
    dhn = mm_nn(dproj, p["w_in_t"], out_dtype=F32, name=f"{tag}_d_hn")
    dw_in = mm_tn(hn, dproj, name=f"{tag}_dw_in")
    dh_prev, dnw = rmsnorm_bwd(dhn, h, nw, dh, name=f"{tag}_d_norm")
    grads = dict(w_in=_conf_cols_inv(dw_in), dw_w=ddw_w[:31], dw_b=ddw_b[0], ln_w=dln_w[0], ln_b=dln_b[0],
                 w_out=dw_out, norm=dnw[0])
    return dh_prev, grads


def ssd_layer_fwd(h, nw, p, tag):
    e = GROUPS * GW
    hn = rmsnorm_fwd(h, nw, name=f"{tag}_norm")
    zx = mm_nn(hn, p["w_zx_t"], out_dtype=BF16, b_rows_are_n=True, name=f"{tag}_proj")
    dt4 = mm_nn(hn, p["w_dt_t"], out_dtype=F32, b_rows_are_n=True, name=f"{tag}_proj_dt")
    xc = dwconv_fwd(zx, p["conv_w"], p["conv_b"], width=5, glu=False, silu=True, col0=e // CONV_TC, name=f"{tag}_conv")
    y0, st0 = ssd_scan_fwd(xc, dt4, p["bias4"], p["alog4"], rev=False, name=f"{tag}_scan_f")
    y, st1 = ssd_scan_fwd(xc, dt4, p["bias4"], p["alog4"], rev=True, prev=y0, dvec=p["dvec"], name=f"{tag}_scan_b")
    yn = ssd_gate_fwd(y, zx, p["norm_w"], name=f"{tag}_gate")
    h2 = mm_nn(yn, p["w_out"], out_dtype=F32, res=h, name=f"{tag}_out")
    return h2, (h, hn, zx, dt4, xc, st0, st1, y, yn)


def ssd_layer_bwd(dh, saved, nw, p, tag):
    e = GROUPS * GW
    h, hn, zx, dt4, xc, st0, st1, y, yn = saved
    dyn = mm_nn(dh, p["w_out_t"], out_dtype=BF16, name=f"{tag}_d_yn")
    dw_out = mm_tn(yn, dh, name=f"{tag}_dw_out")
    dy, dzx, dnorm_w = ssd_gate_bwd(dyn, y, zx, p["norm_w"], name=f"{tag}_d_gate")
    dxc0, ddt0, dalog0, dbias0 = ssd_scan_bwd(xc, dt4, p["bias4"], p["alog4"], dy, st0, rev=False,
                                              name=f"{tag}_d_scan_f")
    dxc, ddt4, dalog1, dbias1, ddvec = ssd_scan_bwd(xc, dt4, p["bias4"], p["alog4"], dy, st1, rev=True,
                                                    prev=(dxc0, ddt0), dvec=p["dvec"], name=f"{tag}_d_scan_b")
    dzx, dconv_w, dconv_b = dwconv_bwd(dxc, zx, p["conv_w"], p["conv_b"], dzx, width=5, glu=False, silu=True,
                                       col0=e // CONV_TC, dcol0=e // CONV_TC, name=f"{tag}_d_conv")
    dhn = mm_nn(dzx, p["w_zx_t"], out_dtype=F32, a2=ddt4, b2=p["w_dt_t"], name=f"{tag}_d_hn")
    dw_zx_t = mm_tn(dzx, hn, name=f"{tag}_dw_zx")
    dw_dt_t = mm_tn(ddt4, hn, name=f"{tag}_dw_dt")
    dh_prev, dnw = rmsnorm_bwd(dhn, h, nw, dh, name=f"{tag}_d_norm")
    dw_in_t = jnp.concatenate([dw_zx_t[:e], _xbc_rows_inv(dw_zx_t[e:]), _dt_rows_inv(dw_dt_t)], axis=0)
    grads = dict(w_in_t=dw_in_t, conv_w=_xbc_cols_inv(dconv_w[:5]), conv_b=_xbc_cols_inv(dconv_b)[0],
                 dt_bias=_dt_cols_inv(dbias0 + dbias1).reshape(2, HEADS),
                 a_log=_dt_cols_inv(dalog0 + dalog1).reshape(2, HEADS),
                 d_skip=jnp.sum(ddvec.reshape(HEADS, HEADDIM), axis=-1), norm_w=dnorm_w[0], w_out=dw_out,
                 norm=dnw[0])
    return dh_prev, grads


ANY = pl.BlockSpec(memory_space=pl.ANY)


def _place():
    return lax.axis_index("x"), lax.axis_index("y"), lax.axis_index("c")


def gather_chips(bufs, *, name):
    n = len(bufs)

    def body(*refs):
        ins, outs = refs[:n], refs[n:2 * n]
        send_sems, recv_sems, local_sems = refs[2 * n:]
        x, y, c = _place()
        k_me = 2 * x + y
        chips = [(1 - x, y), (x, 1 - y), (1 - x, 1 - y)]
        copies = []
        for t in range(n):
            own = pltpu.make_async_copy(ins[t], outs[t].at[k_me], local_sems.at[t])
            own.start()
            copies.append(own)
            for j, (px, py) in enumerate(chips):
                cp = pltpu.make_async_remote_copy(
                    src_ref=ins[t], dst_ref=outs[t].at[k_me], send_sem=send_sems.at[3 * t + j],
                    recv_sem=recv_sems.at[3 * t + j], device_id=(px, py, c), device_id_type=MESH)
                cp.start()
                copies.append(cp)
        for cp in copies:
            cp.wait()

    return pl.pallas_call(
        body, name=name, in_specs=[ANY] * n, out_specs=tuple([ANY] * n),
        out_shape=tuple(jax.ShapeDtypeStruct((N_CHIPS,) + b.shape, b.dtype) for b in bufs),
        scratch_shapes=[pltpu.SemaphoreType.DMA((3 * n,)), pltpu.SemaphoreType.DMA((3 * n,)),
                        pltpu.SemaphoreType.DMA((n,))],
    )(*bufs)


def swap_other_half(g2, *, name):
    def body(g_ref, o_ref, send_sem, recv_sem):
        x, y, c = _place()
        cp = pltpu.make_async_remote_copy(src_ref=g_ref.at[1 - c], dst_ref=o_ref, send_sem=send_sem, recv_sem=recv_sem,
                                          device_id=(x, y, 1 - c), device_id_type=MESH)
        cp.start()
        cp.wait()

    return pl.pallas_call(
        body, name=name, in_specs=[ANY], out_specs=ANY, out_shape=jax.ShapeDtypeStruct(g2.shape[1:], g2.dtype),
        scratch_shapes=[pltpu.SemaphoreType.DMA, pltpu.SemaphoreType.DMA],
    )(g2)


def exchange_chips(p, *, name):
    def body(p_ref, o_ref, send_sems, recv_sems, local_sem):
        x, y, c = _place()
        k_me = 2 * x + y
        own = pltpu.make_async_copy(p_ref.at[k_me], o_ref.at[k_me], local_sem)
        own.start()
        copies = [own]
        for j, (px, py) in enumerate([(1 - x, y), (x, 1 - y), (1 - x, 1 - y)]):
            cp = pltpu.make_async_remote_copy(
                src_ref=p_ref.at[2 * px + py], dst_ref=o_ref.at[k_me], send_sem=send_sems.at[j],
                recv_sem=recv_sems.at[j], device_id=(px, py, c), device_id_type=MESH)
            cp.start()
            copies.append(cp)
        for cp in copies:
            cp.wait()

    return pl.pallas_call(
        body, name=name, in_specs=[ANY], out_specs=ANY, out_shape=jax.ShapeDtypeStruct(p.shape, p.dtype),
        scratch_shapes=[pltpu.SemaphoreType.DMA((3,)), pltpu.SemaphoreType.DMA((3,)), pltpu.SemaphoreType.DMA],
    )(p)


def share_half(full, *, name):
    def body(_, f_ref, send_sem, recv_sem):
        x, y, c = _place()
        cp = pltpu.make_async_remote_copy(src_ref=f_ref.at[c], dst_ref=f_ref.at[c], send_sem=send_sem,
                                          recv_sem=recv_sem, device_id=(x, y, 1 - c), device_id_type=MESH)
        cp.start()
        cp.wait()

    return pl.pallas_call(
        body, name=name, in_specs=[ANY], out_specs=ANY, out_shape=jax.ShapeDtypeStruct(full.shape, full.dtype),
        input_output_aliases={0: 0},
        scratch_shapes=[pltpu.SemaphoreType.DMA, pltpu.SemaphoreType.DMA],
    )(full)


def gather_all(v, *, name):
    def body(v_ref, o_ref, send_sems, recv_sems, local_sem):
        x, y, c = _place()
        me = 4 * x + 2 * y + c
        own = pltpu.make_async_copy(v_ref, o_ref.at[me], local_sem)
        own.start()
        copies = [own]
        idx = 0
        for fx in (0, 1):
            for fy in (0, 1):
                for fc in (0, 1):
                    if not (fx or fy or fc):
                        continue
                    peer = (1 - x if fx else x, 1 - y if fy else y, 1 - c if fc else c)
                    cp = pltpu.make_async_remote_copy(src_ref=v_ref, dst_ref=o_ref.at[me], send_sem=send_sems.at[idx],
                                                      recv_sem=recv_sems.at[idx], device_id=peer, device_id_type=MESH)
                    cp.start()
                    copies.append(cp)
                    idx += 1
        for cp in copies:
            cp.wait()

    return pl.pallas_call(
        body, name=name, in_specs=[ANY], out_specs=ANY, out_shape=jax.ShapeDtypeStruct((N_DEV,) + v.shape, v.dtype),
        scratch_shapes=[pltpu.SemaphoreType.DMA((N_DEV - 1,)), pltpu.SemaphoreType.DMA((N_DEV - 1,)),
                        pltpu.SemaphoreType.DMA],
    )(v)


RED_TR = 432


def pair_sum(g2, recv, cidx, *, name):
    _, K, R, C = g2.shape
    tr = _pick(R, (RED_TR, 8))

    def body(c_ref, a_ref, b_ref, o_ref):
        o_ref[...] = (a_ref[0] + b_ref[...]).astype(BF16)

    blk = pl.BlockSpec((1, tr, C), lambda k, i, c: (k, i, 0))
    return pl.pallas_call(
        body, name=name,
        grid_spec=pltpu.PrefetchScalarGridSpec(
            num_scalar_prefetch=1, grid=(K, R // tr),
            in_specs=[pl.BlockSpec((1, 1, tr, C), lambda k, i, c: (c[0], k, i, 0)), blk], out_specs=blk),
        out_shape=jax.ShapeDtypeStruct((K, R, C), BF16), compiler_params=_params(2),
    )(cidx, g2, recv)


def sum_lead(a, *, name, slot=None, nslots=1):
    K, R, C = a.shape
    tr = _pick(R, (RED_TR, 8))

    def body(s_ref, a_ref, o_ref):
        acc = a_ref[0].astype(F32)
        for k in range(1, K):
            acc = acc + a_ref[k].astype(F32)
        o_ref[0] = acc

    if slot is None:
        slot = jnp.zeros((1,), jnp.int32)
    return pl.pallas_call(
        body, name=name,
        grid_spec=pltpu.PrefetchScalarGridSpec(
            num_scalar_prefetch=1, grid=(R // tr,),
            in_specs=[pl.BlockSpec((K, tr, C), lambda i, s: (0, i, 0))],
            out_specs=pl.BlockSpec((1, tr, C), lambda i, s: (s[0], i, 0))),
        out_shape=jax.ShapeDtypeStruct((nslots, R, C), F32), compiler_params=_params(1),
    )(slot, a)


def adamw(g, w, m, v, *, name):
    R, C = w.shape
    tr = _pick(R, (256, 128, 64, 32, 16, 8))

    def body(g_ref, w_ref, m_ref, v_ref, d_ref, nm_ref, nv_ref):
        gv = g_ref[...]
        m_new = ADAM_B1 * m_ref[...] + (1.0 - ADAM_B1) * gv
        v_new = ADAM_B2 * v_ref[...] + (1.0 - ADAM_B2) * (gv * gv)
        m_hat = m_new / (1.0 - ADAM_B1 ** ADAM_STEP)
        v_hat = v_new / (1.0 - ADAM_B2 ** ADAM_STEP)
        d_ref[...] = -ADAM_LR * (m_hat / (jnp.sqrt(v_hat) + ADAM_EPS) + ADAM_WD * w_ref[...])
        nm_ref[...] = m_new
        nv_ref[...] = v_new

    blk = pl.BlockSpec((tr, C), lambda i: (i, 0))
    sds = jax.ShapeDtypeStruct((R, C), F32)
    return pl.pallas_call(
        body, name=name, grid=(R // tr,), in_specs=[blk] * 4, out_specs=(blk,) * 3, out_shape=(sds,) * 3,
        compiler_params=_params(1),
    )(g, w, m, v)


WEIGHTS = ("norm_w", "final_norm_w", "cm_w_in", "cm_dw_w", "cm_dw_b", "cm_ln_w", "cm_ln_b", "cm_w_out", "ssd_w_in",
           "ssd_conv_w", "ssd_conv_b", "ssd_dt_bias", "ssd_A_log", "ssd_D", "ssd_norm_w", "ssd_w_out")
BIG = (("cm_w_in", 2), ("cm_w_out", 1), ("ssd_w_in", 1), ("ssd_w_out", 1))
TRANSPOSED = ("ssd_w_in",)
SMALL_SHARDED = (("cm_dw_w", 2), ("ssd_conv_w", 2), ("ssd_conv_b", 1), ("ssd_norm_w", 1))
REPLICATED = ("norm_w", "final_norm_w", "cm_dw_b", "cm_ln_w", "cm_ln_b", "ssd_dt_bias", "ssd_A_log", "ssd_D")
ROW = 1024


def _to_shards(g, axis):
    n = g.shape[axis]
    s = g.reshape(g.shape[:axis] + (N_CHIPS, n // N_CHIPS) + g.shape[axis + 1:])
    return jnp.moveaxis(s, axis, 0).reshape(N_CHIPS, -1)


def _from_shards(x4, local_shape, axis):
    local_shape = tuple(local_shape)
    s = jnp.moveaxis(x4.reshape((N_CHIPS,) + local_shape), 0, axis)
    return s.reshape(local_shape[:axis] + (N_CHIPS * local_shape[axis],) + local_shape[axis + 1:])


def _flat_pad(parts, multiple):
    v = jnp.concatenate([p.reshape(-1) for p in parts])
    return jnp.pad(v, (0, (-v.size) % multiple))


def _split(flat, like, names):
    out, off = {}, 0
    for n in names:
        out[n] = flat[off:off + like[n].size].reshape(like[n].shape)
        off += like[n].size
    return out


def kernel(x, norm_w, final_norm_w, cm_w_in, cm_dw_w, cm_dw_b, cm_ln_w, cm_ln_b, cm_w_out, ssd_w_in, ssd_conv_w, ssd_conv_b, ssd_dt_bias, ssd_A_log, ssd_D, ssd_norm_w, ssd_w_out, loss_target, m_norm_w, m_final_norm_w, m_cm_w_in, m_cm_dw_w, m_cm_dw_b, m_cm_ln_w, m_cm_ln_b, m_cm_w_out, m_ssd_w_in, m_ssd_conv_w, m_ssd_conv_b, m_ssd_dt_bias, m_ssd_A_log, m_ssd_D, m_ssd_norm_w, m_ssd_w_out, v_norm_w, v_final_norm_w, v_cm_w_in, v_cm_dw_w, v_cm_dw_b, v_cm_ln_w, v_cm_ln_b, v_cm_w_out, v_ssd_w_in, v_ssd_conv_w, v_ssd_conv_b, v_ssd_dt_bias, v_ssd_A_log, v_ssd_D, v_ssd_norm_w, v_ssd_w_out):
    a = dict(locals())
    w = {n: a[n] for n in WEIGHTS}
    m = {n: a["m_" + n] for n in WEIGHTS}
    v = {n: a["v_" + n] for n in WEIGHTS}
    _, T, D = x.shape
    cidx = lax.axis_index("c").astype(jnp.int32).reshape(1)
    big_names = [n for n, _ in BIG]
    small_names = [n for n, _ in SMALL_SHARDED]

    wx = {n: (jnp.swapaxes(w[n], 1, 2) if n in TRANSPOSED else w[n]) for n in big_names + small_names}
    big = _flat_pad([wx[n] for n in big_names], 16 * ROW).astype(BF16).reshape(-1, ROW)
    small = _flat_pad([wx[n] for n in small_names], 8 * ROW).reshape(-1, ROW)
    g_big, g_small = gather_chips([big, small], name="gather_weights")
    g_big, g_small = g_big.reshape(N_CHIPS, -1), g_small.reshape(N_CHIPS, -1)
    full, off = {}, 0
    for n, ax in BIG:
        full[n] = _from_shards(g_big[:, off:off + wx[n].size], wx[n].shape, ax)
        off += wx[n].size
    off = 0
    for n, ax in SMALL_SHARDED:
        full[n] = _from_shards(g_small[:, off:off + wx[n].size], wx[n].shape, ax)
        off += wx[n].size
    n_layers = norm_w.shape[0]
    lw = []
    for i in range(n_layers):
        j = i // 2
        if i % 2 == 0:
            lw.append(conf_weights(full["cm_w_in"][j], full["cm_dw_w"][j], cm_dw_b[j], cm_ln_w[j], cm_ln_b[j],
                                   full["cm_w_out"][j]))
        else:
            lw.append(ssd_weights(full["ssd_w_in"][j], full["ssd_conv_w"][j], full["ssd_conv_b"][j], ssd_dt_bias[j],
                                  ssd_A_log[j], ssd_D[j], full["ssd_norm_w"][j], full["ssd_w_out"][j]))

    h = x[0]
    saved = []
    for i in range(n_layers):
        fwd = conf_layer_fwd if i % 2 == 0 else ssd_layer_fwd
        h, s = fwd(h, norm_w[i].reshape(1, -1), lw[i], f"l{i}")
        saved.append(s)
    dh, loss_local, d_final = loss_head(h, loss_target[0], final_norm_w.reshape(1, -1), name="loss_head")
    lg = [None] * n_layers
    for i in reversed(range(n_layers)):
        bwd = conf_layer_bwd if i % 2 == 0 else ssd_layer_bwd
        dh, lg[i] = bwd(dh, saved[i], norm_w[i].reshape(1, -1), lw[i], f"l{i}")
    conf_g, ssd_g = lg[0::2], lg[1::2]
    local = {
        "norm_w": jnp.stack([g["norm"] for g in lg]), "final_norm_w": d_final[0],
        "cm_w_in": jnp.stack([g["w_in"] for g in conf_g]), "cm_dw_w": jnp.stack([g["dw_w"] for g in conf_g]),
        "cm_dw_b": jnp.stack([g["dw_b"] for g in conf_g]), "cm_ln_w": jnp.stack([g["ln_w"] for g in conf_g]),
        "cm_ln_b": jnp.stack([g["ln_b"] for g in conf_g]), "cm_w_out": jnp.stack([g["w_out"] for g in conf_g]),
        "ssd_w_in": jnp.stack([g["w_in_t"] for g in ssd_g]), "ssd_conv_w": jnp.stack([g["conv_w"] for g in ssd_g]),
        "ssd_conv_b": jnp.stack([g["conv_b"] for g in ssd_g]), "ssd_dt_bias": jnp.stack([g["dt_bias"] for g in ssd_g]),
        "ssd_A_log": jnp.stack([g["a_log"] for g in ssd_g]), "ssd_D": jnp.stack([g["d_skip"] for g in ssd_g]),
        "ssd_norm_w": jnp.stack([g["norm_w"] for g in ssd_g]), "ssd_w_out": jnp.stack([g["w_out"] for g in ssd_g]),
    }

    flat4 = jnp.concatenate([_to_shards(local[n], ax) for n, ax in BIG + SMALL_SHARDED], axis=1)
    flat4 = jnp.pad(flat4, ((0, 0), (0, (-flat4.shape[1]) % (2 * RED_TR * ROW))))
    g2 = jnp.swapaxes(flat4.reshape(N_CHIPS, 2, -1, ROW), 0, 1)
    theirs = swap_other_half(g2, name="grad_pair_swap")
    part = pair_sum(g2, theirs, cidx, name="grad_pair_sum")
    got = exchange_chips(part, name="grad_chip_exchange")
    half = sum_lead(got, slot=cidx, nslots=2, name="grad_chip_sum")
    shard_flat = share_half(half, name="grad_pair_share").reshape(-1)
    grads = _split(shard_flat, wx, big_names + small_names)
    for n in TRANSPOSED:
        grads[n] = jnp.swapaxes(grads[n], 1, 2)

    rep = _flat_pad([local[n] for n in REPLICATED], 8 * LANES).reshape(-1, LANES)
    rep_sum = sum_lead(gather_all(rep, name="grad_small_gather"), name="grad_small_sum")
    grads.update(_split(rep_sum.reshape(-1), w, REPLICATED))

    delta, new_m, new_v = {}, {}, {}
    for n in big_names:
        two_d = (-1, w[n].shape[-1])
        d_, m_, v_ = adamw(grads[n].reshape(two_d), w[n].reshape(two_d), m[n].reshape(two_d), v[n].reshape(two_d),
                           name="adamw_" + n)
        delta[n], new_m[n], new_v[n] = d_.reshape(w[n].shape), m_.reshape(w[n].shape), v_.reshape(w[n].shape)
    rest = list(REPLICATED) + small_names
    packed = [_flat_pad([t[n] for n in rest], 8 * LANES).reshape(-1, LANES) for t in (grads, w, m, v)]
    for out, res in zip((delta, new_m, new_v), adamw(*packed, name="adamw_small")):
        out.update(_split(res.reshape(-1), w, rest))

    loss = lax.psum(loss_local[0, 0], ("x", "y", "c"))
    return (loss, dh.reshape(x.shape), *[grads[n] for n in WEIGHTS], *[delta[n] for n in WEIGHTS],
            *[new_m[n] for n in WEIGHTS], *[new_v[n] for n in WEIGHTS])
```

```python
import jax
import jax.numpy as jnp
from jax import lax
from jax.experimental import pallas as pl
from jax.experimental.pallas import tpu as pltpu

F32 = jnp.float32
BF16 = jnp.bfloat16
MESH = pl.DeviceIdType.MESH

EPS = 1e-5
HEADDIM = 64
HEADS = 32
GROUPS = 4
HPG = HEADS // GROUPS
D_STATE = 128
CHUNK = 128
GW = HPG * HEADDIM
XCG = GW + 2 * D_STATE
HALO = 16
LANES = 128
N_CHIPS = 4
N_DEV = 8

ADAM_LR = 0.001
ADAM_B1 = 0.9
ADAM_B2 = 0.999
ADAM_EPS = 1e-08
ADAM_WD = 0.01
ADAM_STEP = 10

VMEM_LIMIT = 52 * 1024 * 1024


def _params(n_axes):
    return pltpu.CompilerParams(dimension_semantics=("arbitrary",) * n_axes, vmem_limit_bytes=VMEM_LIMIT)


def _sigmoid(x):
    return 1.0 / (1.0 + jnp.exp(-x))


def _softplus(x):
    return jnp.maximum(x, 0.0) + jnp.log(1.0 + jnp.exp(-jnp.abs(x)))


def _dot(a, b):
    return jnp.dot(a, b, preferred_element_type=F32)


def _dot_nt(a, b):
    return lax.dot_general(a, b, (((1,), (1,)), ((), ())), preferred_element_type=F32)


def _dot_tn(a, b):
    return lax.dot_general(a, b, (((0,), (0,)), ((), ())), preferred_element_type=F32)


def _pick(n, pref):
    for t in pref:
        if n % t == 0:
            return t
    return n


def mm_nn(a, b, *, out_dtype, name, res=None, a2=None, b2=None, b_rows_are_n=False, norm_bwd=None):
    M, K = a.shape
    N = b.shape[0] if b_rows_are_n else b.shape[1]
    has2, has_res, has_nb = a2 is not None, res is not None, norm_bwd is not None
    tm = _pick(M, (512, 256, 128) if has_nb else (1024, 512, 256, 128))
    tn = N if has_nb else _pick(N, (1024, 512, 256, 128))
    tk = _pick(K, (2048, 1024, 512, 256, 128) if a.dtype == BF16 else (1024, 512, 256, 128))
    nk = K // tk

    def body(*refs):
        a_ref, b_ref = refs[0], refs[1]
        pos = 2
        if has2:
            a2_ref, b2_ref = refs[pos], refs[pos + 1]
            pos += 2
        if has_res:
            r_ref = refs[pos]
            pos += 1
        if has_nb:
            h_ref, w_ref, dh_ref = refs[pos:pos + 3]
            pos += 3
        o_ref = refs[pos]
        acc_ref = refs[-1]
        k = pl.program_id(2)
        first_rows = pl.program_id(0) == 0

        @pl.when(k == 0)
        def _():
            if has2:
                acc_ref[...] = _dot(a2_ref[...].astype(BF16), b2_ref[...])
            else:
                acc_ref[...] = jnp.zeros_like(acc_ref)

        acc_ref[...] += (_dot_nt if b_rows_are_n else _dot)(a_ref[...].astype(BF16), b_ref[...])

        @pl.when(k == nk - 1)
        def _():
            r = acc_ref[...]
            if has_res:
                r = r + r_ref[...]
            if has_nb:
                dw_ref = refs[pos + 1]

                @pl.when(first_rows)
                def _():
                    dw_ref[...] = jnp.zeros_like(dw_ref)

                x = h_ref[...]
                rstd = lax.rsqrt(jnp.mean(x * x, axis=-1, keepdims=True) + EPS)
                xhat = x * rstd
                dxh = r * w_ref[...]
                dw_ref[...] += jnp.sum(r * xhat, axis=0, keepdims=True)
                r = dh_ref[...] + rstd * (dxh - xhat * jnp.mean(dxh * xhat, axis=-1, keepdims=True))
            o_ref[...] = r.astype(out_dtype)

    b_spec = pl.BlockSpec((tn, tk), lambda i, j, k: (j, k)) if b_rows_are_n else pl.BlockSpec((tk, tn), lambda i, j, k: (k, j))
    in_specs = [pl.BlockSpec((tm, tk), lambda i, j, k: (i, k)), b_spec]
    args = [a, b]
    if has2:
        k2 = a2.shape[1]
        in_specs += [pl.BlockSpec((tm, k2), lambda i, j, k: (i, 0)), pl.BlockSpec((k2, tn), lambda i, j, k: (0, j))]
        args += [a2, b2]
    tile = pl.BlockSpec((tm, tn), lambda i, j, k: (i, j))
    if has_res:
        in_specs.append(tile)
        args.append(res)
    out_specs, out_shape = tile, jax.ShapeDtypeStruct((M, N), out_dtype)
    if has_nb:
        vec = pl.BlockSpec((1, N), lambda i, j, k: (0, 0))
        in_specs += [tile, vec, tile]
        args += list(norm_bwd)
        out_specs, out_shape = (tile, vec), (out_shape, jax.ShapeDtypeStruct((1, N), F32))
    return pl.pallas_call(
        body, name=name, grid=(M // tm, N // tn, nk), in_specs=in_specs, out_specs=out_specs, out_shape=out_shape,
        scratch_shapes=[pltpu.VMEM((tm, tn), F32)], compiler_params=_params(3),
    )(*args)


def mm_tn(a, b, *, name):
    T, M = a.shape
    N = b.shape[1]
    tm = _pick(M, (1024, 512, 256, 128))
    tn = _pick(N, (1024, 512, 256, 128))
    tt = _pick(T, (2048, 1024, 512, 256, 128))

    def body(a_ref, b_ref, o_ref):
        @pl.when(pl.program_id(2) == 0)
        def _():
            o_ref[...] = jnp.zeros_like(o_ref)

        o_ref[...] += _dot_tn(a_ref[...].astype(BF16), b_ref[...].astype(BF16))

    return pl.pallas_call(
        body, name=name, grid=(M // tm, N // tn, T // tt),
        in_specs=[pl.BlockSpec((tt, tm), lambda i, j, t: (t, i)), pl.BlockSpec((tt, tn), lambda i, j, t: (t, j))],
        out_specs=pl.BlockSpec((tm, tn), lambda i, j, t: (i, j)),
        out_shape=jax.ShapeDtypeStruct((M, N), F32), compiler_params=_params(3),
    )(a, b)


def rmsnorm_fwd(h, w, *, name):
    T, D = h.shape
    tm = _pick(T, (512, 256, 128))

    def body(h_ref, w_ref, o_ref):
        x = h_ref[...]
        rstd = lax.rsqrt(jnp.mean(x * x, axis=-1, keepdims=True) + EPS)
        o_ref[...] = (x * rstd * w_ref[...]).astype(BF16)

    return pl.pallas_call(
        body, name=name, grid=(T // tm,),
        in_specs=[pl.BlockSpec((tm, D), lambda i: (i, 0)), pl.BlockSpec((1, D), lambda i: (0, 0))],
        out_specs=pl.BlockSpec((tm, D), lambda i: (i, 0)),
        out_shape=jax.ShapeDtypeStruct((T, D), BF16), compiler_params=_params(1),
    )(h, w)


def loss_head(h, target, w, *, name):
    T, D = h.shape
    tm = _pick(T, (512, 256, 128))

    def body(h_ref, t_ref, w_ref, dh_ref, loss_ref, dw_ref):
        @pl.when(pl.program_id(0) == 0)
        def _():
            loss_ref[...] = jnp.zeros_like(loss_ref)
            dw_ref[...] = jnp.zeros_like(dw_ref)

        x = h_ref[...]
        rstd = lax.rsqrt(jnp.mean(x * x, axis=-1, keepdims=True) + EPS)
        xhat = x * rstd
        err = xhat * w_ref[...] - t_ref[...]
        rows = jnp.sum(err * err, axis=-1, keepdims=True)
        loss_ref[...] += (0.5 / D) * jnp.sum(rows, axis=0, keepdims=True)
        dy = err * (1.0 / D)
        dxh = dy * w_ref[...]
        dh_ref[...] = rstd * (dxh - xhat * jnp.mean(dxh * xhat, axis=-1, keepdims=True))
        dw_ref[...] += jnp.sum(dy * xhat, axis=0, keepdims=True)

    row = pl.BlockSpec((tm, D), lambda i: (i, 0))
    vec = pl.BlockSpec((1, D), lambda i: (0, 0))
    return pl.pallas_call(
        body, name=name, grid=(T // tm,), in_specs=[row, row, vec],
        out_specs=(row, pl.BlockSpec((1, 1), lambda i: (0, 0)), vec),
        out_shape=(jax.ShapeDtypeStruct((T, D), F32), jax.ShapeDtypeStruct((1, 1), F32),
                   jax.ShapeDtypeStruct((1, D), F32)),
        compiler_params=_params(1),
    )(h, target, w)


CONV_TM = 512
CONV_TC = 512
CONV_RB = 32


def _conv_specs(T, tm, sw, col0):
    hb = tm // HALO
    last = T // HALO - 1
    main = pl.BlockSpec((tm, sw), lambda j, i: (i, col0 + j))
    prev = pl.BlockSpec((HALO, sw), lambda j, i: (jnp.maximum(i * hb - 1, 0), col0 + j))
    nxt = pl.BlockSpec((HALO, sw), lambda j, i: (jnp.minimum((i + 1) * hb, last), col0 + j))
    return main, prev, nxt


def _conv_input(blk, glu, tc):
    x = blk.astype(F32)
    if glu:
        return x[:, :tc] * _sigmoid(x[:, tc:])
    return x


def _fill_padded(pad_ref, main, prev, nxt, first, last, tm):
    pad_ref[0:HALO, :] = jnp.where(first, 0.0, prev)
    pad_ref[HALO:HALO + tm, :] = main
    pad_ref[HALO + tm:HALO + tm + HALO, :] = jnp.where(last, 0.0, nxt)


SH_ROWS = 24


def _tap_plan(offsets):
    plan = [(o % 8, o - o % 8) for o in offsets]
    return plan, sorted({b for b, _ in plan if b})


def _fill_shifted(sh_ref, pad_ref, shifts, tm):
    for b in shifts:
        sh_ref[b] = pad_ref[b:b + tm + SH_ROWS, :]


def _tap_rows(pad_ref, sh_ref, b, start, rows):
    return pad_ref[start:start + rows, :] if b == 0 else sh_ref[b, start:start + rows, :]


def dwconv_fwd(src, w, b, *, width, glu, silu, col0, name):
    T = src.shape[0]
    C = w.shape[1]
    tm, tc = min(CONV_TM, T), CONV_TC
    sw = 2 * tc if glu else tc
    n_i = T // tm
    p = (width - 1) // 2
    rb = CONV_RB
    plan, shifts = _tap_plan([HALO - p + k for k in range(width)])

    def body(m_ref, p_ref, n_ref, w_ref, b_ref, o_ref, pad_ref, sh_ref):
        i = pl.program_id(1)
        _fill_padded(pad_ref, _conv_input(m_ref[...], glu, tc), _conv_input(p_ref[...], glu, tc),
                     _conv_input(n_ref[...], glu, tc), i == 0, i == n_i - 1, tm)
        _fill_shifted(sh_ref, pad_ref, shifts, tm)
        for r in range(tm // rb):
            acc = jnp.zeros((rb, tc), F32)
            for k, (sb, start) in enumerate(plan):
                acc = acc + _tap_rows(pad_ref, sh_ref, sb, start + r * rb, rb) * w_ref[k:k + 1, :]
            acc = acc + b_ref[...]
            if silu:
                acc = acc * _sigmoid(acc)
            o_ref[r * rb:(r + 1) * rb, :] = acc.astype(BF16)

    main, prev, nxt = _conv_specs(T, tm, sw, col0)
    return pl.pallas_call(
        body, name=name, grid=(C // tc, n_i),
        in_specs=[main, prev, nxt, pl.BlockSpec((w.shape[0], tc), lambda j, i: (0, j)),
                  pl.BlockSpec((1, tc), lambda j, i: (0, j))],
        out_specs=pl.BlockSpec((tm, tc), lambda j, i: (i, j)),
        out_shape=jax.ShapeDtypeStruct((T, C), BF16),
        scratch_shapes=[pltpu.VMEM((tm + 2 * HALO, tc), F32), pltpu.VMEM((8, tm + SH_ROWS, tc), F32)],
        compiler_params=_params(2),
    )(src, src, src, w, b)


def dwconv_bwd(dout, src, w, b, dsrc, *, width, glu, silu, col0, dcol0, name):
    T = src.shape[0]
    C = w.shape[1]
    kp = w.shape[0]
    tm, tc = min(CONV_TM, T), CONV_TC
    sw = 2 * tc if glu else tc
    n_i = T // tm
    p = (width - 1) // 2
    rb = CONV_RB
    edge = 8
    assert p <= edge or not silu
    plan, shifts = _tap_plan([HALO - p + k for k in range(width)])
    dplan, dshifts = _tap_plan([HALO + p - k for k in range(width)])

    def body(dm_ref, dp_ref, dn_ref, m_ref, p_ref, n_ref, w_ref, b_ref, _, o_ref, dw_ref, db_ref, pad_ref, dpre_ref,
             sh_ref, dsh_ref, acc_ref):
        i = pl.program_id(1)

        @pl.when(i == 0)
        def _():
            dw_ref[...] = jnp.zeros_like(dw_ref)
            acc_ref[...] = jnp.zeros_like(acc_ref)

        first, last = i == 0, i == n_i - 1
        _fill_padded(pad_ref, _conv_input(m_ref[...], glu, tc), _conv_input(p_ref[...], glu, tc),
                     _conv_input(n_ref[...], glu, tc), first, last, tm)
        _fill_padded(dpre_ref, dm_ref[...].astype(F32), dp_ref[...].astype(F32), dn_ref[...].astype(F32),
                     first, last, tm)
        _fill_shifted(sh_ref, pad_ref, shifts, tm)
        if silu:
            for r0 in range(HALO - edge, HALO + tm + edge, HALO):
                pre = jnp.zeros((HALO, tc), F32)
                for k, (sb, start) in enumerate(plan):
                    pre = pre + _tap_rows(pad_ref, sh_ref, sb, start + r0 - HALO, HALO) * w_ref[k:k + 1, :]
                pre = pre + b_ref[...]
                s = _sigmoid(pre)
                dpre_ref[r0:r0 + HALO, :] = dpre_ref[r0:r0 + HALO, :] * (s * (1.0 + pre * (1.0 - s)))
        _fill_shifted(dsh_ref, dpre_ref, dshifts, tm)

        for r in range(tm // rb):
            acc = jnp.zeros((rb, tc), F32)
            for k, (sb, start) in enumerate(dplan):
                acc = acc + _tap_rows(dpre_ref, dsh_ref, sb, start + r * rb, rb) * w_ref[k:k + 1, :]
            if glu:
                blk = m_ref[r * rb:(r + 1) * rb, :].astype(F32)
                v, s = blk[:, :tc], _sigmoid(blk[:, tc:])
                o_ref[r * rb:(r + 1) * rb, :tc] = (acc * s).astype(BF16)
                o_ref[r * rb:(r + 1) * rb, tc:] = (acc * v * s * (1.0 - s)).astype(BF16)
            else:
                o_ref[r * rb:(r + 1) * rb, :] = acc.astype(BF16)

        for r in range(tm // rb):
            dblk = dpre_ref[HALO + r * rb:HALO + (r + 1) * rb, :]
            for k, (sb, start) in enumerate(plan):
                prod = dblk * _tap_rows(pad_ref, sh_ref, sb, start + r * rb, rb)
                acc_ref[k] += jnp.sum(prod.reshape(rb // 8, 8, tc), axis=0)
            acc_ref[kp] += jnp.sum(dblk.reshape(rb // 8, 8, tc), axis=0)

        @pl.when(last)
        def _():
            for k in range(width):
                dw_ref[k:k + 1, :] = jnp.sum(acc_ref[k], axis=0, keepdims=True)
            db_ref[...] = jnp.sum(acc_ref[kp], axis=0, keepdims=True)

    dmain_s, dprev_s, dnext_s = _conv_specs(T, tm, tc, 0)
    main, prev, nxt = _conv_specs(T, tm, sw, col0)
    wspec = pl.BlockSpec((kp, tc), lambda j, i: (0, j))
    bspec = pl.BlockSpec((1, tc), lambda j, i: (0, j))
    return pl.pallas_call(
        body, name=name, grid=(C // tc, n_i),
        in_specs=[dmain_s, dprev_s, dnext_s, main, prev, nxt, wspec, bspec, pl.BlockSpec(memory_space=pl.ANY)],
        out_specs=(pl.BlockSpec((tm, sw), lambda j, i: (i, dcol0 + j)), wspec, bspec),
        out_shape=(jax.ShapeDtypeStruct(dsrc.shape, dsrc.dtype), jax.ShapeDtypeStruct((kp, C), F32),
                   jax.ShapeDtypeStruct((1, C), F32)),
        input_output_aliases={8: 0},
        scratch_shapes=[pltpu.VMEM((tm + 2 * HALO, tc), F32), pltpu.VMEM((tm + 2 * HALO, tc), F32),
                        pltpu.VMEM((8, tm + SH_ROWS, tc), F32), pltpu.VMEM((8, tm + SH_ROWS, tc), F32),
                        pltpu.VMEM((kp + 1, 8, tc), F32)],
        compiler_params=_params(2),
    )(dout, dout, dout, src, src, src, w, b, dsrc)


def _silu_grad(x, s):
    return s * (1.0 + x * (1.0 - s))


def conf_ln_fwd(u2, proj, ln_w, ln_b, *, name):
    T, E = u2.shape
    zc = proj.shape[1] // E - 1
    tm = _pick(T, (256, 128))

    def body(u_ref, z_ref, w_ref, b_ref, o_ref):
        x = u_ref[...].astype(F32)
        xc = x - jnp.mean(x, axis=-1, keepdims=True)
        rstd = lax.rsqrt(jnp.mean(xc * xc, axis=-1, keepdims=True) + EPS)
        u3 = xc * rstd * w_ref[...] + b_ref[...]
        z = z_ref[...].astype(F32)
        o_ref[...] = (u3 * _sigmoid(u3) * z * _sigmoid(z)).astype(BF16)

    row = pl.BlockSpec((tm, E), lambda i: (i, 0))
    vec = pl.BlockSpec((1, E), lambda i: (0, 0))
    return pl.pallas_call(
        body, name=name, grid=(T // tm,),
        in_specs=[row, pl.BlockSpec((tm, E), lambda i: (i, zc)), vec, vec], out_specs=row,
        out_shape=jax.ShapeDtypeStruct((T, E), BF16), compiler_params=_params(1),
    )(u2, proj, ln_w, ln_b)


def conf_ln_bwd(du4, u2, proj, ln_w, ln_b, *, name):
    T, E = u2.shape
    ncol = proj.shape[1] // E
    zc = ncol - 1
    tm = _pick(T, (256, 128))

    def body(d_ref, u_ref, z_ref, w_ref, b_ref, du_ref, dz_ref, dw_ref, db_ref):
        @pl.when(pl.program_id(0) == 0)
        def _():
            dw_ref[...] = jnp.zeros_like(dw_ref)
            db_ref[...] = jnp.zeros_like(db_ref)

        x = u_ref[...].astype(F32)
        xc = x - jnp.mean(x, axis=-1, keepdims=True)
        rstd = lax.rsqrt(jnp.mean(xc * xc, axis=-1, keepdims=True) + EPS)
        xhat = xc * rstd
        u3 = xhat * w_ref[...] + b_ref[...]
        z = z_ref[...].astype(F32)
        s3, sz = _sigmoid(u3), _sigmoid(z)
        d4 = d_ref[...].astype(F32)
        du3 = d4 * (z * sz) * _silu_grad(u3, s3)
        dz_ref[...] = (d4 * (u3 * s3) * _silu_grad(z, sz)).astype(BF16)
        dw_ref[...] += jnp.sum(du3 * xhat, axis=0, keepdims=True)
        db_ref[...] += jnp.sum(du3, axis=0, keepdims=True)
        dxh = du3 * w_ref[...]
        du = rstd * (dxh - jnp.mean(dxh, axis=-1, keepdims=True) - xhat * jnp.mean(dxh * xhat, axis=-1, keepdims=True))
        du_ref[...] = du.astype(BF16)

    row = pl.BlockSpec((tm, E), lambda i: (i, 0))
    zrow = pl.BlockSpec((tm, E), lambda i: (i, zc))
    vec = pl.BlockSpec((1, E), lambda i: (0, 0))
    return pl.pallas_call(
        body, name=name, grid=(T // tm,), in_specs=[row, row, zrow, vec, vec], out_specs=(row, zrow, vec, vec),
        out_shape=(jax.ShapeDtypeStruct((T, E), BF16), jax.ShapeDtypeStruct(proj.shape, BF16),
                   jax.ShapeDtypeStruct((1, E), F32), jax.ShapeDtypeStruct((1, E), F32)),
        compiler_params=_params(1),
    )(du4, u2, proj, ln_w, ln_b)


def ssd_gate_fwd(y, zx, norm_w, *, name):
    T, E = y.shape
    tm = _pick(T, (256, 128))

    def body(y_ref, z_ref, w_ref, o_ref):
        z = z_ref[...].astype(F32)
        yz = y_ref[...].astype(F32) * (z * _sigmoid(z))
        rstd = lax.rsqrt(jnp.mean(yz * yz, axis=-1, keepdims=True) + EPS)
        o_ref[...] = (yz * rstd * w_ref[...]).astype(BF16)

    row = pl.BlockSpec((tm, E), lambda i: (i, 0))
    vec = pl.BlockSpec((1, E), lambda i: (0, 0))
    return pl.pallas_call(
        body, name=name, grid=(T // tm,), in_specs=[row, row, vec], out_specs=row,
        out_shape=jax.ShapeDtypeStruct((T, E), BF16), compiler_params=_params(1),
    )(y, zx, norm_w)


def ssd_gate_bwd(dyn, y, zx, norm_w, *, name):
    T, E = y.shape
    tm = _pick(T, (256, 128))

    def body(d_ref, y_ref, z_ref, w_ref, dy_ref, dz_ref, dw_ref):
        @pl.when(pl.program_id(0) == 0)
        def _():
            dw_ref[...] = jnp.zeros_like(dw_ref)

        z = z_ref[...].astype(F32)
        sz = _sigmoid(z)
        gate = z * sz
        yv = y_ref[...].astype(F32)
        yz = yv * gate
        rstd = lax.rsqrt(jnp.mean(yz * yz, axis=-1, keepdims=True) + EPS)
        yhat = yz * rstd
        d = d_ref[...].astype(F32)
        dw_ref[...] += jnp.sum(d * yhat, axis=0, keepdims=True)
        dxh = d * w_ref[...]
        dyz = rstd * (dxh - yhat * jnp.mean(dxh * yhat, axis=-1, keepdims=True))
        dy_ref[...] = (dyz * gate).astype(BF16)
        dz_ref[...] = (dyz * yv * _silu_grad(z, sz)).astype(BF16)

    row = pl.BlockSpec((tm, E), lambda i: (i, 0))
    vec = pl.BlockSpec((1, E), lambda i: (0, 0))
    return pl.pallas_call(
        body, name=name, grid=(T // tm,), in_specs=[row, row, row, vec], out_specs=(row, row, vec),
        out_shape=(jax.ShapeDtypeStruct((T, E), BF16), jax.ShapeDtypeStruct(zx.shape, BF16),
                   jax.ShapeDtypeStruct((1, E), F32)),
        compiler_params=_params(1),
    )(dyn, y, zx, norm_w)


def _cumsum_mm(mask, a):
    hi = a.astype(BF16)
    r1 = a - hi.astype(F32)
    mid = r1.astype(BF16)
    lo = (r1 - mid.astype(F32)).astype(BF16)
    out = _dot(jnp.where(mask, 1.0, 0.0).astype(BF16), jnp.concatenate([hi, mid, lo], axis=1))
    return out[:, :LANES] + out[:, LANES:2 * LANES] + out[:, 2 * LANES:]


def _chunk_terms(xcb, dt_raw, bias, alog, rev):
    L = CHUNK
    xs = xcb[:, :GW].astype(F32)
    Bm = xcb[:, GW:GW + D_STATE]
    Cm = xcb[:, GW + D_STATE:]
    pre = dt_raw + bias
    dt = _softplus(pre)
    A = -jnp.exp(alog)
    row = lax.broadcasted_iota(jnp.int32, (L, L), 0)
    col = lax.broadcasted_iota(jnp.int32, (L, L), 1)
    mask = (col >= row) if rev else (col <= row)
    mask_t = (col <= row) if rev else (col >= row)
    cs = _cumsum_mm(mask, dt * A)
    tot = cs[0:1, :] if rev else cs[L - 1:L, :]
    return xs, Bm, Cm, pre, dt, A, mask, mask_t, cs, cs.T, tot


def _decay(cs, cs_t, ln, mask):
    d = cs[:, ln:ln + 1] - cs_t[ln:ln + 1, :]
    return jnp.where(mask, jnp.exp(jnp.where(mask, d, 0.0)), 0.0)


def _pair(v, ln0, lo):
    return jnp.where(lo[:v.shape[0]], v[:, ln0:ln0 + 1], v[:, ln0 + 1:ln0 + 2])


def _scan_specs(nc, rev_order):
    ci = (lambda c: nc - 1 - c) if rev_order else (lambda c: c)
    xc = pl.BlockSpec((CHUNK, GROUPS * XCG), lambda c: (ci(c), 0))
    dt = pl.BlockSpec((CHUNK, GROUPS * LANES), lambda c: (ci(c), 0))
    vec = pl.BlockSpec((1, GROUPS * LANES), lambda c: (0, 0))
    wide = pl.BlockSpec((CHUNK, GROUPS * GW), lambda c: (ci(c), 0))
    wvec = pl.BlockSpec((1, GROUPS * GW), lambda c: (0, 0))
    st = pl.BlockSpec((1, D_STATE, GROUPS * GW), lambda c: (ci(c), 0, 0))
    return xc, dt, vec, wide, wvec, st


def _cols(ref, g, width):
    return ref.at[:, pl.ds(g * width, width)]


def _head_expand(r):
    row = lax.broadcasted_iota(jnp.int32, (LANES, GW), 0)
    col = lax.broadcasted_iota(jnp.int32, (LANES, GW), 1)
    first = (row - r * HPG) * HEADDIM
    return jnp.where(jnp.logical_and(col >= first, col < first + HEADDIM), 1.0, 0.0).astype(BF16)


def _head_collect(r):
    row = lax.broadcasted_iota(jnp.int32, (GW, LANES), 0)
    first = (lax.broadcasted_iota(jnp.int32, (GW, LANES), 1) - r * HPG) * HEADDIM
    return jnp.where(jnp.logical_and(row >= first, row < first + HEADDIM), 1.0, 0.0).astype(BF16)


def _expand(parts, sel):
    n = parts[0].shape[0]
    out = _dot(jnp.concatenate(parts, axis=0).astype(BF16), sel)
    return [out[i * n:(i + 1) * n] for i in range(len(parts))]


def ssd_scan_fwd(xc, dt4, bias4, alog4, *, rev, name, prev=None, dvec=None):
    T = xc.shape[0]
    nc = T // CHUNK
    E = GROUPS * GW
    r = 1 if rev else 0
    skip = prev is not None

    def one_group(sel, xc_ref, dt_ref, bias_ref, alog_ref, prev_ref, dvec_ref, y_ref, st_ref, s_ref):
        xs, Bm, Cm, _, dt, _, mask, _, cs, cs_t, tot = _chunk_terms(xc_ref[...], dt_ref[...], bias_ref[...],
                                                                   alog_ref[...], rev)
        dtx, ex, dx = _expand([dt, jnp.exp(cs), jnp.exp(tot - cs)], sel)
        et = jnp.exp(tot)
        cb = _dot_nt(Cm, Bm)
        sb = s_ref[...].astype(BF16)
        st_ref[...] = sb
        xp_all = xs * dtx
        y_off = _dot(Cm, sb) * ex
        lo = lax.broadcasted_iota(jnp.int32, (CHUNK, LANES), 1) < HEADDIM
        et_parts = []
        for p in range(HPG // 2):
            ln0 = r * HPG + 2 * p
            sl = slice(p * LANES, (p + 1) * LANES)
            xp = xp_all[:, sl]
            mcat = jnp.concatenate([cb * _decay(cs, cs_t, ln0, mask), cb * _decay(cs, cs_t, ln0 + 1, mask)],
                                   axis=1).astype(BF16)
            xbd = jnp.concatenate([jnp.where(lo, xp, 0.0), jnp.where(lo, 0.0, xp)], axis=0).astype(BF16)
            yp = _dot(mcat, xbd) + y_off[:, sl]
            if skip:
                yp = yp + prev_ref[:, sl].astype(F32) + xs[:, sl] * dvec_ref[:, sl]
            y_ref[:, sl] = yp.astype(BF16)
            et_parts.append(_pair(et, ln0, lo))
        s_ref[...] = s_ref[...] * jnp.concatenate(et_parts, axis=1) + _dot_tn(Bm, (xp_all * dx).astype(BF16))

    def body(*refs):
        xc_ref, dt_ref, bias_ref, alog_ref = refs[:4]
        prev_ref, dvec_ref = (refs[4], refs[5]) if skip else (None, None)
        y_ref, st_ref, s_ref = refs[-3:]

        @pl.when(pl.program_id(0) == 0)
        def _():
            s_ref[...] = jnp.zeros_like(s_ref)

        sel = _head_expand(r)
        for g in range(GROUPS):
            one_group(sel, _cols(xc_ref, g, XCG), _cols(dt_ref, g, LANES), _cols(bias_ref, g, LANES),
                      _cols(alog_ref, g, LANES), _cols(prev_ref, g, GW) if skip else None,
                      _cols(dvec_ref, g, GW) if skip else None, _cols(y_ref, g, GW),
                      st_ref.at[0, :, pl.ds(g * GW, GW)], _cols(s_ref, g, GW))

    s_xc, s_dt, s_vec, s_wide, s_wvec, s_st = _scan_specs(nc, rev)
    in_specs = [s_xc, s_dt, s_vec, s_vec]
    args = [xc, dt4, bias4, alog4]
    if skip:
        in_specs += [s_wide, s_wvec]
        args += [prev, dvec]
    return pl.pallas_call(
        body, name=name, grid=(nc,), in_specs=in_specs, out_specs=(s_wide, s_st),
        out_shape=(jax.ShapeDtypeStruct((T, E), BF16), jax.ShapeDtypeStruct((nc, D_STATE, E), BF16)),
        scratch_shapes=[pltpu.VMEM((D_STATE, E), F32)], compiler_params=_params(1),
    )(*args)


def ssd_scan_bwd(xc, dt4, bias4, alog4, dy, states, *, rev, name, prev=None, dvec=None):
    T = xc.shape[0]
    nc = T // CHUNK
    E = GROUPS * GW
    L = CHUNK
    r = 1 if rev else 0
    skip = prev is not None

    def one_group(sel, sel_t, xc_ref, dt_ref, bias_ref, alog_ref, dy_ref, st_ref, pdxc_ref, pddt_ref, dvec_ref,
                  dxc_ref, ddt_ref, dalog_ref, dbias_ref, dd_ref, g_ref):
        xs, Bm, Cm, pre, dt, A, mask, mask_t, cs, cs_t, tot = _chunk_terms(
            xc_ref[...], dt_ref[...], bias_ref[...], alog_ref[...], rev)
        dtx, ex, dx = _expand([dt, jnp.exp(cs), jnp.exp(tot - cs)], sel)
        et = jnp.exp(tot)
        cb = _dot_nt(Cm, Bm)
        s_in = st_ref[...]
        dy_all = dy_ref[...].astype(F32)
        g_f = g_ref[...]
        g_b = g_f.astype(BF16)
        xp_all = xs * dtx
        dye_all = dy_all * ex
        bgd = _dot(Bm, g_b) * dx
        lane = lax.broadcasted_iota(jnp.int32, (L, LANES), 1)
        lo = lane < HEADDIM
        dcb = jnp.zeros((L, L), F32)
        yd_parts, dxd_parts, et_parts = [], [], []
        for p in range(HPG // 2):
            ln0 = r * HPG + 2 * p
            sl = slice(p * LANES, (p + 1) * LANES)
            xp, dy_p = xp_all[:, sl], dy_all[:, sl]
            lam0, lam1 = _decay(cs, cs_t, ln0, mask), _decay(cs, cs_t, ln0 + 1, mask)
            m0, m1 = (cb * lam0).astype(BF16), (cb * lam1).astype(BF16)
            dybd = jnp.concatenate([jnp.where(lo, dy_p, 0.0), jnp.where(lo, 0.0, dy_p)], axis=0).astype(BF16)
            xbd = jnp.concatenate([jnp.where(lo, xp, 0.0), jnp.where(lo, 0.0, xp)], axis=0).astype(BF16)
            dm = _dot_nt(dybd, xp.astype(BF16))
            dcb = dcb + dm[:L] * lam0 + dm[L:] * lam1
            yd_parts.append(_dot(jnp.concatenate([m0, m1], axis=1), xbd))
            dxd_parts.append(_dot_tn(jnp.concatenate([m0, m1], axis=0), dybd))
            et_parts.append(_pair(et, ln0, lo))
        y_diag = jnp.concatenate(yd_parts, axis=1)
        dx_diag = jnp.concatenate(dxd_parts, axis=1)
        etx = jnp.concatenate(et_parts, axis=1)
        dxt = dx_diag + bgd
        w2 = xp_all * bgd
        dy_r, xp_r = dy_all.astype(BF16).astype(F32), xp_all.astype(BF16).astype(F32)
        u = dye_all * _dot(Cm, s_in) + dy_r * y_diag - xp_r * dx_diag - w2
        dxx = dxt * xs
        tail = jnp.broadcast_to(jnp.sum(w2, axis=0, keepdims=True)
                                + jnp.sum(g_f * s_in.astype(F32), axis=0, keepdims=True) * etx, (8, GW))
        u_hi, t_hi = u.astype(BF16), tail.astype(BF16)
        red = _dot(jnp.concatenate([u_hi, (u - u_hi.astype(F32)).astype(BF16), dxx.astype(BF16), t_hi,
                                    (tail - t_hi.astype(F32)).astype(BF16)], axis=0), sel_t)
        dcs = red[:L] + red[L:2 * L]
        ddt = red[2 * L:3 * L]
        dtot = red[3 * L:3 * L + 1] + red[3 * L + 8:3 * L + 9]
        dxs = dxt * dtx
        if skip:
            dxs = dxs + dy_all * dvec_ref[...] + pdxc_ref[:, :GW].astype(F32)
            dd_ref[...] += jnp.sum(dy_all * xs, axis=0, keepdims=True)
        dxc_ref[:, :GW] = dxs.astype(BF16)
        dye_b = dye_all.astype(BF16)
        xd = (xp_all * dx).astype(BF16)
        dcb_b = dcb.astype(BF16)
        d_b = _dot_nt(xd, g_b) + _dot_tn(dcb_b, Cm)
        d_c = _dot_nt(dye_b, s_in) + _dot(dcb_b, Bm)
        if skip:
            d_b = d_b + pdxc_ref[:, GW:GW + D_STATE].astype(F32)
            d_c = d_c + pdxc_ref[:, GW + D_STATE:].astype(F32)
        dxc_ref[:, GW:GW + D_STATE] = d_b.astype(BF16)
        dxc_ref[:, GW + D_STATE:] = d_c.astype(BF16)
        g_ref[...] = g_f * etx + _dot_tn(Cm, dye_b)
        rowi = lax.broadcasted_iota(jnp.int32, (L, LANES), 0)
        da = _cumsum_mm(mask_t, dcs + jnp.where(rowi == (0 if rev else L - 1), dtot, 0.0))
        keep = jnp.logical_and(lane >= r * HPG, lane < (r + 1) * HPG)
        ddr = jnp.where(keep, (da * A + ddt) * _sigmoid(pre), 0.0)
        dbias_ref[...] += jnp.sum(ddr, axis=0, keepdims=True)
        dalog_ref[...] += jnp.sum(jnp.where(keep, da * dt * A, 0.0), axis=0, keepdims=True)
        if skip:
            ddr = ddr + pddt_ref[...]
        ddt_ref[...] = ddr

    def body(*refs):
        xc_ref, dt_ref, bias_ref, alog_ref, dy_ref, st_ref = refs[:6]
        pdxc_ref, pddt_ref, dvec_ref = refs[6:9] if skip else (None, None, None)
        pos = 9 if skip else 6
        dxc_ref, ddt_ref, dalog_ref, dbias_ref = refs[pos:pos + 4]
        dd_ref = refs[pos + 4] if skip else None
        g_ref = refs[-1]

        @pl.when(pl.program_id(0) == 0)
        def _():
            g_ref[...] = jnp.zeros_like(g_ref)
            dalog_ref[...] = jnp.zeros_like(dalog_ref)
            dbias_ref[...] = jnp.zeros_like(dbias_ref)
            if skip:
                dd_ref[...] = jnp.zeros_like(dd_ref)

        sel, sel_t = _head_expand(r), _head_collect(r)
        for g in range(GROUPS):
            one_group(sel, sel_t, _cols(xc_ref, g, XCG), _cols(dt_ref, g, LANES), _cols(bias_ref, g, LANES),
                      _cols(alog_ref, g, LANES), _cols(dy_ref, g, GW), st_ref.at[0, :, pl.ds(g * GW, GW)],
                      _cols(pdxc_ref, g, XCG) if skip else None, _cols(pddt_ref, g, LANES) if skip else None,
                      _cols(dvec_ref, g, GW) if skip else None, _cols(dxc_ref, g, XCG), _cols(ddt_ref, g, LANES),
                      _cols(dalog_ref, g, LANES), _cols(dbias_ref, g, LANES),
                      _cols(dd_ref, g, GW) if skip else None, _cols(g_ref, g, GW))

    s_xc, s_dt, s_vec, s_wide, s_wvec, s_st = _scan_specs(nc, not rev)
    in_specs = [s_xc, s_dt, s_vec, s_vec, s_wide, s_st]
    args = [xc, dt4, bias4, alog4, dy, states]
    out_specs = [s_xc, s_dt, s_vec, s_vec]
    out_shape = [jax.ShapeDtypeStruct((T, GROUPS * XCG), BF16), jax.ShapeDtypeStruct((T, GROUPS * LANES), F32),
                 jax.ShapeDtypeStruct((1, GROUPS * LANES), F32), jax.ShapeDtypeStruct((1, GROUPS * LANES), F32)]
    if skip:
        in_specs += [s_xc, s_dt, s_wvec]
        args += [prev[0], prev[1], dvec]
        out_specs.append(s_wvec)
        out_shape.append(jax.ShapeDtypeStruct((1, E), F32))
    return pl.pallas_call(
        body, name=name, grid=(nc,), in_specs=in_specs, out_specs=tuple(out_specs),
        out_shape=tuple(out_shape), scratch_shapes=[pltpu.VMEM((D_STATE, E), F32)], compiler_params=_params(1),
    )(*args)


def _conf_cols(w):
    e = w.shape[-1] // 3
    lead = w.shape[:-1]
    vg = w[..., :2 * e].reshape(*lead, 2, e // CONV_TC, CONV_TC)
    vg = jnp.swapaxes(vg, -3, -2).reshape(*lead, 2 * e)
    return jnp.concatenate([vg, w[..., 2 * e:]], axis=-1)


def _conf_cols_inv(w):
    e = w.shape[-1] // 3
    lead = w.shape[:-1]
    vg = w[..., :2 * e].reshape(*lead, e // CONV_TC, 2, CONV_TC)
    vg = jnp.swapaxes(vg, -3, -2).reshape(*lead, 2 * e)
    return jnp.concatenate([vg, w[..., 2 * e:]], axis=-1)


def _xbc_cols(w):
    lead = w.shape[:-1]
    e = GROUPS * GW
    gn = GROUPS * D_STATE
    parts = [w[..., :e].reshape(*lead, GROUPS, GW), w[..., e:e + gn].reshape(*lead, GROUPS, D_STATE),
             w[..., e + gn:].reshape(*lead, GROUPS, D_STATE)]
    return jnp.concatenate(parts, axis=-1).reshape(*lead, GROUPS * XCG)


def _xbc_cols_inv(w):
    lead = w.shape[:-1]
    g = w.reshape(*lead, GROUPS, XCG)
    parts = [g[..., :GW].reshape(*lead, GROUPS * GW), g[..., GW:GW + D_STATE].reshape(*lead, GROUPS * D_STATE),
             g[..., GW + D_STATE:].reshape(*lead, GROUPS * D_STATE)]
    return jnp.concatenate(parts, axis=-1)


def _dt_cols(w):
    lead = w.shape[:-1]
    t = jnp.swapaxes(w.reshape(*lead, 2, GROUPS, HPG), -3, -2).reshape(*lead, GROUPS, 2 * HPG)
    pad = [(0, 0)] * (t.ndim - 1) + [(0, LANES - 2 * HPG)]
    return jnp.pad(t, pad).reshape(*lead, GROUPS * LANES)


def _dt_cols_inv(w):
    lead = w.shape[:-1]
    t = w.reshape(*lead, GROUPS, LANES)[..., :2 * HPG].reshape(*lead, GROUPS, 2, HPG)
    return jnp.swapaxes(t, -3, -2).reshape(*lead, 2 * HEADS)


def _pad_rows(w, rows):
    return jnp.pad(w, ((0, rows - w.shape[0]), (0, 0)))


def conf_weights(w_in, dw_w, dw_b, ln_w, ln_b, w_out):
    w_in_p = _conf_cols(w_in)
    return dict(w_in=w_in_p, w_in_t=w_in_p.T, w_out=w_out, w_out_t=w_out.T,
                dw_w=_pad_rows(dw_w, 32), dw_b=dw_b.reshape(1, -1), ln_w=ln_w.reshape(1, -1), ln_b=ln_b.reshape(1, -1))


def _xbc_rows(w):
    e, gn, c = GROUPS * GW, GROUPS * D_STATE, w.shape[1]
    parts = [w[:e].reshape(GROUPS, GW, c), w[e:e + gn].reshape(GROUPS, D_STATE, c),
             w[e + gn:].reshape(GROUPS, D_STATE, c)]
    return jnp.concatenate(parts, axis=1).reshape(GROUPS * XCG, c)


def _xbc_rows_inv(w):
    c = w.shape[1]
    g = w.reshape(GROUPS, XCG, c)
    parts = [g[:, :GW].reshape(GROUPS * GW, c), g[:, GW:GW + D_STATE].reshape(GROUPS * D_STATE, c),
             g[:, GW + D_STATE:].reshape(GROUPS * D_STATE, c)]
    return jnp.concatenate(parts, axis=0)


def _dt_rows(w):
    c = w.shape[1]
    t = jnp.swapaxes(w.reshape(2, GROUPS, HPG, c), 0, 1).reshape(GROUPS, 2 * HPG, c)
    return jnp.pad(t, ((0, 0), (0, LANES - 2 * HPG), (0, 0))).reshape(GROUPS * LANES, c)


def _dt_rows_inv(w):
    c = w.shape[1]
    t = w.reshape(GROUPS, LANES, c)[:, :2 * HPG].reshape(GROUPS, 2, HPG, c)
    return jnp.swapaxes(t, 0, 1).reshape(2 * HEADS, c)


def ssd_weights(w_in_t, conv_w, conv_b, dt_bias, a_log, d_skip, norm_w, w_out):
    e = GROUPS * GW
    xbc = e + 2 * GROUPS * D_STATE
    w_zx_t = jnp.concatenate([w_in_t[:e], _xbc_rows(w_in_t[e:e + xbc])], axis=0)
    return dict(w_zx_t=w_zx_t, w_dt_t=_dt_rows(w_in_t[e + xbc:]), w_out=w_out, w_out_t=w_out.T,
                conv_w=_pad_rows(_xbc_cols(conv_w), 8), conv_b=_xbc_cols(conv_b.reshape(1, -1)),
                bias4=_dt_cols(dt_bias.reshape(1, -1)), alog4=_dt_cols(a_log.reshape(1, -1)),
                dvec=jnp.repeat(d_skip, HEADDIM).reshape(1, -1), norm_w=norm_w.reshape(1, -1))


def conf_layer_fwd(h, nw, p, tag):
    hn = rmsnorm_fwd(h, nw, name=f"{tag}_norm")
    proj = mm_nn(hn, p["w_in"], out_dtype=BF16, name=f"{tag}_proj")
    u2 = dwconv_fwd(proj, p["dw_w"], p["dw_b"], width=31, glu=True, silu=False, col0=0, name=f"{tag}_conv")
    u4 = conf_ln_fwd(u2, proj, p["ln_w"], p["ln_b"], name=f"{tag}_ln")
    h2 = mm_nn(u4, p["w_out"], out_dtype=F32, res=h, name=f"{tag}_out")
    return h2, (h, hn, proj, u2, u4)


def conf_layer_bwd(dh, saved, nw, p, tag):
    h, hn, proj, u2, u4 = saved
    du4 = mm_nn(dh, p["w_out_t"], out_dtype=BF16, name=f"{tag}_d_u4")
    dw_out = mm_tn(u4, dh, name=f"{tag}_dw_out")
    du2, dproj, dln_w, dln_b = conf_ln_bwd(du4, u2, proj, p["ln_w"], p["ln_b"], name=f"{tag}_d_ln")
    dproj, ddw_w, ddw_b = dwconv_bwd(du2, proj, p["dw_w"], p["dw_b"], dproj, width=31, glu=True, silu=False,
                                     col0=0, dcol0=0, name=f"{tag}_d_conv")
    dh_prev, dnw = mm_nn(dproj, p["w_in_t"], out_dtype=F32, norm_bwd=(h, nw, dh), name=f"{tag}_d_hn")
    dw_in = mm_tn(hn, dproj, name=f"{tag}_dw_in")
    grads = dict(w_in=_conf_cols_inv(dw_in), dw_w=ddw_w[:31], dw_b=ddw_b[0], ln_w=dln_w[0], ln_b=dln_b[0],
                 w_out=dw_out, norm=dnw[0])
    return dh_prev, grads


def ssd_layer_fwd(h, nw, p, tag):
    e = GROUPS * GW
    hn = rmsnorm_fwd(h, nw, name=f"{tag}_norm")
    zx = mm_nn(hn, p["w_zx_t"], out_dtype=BF16, b_rows_are_n=True, name=f"{tag}_proj")
    dt4 = mm_nn(hn, p["w_dt_t"], out_dtype=F32, b_rows_are_n=True, name=f"{tag}_proj_dt")
    xc = dwconv_fwd(zx, p["conv_w"], p["conv_b"], width=5, glu=False, silu=True, col0=e // CONV_TC, name=f"{tag}_conv")
    y0, st0 = ssd_scan_fwd(xc, dt4, p["bias4"], p["alog4"], rev=False, name=f"{tag}_scan_f")
    y, st1 = ssd_scan_fwd(xc, dt4, p["bias4"], p["alog4"], rev=True, prev=y0, dvec=p["dvec"], name=f"{tag}_scan_b")
    yn = ssd_gate_fwd(y, zx, p["norm_w"], name=f"{tag}_gate")
    h2 = mm_nn(yn, p["w_out"], out_dtype=F32, res=h, name=f"{tag}_out")
    return h2, (h, hn, zx, dt4, xc, st0, st1, y, yn)


def ssd_layer_bwd(dh, saved, nw, p, tag):
    e = GROUPS * GW
    h, hn, zx, dt4, xc, st0, st1, y, yn = saved
    dyn = mm_nn(dh, p["w_out_t"], out_dtype=BF16, name=f"{tag}_d_yn")
    dw_out = mm_tn(yn, dh, name=f"{tag}_dw_out")
    dy, dzx, dnorm_w = ssd_gate_bwd(dyn, y, zx, p["norm_w"], name=f"{tag}_d_gate")
    dxc0, ddt0, dalog0, dbias0 = ssd_scan_bwd(xc, dt4, p["bias4"], p["alog4"], dy, st0, rev=False,
                                              name=f"{tag}_d_scan_f")
    dxc, ddt4, dalog1, dbias1, ddvec = ssd_scan_bwd(xc, dt4, p["bias4"], p["alog4"], dy, st1, rev=True,
                                                    prev=(dxc0, ddt0), dvec=p["dvec"], name=f"{tag}_d_scan_b")
    dzx, dconv_w, dconv_b = dwconv_bwd(dxc, zx, p["conv_w"], p["conv_b"], dzx, width=5, glu=False, silu=True,
                                       col0=e // CONV_TC, dcol0=e // CONV_TC, name=f"{tag}_d_conv")
    dh_prev, dnw = mm_nn(dzx, p["w_zx_t"], out_dtype=F32, a2=ddt4, b2=p["w_dt_t"], norm_bwd=(h, nw, dh),
                         name=f"{tag}_d_hn")
    dw_zx_t = mm_tn(dzx, hn, name=f"{tag}_dw_zx")
    dw_dt_t = mm_tn(ddt4, hn, name=f"{tag}_dw_dt")
    dw_in_t = jnp.concatenate([dw_zx_t[:e], _xbc_rows_inv(dw_zx_t[e:]), _dt_rows_inv(dw_dt_t)], axis=0)
    grads = dict(w_in_t=dw_in_t, conv_w=_xbc_cols_inv(dconv_w[:5]), conv_b=_xbc_cols_inv(dconv_b)[0],
                 dt_bias=_dt_cols_inv(dbias0 + dbias1).reshape(2, HEADS),
                 a_log=_dt_cols_inv(dalog0 + dalog1).reshape(2, HEADS),
                 d_skip=jnp.sum(ddvec.reshape(HEADS, HEADDIM), axis=-1), norm_w=dnorm_w[0], w_out=dw_out,
                 norm=dnw[0])
    return dh_prev, grads


ANY = pl.BlockSpec(memory_space=pl.ANY)


def _place():
    return lax.axis_index("x"), lax.axis_index("y"), lax.axis_index("c")


def gather_chips(bufs, *, name):
    n = len(bufs)

    def body(*refs):
        ins, outs = refs[:n], refs[n:2 * n]
        ici_send, ici_recv, d2d_send, d2d_recv, local_sems = refs[2 * n:]
        x, y, c = _place()
        k_me = 2 * x + y
        chips = [(1 - x, y), (x, 1 - y), (1 - x, 1 - y)]
        waits = []
        for t in range(n):
            half = bufs[t].shape[0] // 2
            mine = pl.ds(pl.multiple_of(c * half, 8), half)
            own = pltpu.make_async_copy(ins[t], outs[t].at[k_me], local_sems.at[t])
            own.start()
            waits.append(own.wait)
            for j, (px, py) in enumerate(chips):
                out = pltpu.make_async_remote_copy(
                    src_ref=ins[t].at[mine], dst_ref=outs[t].at[k_me, mine], send_sem=ici_send.at[3 * t + j],
                    recv_sem=ici_recv.at[3 * t + j], device_id=(px, py, c), device_id_type=MESH)
                out.start()
                waits.append(out.wait_send)
        for t in range(n):
            half = bufs[t].shape[0] // 2
            mine = pl.ds(pl.multiple_of(c * half, 8), half)
            for j, (px, py) in enumerate(chips):
                landed = outs[t].at[2 * px + py, mine]
                pltpu.make_async_remote_copy(
                    src_ref=ins[t].at[mine], dst_ref=landed, send_sem=ici_send.at[3 * t + j],
                    recv_sem=ici_recv.at[3 * t + j], device_id=(px, py, c), device_id_type=MESH).wait_recv()
                passed = pltpu.make_async_remote_copy(
                    src_ref=landed, dst_ref=landed, send_sem=d2d_send.at[3 * t + j], recv_sem=d2d_recv.at[3 * t + j],
                    device_id=(x, y, 1 - c), device_id_type=MESH)
                passed.start()
                waits.append(passed.wait)
        for w in waits:
            w()

    sems = [pltpu.SemaphoreType.DMA((3 * n,))] * 4 + [pltpu.SemaphoreType.DMA((n,))]
    return pl.pallas_call(
        body, name=name, in_specs=[ANY] * n, out_specs=tuple([ANY] * n),
        out_shape=tuple(jax.ShapeDtypeStruct((N_CHIPS,) + b.shape, b.dtype) for b in bufs),
        scratch_shapes=sems,
    )(*bufs)


def swap_other_half(g2, *, name):
    def body(g_ref, o_ref, send_sem, recv_sem):
        x, y, c = _place()
        cp = pltpu.make_async_remote_copy(src_ref=g_ref.at[1 - c], dst_ref=o_ref, send_sem=send_sem, recv_sem=recv_sem,
                                          device_id=(x, y, 1 - c), device_id_type=MESH)
        cp.start()
        cp.wait()

    return pl.pallas_call(
        body, name=name, in_specs=[ANY], out_specs=ANY, out_shape=jax.ShapeDtypeStruct(g2.shape[1:], g2.dtype),
        scratch_shapes=[pltpu.SemaphoreType.DMA, pltpu.SemaphoreType.DMA],
    )(g2)


def exchange_chips(p, *, name):
    def body(p_ref, o_ref, send_sems, recv_sems, local_sem):
        x, y, c = _place()
        k_me = 2 * x + y
        own = pltpu.make_async_copy(p_ref.at[k_me], o_ref.at[k_me], local_sem)
        own.start()
        copies = [own]
        for j, (px, py) in enumerate([(1 - x, y), (x, 1 - y), (1 - x, 1 - y)]):
            cp = pltpu.make_async_remote_copy(
                src_ref=p_ref.at[2 * px + py], dst_ref=o_ref.at[k_me], send_sem=send_sems.at[j],
                recv_sem=recv_sems.at[j], device_id=(px, py, c), device_id_type=MESH)
            cp.start()
            copies.append(cp)
        for cp in copies:
            cp.wait()

    return pl.pallas_call(
        body, name=name, in_specs=[ANY], out_specs=ANY, out_shape=jax.ShapeDtypeStruct(p.shape, p.dtype),
        scratch_shapes=[pltpu.SemaphoreType.DMA((3,)), pltpu.SemaphoreType.DMA((3,)), pltpu.SemaphoreType.DMA],
    )(p)


def share_half(full, *, name):
    def body(_, f_ref, send_sem, recv_sem):
        x, y, c = _place()
        cp = pltpu.make_async_remote_copy(src_ref=f_ref.at[c], dst_ref=f_ref.at[c], send_sem=send_sem,
                                          recv_sem=recv_sem, device_id=(x, y, 1 - c), device_id_type=MESH)
        cp.start()
        cp.wait()

    return pl.pallas_call(
        body, name=name, in_specs=[ANY], out_specs=ANY, out_shape=jax.ShapeDtypeStruct(full.shape, full.dtype),
        input_output_aliases={0: 0},
        scratch_shapes=[pltpu.SemaphoreType.DMA, pltpu.SemaphoreType.DMA],
    )(full)


def gather_all(v, *, name):
    def body(v_ref, o_ref, send_sems, recv_sems, local_sem):
        x, y, c = _place()
        me = 4 * x + 2 * y + c
        own = pltpu.make_async_copy(v_ref, o_ref.at[me], local_sem)
        own.start()
        copies = [own]
        idx = 0
        for fx in (0, 1):
            for fy in (0, 1):
                for fc in (0, 1):
                    if not (fx or fy or fc):
                        continue
                    peer = (1 - x if fx else x, 1 - y if fy else y, 1 - c if fc else c)
                    cp = pltpu.make_async_remote_copy(src_ref=v_ref, dst_ref=o_ref.at[me], send_sem=send_sems.at[idx],
                                                      recv_sem=recv_sems.at[idx], device_id=peer, device_id_type=MESH)
                    cp.start()
                    copies.append(cp)
                    idx += 1
        for cp in copies:
            cp.wait()

    return pl.pallas_call(
        body, name=name, in_specs=[ANY], out_specs=ANY, out_shape=jax.ShapeDtypeStruct((N_DEV,) + v.shape, v.dtype),
        scratch_shapes=[pltpu.SemaphoreType.DMA((N_DEV - 1,)), pltpu.SemaphoreType.DMA((N_DEV - 1,)),
                        pltpu.SemaphoreType.DMA],
    )(v)


RED_TR = 432


def pair_sum(g2, recv, cidx, *, name):
    _, K, R, C = g2.shape
    tr = _pick(R, (RED_TR, 8))

    def body(c_ref, a_ref, b_ref, o_ref):
        o_ref[...] = (a_ref[0] + b_ref[...]).astype(BF16)

    blk = pl.BlockSpec((1, tr, C), lambda k, i, c: (k, i, 0))
    return pl.pallas_call(
        body, name=name,
        grid_spec=pltpu.PrefetchScalarGridSpec(
            num_scalar_prefetch=1, grid=(K, R // tr),
            in_specs=[pl.BlockSpec((1, 1, tr, C), lambda k, i, c: (c[0], k, i, 0)), blk], out_specs=blk),
        out_shape=jax.ShapeDtypeStruct((K, R, C), BF16), compiler_params=_params(2),
    )(cidx, g2, recv)


def sum_lead(a, *, name, slot=None, nslots=1):
    K, R, C = a.shape
    tr = _pick(R, (RED_TR, 8))

    def body(s_ref, a_ref, o_ref):
        acc = a_ref[0].astype(F32)
        for k in range(1, K):
            acc = acc + a_ref[k].astype(F32)
        o_ref[0] = acc

    if slot is None:
        slot = jnp.zeros((1,), jnp.int32)
    return pl.pallas_call(
        body, name=name,
        grid_spec=pltpu.PrefetchScalarGridSpec(
            num_scalar_prefetch=1, grid=(R // tr,),
            in_specs=[pl.BlockSpec((K, tr, C), lambda i, s: (0, i, 0))],
            out_specs=pl.BlockSpec((1, tr, C), lambda i, s: (s[0], i, 0))),
        out_shape=jax.ShapeDtypeStruct((nslots, R, C), F32), compiler_params=_params(1),
    )(slot, a)


def adamw(g, w, m, v, *, name):
    R, C = w.shape
    tr = _pick(R, (256, 128, 64, 32, 16, 8))

    def body(g_ref, w_ref, m_ref, v_ref, d_ref, nm_ref, nv_ref):
        gv = g_ref[...]
        m_new = ADAM_B1 * m_ref[...] + (1.0 - ADAM_B1) * gv
        v_new = ADAM_B2 * v_ref[...] + (1.0 - ADAM_B2) * (gv * gv)
        m_hat = m_new / (1.0 - ADAM_B1 ** ADAM_STEP)
        v_hat = v_new / (1.0 - ADAM_B2 ** ADAM_STEP)
        d_ref[...] = -ADAM_LR * (m_hat / (jnp.sqrt(v_hat) + ADAM_EPS) + ADAM_WD * w_ref[...])
        nm_ref[...] = m_new
        nv_ref[...] = v_new

    blk = pl.BlockSpec((tr, C), lambda i: (i, 0))
    sds = jax.ShapeDtypeStruct((R, C), F32)
    return pl.pallas_call(
        body, name=name, grid=(R // tr,), in_specs=[blk] * 4, out_specs=(blk,) * 3, out_shape=(sds,) * 3,
        compiler_params=_params(1),
    )(g, w, m, v)


WEIGHTS = ("norm_w", "final_norm_w", "cm_w_in", "cm_dw_w", "cm_dw_b", "cm_ln_w", "cm_ln_b", "cm_w_out", "ssd_w_in",
           "ssd_conv_w", "ssd_conv_b", "ssd_dt_bias", "ssd_A_log", "ssd_D", "ssd_norm_w", "ssd_w_out")
BIG = (("cm_w_in", 2), ("cm_w_out", 1), ("ssd_w_in", 1), ("ssd_w_out", 1))
TRANSPOSED = ("ssd_w_in",)
SMALL_SHARDED = (("cm_dw_w", 2), ("ssd_conv_w", 2), ("ssd_conv_b", 1), ("ssd_norm_w", 1))
REPLICATED = ("norm_w", "final_norm_w", "cm_dw_b", "cm_ln_w", "cm_ln_b", "ssd_dt_bias", "ssd_A_log", "ssd_D")
ROW = 1024


def _to_shards(g, axis):
    n = g.shape[axis]
    s = g.reshape(g.shape[:axis] + (N_CHIPS, n // N_CHIPS) + g.shape[axis + 1:])
    return jnp.moveaxis(s, axis, 0).reshape(N_CHIPS, -1)


def _from_shards(x4, local_shape, axis):
    local_shape = tuple(local_shape)
    s = jnp.moveaxis(x4.reshape((N_CHIPS,) + local_shape), 0, axis)
    return s.reshape(local_shape[:axis] + (N_CHIPS * local_shape[axis],) + local_shape[axis + 1:])


def _flat_pad(parts, multiple):
    v = jnp.concatenate([p.reshape(-1) for p in parts])
    return jnp.pad(v, (0, (-v.size) % multiple))


def _split(flat, like, names):
    out, off = {}, 0
    for n in names:
        out[n] = flat[off:off + like[n].size].reshape(like[n].shape)
        off += like[n].size
    return out


def kernel(x, norm_w, final_norm_w, cm_w_in, cm_dw_w, cm_dw_b, cm_ln_w, cm_ln_b, cm_w_out, ssd_w_in, ssd_conv_w, ssd_conv_b, ssd_dt_bias, ssd_A_log, ssd_D, ssd_norm_w, ssd_w_out, loss_target, m_norm_w, m_final_norm_w, m_cm_w_in, m_cm_dw_w, m_cm_dw_b, m_cm_ln_w, m_cm_ln_b, m_cm_w_out, m_ssd_w_in, m_ssd_conv_w, m_ssd_conv_b, m_ssd_dt_bias, m_ssd_A_log, m_ssd_D, m_ssd_norm_w, m_ssd_w_out, v_norm_w, v_final_norm_w, v_cm_w_in, v_cm_dw_w, v_cm_dw_b, v_cm_ln_w, v_cm_ln_b, v_cm_w_out, v_ssd_w_in, v_ssd_conv_w, v_ssd_conv_b, v_ssd_dt_bias, v_ssd_A_log, v_ssd_D, v_ssd_norm_w, v_ssd_w_out):
    a = dict(locals())
    w = {n: a[n] for n in WEIGHTS}
    m = {n: a["m_" + n] for n in WEIGHTS}
    v = {n: a["v_" + n] for n in WEIGHTS}
    _, T, D = x.shape
    cidx = lax.axis_index("c").astype(jnp.int32).reshape(1)
    big_names = [n for n, _ in BIG]
    small_names = [n for n, _ in SMALL_SHARDED]

    wx = {n: (jnp.swapaxes(w[n], 1, 2) if n in TRANSPOSED else w[n]) for n in big_names + small_names}
    big = _flat_pad([wx[n] for n in big_names], 16 * ROW).astype(BF16).reshape(-1, ROW)
    small = _flat_pad([wx[n] for n in small_names], 8 * ROW).reshape(-1, ROW)
    g_big, g_small = gather_chips([big, small], name="gather_weights")
    g_big, g_small = g_big.reshape(N_CHIPS, -1), g_small.reshape(N_CHIPS, -1)
    full, off = {}, 0
    for n, ax in BIG:
        full[n] = _from_shards(g_big[:, off:off + wx[n].size], wx[n].shape, ax)
        off += wx[n].size
    off = 0
    for n, ax in SMALL_SHARDED:
        full[n] = _from_shards(g_small[:, off:off + wx[n].size], wx[n].shape, ax)
        off += wx[n].size
    n_layers = norm_w.shape[0]
    lw = []
    for i in range(n_layers):
        j = i // 2
        if i % 2 == 0:
            lw.append(conf_weights(full["cm_w_in"][j], full["cm_dw_w"][j], cm_dw_b[j], cm_ln_w[j], cm_ln_b[j],
                                   full["cm_w_out"][j]))
        else:
            lw.append(ssd_weights(full["ssd_w_in"][j], full["ssd_conv_w"][j], full["ssd_conv_b"][j], ssd_dt_bias[j],
                                  ssd_A_log[j], ssd_D[j], full["ssd_norm_w"][j], full["ssd_w_out"][j]))

    h = x[0]
    saved = []
    for i in range(n_layers):
        fwd = conf_layer_fwd if i % 2 == 0 else ssd_layer_fwd
        h, s = fwd(h, norm_w[i].reshape(1, -1), lw[i], f"l{i}")
        saved.append(s)
    dh, loss_local, d_final = loss_head(h, loss_target[0], final_norm_w.reshape(1, -1), name="loss_head")
    lg = [None] * n_layers
    for i in reversed(range(n_layers)):
        bwd = conf_layer_bwd if i % 2 == 0 else ssd_layer_bwd
        dh, lg[i] = bwd(dh, saved[i], norm_w[i].reshape(1, -1), lw[i], f"l{i}")
    conf_g, ssd_g = lg[0::2], lg[1::2]
    local = {
        "norm_w": jnp.stack([g["norm"] for g in lg]), "final_norm_w": d_final[0],
        "cm_w_in": jnp.stack([g["w_in"] for g in conf_g]), "cm_dw_w": jnp.stack([g["dw_w"] for g in conf_g]),
        "cm_dw_b": jnp.stack([g["dw_b"] for g in conf_g]), "cm_ln_w": jnp.stack([g["ln_w"] for g in conf_g]),
        "cm_ln_b": jnp.stack([g["ln_b"] for g in conf_g]), "cm_w_out": jnp.stack([g["w_out"] for g in conf_g]),
        "ssd_w_in": jnp.stack([g["w_in_t"] for g in ssd_g]), "ssd_conv_w": jnp.stack([g["conv_w"] for g in ssd_g]),
        "ssd_conv_b": jnp.stack([g["conv_b"] for g in ssd_g]), "ssd_dt_bias": jnp.stack([g["dt_bias"] for g in ssd_g]),
        "ssd_A_log": jnp.stack([g["a_log"] for g in ssd_g]), "ssd_D": jnp.stack([g["d_skip"] for g in ssd_g]),
        "ssd_norm_w": jnp.stack([g["norm_w"] for g in ssd_g]), "ssd_w_out": jnp.stack([g["w_out"] for g in ssd_g]),
    }

    flat4 = jnp.concatenate([_to_shards(local[n], ax) for n, ax in BIG + SMALL_SHARDED], axis=1)
    flat4 = jnp.pad(flat4, ((0, 0), (0, (-flat4.shape[1]) % (2 * RED_TR * ROW))))
    g2 = jnp.swapaxes(flat4.reshape(N_CHIPS, 2, -1, ROW), 0, 1)
    theirs = swap_other_half(g2, name="grad_pair_swap")
    part = pair_sum(g2, theirs, cidx, name="grad_pair_sum")
    got = exchange_chips(part, name="grad_chip_exchange")
    half = sum_lead(got, slot=cidx, nslots=2, name="grad_chip_sum")
    shard_flat = share_half(half, name="grad_pair_share").reshape(-1)
    grads = _split(shard_flat, wx, big_names + small_names)
    for n in TRANSPOSED:
        grads[n] = jnp.swapaxes(grads[n], 1, 2)

    rep = _flat_pad([local[n] for n in REPLICATED], 8 * LANES).reshape(-1, LANES)
    rep_sum = sum_lead(gather_all(rep, name="grad_small_gather"), name="grad_small_sum")
    grads.update(_split(rep_sum.reshape(-1), w, REPLICATED))

    delta, new_m, new_v = {}, {}, {}
    for n in big_names:
        two_d = (-1, w[n].shape[-1])
        d_, m_, v_ = adamw(grads[n].reshape(two_d), w[n].reshape(two_d), m[n].reshape(two_d), v[n].reshape(two_d),
                           name="adamw_" + n)
        delta[n], new_m[n], new_v[n] = d_.reshape(w[n].shape), m_.reshape(w[n].shape), v_.reshape(w[n].shape)
    rest = list(REPLICATED) + small_names
    packed = [_flat_pad([t[n] for n in rest], 8 * LANES).reshape(-1, LANES) for t in (grads, w, m, v)]
    for out, res in zip((delta, new_m, new_v), adamw(*packed, name="adamw_small")):
        out.update(_split(res.reshape(-1), w, rest))

    loss = lax.psum(loss_local[0, 0], ("x", "y", "c"))
    return (loss, dh.reshape(x.shape), *[grads[n] for n in WEIGHTS], *[delta[n] for n in WEIGHTS],
            *[new_m[n] for n in WEIGHTS], *[new_v[n] for n in WEIGHTS])
```

```python
import math

import jax
import jax.numpy as jnp
from jax import lax
from jax.experimental import pallas as pl
from jax.experimental.pallas import tpu as pltpu

F32 = jnp.float32
BF16 = jnp.bfloat16
MESH = pl.DeviceIdType.MESH

EPS = 1e-5
HEADDIM = 64
HEADS = 32
GROUPS = 4
HPG = HEADS // GROUPS
D_STATE = 128
CHUNK = 128
GW = HPG * HEADDIM
XCG = GW + 2 * D_STATE
HALO = 16
LANES = 128
N_CHIPS = 4
N_DEV = 8

ADAM_LR = 0.001
ADAM_B1 = 0.9
ADAM_B2 = 0.999
ADAM_EPS = 1e-08
ADAM_WD = 0.01
ADAM_STEP = 10

VMEM_LIMIT = 52 * 1024 * 1024


def _params(n_axes):
    return pltpu.CompilerParams(dimension_semantics=("arbitrary",) * n_axes, vmem_limit_bytes=VMEM_LIMIT)


def _sigmoid(x):
    return 1.0 / (1.0 + jnp.exp(-x))


def _softplus(x):
    return jnp.maximum(x, 0.0) + jnp.log(1.0 + jnp.exp(-jnp.abs(x)))


def _dot(a, b):
    return jnp.dot(a, b, preferred_element_type=F32)


def _dot_nt(a, b):
    return lax.dot_general(a, b, (((1,), (1,)), ((), ())), preferred_element_type=F32)


def _dot_tn(a, b):
    return lax.dot_general(a, b, (((0,), (0,)), ((), ())), preferred_element_type=F32)


def _pick(n, pref):
    for t in pref:
        if n % t == 0:
            return t
    return n


def mm_nn(a, b, *, out_dtype, name, res=None, a2=None, b2=None, b_rows_are_n=False, norm_bwd=None):
    M, K = a.shape
    N = b.shape[0] if b_rows_are_n else b.shape[1]
    has2, has_res, has_nb = a2 is not None, res is not None, norm_bwd is not None
    tm = _pick(M, (1024, 512, 256, 128))
    tn = N if has_nb else _pick(N, (1024, 512, 256, 128))
    tk = _pick(K, (2048, 1024, 512, 256, 128) if a.dtype == BF16 and not has_nb else (1024, 512, 256, 128))
    nk = K // tk

    def body(*refs):
        a_ref, b_ref = refs[0], refs[1]
        pos = 2
        if has2:
            a2_ref, b2_ref = refs[pos], refs[pos + 1]
            pos += 2
        if has_res:
            r_ref = refs[pos]
            pos += 1
        if has_nb:
            h_ref, w_ref, dh_ref = refs[pos:pos + 3]
            pos += 3
        o_ref = refs[pos]
        acc_ref = refs[-1]
        k = pl.program_id(2)
        first_rows = pl.program_id(0) == 0

        @pl.when(k == 0)
        def _():
            if has2:
                acc_ref[...] = _dot(a2_ref[...].astype(BF16), b2_ref[...])
            else:
                acc_ref[...] = jnp.zeros_like(acc_ref)

        acc_ref[...] += (_dot_nt if b_rows_are_n else _dot)(a_ref[...].astype(BF16), b_ref[...])

        @pl.when(k == nk - 1)
        def _():
            r = acc_ref[...]
            if has_res:
                r = r + r_ref[...]
            if has_nb:
                dw_ref = refs[pos + 1]

                @pl.when(first_rows)
                def _():
                    dw_ref[...] = jnp.zeros_like(dw_ref)

                x = h_ref[...]
                rstd = lax.rsqrt(jnp.mean(x * x, axis=-1, keepdims=True) + EPS)
                xhat = x * rstd
                dxh = r * w_ref[...]
                dw_ref[...] += jnp.sum(r * xhat, axis=0, keepdims=True)
                r = dh_ref[...] + rstd * (dxh - xhat * jnp.mean(dxh * xhat, axis=-1, keepdims=True))
            o_ref[...] = r.astype(out_dtype)

    b_spec = pl.BlockSpec((tn, tk), lambda i, j, k: (j, k)) if b_rows_are_n else pl.BlockSpec((tk, tn), lambda i, j, k: (k, j))
    in_specs = [pl.BlockSpec((tm, tk), lambda i, j, k: (i, k)), b_spec]
    args = [a, b]
    if has2:
        k2 = a2.shape[1]
        in_specs += [pl.BlockSpec((tm, k2), lambda i, j, k: (i, 0)), pl.BlockSpec((k2, tn), lambda i, j, k: (0, j))]
        args += [a2, b2]
    tile = pl.BlockSpec((tm, tn), lambda i, j, k: (i, j))
    if has_res:
        in_specs.append(tile)
        args.append(res)
    out_specs, out_shape = tile, jax.ShapeDtypeStruct((M, N), out_dtype)
    if has_nb:
        vec = pl.BlockSpec((1, N), lambda i, j, k: (0, 0))
        in_specs += [tile, vec, tile]
        args += list(norm_bwd)
        out_specs, out_shape = (tile, vec), (out_shape, jax.ShapeDtypeStruct((1, N), F32))
    return pl.pallas_call(
        body, name=name, grid=(M // tm, N // tn, nk), in_specs=in_specs, out_specs=out_specs, out_shape=out_shape,
        scratch_shapes=[pltpu.VMEM((tm, tn), F32)], compiler_params=_params(3),
    )(*args)


def mm_tn(a, b, *, name):
    T, M = a.shape
    N = b.shape[1]
    tm = _pick(M, (1024, 512, 256, 128))
    tn = _pick(N, (1024, 512, 256, 128))
    tt = _pick(T, (2048, 1024, 512, 256, 128))

    def body(a_ref, b_ref, o_ref):
        @pl.when(pl.program_id(2) == 0)
        def _():
            o_ref[...] = jnp.zeros_like(o_ref)

        o_ref[...] += _dot_tn(a_ref[...].astype(BF16), b_ref[...].astype(BF16))

    return pl.pallas_call(
        body, name=name, grid=(M // tm, N // tn, T // tt),
        in_specs=[pl.BlockSpec((tt, tm), lambda i, j, t: (t, i)), pl.BlockSpec((tt, tn), lambda i, j, t: (t, j))],
        out_specs=pl.BlockSpec((tm, tn), lambda i, j, t: (i, j)),
        out_shape=jax.ShapeDtypeStruct((M, N), F32), compiler_params=_params(3),
    )(a, b)


def rmsnorm_fwd(h, w, *, name):
    T, D = h.shape
    tm = _pick(T, (512, 256, 128))

    def body(h_ref, w_ref, o_ref):
        x = h_ref[...]
        rstd = lax.rsqrt(jnp.mean(x * x, axis=-1, keepdims=True) + EPS)
        o_ref[...] = (x * rstd * w_ref[...]).astype(BF16)

    return pl.pallas_call(
        body, name=name, grid=(T // tm,),
        in_specs=[pl.BlockSpec((tm, D), lambda i: (i, 0)), pl.BlockSpec((1, D), lambda i: (0, 0))],
        out_specs=pl.BlockSpec((tm, D), lambda i: (i, 0)),
        out_shape=jax.ShapeDtypeStruct((T, D), BF16), compiler_params=_params(1),
    )(h, w)


def loss_head(h, target, w, *, name):
    T, D = h.shape
    tm = _pick(T, (512, 256, 128))

    def body(h_ref, t_ref, w_ref, dh_ref, loss_ref, dw_ref):
        @pl.when(pl.program_id(0) == 0)
        def _():
            loss_ref[...] = jnp.zeros_like(loss_ref)
            dw_ref[...] = jnp.zeros_like(dw_ref)

        x = h_ref[...]
        rstd = lax.rsqrt(jnp.mean(x * x, axis=-1, keepdims=True) + EPS)
        xhat = x * rstd
        err = xhat * w_ref[...] - t_ref[...]
        rows = jnp.sum(err * err, axis=-1, keepdims=True)
        loss_ref[...] += (0.5 / D) * jnp.sum(rows, axis=0, keepdims=True)
        dy = err * (1.0 / D)
        dxh = dy * w_ref[...]
        dh_ref[...] = rstd * (dxh - xhat * jnp.mean(dxh * xhat, axis=-1, keepdims=True))
        dw_ref[...] += jnp.sum(dy * xhat, axis=0, keepdims=True)

    row = pl.BlockSpec((tm, D), lambda i: (i, 0))
    vec = pl.BlockSpec((1, D), lambda i: (0, 0))
    return pl.pallas_call(
        body, name=name, grid=(T // tm,), in_specs=[row, row, vec],
        out_specs=(row, pl.BlockSpec((1, 1), lambda i: (0, 0)), vec),
        out_shape=(jax.ShapeDtypeStruct((T, D), F32), jax.ShapeDtypeStruct((1, 1), F32),
                   jax.ShapeDtypeStruct((1, D), F32)),
        compiler_params=_params(1),
    )(h, target, w)


ANY = pl.BlockSpec(memory_space=pl.ANY)


def _place():
    return lax.axis_index("x"), lax.axis_index("y"), lax.axis_index("c")


def _gather_sems(n):
    return [pltpu.SemaphoreType.DMA((3 * n,))] * 4 + [pltpu.SemaphoreType.DMA((n,))]


def _gather_copies(rows, ins, outs, sems):
    ici_send, ici_recv, d2d_send, d2d_recv, local_sems = sems
    x, y, c = _place()
    k_me = 2 * x + y
    plan = []
    for t in range(len(rows)):
        half = rows[t] // 2
        mine = pl.ds(pl.multiple_of(c * half, 8), half)
        own = pltpu.make_async_copy(ins[t], outs[t].at[k_me], local_sems.at[t])
        sent, passed = [], []
        for j, (px, py) in enumerate([(1 - x, y), (x, 1 - y), (1 - x, 1 - y)]):
            landed = outs[t].at[2 * px + py, mine]
            sent.append(pltpu.make_async_remote_copy(
                src_ref=ins[t].at[mine], dst_ref=outs[t].at[k_me, mine], send_sem=ici_send.at[3 * t + j],
                recv_sem=ici_recv.at[3 * t + j], device_id=(px, py, c), device_id_type=MESH))
            passed.append(pltpu.make_async_remote_copy(
                src_ref=landed, dst_ref=landed, send_sem=d2d_send.at[3 * t + j], recv_sem=d2d_recv.at[3 * t + j],
                device_id=(x, y, 1 - c), device_id_type=MESH))
        plan.append((own, sent, passed))
    return plan


def _gather_start(rows, ins, outs, sems):
    for own, sent, _ in _gather_copies(rows, ins, outs, sems):
        own.start()
        for cp in sent:
            cp.start()


def _gather_finish(rows, ins, outs, sems):
    plan = _gather_copies(rows, ins, outs, sems)
    for _, sent, passed in plan:
        for cp, fwd in zip(sent, passed):
            cp.wait_recv()
            fwd.start()
    for own, sent, passed in plan:
        own.wait()
        for cp, fwd in zip(sent, passed):
            cp.wait_send()
            fwd.wait()


CONV_TM = 512
CONV_TC = 512
CONV_RB = 32


def _conv_specs(T, tm, sw, col0):
    hb = tm // HALO
    last = T // HALO - 1
    main = pl.BlockSpec((tm, sw), lambda j, i: (i, col0 + j))
    prev = pl.BlockSpec((HALO, sw), lambda j, i: (jnp.maximum(i * hb - 1, 0), col0 + j))
    nxt = pl.BlockSpec((HALO, sw), lambda j, i: (jnp.minimum((i + 1) * hb, last), col0 + j))
    return main, prev, nxt


def _conv_input(blk, glu, tc):
    x = blk.astype(F32)
    if glu:
        return x[:, :tc] * _sigmoid(x[:, tc:])
    return x


def _fill_padded(pad_ref, main, prev, nxt, first, last, tm):
    pad_ref[0:HALO, :] = jnp.where(first, 0.0, prev)
    pad_ref[HALO:HALO + tm, :] = main
    pad_ref[HALO + tm:HALO + tm + HALO, :] = jnp.where(last, 0.0, nxt)


SH_ROWS = 24


def _tap_plan(offsets):
    plan = [(o % 8, o - o % 8) for o in offsets]
    return plan, sorted({b for b, _ in plan if b})


def _fill_shifted(sh_ref, pad_ref, shifts, tm):
    for b in shifts:
        sh_ref[b] = pad_ref[b:b + tm + SH_ROWS, :]


def _tap_rows(pad_ref, sh_ref, b, start, rows):
    return pad_ref[start:start + rows, :] if b == 0 else sh_ref[b, start:start + rows, :]


def dwconv_fwd(src, w, b, *, width, glu, silu, col0, name, side=()):
    T = src.shape[0]
    C = w.shape[1]
    tm, tc = min(CONV_TM, T), CONV_TC
    sw = 2 * tc if glu else tc
    n_i = T // tm
    p = (width - 1) // 2
    rb = CONV_RB
    plan, shifts = _tap_plan([HALO - p + k for k in range(width)])

    n_side = len(side)
    side_rows = [t.shape[0] for t in side]

    def body(*refs):
        m_ref, p_ref, n_ref, w_ref, b_ref = refs[:5]
        side_in = refs[5:5 + n_side]
        o_ref = refs[5 + n_side]
        side_out = refs[6 + n_side:6 + 2 * n_side]
        pad_ref, sh_ref = refs[6 + 2 * n_side:8 + 2 * n_side]
        sems = refs[8 + 2 * n_side:]
        i = pl.program_id(1)
        j = pl.program_id(0)
        if n_side:
            @pl.when(jnp.logical_and(i == 0, j == 0))
            def _():
                _gather_start(side_rows, side_in, side_out, sems)

        _fill_padded(pad_ref, _conv_input(m_ref[...], glu, tc), _conv_input(p_ref[...], glu, tc),
                     _conv_input(n_ref[...], glu, tc), i == 0, i == n_i - 1, tm)
        _fill_shifted(sh_ref, pad_ref, shifts, tm)
        for r in range(tm // rb):
            acc = jnp.zeros((rb, tc), F32)
            for k, (sb, start) in enumerate(plan):
                acc = acc + _tap_rows(pad_ref, sh_ref, sb, start + r * rb, rb) * w_ref[k:k + 1, :]
            acc = acc + b_ref[...]
            if silu:
                acc = acc * _sigmoid(acc)
            o_ref[r * rb:(r + 1) * rb, :] = acc.astype(BF16)

        if n_side:
            @pl.when(jnp.logical_and(i == n_i - 1, j == C // tc - 1))
            def _():
                _gather_finish(side_rows, side_in, side_out, sems)

    main, prev, nxt = _conv_specs(T, tm, sw, col0)
    out = pl.pallas_call(
        body, name=name, grid=(C // tc, n_i),
        in_specs=[main, prev, nxt, pl.BlockSpec((w.shape[0], tc), lambda j, i: (0, j)),
                  pl.BlockSpec((1, tc), lambda j, i: (0, j))] + [ANY] * n_side,
        out_specs=tuple([pl.BlockSpec((tm, tc), lambda j, i: (i, j))] + [ANY] * n_side),
        out_shape=tuple([jax.ShapeDtypeStruct((T, C), BF16)]
                        + [jax.ShapeDtypeStruct((N_CHIPS,) + t.shape, t.dtype) for t in side]),
        scratch_shapes=[pltpu.VMEM((tm + 2 * HALO, tc), F32), pltpu.VMEM((8, tm + SH_ROWS, tc), F32)]
        + (_gather_sems(n_side) if n_side else []),
        compiler_params=_params(2),
    )(src, src, src, w, b, *side)
    return (out[0], list(out[1:])) if n_side else out[0]


def dwconv_bwd(dout, src, w, b, dsrc, *, width, glu, silu, col0, dcol0, name):
    T = src.shape[0]
    C = w.shape[1]
    kp = w.shape[0]
    tm, tc = min(CONV_TM, T), CONV_TC
    sw = 2 * tc if glu else tc
    n_i = T // tm
    p = (width - 1) // 2
    rb = CONV_RB
    edge = 8
    assert p <= edge or not silu
    plan, shifts = _tap_plan([HALO - p + k for k in range(width)])
    dplan, dshifts = _tap_plan([HALO + p - k for k in range(width)])

    def body(dm_ref, dp_ref, dn_ref, m_ref, p_ref, n_ref, w_ref, b_ref, _, o_ref, dw_ref, db_ref, pad_ref, dpre_ref,
             sh_ref, dsh_ref, acc_ref):
        i = pl.program_id(1)

        @pl.when(i == 0)
        def _():
            dw_ref[...] = jnp.zeros_like(dw_ref)
            acc_ref[...] = jnp.zeros_like(acc_ref)

        first, last = i == 0, i == n_i - 1
        _fill_padded(pad_ref, _conv_input(m_ref[...], glu, tc), _conv_input(p_ref[...], glu, tc),
                     _conv_input(n_ref[...], glu, tc), first, last, tm)
        _fill_padded(dpre_ref, dm_ref[...].astype(F32), dp_ref[...].astype(F32), dn_ref[...].astype(F32),
                     first, last, tm)
        _fill_shifted(sh_ref, pad_ref, shifts, tm)
        if silu:
            for r0 in range(HALO - edge, HALO + tm + edge, HALO):
                pre = jnp.zeros((HALO, tc), F32)
                for k, (sb, start) in enumerate(plan):
                    pre = pre + _tap_rows(pad_ref, sh_ref, sb, start + r0 - HALO, HALO) * w_ref[k:k + 1, :]
                pre = pre + b_ref[...]
                s = _sigmoid(pre)
                dpre_ref[r0:r0 + HALO, :] = dpre_ref[r0:r0 + HALO, :] * (s * (1.0 + pre * (1.0 - s)))
        _fill_shifted(dsh_ref, dpre_ref, dshifts, tm)

        for r in range(tm // rb):
            acc = jnp.zeros((rb, tc), F32)
            for k, (sb, start) in enumerate(dplan):
                acc = acc + _tap_rows(dpre_ref, dsh_ref, sb, start + r * rb, rb) * w_ref[k:k + 1, :]
            if glu:
                blk = m_ref[r * rb:(r + 1) * rb, :].astype(F32)
                v, s = blk[:, :tc], _sigmoid(blk[:, tc:])
                o_ref[r * rb:(r + 1) * rb, :tc] = (acc * s).astype(BF16)
                o_ref[r * rb:(r + 1) * rb, tc:] = (acc * v * s * (1.0 - s)).astype(BF16)
            else:
                o_ref[r * rb:(r + 1) * rb, :] = acc.astype(BF16)

        for r in range(tm // rb):
            dblk = dpre_ref[HALO + r * rb:HALO + (r + 1) * rb, :]
            for k, (sb, start) in enumerate(plan):
                prod = dblk * _tap_rows(pad_ref, sh_ref, sb, start + r * rb, rb)
                acc_ref[k] += jnp.sum(prod.reshape(rb // 8, 8, tc), axis=0)
            acc_ref[kp] += jnp.sum(dblk.reshape(rb // 8, 8, tc), axis=0)

        @pl.when(last)
        def _():
            for k in range(width):
                dw_ref[k:k + 1, :] = jnp.sum(acc_ref[k], axis=0, keepdims=True)
            db_ref[...] = jnp.sum(acc_ref[kp], axis=0, keepdims=True)

    dmain_s, dprev_s, dnext_s = _conv_specs(T, tm, tc, 0)
    main, prev, nxt = _conv_specs(T, tm, sw, col0)
    wspec = pl.BlockSpec((kp, tc), lambda j, i: (0, j))
    bspec = pl.BlockSpec((1, tc), lambda j, i: (0, j))
    return pl.pallas_call(
        body, name=name, grid=(C // tc, n_i),
        in_specs=[dmain_s, dprev_s, dnext_s, main, prev, nxt, wspec, bspec, pl.BlockSpec(memory_space=pl.ANY)],
        out_specs=(pl.BlockSpec((tm, sw), lambda j, i: (i, dcol0 + j)), wspec, bspec),
        out_shape=(jax.ShapeDtypeStruct(dsrc.shape, dsrc.dtype), jax.ShapeDtypeStruct((kp, C), F32),
                   jax.ShapeDtypeStruct((1, C), F32)),
        input_output_aliases={8: 0},
        scratch_shapes=[pltpu.VMEM((tm + 2 * HALO, tc), F32), pltpu.VMEM((tm + 2 * HALO, tc), F32),
                        pltpu.VMEM((8, tm + SH_ROWS, tc), F32), pltpu.VMEM((8, tm + SH_ROWS, tc), F32),
                        pltpu.VMEM((kp + 1, 8, tc), F32)],
        compiler_params=_params(2),
    )(dout, dout, dout, src, src, src, w, b, dsrc)


def _silu_grad(x, s):
    return s * (1.0 + x * (1.0 - s))


def conf_ln_fwd(u2, proj, ln_w, ln_b, *, name):
    T, E = u2.shape
    zc = proj.shape[1] // E - 1
    tm = _pick(T, (256, 128))

    def body(u_ref, z_ref, w_ref, b_ref, o_ref):
        x = u_ref[...].astype(F32)
        xc = x - jnp.mean(x, axis=-1, keepdims=True)
        rstd = lax.rsqrt(jnp.mean(xc * xc, axis=-1, keepdims=True) + EPS)
        u3 = xc * rstd * w_ref[...] + b_ref[...]
        z = z_ref[...].astype(F32)
        o_ref[...] = (u3 * _sigmoid(u3) * z * _sigmoid(z)).astype(BF16)

    row = pl.BlockSpec((tm, E), lambda i: (i, 0))
    vec = pl.BlockSpec((1, E), lambda i: (0, 0))
    return pl.pallas_call(
        body, name=name, grid=(T // tm,),
        in_specs=[row, pl.BlockSpec((tm, E), lambda i: (i, zc)), vec, vec], out_specs=row,
        out_shape=jax.ShapeDtypeStruct((T, E), BF16), compiler_params=_params(1),
    )(u2, proj, ln_w, ln_b)


def conf_ln_bwd(du4, u2, proj, ln_w, ln_b, *, name):
    T, E = u2.shape
    ncol = proj.shape[1] // E
    zc = ncol - 1
    tm = _pick(T, (256, 128))

    def body(d_ref, u_ref, z_ref, w_ref, b_ref, du_ref, dz_ref, dw_ref, db_ref):
        @pl.when(pl.program_id(0) == 0)
        def _():
            dw_ref[...] = jnp.zeros_like(dw_ref)
            db_ref[...] = jnp.zeros_like(db_ref)

        x = u_ref[...].astype(F32)
        xc = x - jnp.mean(x, axis=-1, keepdims=True)
        rstd = lax.rsqrt(jnp.mean(xc * xc, axis=-1, keepdims=True) + EPS)
        xhat = xc * rstd
        u3 = xhat * w_ref[...] + b_ref[...]
        z = z_ref[...].astype(F32)
        s3, sz = _sigmoid(u3), _sigmoid(z)
        d4 = d_ref[...].astype(F32)
        du3 = d4 * (z * sz) * _silu_grad(u3, s3)
        dz_ref[...] = (d4 * (u3 * s3) * _silu_grad(z, sz)).astype(BF16)
        dw_ref[...] += jnp.sum(du3 * xhat, axis=0, keepdims=True)
        db_ref[...] += jnp.sum(du3, axis=0, keepdims=True)
        dxh = du3 * w_ref[...]
        du = rstd * (dxh - jnp.mean(dxh, axis=-1, keepdims=True) - xhat * jnp.mean(dxh * xhat, axis=-1, keepdims=True))
        du_ref[...] = du.astype(BF16)

    row = pl.BlockSpec((tm, E), lambda i: (i, 0))
    zrow = pl.BlockSpec((tm, E), lambda i: (i, zc))
    vec = pl.BlockSpec((1, E), lambda i: (0, 0))
    return pl.pallas_call(
        body, name=name, grid=(T // tm,), in_specs=[row, row, zrow, vec, vec], out_specs=(row, zrow, vec, vec),
        out_shape=(jax.ShapeDtypeStruct((T, E), BF16), jax.ShapeDtypeStruct(proj.shape, BF16),
                   jax.ShapeDtypeStruct((1, E), F32), jax.ShapeDtypeStruct((1, E), F32)),
        compiler_params=_params(1),
    )(du4, u2, proj, ln_w, ln_b)


def ssd_gate_fwd(y, zx, norm_w, *, name):
    T, E = y.shape
    tm = _pick(T, (256, 128))

    def body(y_ref, z_ref, w_ref, o_ref):
        z = z_ref[...].astype(F32)
        yz = y_ref[...].astype(F32) * (z * _sigmoid(z))
        rstd = lax.rsqrt(jnp.mean(yz * yz, axis=-1, keepdims=True) + EPS)
        o_ref[...] = (yz * rstd * w_ref[...]).astype(BF16)

    row = pl.BlockSpec((tm, E), lambda i: (i, 0))
    vec = pl.BlockSpec((1, E), lambda i: (0, 0))
    return pl.pallas_call(
        body, name=name, grid=(T // tm,), in_specs=[row, row, vec], out_specs=row,
        out_shape=jax.ShapeDtypeStruct((T, E), BF16), compiler_params=_params(1),
    )(y, zx, norm_w)


def ssd_gate_bwd(dyn, y, zx, norm_w, *, name):
    T, E = y.shape
    tm = _pick(T, (256, 128))

    def body(d_ref, y_ref, z_ref, w_ref, dy_ref, dz_ref, dw_ref):
        @pl.when(pl.program_id(0) == 0)
        def _():
            dw_ref[...] = jnp.zeros_like(dw_ref)

        z = z_ref[...].astype(F32)
        sz = _sigmoid(z)
        gate = z * sz
        yv = y_ref[...].astype(F32)
        yz = yv * gate
        rstd = lax.rsqrt(jnp.mean(yz * yz, axis=-1, keepdims=True) + EPS)
        yhat = yz * rstd
        d = d_ref[...].astype(F32)
        dw_ref[...] += jnp.sum(d * yhat, axis=0, keepdims=True)
        dxh = d * w_ref[...]
        dyz = rstd * (dxh - yhat * jnp.mean(dxh * yhat, axis=-1, keepdims=True))
        dy_ref[...] = (dyz * gate).astype(BF16)
        dz_ref[...] = (dyz * yv * _silu_grad(z, sz)).astype(BF16)

    row = pl.BlockSpec((tm, E), lambda i: (i, 0))
    vec = pl.BlockSpec((1, E), lambda i: (0, 0))
    return pl.pallas_call(
        body, name=name, grid=(T // tm,), in_specs=[row, row, row, vec], out_specs=(row, row, vec),
        out_shape=(jax.ShapeDtypeStruct((T, E), BF16), jax.ShapeDtypeStruct(zx.shape, BF16),
                   jax.ShapeDtypeStruct((1, E), F32)),
        compiler_params=_params(1),
    )(dyn, y, zx, norm_w)


def _cumsum_mm(mask, a):
    hi = a.astype(BF16)
    r1 = a - hi.astype(F32)
    mid = r1.astype(BF16)
    lo = (r1 - mid.astype(F32)).astype(BF16)
    out = _dot(jnp.where(mask, 1.0, 0.0).astype(BF16), jnp.concatenate([hi, mid, lo], axis=1))
    return out[:, :LANES] + out[:, LANES:2 * LANES] + out[:, 2 * LANES:]


def _chunk_terms(xcb, dt_raw, bias, alog, rev):
    L = CHUNK
    xs = xcb[:, :GW].astype(F32)
    Bm = xcb[:, GW:GW + D_STATE]
    Cm = xcb[:, GW + D_STATE:]
    pre = dt_raw + bias
    dt = _softplus(pre)
    A = -jnp.exp(alog)
    row = lax.broadcasted_iota(jnp.int32, (L, L), 0)
    col = lax.broadcasted_iota(jnp.int32, (L, L), 1)
    mask = (col >= row) if rev else (col <= row)
    mask_t = (col <= row) if rev else (col >= row)
    cs = _cumsum_mm(mask, dt * A)
    tot = cs[0:1, :] if rev else cs[L - 1:L, :]
    return xs, Bm, Cm, pre, dt, A, mask, mask_t, cs, cs.T, tot


def _decay(cs, cs_t, ln, mask):
    d = cs[:, ln:ln + 1] - cs_t[ln:ln + 1, :]
    return jnp.where(mask, jnp.exp(jnp.where(mask, d, 0.0)), 0.0)


def _pair(v, ln0, lo):
    return jnp.where(lo[:v.shape[0]], v[:, ln0:ln0 + 1], v[:, ln0 + 1:ln0 + 2])


def _scan_specs(nc, rev_order):
    ci = (lambda c: nc - 1 - c) if rev_order else (lambda c: c)
    xc = pl.BlockSpec((CHUNK, GROUPS * XCG), lambda c: (ci(c), 0))
    dt = pl.BlockSpec((CHUNK, GROUPS * LANES), lambda c: (ci(c), 0))
    vec = pl.BlockSpec((1, GROUPS * LANES), lambda c: (0, 0))
    wide = pl.BlockSpec((CHUNK, GROUPS * GW), lambda c: (ci(c), 0))
    wvec = pl.BlockSpec((1, GROUPS * GW), lambda c: (0, 0))
    st = pl.BlockSpec((1, D_STATE, GROUPS * GW), lambda c: (ci(c), 0, 0))
    return xc, dt, vec, wide, wvec, st


def _cols(ref, g, width):
    return ref.at[:, pl.ds(g * width, width)]


def _head_expand(r):
    row = lax.broadcasted_iota(jnp.int32, (LANES, GW), 0)
    col = lax.broadcasted_iota(jnp.int32, (LANES, GW), 1)
    first = (row - r * HPG) * HEADDIM
    return jnp.where(jnp.logical_and(col >= first, col < first + HEADDIM), 1.0, 0.0).astype(BF16)


def _head_collect(r):
    row = lax.broadcasted_iota(jnp.int32, (GW, LANES), 0)
    first = (lax.broadcasted_iota(jnp.int32, (GW, LANES), 1) - r * HPG) * HEADDIM
    return jnp.where(jnp.logical_and(row >= first, row < first + HEADDIM), 1.0, 0.0).astype(BF16)


def _expand(parts, sel):
    n = parts[0].shape[0]
    out = _dot(jnp.concatenate(parts, axis=0).astype(BF16), sel)
    return [out[i * n:(i + 1) * n] for i in range(len(parts))]


def ssd_scan_fwd(xc, dt4, bias4, alog4, *, rev, name, prev=None, dvec=None):
    T = xc.shape[0]
    nc = T // CHUNK
    E = GROUPS * GW
    r = 1 if rev else 0
    skip = prev is not None

    def one_group(sel, xc_ref, dt_ref, bias_ref, alog_ref, prev_ref, dvec_ref, y_ref, st_ref, s_ref):
        xs, Bm, Cm, _, dt, _, mask, _, cs, cs_t, tot = _chunk_terms(xc_ref[...], dt_ref[...], bias_ref[...],
                                                                   alog_ref[...], rev)
        dtx, ex, dx = _expand([dt, jnp.exp(cs), jnp.exp(tot - cs)], sel)
        et = jnp.exp(tot)
        cb = _dot_nt(Cm, Bm)
        sb = s_ref[...].astype(BF16)
        st_ref[...] = sb
        xp_all = xs * dtx
        y_off = _dot(Cm, sb) * ex
        lo = lax.broadcasted_iota(jnp.int32, (CHUNK, LANES), 1) < HEADDIM
        et_parts = []
        for p in range(HPG // 2):
            ln0 = r * HPG + 2 * p
            sl = slice(p * LANES, (p + 1) * LANES)
            xp = xp_all[:, sl]
            mcat = jnp.concatenate([cb * _decay(cs, cs_t, ln0, mask), cb * _decay(cs, cs_t, ln0 + 1, mask)],
                                   axis=1).astype(BF16)
            xbd = jnp.concatenate([jnp.where(lo, xp, 0.0), jnp.where(lo, 0.0, xp)], axis=0).astype(BF16)
            yp = _dot(mcat, xbd) + y_off[:, sl]
            if skip:
                yp = yp + prev_ref[:, sl].astype(F32) + xs[:, sl] * dvec_ref[:, sl]
            y_ref[:, sl] = yp.astype(BF16)
            et_parts.append(_pair(et, ln0, lo))
        s_ref[...] = s_ref[...] * jnp.concatenate(et_parts, axis=1) + _dot_tn(Bm, (xp_all * dx).astype(BF16))

    def body(*refs):
        xc_ref, dt_ref, bias_ref, alog_ref = refs[:4]
        prev_ref, dvec_ref = (refs[4], refs[5]) if skip else (None, None)
        y_ref, st_ref, s_ref = refs[-3:]

        @pl.when(pl.program_id(0) == 0)
        def _():
            s_ref[...] = jnp.zeros_like(s_ref)

        sel = _head_expand(r)
        for g in range(GROUPS):
            one_group(sel, _cols(xc_ref, g, XCG), _cols(dt_ref, g, LANES), _cols(bias_ref, g, LANES),
                      _cols(alog_ref, g, LANES), _cols(prev_ref, g, GW) if skip else None,
                      _cols(dvec_ref, g, GW) if skip else None, _cols(y_ref, g, GW),
                      st_ref.at[0, :, pl.ds(g * GW, GW)], _cols(s_ref, g, GW))

    s_xc, s_dt, s_vec, s_wide, s_wvec, s_st = _scan_specs(nc, rev)
    in_specs = [s_xc, s_dt, s_vec, s_vec]
    args = [xc, dt4, bias4, alog4]
    if skip:
        in_specs += [s_wide, s_wvec]
        args += [prev, dvec]
    return pl.pallas_call(
        body, name=name, grid=(nc,), in_specs=in_specs, out_specs=(s_wide, s_st),
        out_shape=(jax.ShapeDtypeStruct((T, E), BF16), jax.ShapeDtypeStruct((nc, D_STATE, E), BF16)),
        scratch_shapes=[pltpu.VMEM((D_STATE, E), F32)], compiler_params=_params(1),
    )(*args)


def ssd_scan_bwd(xc, dt4, bias4, alog4, dy, states, *, rev, name, prev=None, dvec=None):
    T = xc.shape[0]
    nc = T // CHUNK
    E = GROUPS * GW
    L = CHUNK
    r = 1 if rev else 0
    skip = prev is not None

    def one_group(sel, sel_t, xc_ref, dt_ref, bias_ref, alog_ref, dy_ref, st_ref, pdxc_ref, pddt_ref, dvec_ref,
                  dxc_ref, ddt_ref, dalog_ref, dbias_ref, dd_ref, g_ref):
        xs, Bm, Cm, pre, dt, A, mask, mask_t, cs, cs_t, tot = _chunk_terms(
            xc_ref[...], dt_ref[...], bias_ref[...], alog_ref[...], rev)
        dtx, ex, dx = _expand([dt, jnp.exp(cs), jnp.exp(tot - cs)], sel)
        et = jnp.exp(tot)
        cb = _dot_nt(Cm, Bm)
        s_in = st_ref[...]
        dy_all = dy_ref[...].astype(F32)
        g_f = g_ref[...]
        g_b = g_f.astype(BF16)
        xp_all = xs * dtx
        dye_all = dy_all * ex
        bgd = _dot(Bm, g_b) * dx
        lane = lax.broadcasted_iota(jnp.int32, (L, LANES), 1)
        lo = lane < HEADDIM
        dcb = jnp.zeros((L, L), F32)
        yd_parts, dxd_parts, et_parts = [], [], []
        for p in range(HPG // 2):
            ln0 = r * HPG + 2 * p
            sl = slice(p * LANES, (p + 1) * LANES)
            xp, dy_p = xp_all[:, sl], dy_all[:, sl]
            lam0, lam1 = _decay(cs, cs_t, ln0, mask), _decay(cs, cs_t, ln0 + 1, mask)
            m0, m1 = (cb * lam0).astype(BF16), (cb * lam1).astype(BF16)
            dybd = jnp.concatenate([jnp.where(lo, dy_p, 0.0), jnp.where(lo, 0.0, dy_p)], axis=0).astype(BF16)
            xbd = jnp.concatenate([jnp.where(lo, xp, 0.0), jnp.where(lo, 0.0, xp)], axis=0).astype(BF16)
            dm = _dot_nt(dybd, xp.astype(BF16))
            dcb = dcb + dm[:L] * lam0 + dm[L:] * lam1
            yd_parts.append(_dot(jnp.concatenate([m0, m1], axis=1), xbd))
            dxd_parts.append(_dot_tn(jnp.concatenate([m0, m1], axis=0), dybd))
            et_parts.append(_pair(et, ln0, lo))
        y_diag = jnp.concatenate(yd_parts, axis=1)
        dx_diag = jnp.concatenate(dxd_parts, axis=1)
        etx = jnp.concatenate(et_parts, axis=1)
        dxt = dx_diag + bgd
        w2 = xp_all * bgd
        dy_r, xp_r = dy_all.astype(BF16).astype(F32), xp_all.astype(BF16).astype(F32)
        u = dye_all * _dot(Cm, s_in) + dy_r * y_diag - xp_r * dx_diag - w2
        dxx = dxt * xs
        tail = jnp.broadcast_to(jnp.sum(w2, axis=0, keepdims=True)
                                + jnp.sum(g_f * s_in.astype(F32), axis=0, keepdims=True) * etx, (8, GW))
        u_hi, t_hi = u.astype(BF16), tail.astype(BF16)
        red = _dot(jnp.concatenate([u_hi, (u - u_hi.astype(F32)).astype(BF16), dxx.astype(BF16), t_hi,
                                    (tail - t_hi.astype(F32)).astype(BF16)], axis=0), sel_t)
        dcs = red[:L] + red[L:2 * L]
        ddt = red[2 * L:3 * L]
        dtot = red[3 * L:3 * L + 1] + red[3 * L + 8:3 * L + 9]
        dxs = dxt * dtx
        if skip:
            dxs = dxs + dy_all * dvec_ref[...] + pdxc_ref[:, :GW].astype(F32)
            dd_ref[...] += jnp.sum(dy_all * xs, axis=0, keepdims=True)
        dxc_ref[:, :GW] = dxs.astype(BF16)
        dye_b = dye_all.astype(BF16)
        xd = (xp_all * dx).astype(BF16)
        dcb_b = dcb.astype(BF16)
        d_b = _dot_nt(xd, g_b) + _dot_tn(dcb_b, Cm)
        d_c = _dot_nt(dye_b, s_in) + _dot(dcb_b, Bm)
        if skip:
            d_b = d_b + pdxc_ref[:, GW:GW + D_STATE].astype(F32)
            d_c = d_c + pdxc_ref[:, GW + D_STATE:].astype(F32)
        dxc_ref[:, GW:GW + D_STATE] = d_b.astype(BF16)
        dxc_ref[:, GW + D_STATE:] = d_c.astype(BF16)
        g_ref[...] = g_f * etx + _dot_tn(Cm, dye_b)
        rowi = lax.broadcasted_iota(jnp.int32, (L, LANES), 0)
        da = _cumsum_mm(mask_t, dcs + jnp.where(rowi == (0 if rev else L - 1), dtot, 0.0))
        keep = jnp.logical_and(lane >= r * HPG, lane < (r + 1) * HPG)
        ddr = jnp.where(keep, (da * A + ddt) * _sigmoid(pre), 0.0)
        dbias_ref[...] += jnp.sum(ddr, axis=0, keepdims=True)
        dalog_ref[...] += jnp.sum(jnp.where(keep, da * dt * A, 0.0), axis=0, keepdims=True)
        if skip:
            ddr = ddr + pddt_ref[...]
        ddt_ref[...] = ddr

    def body(*refs):
        xc_ref, dt_ref, bias_ref, alog_ref, dy_ref, st_ref = refs[:6]
        pdxc_ref, pddt_ref, dvec_ref = refs[6:9] if skip else (None, None, None)
        pos = 9 if skip else 6
        dxc_ref, ddt_ref, dalog_ref, dbias_ref = refs[pos:pos + 4]
        dd_ref = refs[pos + 4] if skip else None
        g_ref = refs[-1]

        @pl.when(pl.program_id(0) == 0)
        def _():
            g_ref[...] = jnp.zeros_like(g_ref)
            dalog_ref[...] = jnp.zeros_like(dalog_ref)
            dbias_ref[...] = jnp.zeros_like(dbias_ref)
            if skip:
                dd_ref[...] = jnp.zeros_like(dd_ref)

        sel, sel_t = _head_expand(r), _head_collect(r)
        for g in range(GROUPS):
            one_group(sel, sel_t, _cols(xc_ref, g, XCG), _cols(dt_ref, g, LANES), _cols(bias_ref, g, LANES),
                      _cols(alog_ref, g, LANES), _cols(dy_ref, g, GW), st_ref.at[0, :, pl.ds(g * GW, GW)],
                      _cols(pdxc_ref, g, XCG) if skip else None, _cols(pddt_ref, g, LANES) if skip else None,
                      _cols(dvec_ref, g, GW) if skip else None, _cols(dxc_ref, g, XCG), _cols(ddt_ref, g, LANES),
                      _cols(dalog_ref, g, LANES), _cols(dbias_ref, g, LANES),
                      _cols(dd_ref, g, GW) if skip else None, _cols(g_ref, g, GW))

    s_xc, s_dt, s_vec, s_wide, s_wvec, s_st = _scan_specs(nc, not rev)
    in_specs = [s_xc, s_dt, s_vec, s_vec, s_wide, s_st]
    args = [xc, dt4, bias4, alog4, dy, states]
    out_specs = [s_xc, s_dt, s_vec, s_vec]
    out_shape = [jax.ShapeDtypeStruct((T, GROUPS * XCG), BF16), jax.ShapeDtypeStruct((T, GROUPS * LANES), F32),
                 jax.ShapeDtypeStruct((1, GROUPS * LANES), F32), jax.ShapeDtypeStruct((1, GROUPS * LANES), F32)]
    if skip:
        in_specs += [s_xc, s_dt, s_wvec]
        args += [prev[0], prev[1], dvec]
        out_specs.append(s_wvec)
        out_shape.append(jax.ShapeDtypeStruct((1, E), F32))
    return pl.pallas_call(
        body, name=name, grid=(nc,), in_specs=in_specs, out_specs=tuple(out_specs),
        out_shape=tuple(out_shape), scratch_shapes=[pltpu.VMEM((D_STATE, E), F32)], compiler_params=_params(1),
    )(*args)


def _conf_cols(w):
    e = w.shape[-1] // 3
    lead = w.shape[:-1]
    vg = w[..., :2 * e].reshape(*lead, 2, e // CONV_TC, CONV_TC)
    vg = jnp.swapaxes(vg, -3, -2).reshape(*lead, 2 * e)
    return jnp.concatenate([vg, w[..., 2 * e:]], axis=-1)


def _conf_cols_inv(w):
    e = w.shape[-1] // 3
    lead = w.shape[:-1]
    vg = w[..., :2 * e].reshape(*lead, e // CONV_TC, 2, CONV_TC)
    vg = jnp.swapaxes(vg, -3, -2).reshape(*lead, 2 * e)
    return jnp.concatenate([vg, w[..., 2 * e:]], axis=-1)


def _xbc_cols(w):
    lead = w.shape[:-1]
    e = GROUPS * GW
    gn = GROUPS * D_STATE
    parts = [w[..., :e].reshape(*lead, GROUPS, GW), w[..., e:e + gn].reshape(*lead, GROUPS, D_STATE),
             w[..., e + gn:].reshape(*lead, GROUPS, D_STATE)]
    return jnp.concatenate(parts, axis=-1).reshape(*lead, GROUPS * XCG)


def _xbc_cols_inv(w):
    lead = w.shape[:-1]
    g = w.reshape(*lead, GROUPS, XCG)
    parts = [g[..., :GW].reshape(*lead, GROUPS * GW), g[..., GW:GW + D_STATE].reshape(*lead, GROUPS * D_STATE),
             g[..., GW + D_STATE:].reshape(*lead, GROUPS * D_STATE)]
    return jnp.concatenate(parts, axis=-1)


def _dt_cols(w):
    lead = w.shape[:-1]
    t = jnp.swapaxes(w.reshape(*lead, 2, GROUPS, HPG), -3, -2).reshape(*lead, GROUPS, 2 * HPG)
    pad = [(0, 0)] * (t.ndim - 1) + [(0, LANES - 2 * HPG)]
    return jnp.pad(t, pad).reshape(*lead, GROUPS * LANES)


def _dt_cols_inv(w):
    lead = w.shape[:-1]
    t = w.reshape(*lead, GROUPS, LANES)[..., :2 * HPG].reshape(*lead, GROUPS, 2, HPG)
    return jnp.swapaxes(t, -3, -2).reshape(*lead, 2 * HEADS)


def _pad_rows(w, rows):
    return jnp.pad(w, ((0, rows - w.shape[0]), (0, 0)))


def conf_weights(w_in, dw_w, dw_b, ln_w, ln_b):
    w_in_p = _conf_cols(w_in)
    return dict(w_in=w_in_p, w_in_t=w_in_p.T, dw_w=_pad_rows(dw_w, 32), dw_b=dw_b.reshape(1, -1),
                ln_w=ln_w.reshape(1, -1), ln_b=ln_b.reshape(1, -1))


def _xbc_rows(w):
    e, gn, c = GROUPS * GW, GROUPS * D_STATE, w.shape[1]
    parts = [w[:e].reshape(GROUPS, GW, c), w[e:e + gn].reshape(GROUPS, D_STATE, c),
             w[e + gn:].reshape(GROUPS, D_STATE, c)]
    return jnp.concatenate(parts, axis=1).reshape(GROUPS * XCG, c)


def _xbc_rows_inv(w):
    c = w.shape[1]
    g = w.reshape(GROUPS, XCG, c)
    parts = [g[:, :GW].reshape(GROUPS * GW, c), g[:, GW:GW + D_STATE].reshape(GROUPS * D_STATE, c),
             g[:, GW + D_STATE:].reshape(GROUPS * D_STATE, c)]
    return jnp.concatenate(parts, axis=0)


def _dt_rows(w):
    c = w.shape[1]
    t = jnp.swapaxes(w.reshape(2, GROUPS, HPG, c), 0, 1).reshape(GROUPS, 2 * HPG, c)
    return jnp.pad(t, ((0, 0), (0, LANES - 2 * HPG), (0, 0))).reshape(GROUPS * LANES, c)


def _dt_rows_inv(w):
    c = w.shape[1]
    t = w.reshape(GROUPS, LANES, c)[:, :2 * HPG].reshape(GROUPS, 2, HPG, c)
    return jnp.swapaxes(t, 0, 1).reshape(2 * HEADS, c)


def ssd_weights(w_in_t, conv_w, conv_b, dt_bias, a_log, d_skip, norm_w, w_out):
    e = GROUPS * GW
    xbc = e + 2 * GROUPS * D_STATE
    w_zx_t = jnp.concatenate([w_in_t[:e], _xbc_rows(w_in_t[e:e + xbc])], axis=0)
    return dict(w_zx_t=w_zx_t, w_dt_t=_dt_rows(w_in_t[e + xbc:]), w_out=w_out, w_out_t=w_out.T,
                conv_w=_pad_rows(_xbc_cols(conv_w), 8), conv_b=_xbc_cols(conv_b.reshape(1, -1)),
                bias4=_dt_cols(dt_bias.reshape(1, -1)), alog4=_dt_cols(a_log.reshape(1, -1)),
                dvec=jnp.repeat(d_skip, HEADDIM).reshape(1, -1), norm_w=norm_w.reshape(1, -1))


def conf_layer_fwd(h, nw, p, tag, side=(), w_out=None):
    hn = rmsnorm_fwd(h, nw, name=f"{tag}_norm")
    proj = mm_nn(hn, p["w_in"], out_dtype=BF16, name=f"{tag}_proj")
    u2 = dwconv_fwd(proj, p["dw_w"], p["dw_b"], width=31, glu=True, silu=False, col0=0, name=f"{tag}_conv", side=side)
    if side:
        u2, gathered = u2
        w_out = w_out(gathered)
    p.update(w_out=w_out, w_out_t=w_out.T)
    u4 = conf_ln_fwd(u2, proj, p["ln_w"], p["ln_b"], name=f"{tag}_ln")
    h2 = mm_nn(u4, p["w_out"], out_dtype=F32, res=h, name=f"{tag}_out")
    return h2, (h, hn, proj, u2, u4)


def conf_layer_bwd(dh, saved, nw, p, tag):
    h, hn, proj, u2, u4 = saved
    du4 = mm_nn(dh, p["w_out_t"], out_dtype=BF16, name=f"{tag}_d_u4")
    dw_out = mm_tn(u4, dh, name=f"{tag}_dw_out")
    du2, dproj, dln_w, dln_b = conf_ln_bwd(du4, u2, proj, p["ln_w"], p["ln_b"], name=f"{tag}_d_ln")
    dproj, ddw_w, ddw_b = dwconv_bwd(du2, proj, p["dw_w"], p["dw_b"], dproj, width=31, glu=True, silu=False,
                                     col0=0, dcol0=0, name=f"{tag}_d_conv")
    dh_prev, dnw = mm_nn(dproj, p["w_in_t"], out_dtype=F32, norm_bwd=(h, nw, dh), name=f"{tag}_d_hn")
    dw_in = mm_tn(hn, dproj, name=f"{tag}_dw_in")
    grads = dict(w_in=_conf_cols_inv(dw_in), dw_w=ddw_w[:31], dw_b=ddw_b[0], ln_w=dln_w[0], ln_b=dln_b[0],
                 w_out=dw_out, norm=dnw[0])
    return dh_prev, grads


def ssd_layer_fwd(h, nw, p, tag):
    e = GROUPS * GW
    hn = rmsnorm_fwd(h, nw, name=f"{tag}_norm")
    zx = mm_nn(hn, p["w_zx_t"], out_dtype=BF16, b_rows_are_n=True, name=f"{tag}_proj")
    dt4 = mm_nn(hn, p["w_dt_t"], out_dtype=F32, b_rows_are_n=True, name=f"{tag}_proj_dt")
    xc = dwconv_fwd(zx, p["conv_w"], p["conv_b"], width=5, glu=False, silu=True, col0=e // CONV_TC, name=f"{tag}_conv")
    y0, st0 = ssd_scan_fwd(xc, dt4, p["bias4"], p["alog4"], rev=False, name=f"{tag}_scan_f")
    y, st1 = ssd_scan_fwd(xc, dt4, p["bias4"], p["alog4"], rev=True, prev=y0, dvec=p["dvec"], name=f"{tag}_scan_b")
    yn = ssd_gate_fwd(y, zx, p["norm_w"], name=f"{tag}_gate")
    h2 = mm_nn(yn, p["w_out"], out_dtype=F32, res=h, name=f"{tag}_out")
    return h2, (h, hn, zx, dt4, xc, st0, st1, y, yn)


def ssd_layer_bwd(dh, saved, nw, p, tag):
    e = GROUPS * GW
    h, hn, zx, dt4, xc, st0, st1, y, yn = saved
    dyn = mm_nn(dh, p["w_out_t"], out_dtype=BF16, name=f"{tag}_d_yn")
    dw_out = mm_tn(yn, dh, name=f"{tag}_dw_out")
    dy, dzx, dnorm_w = ssd_gate_bwd(dyn, y, zx, p["norm_w"], name=f"{tag}_d_gate")
    dxc0, ddt0, dalog0, dbias0 = ssd_scan_bwd(xc, dt4, p["bias4"], p["alog4"], dy, st0, rev=False,
                                              name=f"{tag}_d_scan_f")
    dxc, ddt4, dalog1, dbias1, ddvec = ssd_scan_bwd(xc, dt4, p["bias4"], p["alog4"], dy, st1, rev=True,
                                                    prev=(dxc0, ddt0), dvec=p["dvec"], name=f"{tag}_d_scan_b")
    dzx, dconv_w, dconv_b = dwconv_bwd(dxc, zx, p["conv_w"], p["conv_b"], dzx, width=5, glu=False, silu=True,
                                       col0=e // CONV_TC, dcol0=e // CONV_TC, name=f"{tag}_d_conv")
    dh_prev, dnw = mm_nn(dzx, p["w_zx_t"], out_dtype=F32, a2=ddt4, b2=p["w_dt_t"], norm_bwd=(h, nw, dh),
                         name=f"{tag}_d_hn")
    dw_zx_t = mm_tn(dzx, hn, name=f"{tag}_dw_zx")
    dw_dt_t = mm_tn(ddt4, hn, name=f"{tag}_dw_dt")
    dw_in_t = jnp.concatenate([dw_zx_t[:e], _xbc_rows_inv(dw_zx_t[e:]), _dt_rows_inv(dw_dt_t)], axis=0)
    grads = dict(w_in_t=dw_in_t, conv_w=_xbc_cols_inv(dconv_w[:5]), conv_b=_xbc_cols_inv(dconv_b)[0],
                 dt_bias=_dt_cols_inv(dbias0 + dbias1).reshape(2, HEADS),
                 a_log=_dt_cols_inv(dalog0 + dalog1).reshape(2, HEADS),
                 d_skip=jnp.sum(ddvec.reshape(HEADS, HEADDIM), axis=-1), norm_w=dnorm_w[0], w_out=dw_out,
                 norm=dnw[0])
    return dh_prev, grads


def gather_chips(bufs, *, name):
    n = len(bufs)
    rows = [b.shape[0] for b in bufs]

    def body(*refs):
        ins, outs, sems = refs[:n], refs[n:2 * n], refs[2 * n:]
        _gather_start(rows, ins, outs, sems)
        _gather_finish(rows, ins, outs, sems)

    return pl.pallas_call(
        body, name=name, in_specs=[ANY] * n, out_specs=tuple([ANY] * n),
        out_shape=tuple(jax.ShapeDtypeStruct((N_CHIPS,) + b.shape, b.dtype) for b in bufs),
        scratch_shapes=_gather_sems(n),
    )(*bufs)


def swap_other_half(g2, *, name):
    def body(g_ref, o_ref, send_sem, recv_sem):
        x, y, c = _place()
        cp = pltpu.make_async_remote_copy(src_ref=g_ref.at[1 - c], dst_ref=o_ref, send_sem=send_sem, recv_sem=recv_sem,
                                          device_id=(x, y, 1 - c), device_id_type=MESH)
        cp.start()
        cp.wait()

    return pl.pallas_call(
        body, name=name, in_specs=[ANY], out_specs=ANY, out_shape=jax.ShapeDtypeStruct(g2.shape[1:], g2.dtype),
        scratch_shapes=[pltpu.SemaphoreType.DMA, pltpu.SemaphoreType.DMA],
    )(g2)


def exchange_chips(p, *, name):
    def body(p_ref, o_ref, send_sems, recv_sems, local_sem):
        x, y, c = _place()
        k_me = 2 * x + y
        own = pltpu.make_async_copy(p_ref.at[k_me], o_ref.at[k_me], local_sem)
        own.start()
        copies = [own]
        for j, (px, py) in enumerate([(1 - x, y), (x, 1 - y), (1 - x, 1 - y)]):
            cp = pltpu.make_async_remote_copy(
                src_ref=p_ref.at[2 * px + py], dst_ref=o_ref.at[k_me], send_sem=send_sems.at[j],
                recv_sem=recv_sems.at[j], device_id=(px, py, c), device_id_type=MESH)
            cp.start()
            copies.append(cp)
        for cp in copies:
            cp.wait()

    return pl.pallas_call(
        body, name=name, in_specs=[ANY], out_specs=ANY, out_shape=jax.ShapeDtypeStruct(p.shape, p.dtype),
        scratch_shapes=[pltpu.SemaphoreType.DMA((3,)), pltpu.SemaphoreType.DMA((3,)), pltpu.SemaphoreType.DMA],
    )(p)


def share_half(full, *, name):
    def body(_, f_ref, send_sem, recv_sem):
        x, y, c = _place()
        cp = pltpu.make_async_remote_copy(src_ref=f_ref.at[c], dst_ref=f_ref.at[c], send_sem=send_sem,
                                          recv_sem=recv_sem, device_id=(x, y, 1 - c), device_id_type=MESH)
        cp.start()
        cp.wait()

    return pl.pallas_call(
        body, name=name, in_specs=[ANY], out_specs=ANY, out_shape=jax.ShapeDtypeStruct(full.shape, full.dtype),
        input_output_aliases={0: 0},
        scratch_shapes=[pltpu.SemaphoreType.DMA, pltpu.SemaphoreType.DMA],
    )(full)


def gather_all(v, *, name):
    def body(v_ref, o_ref, send_sems, recv_sems, local_sem):
        x, y, c = _place()
        me = 4 * x + 2 * y + c
        own = pltpu.make_async_copy(v_ref, o_ref.at[me], local_sem)
        own.start()
        copies = [own]
        idx = 0
        for fx in (0, 1):
            for fy in (0, 1):
                for fc in (0, 1):
                    if not (fx or fy or fc):
                        continue
                    peer = (1 - x if fx else x, 1 - y if fy else y, 1 - c if fc else c)
                    cp = pltpu.make_async_remote_copy(src_ref=v_ref, dst_ref=o_ref.at[me], send_sem=send_sems.at[idx],
                                                      recv_sem=recv_sems.at[idx], device_id=peer, device_id_type=MESH)
                    cp.start()
                    copies.append(cp)
                    idx += 1
        for cp in copies:
            cp.wait()

    return pl.pallas_call(
        body, name=name, in_specs=[ANY], out_specs=ANY, out_shape=jax.ShapeDtypeStruct((N_DEV,) + v.shape, v.dtype),
        scratch_shapes=[pltpu.SemaphoreType.DMA((N_DEV - 1,)), pltpu.SemaphoreType.DMA((N_DEV - 1,)),
                        pltpu.SemaphoreType.DMA],
    )(v)


RED_TR = 432


def pair_sum(g2, recv, cidx, *, name):
    _, K, R, C = g2.shape
    tr = _pick(R, (RED_TR, 8))

    def body(c_ref, a_ref, b_ref, o_ref):
        o_ref[...] = (a_ref[0] + b_ref[...]).astype(BF16)

    blk = pl.BlockSpec((1, tr, C), lambda k, i, c: (k, i, 0))
    return pl.pallas_call(
        body, name=name,
        grid_spec=pltpu.PrefetchScalarGridSpec(
            num_scalar_prefetch=1, grid=(K, R // tr),
            in_specs=[pl.BlockSpec((1, 1, tr, C), lambda k, i, c: (c[0], k, i, 0)), blk], out_specs=blk),
        out_shape=jax.ShapeDtypeStruct((K, R, C), BF16), compiler_params=_params(2),
    )(cidx, g2, recv)


def sum_lead(a, *, name, slot=None, nslots=1):
    K, R, C = a.shape
    tr = _pick(R, (RED_TR, 8))

    def body(s_ref, a_ref, o_ref):
        acc = a_ref[0].astype(F32)
        for k in range(1, K):
            acc = acc + a_ref[k].astype(F32)
        o_ref[0] = acc

    if slot is None:
        slot = jnp.zeros((1,), jnp.int32)
    return pl.pallas_call(
        body, name=name,
        grid_spec=pltpu.PrefetchScalarGridSpec(
            num_scalar_prefetch=1, grid=(R // tr,),
            in_specs=[pl.BlockSpec((K, tr, C), lambda i, s: (0, i, 0))],
            out_specs=pl.BlockSpec((1, tr, C), lambda i, s: (s[0], i, 0))),
        out_shape=jax.ShapeDtypeStruct((nslots, R, C), F32), compiler_params=_params(1),
    )(slot, a)


def adamw(g, w, m, v, *, name):
    R, C = w.shape
    tr = _pick(R, (256, 128, 64, 32, 16, 8))

    def body(g_ref, w_ref, m_ref, v_ref, d_ref, nm_ref, nv_ref):
        gv = g_ref[...]
        m_new = ADAM_B1 * m_ref[...] + (1.0 - ADAM_B1) * gv
        v_new = ADAM_B2 * v_ref[...] + (1.0 - ADAM_B2) * (gv * gv)
        m_hat = m_new / (1.0 - ADAM_B1 ** ADAM_STEP)
        v_hat = v_new / (1.0 - ADAM_B2 ** ADAM_STEP)
        d_ref[...] = -ADAM_LR * (m_hat / (jnp.sqrt(v_hat) + ADAM_EPS) + ADAM_WD * w_ref[...])
        nm_ref[...] = m_new
        nv_ref[...] = v_new

    blk = pl.BlockSpec((tr, C), lambda i: (i, 0))
    sds = jax.ShapeDtypeStruct((R, C), F32)
    return pl.pallas_call(
        body, name=name, grid=(R // tr,), in_specs=[blk] * 4, out_specs=(blk,) * 3, out_shape=(sds,) * 3,
        compiler_params=_params(1),
    )(g, w, m, v)


WEIGHTS = ("norm_w", "final_norm_w", "cm_w_in", "cm_dw_w", "cm_dw_b", "cm_ln_w", "cm_ln_b", "cm_w_out", "ssd_w_in",
           "ssd_conv_w", "ssd_conv_b", "ssd_dt_bias", "ssd_A_log", "ssd_D", "ssd_norm_w", "ssd_w_out")
BIG = (("cm_w_in", 2), ("cm_w_out", 1), ("ssd_w_in", 1), ("ssd_w_out", 1))
TRANSPOSED = ("ssd_w_in",)
SMALL_SHARDED = (("cm_dw_w", 2), ("ssd_conv_w", 2), ("ssd_conv_b", 1), ("ssd_norm_w", 1))
REPLICATED = ("norm_w", "final_norm_w", "cm_dw_b", "cm_ln_w", "cm_ln_b", "ssd_dt_bias", "ssd_A_log", "ssd_D")
ROW = 1024


def _to_shards(g, axis):
    n = g.shape[axis]
    s = g.reshape(g.shape[:axis] + (N_CHIPS, n // N_CHIPS) + g.shape[axis + 1:])
    return jnp.moveaxis(s, axis, 0).reshape(N_CHIPS, -1)


def _from_shards(x4, local_shape, axis):
    local_shape = tuple(local_shape)
    s = jnp.moveaxis(x4.reshape((N_CHIPS,) + local_shape), 0, axis)
    return s.reshape(local_shape[:axis] + (N_CHIPS * local_shape[axis],) + local_shape[axis + 1:])


def _flat_pad(parts, multiple):
    n = sum(p.size for p in parts)
    fill = [jnp.zeros(((-n) % multiple,), parts[0].dtype)] if n % multiple else []
    return jnp.concatenate([p.reshape(-1) for p in parts] + fill)


def _split(flat, like, names):
    out, off = {}, 0
    for n in names:
        out[n] = flat[off:off + like[n].size].reshape(like[n].shape)
        off += like[n].size
    return out


def kernel(x, norm_w, final_norm_w, cm_w_in, cm_dw_w, cm_dw_b, cm_ln_w, cm_ln_b, cm_w_out, ssd_w_in, ssd_conv_w, ssd_conv_b, ssd_dt_bias, ssd_A_log, ssd_D, ssd_norm_w, ssd_w_out, loss_target, m_norm_w, m_final_norm_w, m_cm_w_in, m_cm_dw_w, m_cm_dw_b, m_cm_ln_w, m_cm_ln_b, m_cm_w_out, m_ssd_w_in, m_ssd_conv_w, m_ssd_conv_b, m_ssd_dt_bias, m_ssd_A_log, m_ssd_D, m_ssd_norm_w, m_ssd_w_out, v_norm_w, v_final_norm_w, v_cm_w_in, v_cm_dw_w, v_cm_dw_b, v_cm_ln_w, v_cm_ln_b, v_cm_w_out, v_ssd_w_in, v_ssd_conv_w, v_ssd_conv_b, v_ssd_dt_bias, v_ssd_A_log, v_ssd_D, v_ssd_norm_w, v_ssd_w_out):
    a = dict(locals())
    w = {n: a[n] for n in WEIGHTS}
    m = {n: a["m_" + n] for n in WEIGHTS}
    v = {n: a["v_" + n] for n in WEIGHTS}
    _, T, D = x.shape
    cidx = lax.axis_index("c").astype(jnp.int32).reshape(1)
    big_names = [n for n, _ in BIG]
    small_names = [n for n, _ in SMALL_SHARDED]

    wx = {n: (jnp.swapaxes(w[n], 1, 2) if n in TRANSPOSED else w[n]) for n in big_names + small_names}
    big = _flat_pad([wx[n] for n in big_names], 16 * ROW).astype(BF16).reshape(-1, ROW)
    small = _flat_pad([wx[n] for n in small_names], 8 * ROW).reshape(-1, ROW)
    first_name, first_axis = BIG[0]
    first_shape = wx[first_name].shape[1:]
    n_first = math.prod(first_shape) // ROW
    g_first, g_small = gather_chips([big[:n_first], small], name="gather_weights")
    g_small = g_small.reshape(N_CHIPS, -1)
    full, off = {}, 0
    for n, ax in SMALL_SHARDED:
        full[n] = _from_shards(g_small[:, off:off + wx[n].size], wx[n].shape, ax)
        off += wx[n].size
    w_in_0 = _from_shards(g_first.reshape(N_CHIPS, -1), first_shape, first_axis - 1)
    n_layers = norm_w.shape[0]
    lw = [None] * n_layers
    lw[0] = conf_weights(w_in_0, full["cm_dw_w"][0], cm_dw_b[0], cm_ln_w[0], cm_ln_b[0])

    def unpack_rest(gathered):
        g_big = jnp.concatenate([g_first, gathered[0]], axis=1).reshape(N_CHIPS, -1)
        off = 0
        for n, ax in BIG:
            full[n] = _from_shards(g_big[:, off:off + wx[n].size], wx[n].shape, ax)
            off += wx[n].size
        for i in range(1, n_layers):
            j = i // 2
            if i % 2 == 0:
                lw[i] = conf_weights(full["cm_w_in"][j], full["cm_dw_w"][j], cm_dw_b[j], cm_ln_w[j], cm_ln_b[j])
            else:
                lw[i] = ssd_weights(full["ssd_w_in"][j], full["ssd_conv_w"][j], full["ssd_conv_b"][j], ssd_dt_bias[j],
                                    ssd_A_log[j], ssd_D[j], full["ssd_norm_w"][j], full["ssd_w_out"][j])
        return full["cm_w_out"][0]

    h = x[0]
    saved = []
    for i in range(n_layers):
        nw_i = norm_w[i].reshape(1, -1)
        if i == 0:
            h, s = conf_layer_fwd(h, nw_i, lw[0], "l0", side=(big[n_first:],), w_out=unpack_rest)
        elif i % 2 == 0:
            h, s = conf_layer_fwd(h, nw_i, lw[i], f"l{i}", w_out=full["cm_w_out"][i // 2])
        else:
            h, s = ssd_layer_fwd(h, nw_i, lw[i], f"l{i}")
        saved.append(s)
    dh, loss_local, d_final = loss_head(h, loss_target[0], final_norm_w.reshape(1, -1), name="loss_head")
    lg = [None] * n_layers
    for i in reversed(range(n_layers)):
        bwd = conf_layer_bwd if i % 2 == 0 else ssd_layer_bwd
        dh, lg[i] = bwd(dh, saved[i], norm_w[i].reshape(1, -1), lw[i], f"l{i}")
    conf_g, ssd_g = lg[0::2], lg[1::2]
    local = {
        "norm_w": jnp.stack([g["norm"] for g in lg]), "final_norm_w": d_final[0],
        "cm_w_in": jnp.stack([g["w_in"] for g in conf_g]), "cm_dw_w": jnp.stack([g["dw_w"] for g in conf_g]),
        "cm_dw_b": jnp.stack([g["dw_b"] for g in conf_g]), "cm_ln_w": jnp.stack([g["ln_w"] for g in conf_g]),
        "cm_ln_b": jnp.stack([g["ln_b"] for g in conf_g]), "cm_w_out": jnp.stack([g["w_out"] for g in conf_g]),
        "ssd_w_in": jnp.stack([g["w_in_t"] for g in ssd_g]), "ssd_conv_w": jnp.stack([g["conv_w"] for g in ssd_g]),
        "ssd_conv_b": jnp.stack([g["conv_b"] for g in ssd_g]), "ssd_dt_bias": jnp.stack([g["dt_bias"] for g in ssd_g]),
        "ssd_A_log": jnp.stack([g["a_log"] for g in ssd_g]), "ssd_D": jnp.stack([g["d_skip"] for g in ssd_g]),
        "ssd_norm_w": jnp.stack([g["norm_w"] for g in ssd_g]), "ssd_w_out": jnp.stack([g["w_out"] for g in ssd_g]),
    }

    shards = [_to_shards(local[n], ax) for n, ax in BIG + SMALL_SHARDED]
    fill = (-sum(t.shape[1] for t in shards)) % (2 * RED_TR * ROW)
    flat4 = jnp.concatenate(shards + [jnp.zeros((N_CHIPS, fill), F32)], axis=1)
    g2 = jnp.swapaxes(flat4.reshape(N_CHIPS, 2, -1, ROW), 0, 1)
    theirs = swap_other_half(g2, name="grad_pair_swap")
    part = pair_sum(g2, theirs, cidx, name="grad_pair_sum")
    got = exchange_chips(part, name="grad_chip_exchange")
    half = sum_lead(got, slot=cidx, nslots=2, name="grad_chip_sum")
    shard_flat = share_half(half, name="grad_pair_share").reshape(-1)
    grads = _split(shard_flat, wx, big_names + small_names)
    for n in TRANSPOSED:
        grads[n] = jnp.swapaxes(grads[n], 1, 2)

    rep = _flat_pad([local[n] for n in REPLICATED], 8 * LANES).reshape(-1, LANES)
    rep_sum = sum_lead(gather_all(rep, name="grad_small_gather"), name="grad_small_sum")
    grads.update(_split(rep_sum.reshape(-1), w, REPLICATED))

    delta, new_m, new_v = {}, {}, {}
    for n in big_names:
        two_d = (-1, w[n].shape[-1])
        d_, m_, v_ = adamw(grads[n].reshape(two_d), w[n].reshape(two_d), m[n].reshape(two_d), v[n].reshape(two_d),
                           name="adamw_" + n)
        delta[n], new_m[n], new_v[n] = d_.reshape(w[n].shape), m_.reshape(w[n].shape), v_.reshape(w[n].shape)
    rest = list(REPLICATED) + small_names
    packed = [_flat_pad([t[n] for n in rest], 8 * LANES).reshape(-1, LANES) for t in (grads, w, m, v)]
    for out, res in zip((delta, new_m, new_v), adamw(*packed, name="adamw_small")):
        out.update(_split(res.reshape(-1), w, rest))

    loss = lax.psum(loss_local[0, 0], ("x", "y", "c"))
    return (loss, dh.reshape(x.shape), *[grads[n] for n in WEIGHTS], *[delta[n] for n in WEIGHTS],
            *[new_m[n] for n in WEIGHTS], *[new_v[n] for n in WEIGHTS])
```

```python
import math

import jax
import jax.numpy as jnp
from jax import lax
from jax.experimental import pallas as pl
from jax.experimental.pallas import tpu as pltpu

F32 = jnp.float32
BF16 = jnp.bfloat16
MESH = pl.DeviceIdType.MESH

EPS = 1e-5
HEADDIM = 64
HEADS = 32
GROUPS = 4
HPG = HEADS // GROUPS
D_STATE = 128
CHUNK = 128
GW = HPG * HEADDIM
XCG = GW + 2 * D_STATE
HALO = 16
LANES = 128
N_CHIPS = 4
N_DEV = 8

ADAM_LR = 0.001
ADAM_B1 = 0.9
ADAM_B2 = 0.999
ADAM_EPS = 1e-08
ADAM_WD = 0.01
ADAM_STEP = 10

VMEM_LIMIT = 52 * 1024 * 1024


def _params(n_axes):
    return pltpu.CompilerParams(dimension_semantics=("arbitrary",) * n_axes, vmem_limit_bytes=VMEM_LIMIT)


def _sigmoid(x):
    return 1.0 / (1.0 + jnp.exp(-x))


def _softplus(x):
    return jnp.maximum(x, 0.0) + jnp.log(1.0 + jnp.exp(-jnp.abs(x)))


def _dot(a, b):
    return jnp.dot(a, b, preferred_element_type=F32)


def _dot_nt(a, b):
    return lax.dot_general(a, b, (((1,), (1,)), ((), ())), preferred_element_type=F32)


def _dot_tn(a, b):
    return lax.dot_general(a, b, (((0,), (0,)), ((), ())), preferred_element_type=F32)


def _pick(n, pref):
    for t in pref:
        if n % t == 0:
            return t
    return n


def mm_nn(a, b, *, out_dtype, name, res=None, a2=None, b2=None, b_rows_are_n=False, norm_bwd=None):
    M, K = a.shape
    N = b.shape[0] if b_rows_are_n else b.shape[1]
    has2, has_res, has_nb = a2 is not None, res is not None, norm_bwd is not None
    tm = _pick(M, (1024, 512, 256, 128))
    tn = N if has_nb else _pick(N, (1024, 512, 256, 128))
    tk = _pick(K, (2048, 1024, 512, 256, 128) if a.dtype == BF16 and not has_nb else (1024, 512, 256, 128))
    nk = K // tk

    def body(*refs):
        a_ref, b_ref = refs[0], refs[1]
        pos = 2
        if has2:
            a2_ref, b2_ref = refs[pos], refs[pos + 1]
            pos += 2
        if has_res:
            r_ref = refs[pos]
            pos += 1
        if has_nb:
            h_ref, w_ref, dh_ref = refs[pos:pos + 3]
            pos += 3
        o_ref = refs[pos]
        acc_ref = refs[-1]
        k = pl.program_id(2)
        first_rows = pl.program_id(0) == 0

        @pl.when(k == 0)
        def _():
            if has2:
                acc_ref[...] = _dot(a2_ref[...].astype(BF16), b2_ref[...])
            else:
                acc_ref[...] = jnp.zeros_like(acc_ref)

        acc_ref[...] += (_dot_nt if b_rows_are_n else _dot)(a_ref[...].astype(BF16), b_ref[...])

        @pl.when(k == nk - 1)
        def _():
            r = acc_ref[...]
            if has_res:
                r = r + r_ref[...]
            if has_nb:
                dw_ref = refs[pos + 1]

                @pl.when(first_rows)
                def _():
                    dw_ref[...] = jnp.zeros_like(dw_ref)

                x = h_ref[...]
                rstd = lax.rsqrt(jnp.mean(x * x, axis=-1, keepdims=True) + EPS)
                xhat = x * rstd
                dxh = r * w_ref[...]
                dw_ref[...] += jnp.sum(r * xhat, axis=0, keepdims=True)
                r = dh_ref[...] + rstd * (dxh - xhat * jnp.mean(dxh * xhat, axis=-1, keepdims=True))
            o_ref[...] = r.astype(out_dtype)

    b_spec = pl.BlockSpec((tn, tk), lambda i, j, k: (j, k)) if b_rows_are_n else pl.BlockSpec((tk, tn), lambda i, j, k: (k, j))
    in_specs = [pl.BlockSpec((tm, tk), lambda i, j, k: (i, k)), b_spec]
    args = [a, b]
    if has2:
        k2 = a2.shape[1]
        in_specs += [pl.BlockSpec((tm, k2), lambda i, j, k: (i, 0)), pl.BlockSpec((k2, tn), lambda i, j, k: (0, j))]
        args += [a2, b2]
    tile = pl.BlockSpec((tm, tn), lambda i, j, k: (i, j))
    if has_res:
        in_specs.append(tile)
        args.append(res)
    out_specs, out_shape = tile, jax.ShapeDtypeStruct((M, N), out_dtype)
    if has_nb:
        vec = pl.BlockSpec((1, N), lambda i, j, k: (0, 0))
        in_specs += [tile, vec, tile]
        args += list(norm_bwd)
        out_specs, out_shape = (tile, vec), (out_shape, jax.ShapeDtypeStruct((1, N), F32))
    return pl.pallas_call(
        body, name=name, grid=(M // tm, N // tn, nk), in_specs=in_specs, out_specs=out_specs, out_shape=out_shape,
        scratch_shapes=[pltpu.VMEM((tm, tn), F32)], compiler_params=_params(3),
    )(*args)


def mm_tn(a, b, *, name):
    T, M = a.shape
    N = b.shape[1]
    tm = _pick(M, (1024, 512, 256, 128))
    tn = _pick(N, (1024, 512, 256, 128))
    tt = _pick(T, (2048, 1024, 512, 256, 128))

    def body(a_ref, b_ref, o_ref):
        @pl.when(pl.program_id(2) == 0)
        def _():
            o_ref[...] = jnp.zeros_like(o_ref)

        o_ref[...] += _dot_tn(a_ref[...].astype(BF16), b_ref[...].astype(BF16))

    return pl.pallas_call(
        body, name=name, grid=(M // tm, N // tn, T // tt),
        in_specs=[pl.BlockSpec((tt, tm), lambda i, j, t: (t, i)), pl.BlockSpec((tt, tn), lambda i, j, t: (t, j))],
        out_specs=pl.BlockSpec((tm, tn), lambda i, j, t: (i, j)),
        out_shape=jax.ShapeDtypeStruct((M, N), F32), compiler_params=_params(3),
    )(a, b)


def rmsnorm_fwd(h, w, *, name):
    T, D = h.shape
    tm = _pick(T, (512, 256, 128))

    def body(h_ref, w_ref, o_ref):
        x = h_ref[...]
        rstd = lax.rsqrt(jnp.mean(x * x, axis=-1, keepdims=True) + EPS)
        o_ref[...] = (x * rstd * w_ref[...]).astype(BF16)

    return pl.pallas_call(
        body, name=name, grid=(T // tm,),
        in_specs=[pl.BlockSpec((tm, D), lambda i: (i, 0)), pl.BlockSpec((1, D), lambda i: (0, 0))],
        out_specs=pl.BlockSpec((tm, D), lambda i: (i, 0)),
        out_shape=jax.ShapeDtypeStruct((T, D), BF16), compiler_params=_params(1),
    )(h, w)


def loss_head(h, target, w, *, name):
    T, D = h.shape
    tm = _pick(T, (512, 256, 128))

    def body(h_ref, t_ref, w_ref, dh_ref, loss_ref, dw_ref):
        @pl.when(pl.program_id(0) == 0)
        def _():
            loss_ref[...] = jnp.zeros_like(loss_ref)
            dw_ref[...] = jnp.zeros_like(dw_ref)

        x = h_ref[...]
        rstd = lax.rsqrt(jnp.mean(x * x, axis=-1, keepdims=True) + EPS)
        xhat = x * rstd
        err = xhat * w_ref[...] - t_ref[...]
        rows = jnp.sum(err * err, axis=-1, keepdims=True)
        loss_ref[...] += (0.5 / D) * jnp.sum(rows, axis=0, keepdims=True)
        dy = err * (1.0 / D)
        dxh = dy * w_ref[...]
        dh_ref[...] = rstd * (dxh - xhat * jnp.mean(dxh * xhat, axis=-1, keepdims=True))
        dw_ref[...] += jnp.sum(dy * xhat, axis=0, keepdims=True)

    row = pl.BlockSpec((tm, D), lambda i: (i, 0))
    vec = pl.BlockSpec((1, D), lambda i: (0, 0))
    return pl.pallas_call(
        body, name=name, grid=(T // tm,), in_specs=[row, row, vec],
        out_specs=(row, pl.BlockSpec((1, 1), lambda i: (0, 0)), vec),
        out_shape=(jax.ShapeDtypeStruct((T, D), F32), jax.ShapeDtypeStruct((1, 1), F32),
                   jax.ShapeDtypeStruct((1, D), F32)),
        compiler_params=_params(1),
    )(h, target, w)


ANY = pl.BlockSpec(memory_space=pl.ANY)


def _place():
    return lax.axis_index("x"), lax.axis_index("y"), lax.axis_index("c")


def _gather_sems(n):
    return [pltpu.SemaphoreType.DMA((3 * n,))] * 4 + [pltpu.SemaphoreType.DMA((n,))]


def _gather_copies(rows, ins, outs, sems):
    ici_send, ici_recv, d2d_send, d2d_recv, local_sems = sems
    x, y, c = _place()
    k_me = 2 * x + y
    plan = []
    for t in range(len(rows)):
        half = rows[t] // 2
        mine = pl.ds(pl.multiple_of(c * half, 8), half)
        own = pltpu.make_async_copy(ins[t], outs[t].at[k_me], local_sems.at[t])
        sent, passed = [], []
        for j, (px, py) in enumerate([(1 - x, y), (x, 1 - y), (1 - x, 1 - y)]):
            landed = outs[t].at[2 * px + py, mine]
            sent.append(pltpu.make_async_remote_copy(
                src_ref=ins[t].at[mine], dst_ref=outs[t].at[k_me, mine], send_sem=ici_send.at[3 * t + j],
                recv_sem=ici_recv.at[3 * t + j], device_id=(px, py, c), device_id_type=MESH))
            passed.append(pltpu.make_async_remote_copy(
                src_ref=landed, dst_ref=landed, send_sem=d2d_send.at[3 * t + j], recv_sem=d2d_recv.at[3 * t + j],
                device_id=(x, y, 1 - c), device_id_type=MESH))
        plan.append((own, sent, passed))
    return plan


def _gather_start(rows, ins, outs, sems):
    for own, sent, _ in _gather_copies(rows, ins, outs, sems):
        own.start()
        for cp in sent:
            cp.start()


def _gather_finish(rows, ins, outs, sems):
    plan = _gather_copies(rows, ins, outs, sems)
    for _, sent, passed in plan:
        for cp, fwd in zip(sent, passed):
            cp.wait_recv()
            fwd.start()
    for own, sent, passed in plan:
        own.wait()
        for cp, fwd in zip(sent, passed):
            cp.wait_send()
            fwd.wait()


CONV_TM = 512
CONV_TC = 512
CONV_RB = 32


def _conv_specs(T, tm, sw, col0):
    hb = tm // HALO
    last = T // HALO - 1
    main = pl.BlockSpec((tm, sw), lambda j, i: (i, col0 + j))
    prev = pl.BlockSpec((HALO, sw), lambda j, i: (jnp.maximum(i * hb - 1, 0), col0 + j))
    nxt = pl.BlockSpec((HALO, sw), lambda j, i: (jnp.minimum((i + 1) * hb, last), col0 + j))
    return main, prev, nxt


def _conv_input(blk, glu, tc):
    x = blk.astype(F32)
    if glu:
        return x[:, :tc] * _sigmoid(x[:, tc:])
    return x


def _fill_padded(pad_ref, main, prev, nxt, first, last, tm):
    pad_ref[0:HALO, :] = jnp.where(first, 0.0, prev)
    pad_ref[HALO:HALO + tm, :] = main
    pad_ref[HALO + tm:HALO + tm + HALO, :] = jnp.where(last, 0.0, nxt)


SH_ROWS = 24


def _tap_plan(offsets):
    plan = [(o % 8, o - o % 8) for o in offsets]
    return plan, sorted({b for b, _ in plan if b})


def _fill_shifted(sh_ref, pad_ref, shifts, tm):
    for b in shifts:
        sh_ref[b] = pad_ref[b:b + tm + SH_ROWS, :]


def _tap_rows(pad_ref, sh_ref, b, start, rows):
    return pad_ref[start:start + rows, :] if b == 0 else sh_ref[b, start:start + rows, :]


def dwconv_fwd(src, w, b, *, width, glu, silu, col0, name, side=()):
    T = src.shape[0]
    C = w.shape[1]
    tm, tc = min(CONV_TM, T), CONV_TC
    sw = 2 * tc if glu else tc
    n_i = T // tm
    p = (width - 1) // 2
    rb = CONV_RB
    plan, shifts = _tap_plan([HALO - p + k for k in range(width)])

    n_side = len(side)
    side_rows = [t.shape[0] for t in side]

    def body(*refs):
        m_ref, p_ref, n_ref, w_ref, b_ref = refs[:5]
        side_in = refs[5:5 + n_side]
        o_ref = refs[5 + n_side]
        side_out = refs[6 + n_side:6 + 2 * n_side]
        pad_ref, sh_ref = refs[6 + 2 * n_side:8 + 2 * n_side]
        sems = refs[8 + 2 * n_side:]
        i = pl.program_id(1)
        j = pl.program_id(0)
        if n_side:
            @pl.when(jnp.logical_and(i == 0, j == 0))
            def _():
                _gather_start(side_rows, side_in, side_out, sems)

        _fill_padded(pad_ref, _conv_input(m_ref[...], glu, tc), _conv_input(p_ref[...], glu, tc),
                     _conv_input(n_ref[...], glu, tc), i == 0, i == n_i - 1, tm)
        _fill_shifted(sh_ref, pad_ref, shifts, tm)
        for r in range(tm // rb):
            acc = jnp.zeros((rb, tc), F32)
            for k, (sb, start) in enumerate(plan):
                acc = acc + _tap_rows(pad_ref, sh_ref, sb, start + r * rb, rb) * w_ref[k:k + 1, :]
            acc = acc + b_ref[...]
            if silu:
                acc = acc * _sigmoid(acc)
            o_ref[r * rb:(r + 1) * rb, :] = acc.astype(BF16)

        if n_side:
            @pl.when(jnp.logical_and(i == n_i - 1, j == C // tc - 1))
            def _():
                _gather_finish(side_rows, side_in, side_out, sems)

    main, prev, nxt = _conv_specs(T, tm, sw, col0)
    out = pl.pallas_call(
        body, name=name, grid=(C // tc, n_i),
        in_specs=[main, prev, nxt, pl.BlockSpec((w.shape[0], tc), lambda j, i: (0, j)),
                  pl.BlockSpec((1, tc), lambda j, i: (0, j))] + [ANY] * n_side,
        out_specs=tuple([pl.BlockSpec((tm, tc), lambda j, i: (i, j))] + [ANY] * n_side),
        out_shape=tuple([jax.ShapeDtypeStruct((T, C), BF16)]
                        + [jax.ShapeDtypeStruct((N_CHIPS,) + t.shape, t.dtype) for t in side]),
        scratch_shapes=[pltpu.VMEM((tm + 2 * HALO, tc), F32), pltpu.VMEM((8, tm + SH_ROWS, tc), F32)]
        + (_gather_sems(n_side) if n_side else []),
        compiler_params=_params(2),
    )(src, src, src, w, b, *side)
    return (out[0], list(out[1:])) if n_side else out[0]


def dwconv_bwd(dout, src, w, b, dsrc, *, width, glu, silu, col0, dcol0, name):
    T = src.shape[0]
    C = w.shape[1]
    kp = w.shape[0]
    tm, tc = min(CONV_TM, T), CONV_TC
    sw = 2 * tc if glu else tc
    n_i = T // tm
    p = (width - 1) // 2
    rb = CONV_RB
    edge = 8
    assert p <= edge or not silu
    plan, shifts = _tap_plan([HALO - p + k for k in range(width)])
    dplan, dshifts = _tap_plan([HALO + p - k for k in range(width)])

    def body(dm_ref, dp_ref, dn_ref, m_ref, p_ref, n_ref, w_ref, b_ref, _, o_ref, dw_ref, db_ref, pad_ref, dpre_ref,
             sh_ref, dsh_ref, acc_ref):
        i = pl.program_id(1)

        @pl.when(i == 0)
        def _():
            dw_ref[...] = jnp.zeros_like(dw_ref)
            acc_ref[...] = jnp.zeros_like(acc_ref)

        first, last = i == 0, i == n_i - 1
        _fill_padded(pad_ref, _conv_input(m_ref[...], glu, tc), _conv_input(p_ref[...], glu, tc),
                     _conv_input(n_ref[...], glu, tc), first, last, tm)
        _fill_padded(dpre_ref, dm_ref[...].astype(F32), dp_ref[...].astype(F32), dn_ref[...].astype(F32),
                     first, last, tm)
        _fill_shifted(sh_ref, pad_ref, shifts, tm)
        if silu:
            for r0 in range(HALO - edge, HALO + tm + edge, HALO):
                pre = jnp.zeros((HALO, tc), F32)
                for k, (sb, start) in enumerate(plan):
                    pre = pre + _tap_rows(pad_ref, sh_ref, sb, start + r0 - HALO, HALO) * w_ref[k:k + 1, :]
                pre = pre + b_ref[...]
                s = _sigmoid(pre)
                dpre_ref[r0:r0 + HALO, :] = dpre_ref[r0:r0 + HALO, :] * (s * (1.0 + pre * (1.0 - s)))
        _fill_shifted(dsh_ref, dpre_ref, dshifts, tm)

        for r in range(tm // rb):
            acc = jnp.zeros((rb, tc), F32)
            for k, (sb, start) in enumerate(dplan):
                acc = acc + _tap_rows(dpre_ref, dsh_ref, sb, start + r * rb, rb) * w_ref[k:k + 1, :]
            if glu:
                blk = m_ref[r * rb:(r + 1) * rb, :].astype(F32)
                v, s = blk[:, :tc], _sigmoid(blk[:, tc:])
                o_ref[r * rb:(r + 1) * rb, :tc] = (acc * s).astype(BF16)
                o_ref[r * rb:(r + 1) * rb, tc:] = (acc * v * s * (1.0 - s)).astype(BF16)
            else:
                o_ref[r * rb:(r + 1) * rb, :] = acc.astype(BF16)

        for r in range(tm // rb):
            dblk = dpre_ref[HALO + r * rb:HALO + (r + 1) * rb, :]
            for k, (sb, start) in enumerate(plan):
                prod = dblk * _tap_rows(pad_ref, sh_ref, sb, start + r * rb, rb)
                acc_ref[k] += jnp.sum(prod.reshape(rb // 8, 8, tc), axis=0)
            acc_ref[kp] += jnp.sum(dblk.reshape(rb // 8, 8, tc), axis=0)

        @pl.when(last)
        def _():
            for k in range(width):
                dw_ref[k:k + 1, :] = jnp.sum(acc_ref[k], axis=0, keepdims=True)
            db_ref[...] = jnp.sum(acc_ref[kp], axis=0, keepdims=True)

    dmain_s, dprev_s, dnext_s = _conv_specs(T, tm, tc, 0)
    main, prev, nxt = _conv_specs(T, tm, sw, col0)
    wspec = pl.BlockSpec((kp, tc), lambda j, i: (0, j))
    bspec = pl.BlockSpec((1, tc), lambda j, i: (0, j))
    return pl.pallas_call(
        body, name=name, grid=(C // tc, n_i),
        in_specs=[dmain_s, dprev_s, dnext_s, main, prev, nxt, wspec, bspec, pl.BlockSpec(memory_space=pl.ANY)],
        out_specs=(pl.BlockSpec((tm, sw), lambda j, i: (i, dcol0 + j)), wspec, bspec),
        out_shape=(jax.ShapeDtypeStruct(dsrc.shape, dsrc.dtype), jax.ShapeDtypeStruct((kp, C), F32),
                   jax.ShapeDtypeStruct((1, C), F32)),
        input_output_aliases={8: 0},
        scratch_shapes=[pltpu.VMEM((tm + 2 * HALO, tc), F32), pltpu.VMEM((tm + 2 * HALO, tc), F32),
                        pltpu.VMEM((8, tm + SH_ROWS, tc), F32), pltpu.VMEM((8, tm + SH_ROWS, tc), F32),
                        pltpu.VMEM((kp + 1, 8, tc), F32)],
        compiler_params=_params(2),
    )(dout, dout, dout, src, src, src, w, b, dsrc)


def _silu_grad(x, s):
    return s * (1.0 + x * (1.0 - s))


STRIP = 16
LCH = 512


def _strips(tm, fn):
    def step(s, carry):
        fn(pl.ds(pl.multiple_of(s * STRIP, STRIP), STRIP))
        return carry

    lax.fori_loop(0, tm // STRIP, step, 0, unroll=8)


def _chunks(e):
    return [slice(k, k + LCH) for k in range(0, e, LCH)]


def _row_sum(parts):
    acc = parts[0]
    for p in parts[1:]:
        acc = acc + p
    return jnp.sum(acc, axis=-1, keepdims=True)


def _fold8(x):
    return jnp.sum(x.reshape(STRIP // 8, 8, x.shape[-1]), axis=0)


def _ln_stats(u_ref, r, cks, e):
    mu = _row_sum([u_ref[r, ck].astype(F32) for ck in cks]) * (1.0 / e)
    var = _row_sum([jnp.square(u_ref[r, ck].astype(F32) - mu) for ck in cks]) * (1.0 / e)
    return mu, lax.rsqrt(var + EPS)


def conf_ln_fwd(u2, proj, ln_w, ln_b, *, name):
    T, E = u2.shape
    zc = proj.shape[1] // E - 1
    tm = _pick(T, (256, 128))
    cks = _chunks(E)

    def body(u_ref, z_ref, w_ref, b_ref, o_ref):
        def strip(r):
            mu, rstd = _ln_stats(u_ref, r, cks, E)
            for ck in cks:
                u3 = (u_ref[r, ck].astype(F32) - mu) * rstd * w_ref[:, ck] + b_ref[:, ck]
                z = z_ref[r, ck].astype(F32)
                o_ref[r, ck] = (u3 * _sigmoid(u3) * z * _sigmoid(z)).astype(BF16)

        _strips(tm, strip)

    row = pl.BlockSpec((tm, E), lambda i: (i, 0))
    vec = pl.BlockSpec((1, E), lambda i: (0, 0))
    return pl.pallas_call(
        body, name=name, grid=(T // tm,),
        in_specs=[row, pl.BlockSpec((tm, E), lambda i: (i, zc)), vec, vec], out_specs=row,
        out_shape=jax.ShapeDtypeStruct((T, E), BF16), compiler_params=_params(1),
    )(u2, proj, ln_w, ln_b)


def conf_ln_bwd(du4, u2, proj, ln_w, ln_b, *, name):
    T, E = u2.shape
    ncol = proj.shape[1] // E
    zc = ncol - 1
    tm = _pick(T, (256, 128))
    n_i = T // tm
    cks = _chunks(E)

    def body(d_ref, u_ref, z_ref, w_ref, b_ref, du_ref, dz_ref, dw_ref, db_ref, dxh_ref, accw_ref, accb_ref):
        i = pl.program_id(0)

        @pl.when(i == 0)
        def _():
            accw_ref[...] = jnp.zeros_like(accw_ref)
            accb_ref[...] = jnp.zeros_like(accb_ref)

        def strip(r):
            mu, rstd = _ln_stats(u_ref, r, cks, E)
            s1, s2 = [], []
            for ck in cks:
                xhat = (u_ref[r, ck].astype(F32) - mu) * rstd
                u3 = xhat * w_ref[:, ck] + b_ref[:, ck]
                z = z_ref[r, ck].astype(F32)
                s3, sz = _sigmoid(u3), _sigmoid(z)
                d4 = d_ref[r, ck].astype(F32)
                du3 = d4 * (z * sz) * _silu_grad(u3, s3)
                dz_ref[r, ck] = (d4 * (u3 * s3) * _silu_grad(z, sz)).astype(BF16)
                accw_ref[:, ck] += _fold8(du3 * xhat)
                accb_ref[:, ck] += _fold8(du3)
                dxh = du3 * w_ref[:, ck]
                dxh_ref[:, ck] = dxh
                s1.append(dxh)
                s2.append(dxh * xhat)
            m1, m2 = _row_sum(s1) * (1.0 / E), _row_sum(s2) * (1.0 / E)
            for ck in cks:
                xhat = (u_ref[r, ck].astype(F32) - mu) * rstd
                du_ref[r, ck] = (rstd * (dxh_ref[:, ck] - m1 - xhat * m2)).astype(BF16)

        _strips(tm, strip)

        @pl.when(i == n_i - 1)
        def _():
            dw_ref[...] = jnp.sum(accw_ref[...], axis=0, keepdims=True)
            db_ref[...] = jnp.sum(accb_ref[...], axis=0, keepdims=True)

    row = pl.BlockSpec((tm, E), lambda i: (i, 0))
    zrow = pl.BlockSpec((tm, E), lambda i: (i, zc))
    vec = pl.BlockSpec((1, E), lambda i: (0, 0))
    return pl.pallas_call(
        body, name=name, grid=(n_i,), in_specs=[row, row, zrow, vec, vec], out_specs=(row, zrow, vec, vec),
        out_shape=(jax.ShapeDtypeStruct((T, E), BF16), jax.ShapeDtypeStruct(proj.shape, BF16),
                   jax.ShapeDtypeStruct((1, E), F32), jax.ShapeDtypeStruct((1, E), F32)),
        scratch_shapes=[pltpu.VMEM((STRIP, E), F32), pltpu.VMEM((8, E), F32), pltpu.VMEM((8, E), F32)],
        compiler_params=_params(1),
    )(du4, u2, proj, ln_w, ln_b)


def _gated(y_ref, z_ref, r, ck):
    z = z_ref[r, ck].astype(F32)
    sz = _sigmoid(z)
    yv = y_ref[r, ck].astype(F32)
    return z, sz, yv, yv * (z * sz)


def ssd_gate_fwd(y, zx, norm_w, *, name):
    T, E = y.shape
    tm = _pick(T, (256, 128))
    cks = _chunks(E)

    def body(y_ref, z_ref, w_ref, o_ref, yz_ref):
        def strip(r):
            sq = []
            for ck in cks:
                yz = _gated(y_ref, z_ref, r, ck)[3]
                yz_ref[:, ck] = yz
                sq.append(yz * yz)
            rstd = lax.rsqrt(_row_sum(sq) * (1.0 / E) + EPS)
            for ck in cks:
                o_ref[r, ck] = (yz_ref[:, ck] * rstd * w_ref[:, ck]).astype(BF16)

        _strips(tm, strip)

    row = pl.BlockSpec((tm, E), lambda i: (i, 0))
    vec = pl.BlockSpec((1, E), lambda i: (0, 0))
    return pl.pallas_call(
        body, name=name, grid=(T // tm,), in_specs=[row, row, vec], out_specs=row,
        out_shape=jax.ShapeDtypeStruct((T, E), BF16), scratch_shapes=[pltpu.VMEM((STRIP, E), F32)],
        compiler_params=_params(1),
    )(y, zx, norm_w)


def ssd_gate_bwd(dyn, y, zx, norm_w, *, name):
    T, E = y.shape
    tm = _pick(T, (256, 128))
    n_i = T // tm
    cks = _chunks(E)

    def body(d_ref, y_ref, z_ref, w_ref, dy_ref, dz_ref, dw_ref, yz_ref, accw_ref):
        i = pl.program_id(0)

        @pl.when(i == 0)
        def _():
            accw_ref[...] = jnp.zeros_like(accw_ref)

        def strip(r):
            sq = []
            for ck in cks:
                yz = _gated(y_ref, z_ref, r, ck)[3]
                yz_ref[:, ck] = yz
                sq.append(yz * yz)
            rstd = lax.rsqrt(_row_sum(sq) * (1.0 / E) + EPS)
            s2 = []
            for ck in cks:
                yhat = yz_ref[:, ck] * rstd
                d = d_ref[r, ck].astype(F32)
                accw_ref[:, ck] += _fold8(d * yhat)
                s2.append(d * w_ref[:, ck] * yhat)
            m2 = _row_sum(s2) * (1.0 / E)
            for ck in cks:
                z, sz, yv, _ = _gated(y_ref, z_ref, r, ck)
                dyz = rstd * (d_ref[r, ck].astype(F32) * w_ref[:, ck] - yz_ref[:, ck] * rstd * m2)
                dy_ref[r, ck] = (dyz * (z * sz)).astype(BF16)
                dz_ref[r, ck] = (dyz * yv * _silu_grad(z, sz)).astype(BF16)

        _strips(tm, strip)

        @pl.when(i == n_i - 1)
        def _():
            dw_ref[...] = jnp.sum(accw_ref[...], axis=0, keepdims=True)

    row = pl.BlockSpec((tm, E), lambda i: (i, 0))
    vec = pl.BlockSpec((1, E), lambda i: (0, 0))
    return pl.pallas_call(
        body, name=name, grid=(n_i,), in_specs=[row, row, row, vec], out_specs=(row, row, vec),
        out_shape=(jax.ShapeDtypeStruct((T, E), BF16), jax.ShapeDtypeStruct(zx.shape, BF16),
                   jax.ShapeDtypeStruct((1, E), F32)),
        scratch_shapes=[pltpu.VMEM((STRIP, E), F32), pltpu.VMEM((8, E), F32)],
        compiler_params=_params(1),
    )(dyn, y, zx, norm_w)


def _cumsum_mm(mask, a):
    hi = a.astype(BF16)
    r1 = a - hi.astype(F32)
    mid = r1.astype(BF16)
    lo = (r1 - mid.astype(F32)).astype(BF16)
    out = _dot(jnp.where(mask, 1.0, 0.0).astype(BF16), jnp.concatenate([hi, mid, lo], axis=1))
    return out[:, :LANES] + out[:, LANES:2 * LANES] + out[:, 2 * LANES:]


def _chunk_terms(xcb, dt_raw, bias, alog, rev):
    L = CHUNK
    xs = xcb[:, :GW].astype(F32)
    Bm = xcb[:, GW:GW + D_STATE]
    Cm = xcb[:, GW + D_STATE:]
    pre = dt_raw + bias
    dt = _softplus(pre)
    A = -jnp.exp(alog)
    row = lax.broadcasted_iota(jnp.int32, (L, L), 0)
    col = lax.broadcasted_iota(jnp.int32, (L, L), 1)
    mask = (col >= row) if rev else (col <= row)
    mask_t = (col <= row) if rev else (col >= row)
    cs = _cumsum_mm(mask, dt * A)
    tot = cs[0:1, :] if rev else cs[L - 1:L, :]
    return xs, Bm, Cm, pre, dt, A, mask, mask_t, cs, cs.T, tot


def _decay(cs, cs_t, ln, mask):
    d = cs[:, ln:ln + 1] - cs_t[ln:ln + 1, :]
    return jnp.where(mask, jnp.exp(jnp.where(mask, d, 0.0)), 0.0)


def _pair(v, ln0, lo):
    return jnp.where(lo[:v.shape[0]], v[:, ln0:ln0 + 1], v[:, ln0 + 1:ln0 + 2])


def _scan_specs(nc, rev_order):
    ci = (lambda c: nc - 1 - c) if rev_order else (lambda c: c)
    xc = pl.BlockSpec((CHUNK, GROUPS * XCG), lambda c: (ci(c), 0))
    dt = pl.BlockSpec((CHUNK, GROUPS * LANES), lambda c: (ci(c), 0))
    vec = pl.BlockSpec((1, GROUPS * LANES), lambda c: (0, 0))
    wide = pl.BlockSpec((CHUNK, GROUPS * GW), lambda c: (ci(c), 0))
    wvec = pl.BlockSpec((1, GROUPS * GW), lambda c: (0, 0))
    st = pl.BlockSpec((1, D_STATE, GROUPS * GW), lambda c: (ci(c), 0, 0))
    return xc, dt, vec, wide, wvec, st


def _cols(ref, g, width):
    return ref.at[:, pl.ds(g * width, width)]


def _head_expand(r):
    row = lax.broadcasted_iota(jnp.int32, (LANES, GW), 0)
    col = lax.broadcasted_iota(jnp.int32, (LANES, GW), 1)
    first = (row - r * HPG) * HEADDIM
    return jnp.where(jnp.logical_and(col >= first, col < first + HEADDIM), 1.0, 0.0).astype(BF16)


def _head_collect(r):
    row = lax.broadcasted_iota(jnp.int32, (GW, LANES), 0)
    first = (lax.broadcasted_iota(jnp.int32, (GW, LANES), 1) - r * HPG) * HEADDIM
    return jnp.where(jnp.logical_and(row >= first, row < first + HEADDIM), 1.0, 0.0).astype(BF16)


def _expand(parts, sel):
    n = parts[0].shape[0]
    out = _dot(jnp.concatenate(parts, axis=0).astype(BF16), sel)
    return [out[i * n:(i + 1) * n] for i in range(len(parts))]


def ssd_scan_fwd(xc, dt4, bias4, alog4, *, rev, name, prev=None, dvec=None):
    T = xc.shape[0]
    nc = T // CHUNK
    E = GROUPS * GW
    r = 1 if rev else 0
    skip = prev is not None

    def one_group(sel, xc_ref, dt_ref, bias_ref, alog_ref, prev_ref, dvec_ref, y_ref, st_ref, s_ref):
        xs, Bm, Cm, _, dt, _, mask, _, cs, cs_t, tot = _chunk_terms(xc_ref[...], dt_ref[...], bias_ref[...],
                                                                   alog_ref[...], rev)
        dtx, ex, dx = _expand([dt, jnp.exp(cs), jnp.exp(tot - cs)], sel)
        et = jnp.exp(tot)
        cb = _dot_nt(Cm, Bm)
        sb = s_ref[...].astype(BF16)
        st_ref[...] = sb
        xp_all = xs * dtx
        y_off = _dot(Cm, sb) * ex
        lo = lax.broadcasted_iota(jnp.int32, (CHUNK, LANES), 1) < HEADDIM
        et_parts = []
        for p in range(HPG // 2):
            ln0 = r * HPG + 2 * p
            sl = slice(p * LANES, (p + 1) * LANES)
            xp = xp_all[:, sl]
            mcat = jnp.concatenate([cb * _decay(cs, cs_t, ln0, mask), cb * _decay(cs, cs_t, ln0 + 1, mask)],
                                   axis=1).astype(BF16)
            xbd = jnp.concatenate([jnp.where(lo, xp, 0.0), jnp.where(lo, 0.0, xp)], axis=0).astype(BF16)
            yp = _dot(mcat, xbd) + y_off[:, sl]
            if skip:
                yp = yp + prev_ref[:, sl].astype(F32) + xs[:, sl] * dvec_ref[:, sl]
            y_ref[:, sl] = yp.astype(BF16)
            et_parts.append(_pair(et, ln0, lo))
        s_ref[...] = s_ref[...] * jnp.concatenate(et_parts, axis=1) + _dot_tn(Bm, (xp_all * dx).astype(BF16))

    def body(*refs):
        xc_ref, dt_ref, bias_ref, alog_ref = refs[:4]
        prev_ref, dvec_ref = (refs[4], refs[5]) if skip else (None, None)
        y_ref, st_ref, s_ref = refs[-3:]

        @pl.when(pl.program_id(0) == 0)
        def _():
            s_ref[...] = jnp.zeros_like(s_ref)

        sel = _head_expand(r)
        for g in range(GROUPS):
            one_group(sel, _cols(xc_ref, g, XCG), _cols(dt_ref, g, LANES), _cols(bias_ref, g, LANES),
                      _cols(alog_ref, g, LANES), _cols(prev_ref, g, GW) if skip else None,
                      _cols(dvec_ref, g, GW) if skip else None, _cols(y_ref, g, GW),
                      st_ref.at[0, :, pl.ds(g * GW, GW)], _cols(s_ref, g, GW))

    s_xc, s_dt, s_vec, s_wide, s_wvec, s_st = _scan_specs(nc, rev)
    in_specs = [s_xc, s_dt, s_vec, s_vec]
    args = [xc, dt4, bias4, alog4]
    if skip:
        in_specs += [s_wide, s_wvec]
        args += [prev, dvec]
    return pl.pallas_call(
        body, name=name, grid=(nc,), in_specs=in_specs, out_specs=(s_wide, s_st),
        out_shape=(jax.ShapeDtypeStruct((T, E), BF16), jax.ShapeDtypeStruct((nc, D_STATE, E), BF16)),
        scratch_shapes=[pltpu.VMEM((D_STATE, E), F32)], compiler_params=_params(1),
    )(*args)


def ssd_scan_bwd(xc, dt4, bias4, alog4, dy, states, *, rev, name, prev=None, dvec=None):
    T = xc.shape[0]
    nc = T // CHUNK
    E = GROUPS * GW
    L = CHUNK
    r = 1 if rev else 0
    skip = prev is not None

    def one_group(sel, sel_t, xc_ref, dt_ref, bias_ref, alog_ref, dy_ref, st_ref, pdxc_ref, pddt_ref, dvec_ref,
                  dxc_ref, ddt_ref, dalog_ref, dbias_ref, dd_ref, g_ref):
        xs, Bm, Cm, pre, dt, A, mask, mask_t, cs, cs_t, tot = _chunk_terms(
            xc_ref[...], dt_ref[...], bias_ref[...], alog_ref[...], rev)
        dtx, ex, dx = _expand([dt, jnp.exp(cs), jnp.exp(tot - cs)], sel)
        et = jnp.exp(tot)
        cb = _dot_nt(Cm, Bm)
        s_in = st_ref[...]
        dy_all = dy_ref[...].astype(F32)
        g_f = g_ref[...]
        g_b = g_f.astype(BF16)
        xp_all = xs * dtx
        dye_all = dy_all * ex
        bgd = _dot(Bm, g_b) * dx
        lane = lax.broadcasted_iota(jnp.int32, (L, LANES), 1)
        lo = lane < HEADDIM
        dcb = jnp.zeros((L, L), F32)
        yd_parts, dxd_parts, et_parts = [], [], []
        for p in range(HPG // 2):
            ln0 = r * HPG + 2 * p
            sl = slice(p * LANES, (p + 1) * LANES)
            xp, dy_p = xp_all[:, sl], dy_all[:, sl]
            lam0, lam1 = _decay(cs, cs_t, ln0, mask), _decay(cs, cs_t, ln0 + 1, mask)
            m0, m1 = (cb * lam0).astype(BF16), (cb * lam1).astype(BF16)
            dybd = jnp.concatenate([jnp.where(lo, dy_p, 0.0), jnp.where(lo, 0.0, dy_p)], axis=0).astype(BF16)
            xbd = jnp.concatenate([jnp.where(lo, xp, 0.0), jnp.where(lo, 0.0, xp)], axis=0).astype(BF16)
            dm = _dot_nt(dybd, xp.astype(BF16))
            dcb = dcb + dm[:L] * lam0 + dm[L:] * lam1
            yd_parts.append(_dot(jnp.concatenate([m0, m1], axis=1), xbd))
            dxd_parts.append(_dot_tn(jnp.concatenate([m0, m1], axis=0), dybd))
            et_parts.append(_pair(et, ln0, lo))
        y_diag = jnp.concatenate(yd_parts, axis=1)
        dx_diag = jnp.concatenate(dxd_parts, axis=1)
        etx = jnp.concatenate(et_parts, axis=1)
        dxt = dx_diag + bgd
        w2 = xp_all * bgd
        dy_r, xp_r = dy_all.astype(BF16).astype(F32), xp_all.astype(BF16).astype(F32)
        u = dye_all * _dot(Cm, s_in) + dy_r * y_diag - xp_r * dx_diag - w2
        dxx = dxt * xs
        tail = jnp.broadcast_to(jnp.sum(w2, axis=0, keepdims=True)
                                + jnp.sum(g_f * s_in.astype(F32), axis=0, keepdims=True) * etx, (8, GW))
        u_hi, t_hi = u.astype(BF16), tail.astype(BF16)
        red = _dot(jnp.concatenate([u_hi, (u - u_hi.astype(F32)).astype(BF16), dxx.astype(BF16), t_hi,
                                    (tail - t_hi.astype(F32)).astype(BF16)], axis=0), sel_t)
        dcs = red[:L] + red[L:2 * L]
        ddt = red[2 * L:3 * L]
        dtot = red[3 * L:3 * L + 1] + red[3 * L + 8:3 * L + 9]
        dxs = dxt * dtx
        if skip:
            dxs = dxs + dy_all * dvec_ref[...] + pdxc_ref[:, :GW].astype(F32)
            dd_ref[...] += jnp.sum(dy_all * xs, axis=0, keepdims=True)
        dxc_ref[:, :GW] = dxs.astype(BF16)
        dye_b = dye_all.astype(BF16)
        xd = (xp_all * dx).astype(BF16)
        dcb_b = dcb.astype(BF16)
        d_b = _dot_nt(xd, g_b) + _dot_tn(dcb_b, Cm)
        d_c = _dot_nt(dye_b, s_in) + _dot(dcb_b, Bm)
        if skip:
            d_b = d_b + pdxc_ref[:, GW:GW + D_STATE].astype(F32)
            d_c = d_c + pdxc_ref[:, GW + D_STATE:].astype(F32)
        dxc_ref[:, GW:GW + D_STATE] = d_b.astype(BF16)
        dxc_ref[:, GW + D_STATE:] = d_c.astype(BF16)
        g_ref[...] = g_f * etx + _dot_tn(Cm, dye_b)
        rowi = lax.broadcasted_iota(jnp.int32, (L, LANES), 0)
        da = _cumsum_mm(mask_t, dcs + jnp.where(rowi == (0 if rev else L - 1), dtot, 0.0))
        keep = jnp.logical_and(lane >= r * HPG, lane < (r + 1) * HPG)
        ddr = jnp.where(keep, (da * A + ddt) * _sigmoid(pre), 0.0)
        dbias_ref[...] += jnp.sum(ddr, axis=0, keepdims=True)
        dalog_ref[...] += jnp.sum(jnp.where(keep, da * dt * A, 0.0), axis=0, keepdims=True)
        if skip:
            ddr = ddr + pddt_ref[...]
        ddt_ref[...] = ddr

    def body(*refs):
        xc_ref, dt_ref, bias_ref, alog_ref, dy_ref, st_ref = refs[:6]
        pdxc_ref, pddt_ref, dvec_ref = refs[6:9] if skip else (None, None, None)
        pos = 9 if skip else 6
        dxc_ref, ddt_ref, dalog_ref, dbias_ref = refs[pos:pos + 4]
        dd_ref = refs[pos + 4] if skip else None
        g_ref = refs[-1]

        @pl.when(pl.program_id(0) == 0)
        def _():
            g_ref[...] = jnp.zeros_like(g_ref)
            dalog_ref[...] = jnp.zeros_like(dalog_ref)
            dbias_ref[...] = jnp.zeros_like(dbias_ref)
            if skip:
                dd_ref[...] = jnp.zeros_like(dd_ref)

        sel, sel_t = _head_expand(r), _head_collect(r)
        for g in range(GROUPS):
            one_group(sel, sel_t, _cols(xc_ref, g, XCG), _cols(dt_ref, g, LANES), _cols(bias_ref, g, LANES),
                      _cols(alog_ref, g, LANES), _cols(dy_ref, g, GW), st_ref.at[0, :, pl.ds(g * GW, GW)],
                      _cols(pdxc_ref, g, XCG) if skip else None, _cols(pddt_ref, g, LANES) if skip else None,
                      _cols(dvec_ref, g, GW) if skip else None, _cols(dxc_ref, g, XCG), _cols(ddt_ref, g, LANES),
                      _cols(dalog_ref, g, LANES), _cols(dbias_ref, g, LANES),
                      _cols(dd_ref, g, GW) if skip else None, _cols(g_ref, g, GW))

    s_xc, s_dt, s_vec, s_wide, s_wvec, s_st = _scan_specs(nc, not rev)
    in_specs = [s_xc, s_dt, s_vec, s_vec, s_wide, s_st]
    args = [xc, dt4, bias4, alog4, dy, states]
    out_specs = [s_xc, s_dt, s_vec, s_vec]
    out_shape = [jax.ShapeDtypeStruct((T, GROUPS * XCG), BF16), jax.ShapeDtypeStruct((T, GROUPS * LANES), F32),
                 jax.ShapeDtypeStruct((1, GROUPS * LANES), F32), jax.ShapeDtypeStruct((1, GROUPS * LANES), F32)]
    if skip:
        in_specs += [s_xc, s_dt, s_wvec]
        args += [prev[0], prev[1], dvec]
        out_specs.append(s_wvec)
        out_shape.append(jax.ShapeDtypeStruct((1, E), F32))
    return pl.pallas_call(
        body, name=name, grid=(nc,), in_specs=in_specs, out_specs=tuple(out_specs),
        out_shape=tuple(out_shape), scratch_shapes=[pltpu.VMEM((D_STATE, E), F32)], compiler_params=_params(1),
    )(*args)


def _conf_cols(w):
    e = w.shape[-1] // 3
    lead = w.shape[:-1]
    vg = w[..., :2 * e].reshape(*lead, 2, e // CONV_TC, CONV_TC)
    vg = jnp.swapaxes(vg, -3, -2).reshape(*lead, 2 * e)
    return jnp.concatenate([vg, w[..., 2 * e:]], axis=-1)


def _conf_cols_inv(w):
    e = w.shape[-1] // 3
    lead = w.shape[:-1]
    vg = w[..., :2 * e].reshape(*lead, e // CONV_TC, 2, CONV_TC)
    vg = jnp.swapaxes(vg, -3, -2).reshape(*lead, 2 * e)
    return jnp.concatenate([vg, w[..., 2 * e:]], axis=-1)


def _xbc_cols(w):
    lead = w.shape[:-1]
    e = GROUPS * GW
    gn = GROUPS * D_STATE
    parts = [w[..., :e].reshape(*lead, GROUPS, GW), w[..., e:e + gn].reshape(*lead, GROUPS, D_STATE),
             w[..., e + gn:].reshape(*lead, GROUPS, D_STATE)]
    return jnp.concatenate(parts, axis=-1).reshape(*lead, GROUPS * XCG)


def _xbc_cols_inv(w):
    lead = w.shape[:-1]
    g = w.reshape(*lead, GROUPS, XCG)
    parts = [g[..., :GW].reshape(*lead, GROUPS * GW), g[..., GW:GW + D_STATE].reshape(*lead, GROUPS * D_STATE),
             g[..., GW + D_STATE:].reshape(*lead, GROUPS * D_STATE)]
    return jnp.concatenate(parts, axis=-1)


def _dt_cols(w):
    lead = w.shape[:-1]
    t = jnp.swapaxes(w.reshape(*lead, 2, GROUPS, HPG), -3, -2).reshape(*lead, GROUPS, 2 * HPG)
    pad = [(0, 0)] * (t.ndim - 1) + [(0, LANES - 2 * HPG)]
    return jnp.pad(t, pad).reshape(*lead, GROUPS * LANES)


def _dt_cols_inv(w):
    lead = w.shape[:-1]
    t = w.reshape(*lead, GROUPS, LANES)[..., :2 * HPG].reshape(*lead, GROUPS, 2, HPG)
    return jnp.swapaxes(t, -3, -2).reshape(*lead, 2 * HEADS)


def _pad_rows(w, rows):
    return jnp.pad(w, ((0, rows - w.shape[0]), (0, 0)))


def conf_weights(w_in, dw_w, dw_b, ln_w, ln_b):
    w_in_p = _conf_cols(w_in)
    return dict(w_in=w_in_p, w_in_t=w_in_p.T, dw_w=_pad_rows(dw_w, 32), dw_b=dw_b.reshape(1, -1),
                ln_w=ln_w.reshape(1, -1), ln_b=ln_b.reshape(1, -1))


def _xbc_rows(w):
    e, gn, c = GROUPS * GW, GROUPS * D_STATE, w.shape[1]
    parts = [w[:e].reshape(GROUPS, GW, c), w[e:e + gn].reshape(GROUPS, D_STATE, c),
             w[e + gn:].reshape(GROUPS, D_STATE, c)]
    return jnp.concatenate(parts, axis=1).reshape(GROUPS * XCG, c)


def _xbc_rows_inv(w):
    c = w.shape[1]
    g = w.reshape(GROUPS, XCG, c)
    parts = [g[:, :GW].reshape(GROUPS * GW, c), g[:, GW:GW + D_STATE].reshape(GROUPS * D_STATE, c),
             g[:, GW + D_STATE:].reshape(GROUPS * D_STATE, c)]
    return jnp.concatenate(parts, axis=0)


def _dt_rows(w):
    c = w.shape[1]
    t = jnp.swapaxes(w.reshape(2, GROUPS, HPG, c), 0, 1).reshape(GROUPS, 2 * HPG, c)
    return jnp.pad(t, ((0, 0), (0, LANES - 2 * HPG), (0, 0))).reshape(GROUPS * LANES, c)


def _dt_rows_inv(w):
    c = w.shape[1]
    t = w.reshape(GROUPS, LANES, c)[:, :2 * HPG].reshape(GROUPS, 2, HPG, c)
    return jnp.swapaxes(t, 0, 1).reshape(2 * HEADS, c)


def ssd_weights(w_in_t, conv_w, conv_b, dt_bias, a_log, d_skip, norm_w, w_out):
    e = GROUPS * GW
    xbc = e + 2 * GROUPS * D_STATE
    w_zx_t = jnp.concatenate([w_in_t[:e], _xbc_rows(w_in_t[e:e + xbc])], axis=0)
    return dict(w_zx_t=w_zx_t, w_dt_t=_dt_rows(w_in_t[e + xbc:]), w_out=w_out, w_out_t=w_out.T,
                conv_w=_pad_rows(_xbc_cols(conv_w), 8), conv_b=_xbc_cols(conv_b.reshape(1, -1)),
                bias4=_dt_cols(dt_bias.reshape(1, -1)), alog4=_dt_cols(a_log.reshape(1, -1)),
                dvec=jnp.repeat(d_skip, HEADDIM).reshape(1, -1), norm_w=norm_w.reshape(1, -1))


def conf_layer_fwd(h, nw, p, tag, side=(), w_out=None):
    hn = rmsnorm_fwd(h, nw, name=f"{tag}_norm")
    proj = mm_nn(hn, p["w_in"], out_dtype=BF16, name=f"{tag}_proj")
    u2 = dwconv_fwd(proj, p["dw_w"], p["dw_b"], width=31, glu=True, silu=False, col0=0, name=f"{tag}_conv", side=side)
    if side:
        u2, gathered = u2
        w_out = w_out(gathered)
    p.update(w_out=w_out, w_out_t=w_out.T)
    u4 = conf_ln_fwd(u2, proj, p["ln_w"], p["ln_b"], name=f"{tag}_ln")
    h2 = mm_nn(u4, p["w_out"], out_dtype=F32, res=h, name=f"{tag}_out")
    return h2, (h, hn, proj, u2, u4)


def conf_layer_bwd(dh, saved, nw, p, tag):
    h, hn, proj, u2, u4 = saved
    du4 = mm_nn(dh, p["w_out_t"], out_dtype=BF16, name=f"{tag}_d_u4")
    dw_out = mm_tn(u4, dh, name=f"{tag}_dw_out")
    du2, dproj, dln_w, dln_b = conf_ln_bwd(du4, u2, proj, p["ln_w"], p["ln_b"], name=f"{tag}_d_ln")
    dproj, ddw_w, ddw_b = dwconv_bwd(du2, proj, p["dw_w"], p["dw_b"], dproj, width=31, glu=True, silu=False,
                                     col0=0, dcol0=0, name=f"{tag}_d_conv")
    dh_prev, dnw = mm_nn(dproj, p["w_in_t"], out_dtype=F32, norm_bwd=(h, nw, dh), name=f"{tag}_d_hn")
    dw_in = mm_tn(hn, dproj, name=f"{tag}_dw_in")
    grads = dict(w_in=_conf_cols_inv(dw_in), dw_w=ddw_w[:31], dw_b=ddw_b[0], ln_w=dln_w[0], ln_b=dln_b[0],
                 w_out=dw_out, norm=dnw[0])
    return dh_prev, grads


def ssd_layer_fwd(h, nw, p, tag):
    e = GROUPS * GW
    hn = rmsnorm_fwd(h, nw, name=f"{tag}_norm")
    zx = mm_nn(hn, p["w_zx_t"], out_dtype=BF16, b_rows_are_n=True, name=f"{tag}_proj")
    dt4 = mm_nn(hn, p["w_dt_t"], out_dtype=F32, b_rows_are_n=True, name=f"{tag}_proj_dt")
    xc = dwconv_fwd(zx, p["conv_w"], p["conv_b"], width=5, glu=False, silu=True, col0=e // CONV_TC, name=f"{tag}_conv")
    y0, st0 = ssd_scan_fwd(xc, dt4, p["bias4"], p["alog4"], rev=False, name=f"{tag}_scan_f")
    y, st1 = ssd_scan_fwd(xc, dt4, p["bias4"], p["alog4"], rev=True, prev=y0, dvec=p["dvec"], name=f"{tag}_scan_b")
    yn = ssd_gate_fwd(y, zx, p["norm_w"], name=f"{tag}_gate")
    h2 = mm_nn(yn, p["w_out"], out_dtype=F32, res=h, name=f"{tag}_out")
    return h2, (h, hn, zx, dt4, xc, st0, st1, y, yn)


def ssd_layer_bwd(dh, saved, nw, p, tag):
    e = GROUPS * GW
    h, hn, zx, dt4, xc, st0, st1, y, yn = saved
    dyn = mm_nn(dh, p["w_out_t"], out_dtype=BF16, name=f"{tag}_d_yn")
    dw_out = mm_tn(yn, dh, name=f"{tag}_dw_out")
    dy, dzx, dnorm_w = ssd_gate_bwd(dyn, y, zx, p["norm_w"], name=f"{tag}_d_gate")
    dxc0, ddt0, dalog0, dbias0 = ssd_scan_bwd(xc, dt4, p["bias4"], p["alog4"], dy, st0, rev=False,
                                              name=f"{tag}_d_scan_f")
    dxc, ddt4, dalog1, dbias1, ddvec = ssd_scan_bwd(xc, dt4, p["bias4"], p["alog4"], dy, st1, rev=True,
                                                    prev=(dxc0, ddt0), dvec=p["dvec"], name=f"{tag}_d_scan_b")
    dzx, dconv_w, dconv_b = dwconv_bwd(dxc, zx, p["conv_w"], p["conv_b"], dzx, width=5, glu=False, silu=True,
                                       col0=e // CONV_TC, dcol0=e // CONV_TC, name=f"{tag}_d_conv")
    dh_prev, dnw = mm_nn(dzx, p["w_zx_t"], out_dtype=F32, a2=ddt4, b2=p["w_dt_t"], norm_bwd=(h, nw, dh),
                         name=f"{tag}_d_hn")
    dw_zx_t = mm_tn(dzx, hn, name=f"{tag}_dw_zx")
    dw_dt_t = mm_tn(ddt4, hn, name=f"{tag}_dw_dt")
    dw_in_t = jnp.concatenate([dw_zx_t[:e], _xbc_rows_inv(dw_zx_t[e:]), _dt_rows_inv(dw_dt_t)], axis=0)
    grads = dict(w_in_t=dw_in_t, conv_w=_xbc_cols_inv(dconv_w[:5]), conv_b=_xbc_cols_inv(dconv_b)[0],
                 dt_bias=_dt_cols_inv(dbias0 + dbias1).reshape(2, HEADS),
                 a_log=_dt_cols_inv(dalog0 + dalog1).reshape(2, HEADS),
                 d_skip=jnp.sum(ddvec.reshape(HEADS, HEADDIM), axis=-1), norm_w=dnorm_w[0], w_out=dw_out,
                 norm=dnw[0])
    return dh_prev, grads


def gather_chips(bufs, *, name):
    n = len(bufs)
    rows = [b.shape[0] for b in bufs]

    def body(*refs):
        ins, outs, sems = refs[:n], refs[n:2 * n], refs[2 * n:]
        _gather_start(rows, ins, outs, sems)
        _gather_finish(rows, ins, outs, sems)

    return pl.pallas_call(
        body, name=name, in_specs=[ANY] * n, out_specs=tuple([ANY] * n),
        out_shape=tuple(jax.ShapeDtypeStruct((N_CHIPS,) + b.shape, b.dtype) for b in bufs),
        scratch_shapes=_gather_sems(n),
    )(*bufs)


def swap_other_half(g2, *, name):
    def body(g_ref, o_ref, send_sem, recv_sem):
        x, y, c = _place()
        cp = pltpu.make_async_remote_copy(src_ref=g_ref.at[1 - c], dst_ref=o_ref, send_sem=send_sem, recv_sem=recv_sem,
                                          device_id=(x, y, 1 - c), device_id_type=MESH)
        cp.start()
        cp.wait()

    return pl.pallas_call(
        body, name=name, in_specs=[ANY], out_specs=ANY, out_shape=jax.ShapeDtypeStruct(g2.shape[1:], g2.dtype),
        scratch_shapes=[pltpu.SemaphoreType.DMA, pltpu.SemaphoreType.DMA],
    )(g2)


def exchange_chips(p, *, name):
    def body(p_ref, o_ref, send_sems, recv_sems, local_sem):
        x, y, c = _place()
        k_me = 2 * x + y
        own = pltpu.make_async_copy(p_ref.at[k_me], o_ref.at[k_me], local_sem)
        own.start()
        copies = [own]
        for j, (px, py) in enumerate([(1 - x, y), (x, 1 - y), (1 - x, 1 - y)]):
            cp = pltpu.make_async_remote_copy(
                src_ref=p_ref.at[2 * px + py], dst_ref=o_ref.at[k_me], send_sem=send_sems.at[j],
                recv_sem=recv_sems.at[j], device_id=(px, py, c), device_id_type=MESH)
            cp.start()
            copies.append(cp)
        for cp in copies:
            cp.wait()

    return pl.pallas_call(
        body, name=name, in_specs=[ANY], out_specs=ANY, out_shape=jax.ShapeDtypeStruct(p.shape, p.dtype),
        scratch_shapes=[pltpu.SemaphoreType.DMA((3,)), pltpu.SemaphoreType.DMA((3,)), pltpu.SemaphoreType.DMA],
    )(p)


def share_half(full, *, name):
    def body(_, f_ref, send_sem, recv_sem):
        x, y, c = _place()
        cp = pltpu.make_async_remote_copy(src_ref=f_ref.at[c], dst_ref=f_ref.at[c], send_sem=send_sem,
                                          recv_sem=recv_sem, device_id=(x, y, 1 - c), device_id_type=MESH)
        cp.start()
        cp.wait()

    return pl.pallas_call(
        body, name=name, in_specs=[ANY], out_specs=ANY, out_shape=jax.ShapeDtypeStruct(full.shape, full.dtype),
        input_output_aliases={0: 0},
        scratch_shapes=[pltpu.SemaphoreType.DMA, pltpu.SemaphoreType.DMA],
    )(full)


def gather_all(v, *, name):
    def body(v_ref, o_ref, send_sems, recv_sems, local_sem):
        x, y, c = _place()
        me = 4 * x + 2 * y + c
        own = pltpu.make_async_copy(v_ref, o_ref.at[me], local_sem)
        own.start()
        copies = [own]
        idx = 0
        for fx in (0, 1):
            for fy in (0, 1):
                for fc in (0, 1):
                    if not (fx or fy or fc):
                        continue
                    peer = (1 - x if fx else x, 1 - y if fy else y, 1 - c if fc else c)
                    cp = pltpu.make_async_remote_copy(src_ref=v_ref, dst_ref=o_ref.at[me], send_sem=send_sems.at[idx],
                                                      recv_sem=recv_sems.at[idx], device_id=peer, device_id_type=MESH)
                    cp.start()
                    copies.append(cp)
                    idx += 1
        for cp in copies:
            cp.wait()

    return pl.pallas_call(
        body, name=name, in_specs=[ANY], out_specs=ANY, out_shape=jax.ShapeDtypeStruct((N_DEV,) + v.shape, v.dtype),
        scratch_shapes=[pltpu.SemaphoreType.DMA((N_DEV - 1,)), pltpu.SemaphoreType.DMA((N_DEV - 1,)),
                        pltpu.SemaphoreType.DMA],
    )(v)


RED_TR = 432


def pair_sum(g2, recv, cidx, *, name):
    _, K, R, C = g2.shape
    tr = _pick(R, (RED_TR, 8))

    def body(c_ref, a_ref, b_ref, o_ref):
        o_ref[...] = (a_ref[0] + b_ref[...]).astype(BF16)

    blk = pl.BlockSpec((1, tr, C), lambda k, i, c: (k, i, 0))
    return pl.pallas_call(
        body, name=name,
        grid_spec=pltpu.PrefetchScalarGridSpec(
            num_scalar_prefetch=1, grid=(K, R // tr),
            in_specs=[pl.BlockSpec((1, 1, tr, C), lambda k, i, c: (c[0], k, i, 0)), blk], out_specs=blk),
        out_shape=jax.ShapeDtypeStruct((K, R, C), BF16), compiler_params=_params(2),
    )(cidx, g2, recv)


def sum_lead(a, *, name, slot=None, nslots=1):
    K, R, C = a.shape
    tr = _pick(R, (RED_TR, 8))

    def body(s_ref, a_ref, o_ref):
        acc = a_ref[0].astype(F32)
        for k in range(1, K):
            acc = acc + a_ref[k].astype(F32)
        o_ref[0] = acc

    if slot is None:
        slot = jnp.zeros((1,), jnp.int32)
    return pl.pallas_call(
        body, name=name,
        grid_spec=pltpu.PrefetchScalarGridSpec(
            num_scalar_prefetch=1, grid=(R // tr,),
            in_specs=[pl.BlockSpec((K, tr, C), lambda i, s: (0, i, 0))],
            out_specs=pl.BlockSpec((1, tr, C), lambda i, s: (s[0], i, 0))),
        out_shape=jax.ShapeDtypeStruct((nslots, R, C), F32), compiler_params=_params(1),
    )(slot, a)


def adamw(g, w, m, v, *, name):
    R, C = w.shape
    tr = _pick(R, (256, 128, 64, 32, 16, 8))

    def body(g_ref, w_ref, m_ref, v_ref, d_ref, nm_ref, nv_ref):
        gv = g_ref[...]
        m_new = ADAM_B1 * m_ref[...] + (1.0 - ADAM_B1) * gv
        v_new = ADAM_B2 * v_ref[...] + (1.0 - ADAM_B2) * (gv * gv)
        m_hat = m_new / (1.0 - ADAM_B1 ** ADAM_STEP)
        v_hat = v_new / (1.0 - ADAM_B2 ** ADAM_STEP)
        d_ref[...] = -ADAM_LR * (m_hat / (jnp.sqrt(v_hat) + ADAM_EPS) + ADAM_WD * w_ref[...])
        nm_ref[...] = m_new
        nv_ref[...] = v_new

    blk = pl.BlockSpec((tr, C), lambda i: (i, 0))
    sds = jax.ShapeDtypeStruct((R, C), F32)
    return pl.pallas_call(
        body, name=name, grid=(R // tr,), in_specs=[blk] * 4, out_specs=(blk,) * 3, out_shape=(sds,) * 3,
        compiler_params=_params(1),
    )(g, w, m, v)


WEIGHTS = ("norm_w", "final_norm_w", "cm_w_in", "cm_dw_w", "cm_dw_b", "cm_ln_w", "cm_ln_b", "cm_w_out", "ssd_w_in",
           "ssd_conv_w", "ssd_conv_b", "ssd_dt_bias", "ssd_A_log", "ssd_D", "ssd_norm_w", "ssd_w_out")
BIG = (("cm_w_in", 2), ("cm_w_out", 1), ("ssd_w_in", 1), ("ssd_w_out", 1))
TRANSPOSED = ("ssd_w_in",)
SMALL_SHARDED = (("cm_dw_w", 2), ("ssd_conv_w", 2), ("ssd_conv_b", 1), ("ssd_norm_w", 1))
REPLICATED = ("norm_w", "final_norm_w", "cm_dw_b", "cm_ln_w", "cm_ln_b", "ssd_dt_bias", "ssd_A_log", "ssd_D")
ROW = 1024


def _to_shards(g, axis):
    n = g.shape[axis]
    s = g.reshape(g.shape[:axis] + (N_CHIPS, n // N_CHIPS) + g.shape[axis + 1:])
    return jnp.moveaxis(s, axis, 0).reshape(N_CHIPS, -1)


def _from_shards(x4, local_shape, axis):
    local_shape = tuple(local_shape)
    s = jnp.moveaxis(x4.reshape((N_CHIPS,) + local_shape), 0, axis)
    return s.reshape(local_shape[:axis] + (N_CHIPS * local_shape[axis],) + local_shape[axis + 1:])


def _flat_pad(parts, multiple):
    n = sum(p.size for p in parts)
    fill = [jnp.zeros(((-n) % multiple,), parts[0].dtype)] if n % multiple else []
    return jnp.concatenate([p.reshape(-1) for p in parts] + fill)


def _split(flat, like, names):
    out, off = {}, 0
    for n in names:
        out[n] = flat[off:off + like[n].size].reshape(like[n].shape)
        off += like[n].size
    return out


def kernel(x, norm_w, final_norm_w, cm_w_in, cm_dw_w, cm_dw_b, cm_ln_w, cm_ln_b, cm_w_out, ssd_w_in, ssd_conv_w, ssd_conv_b, ssd_dt_bias, ssd_A_log, ssd_D, ssd_norm_w, ssd_w_out, loss_target, m_norm_w, m_final_norm_w, m_cm_w_in, m_cm_dw_w, m_cm_dw_b, m_cm_ln_w, m_cm_ln_b, m_cm_w_out, m_ssd_w_in, m_ssd_conv_w, m_ssd_conv_b, m_ssd_dt_bias, m_ssd_A_log, m_ssd_D, m_ssd_norm_w, m_ssd_w_out, v_norm_w, v_final_norm_w, v_cm_w_in, v_cm_dw_w, v_cm_dw_b, v_cm_ln_w, v_cm_ln_b, v_cm_w_out, v_ssd_w_in, v_ssd_conv_w, v_ssd_conv_b, v_ssd_dt_bias, v_ssd_A_log, v_ssd_D, v_ssd_norm_w, v_ssd_w_out):
    a = dict(locals())
    w = {n: a[n] for n in WEIGHTS}
    m = {n: a["m_" + n] for n in WEIGHTS}
    v = {n: a["v_" + n] for n in WEIGHTS}
    _, T, D = x.shape
    cidx = lax.axis_index("c").astype(jnp.int32).reshape(1)
    big_names = [n for n, _ in BIG]
    small_names = [n for n, _ in SMALL_SHARDED]

    wx = {n: (jnp.swapaxes(w[n], 1, 2) if n in TRANSPOSED else w[n]) for n in big_names + small_names}
    big = _flat_pad([wx[n] for n in big_names], 16 * ROW).astype(BF16).reshape(-1, ROW)
    small = _flat_pad([wx[n] for n in small_names], 8 * ROW).reshape(-1, ROW)
    first_name, first_axis = BIG[0]
    first_shape = wx[first_name].shape[1:]
    n_first = math.prod(first_shape) // ROW
    g_first, g_small = gather_chips([big[:n_first], small], name="gather_weights")
    g_small = g_small.reshape(N_CHIPS, -1)
    full, off = {}, 0
    for n, ax in SMALL_SHARDED:
        full[n] = _from_shards(g_small[:, off:off + wx[n].size], wx[n].shape, ax)
        off += wx[n].size
    w_in_0 = _from_shards(g_first.reshape(N_CHIPS, -1), first_shape, first_axis - 1)
    n_layers = norm_w.shape[0]
    lw = [None] * n_layers
    lw[0] = conf_weights(w_in_0, full["cm_dw_w"][0], cm_dw_b[0], cm_ln_w[0], cm_ln_b[0])

    def unpack_rest(gathered):
        g_big = jnp.concatenate([g_first, gathered[0]], axis=1).reshape(N_CHIPS, -1)
        off = 0
        for n, ax in BIG:
            full[n] = _from_shards(g_big[:, off:off + wx[n].size], wx[n].shape, ax)
            off += wx[n].size
        for i in range(1, n_layers):
            j = i // 2
            if i % 2 == 0:
                lw[i] = conf_weights(full["cm_w_in"][j], full["cm_dw_w"][j], cm_dw_b[j], cm_ln_w[j], cm_ln_b[j])
            else:
                lw[i] = ssd_weights(full["ssd_w_in"][j], full["ssd_conv_w"][j], full["ssd_conv_b"][j], ssd_dt_bias[j],
                                    ssd_A_log[j], ssd_D[j], full["ssd_norm_w"][j], full["ssd_w_out"][j])
        return full["cm_w_out"][0]

    h = x[0]
    saved = []
    for i in range(n_layers):
        nw_i = norm_w[i].reshape(1, -1)
        if i == 0:
            h, s = conf_layer_fwd(h, nw_i, lw[0], "l0", side=(big[n_first:],), w_out=unpack_rest)
        elif i % 2 == 0:
            h, s = conf_layer_fwd(h, nw_i, lw[i], f"l{i}", w_out=full["cm_w_out"][i // 2])
        else:
            h, s = ssd_layer_fwd(h, nw_i, lw[i], f"l{i}")
        saved.append(s)
    dh, loss_local, d_final = loss_head(h, loss_target[0], final_norm_w.reshape(1, -1), name="loss_head")
    lg = [None] * n_layers
    for i in reversed(range(n_layers)):
        bwd = conf_layer_bwd if i % 2 == 0 else ssd_layer_bwd
        dh, lg[i] = bwd(dh, saved[i], norm_w[i].reshape(1, -1), lw[i], f"l{i}")
    conf_g, ssd_g = lg[0::2], lg[1::2]
    local = {
        "norm_w": jnp.stack([g["norm"] for g in lg]), "final_norm_w": d_final[0],
        "cm_w_in": jnp.stack([g["w_in"] for g in conf_g]), "cm_dw_w": jnp.stack([g["dw_w"] for g in conf_g]),
        "cm_dw_b": jnp.stack([g["dw_b"] for g in conf_g]), "cm_ln_w": jnp.stack([g["ln_w"] for g in conf_g]),
        "cm_ln_b": jnp.stack([g["ln_b"] for g in conf_g]), "cm_w_out": jnp.stack([g["w_out"] for g in conf_g]),
        "ssd_w_in": jnp.stack([g["w_in_t"] for g in ssd_g]), "ssd_conv_w": jnp.stack([g["conv_w"] for g in ssd_g]),
        "ssd_conv_b": jnp.stack([g["conv_b"] for g in ssd_g]), "ssd_dt_bias": jnp.stack([g["dt_bias"] for g in ssd_g]),
        "ssd_A_log": jnp.stack([g["a_log"] for g in ssd_g]), "ssd_D": jnp.stack([g["d_skip"] for g in ssd_g]),
        "ssd_norm_w": jnp.stack([g["norm_w"] for g in ssd_g]), "ssd_w_out": jnp.stack([g["w_out"] for g in ssd_g]),
    }

    shards = [_to_shards(local[n], ax) for n, ax in BIG + SMALL_SHARDED]
    fill = (-sum(t.shape[1] for t in shards)) % (2 * RED_TR * ROW)
    flat4 = jnp.concatenate(shards + [jnp.zeros((N_CHIPS, fill), F32)], axis=1)
    g2 = jnp.swapaxes(flat4.reshape(N_CHIPS, 2, -1, ROW), 0, 1)
    theirs = swap_other_half(g2, name="grad_pair_swap")
    part = pair_sum(g2, theirs, cidx, name="grad_pair_sum")
    got = exchange_chips(part, name="grad_chip_exchange")
    half = sum_lead(got, slot=cidx, nslots=2, name="grad_chip_sum")
    shard_flat = share_half(half, name="grad_pair_share").reshape(-1)
    grads = _split(shard_flat, wx, big_names + small_names)
    for n in TRANSPOSED:
        grads[n] = jnp.swapaxes(grads[n], 1, 2)

    rep = _flat_pad([local[n] for n in REPLICATED], 8 * LANES).reshape(-1, LANES)
    rep_sum = sum_lead(gather_all(rep, name="grad_small_gather"), name="grad_small_sum")
    grads.update(_split(rep_sum.reshape(-1), w, REPLICATED))

    delta, new_m, new_v = {}, {}, {}
    for n in big_names:
        two_d = (-1, w[n].shape[-1])
        d_, m_, v_ = adamw(grads[n].reshape(two_d), w[n].reshape(two_d), m[n].reshape(two_d), v[n].reshape(two_d),
                           name="adamw_" + n)
        delta[n], new_m[n], new_v[n] = d_.reshape(w[n].shape), m_.reshape(w[n].shape), v_.reshape(w[n].shape)
    rest = list(REPLICATED) + small_names
    packed = [_flat_pad([t[n] for n in rest], 8 * LANES).reshape(-1, LANES) for t in (grads, w, m, v)]
    for out, res in zip((delta, new_m, new_v), adamw(*packed, name="adamw_small")):
        out.update(_split(res.reshape(-1), w, rest))

    loss = lax.psum(loss_local[0, 0], ("x", "y", "c"))
    return (loss, dh.reshape(x.shape), *[grads[n] for n in WEIGHTS], *[delta[n] for n in WEIGHTS],
            *[new_m[n] for n in WEIGHTS], *[new_v[n] for n in WEIGHTS])
```

```python
import itertools
import math

import jax
import jax.numpy as jnp
from jax import lax
from jax.experimental import pallas as pl
from jax.experimental.pallas import tpu as pltpu

F32 = jnp.float32
BF16 = jnp.bfloat16
MESH = pl.DeviceIdType.MESH

EPS = 1e-5
HEADDIM = 64
HEADS = 32
GROUPS = 4
HPG = HEADS // GROUPS
D_STATE = 128
CHUNK = 128
GW = HPG * HEADDIM
XCG = GW + 2 * D_STATE
HALO = 16
LANES = 128
N_CHIPS = 4
N_DEV = 8

ADAM_LR = 0.001
ADAM_B1 = 0.9
ADAM_B2 = 0.999
ADAM_EPS = 1e-08
ADAM_WD = 0.01
ADAM_STEP = 10

VMEM_LIMIT = 52 * 1024 * 1024


def _params(n_axes):
    return pltpu.CompilerParams(dimension_semantics=("arbitrary",) * n_axes, vmem_limit_bytes=VMEM_LIMIT)


def _sigmoid(x):
    return 1.0 / (1.0 + jnp.exp(-x))


def _softplus(x):
    return jnp.maximum(x, 0.0) + jnp.log(1.0 + jnp.exp(-jnp.abs(x)))


def _dot(a, b):
    return jnp.dot(a, b, preferred_element_type=F32)


def _dot_nt(a, b):
    return lax.dot_general(a, b, (((1,), (1,)), ((), ())), preferred_element_type=F32)


def _dot_tn(a, b):
    return lax.dot_general(a, b, (((0,), (0,)), ((), ())), preferred_element_type=F32)


def _pick(n, pref):
    for t in pref:
        if n % t == 0:
            return t
    return n


def mm_nn(a, b, *, out_dtype, name, res=None, a2=None, b2=None, b_rows_are_n=False, norm_bwd=None):
    M, K = a.shape
    N = b.shape[0] if b_rows_are_n else b.shape[1]
    has2, has_res, has_nb = a2 is not None, res is not None, norm_bwd is not None
    tm = _pick(M, (1024, 512, 256, 128))
    tn = N if has_nb else _pick(N, (1024, 512, 256, 128))
    tk = _pick(K, (2048, 1024, 512, 256, 128) if a.dtype == BF16 and not has_nb else (1024, 512, 256, 128))
    nk = K // tk

    def body(*refs):
        a_ref, b_ref = refs[0], refs[1]
        pos = 2
        if has2:
            a2_ref, b2_ref = refs[pos], refs[pos + 1]
            pos += 2
        if has_res:
            r_ref = refs[pos]
            pos += 1
        if has_nb:
            h_ref, w_ref, dh_ref = refs[pos:pos + 3]
            pos += 3
        o_ref = refs[pos]
        acc_ref = refs[-1]
        k = pl.program_id(2)
        first_rows = pl.program_id(0) == 0

        @pl.when(k == 0)
        def _():
            if has2:
                acc_ref[...] = _dot(a2_ref[...].astype(BF16), b2_ref[...])
            else:
                acc_ref[...] = jnp.zeros_like(acc_ref)

        acc_ref[...] += (_dot_nt if b_rows_are_n else _dot)(a_ref[...].astype(BF16), b_ref[...])

        @pl.when(k == nk - 1)
        def _():
            r = acc_ref[...]
            if has_res:
                r = r + r_ref[...]
            if has_nb:
                dw_ref = refs[pos + 1]

                @pl.when(first_rows)
                def _():
                    dw_ref[...] = jnp.zeros_like(dw_ref)

                x = h_ref[...]
                rstd = lax.rsqrt(jnp.mean(x * x, axis=-1, keepdims=True) + EPS)
                xhat = x * rstd
                dxh = r * w_ref[...]
                dw_ref[...] += jnp.sum(r * xhat, axis=0, keepdims=True)
                r = dh_ref[...] + rstd * (dxh - xhat * jnp.mean(dxh * xhat, axis=-1, keepdims=True))
            o_ref[...] = r.astype(out_dtype)

    b_spec = pl.BlockSpec((tn, tk), lambda i, j, k: (j, k)) if b_rows_are_n else pl.BlockSpec((tk, tn), lambda i, j, k: (k, j))
    in_specs = [pl.BlockSpec((tm, tk), lambda i, j, k: (i, k)), b_spec]
    args = [a, b]
    if has2:
        k2 = a2.shape[1]
        in_specs += [pl.BlockSpec((tm, k2), lambda i, j, k: (i, 0)), pl.BlockSpec((k2, tn), lambda i, j, k: (0, j))]
        args += [a2, b2]
    tile = pl.BlockSpec((tm, tn), lambda i, j, k: (i, j))
    if has_res:
        in_specs.append(tile)
        args.append(res)
    out_specs, out_shape = tile, jax.ShapeDtypeStruct((M, N), out_dtype)
    if has_nb:
        vec = pl.BlockSpec((1, N), lambda i, j, k: (0, 0))
        in_specs += [tile, vec, tile]
        args += list(norm_bwd)
        out_specs, out_shape = (tile, vec), (out_shape, jax.ShapeDtypeStruct((1, N), F32))
    return pl.pallas_call(
        body, name=name, grid=(M // tm, N // tn, nk), in_specs=in_specs, out_specs=out_specs, out_shape=out_shape,
        scratch_shapes=[pltpu.VMEM((tm, tn), F32)], compiler_params=_params(3),
    )(*args)


def mm_tn(a, b, *, name):
    T, M = a.shape
    N = b.shape[1]
    tm = _pick(M, (1024, 512, 256, 128))
    tn = _pick(N, (1024, 512, 256, 128))
    tt = _pick(T, (2048, 1024, 512, 256, 128))

    def body(a_ref, b_ref, o_ref):
        @pl.when(pl.program_id(2) == 0)
        def _():
            o_ref[...] = jnp.zeros_like(o_ref)

        o_ref[...] += _dot_tn(a_ref[...].astype(BF16), b_ref[...].astype(BF16))

    return pl.pallas_call(
        body, name=name, grid=(M // tm, N // tn, T // tt),
        in_specs=[pl.BlockSpec((tt, tm), lambda i, j, t: (t, i)), pl.BlockSpec((tt, tn), lambda i, j, t: (t, j))],
        out_specs=pl.BlockSpec((tm, tn), lambda i, j, t: (i, j)),
        out_shape=jax.ShapeDtypeStruct((M, N), F32), compiler_params=_params(3),
    )(a, b)


def rmsnorm_fwd(h, w, *, name):
    T, D = h.shape
    tm = _pick(T, (512, 256, 128))

    def body(h_ref, w_ref, o_ref):
        x = h_ref[...]
        rstd = lax.rsqrt(jnp.mean(x * x, axis=-1, keepdims=True) + EPS)
        o_ref[...] = (x * rstd * w_ref[...]).astype(BF16)

    return pl.pallas_call(
        body, name=name, grid=(T // tm,),
        in_specs=[pl.BlockSpec((tm, D), lambda i: (i, 0)), pl.BlockSpec((1, D), lambda i: (0, 0))],
        out_specs=pl.BlockSpec((tm, D), lambda i: (i, 0)),
        out_shape=jax.ShapeDtypeStruct((T, D), BF16), compiler_params=_params(1),
    )(h, w)


def loss_head(h, target, w, *, name):
    T, D = h.shape
    tm = _pick(T, (512, 256, 128))

    def body(h_ref, t_ref, w_ref, dh_ref, loss_ref, dw_ref):
        @pl.when(pl.program_id(0) == 0)
        def _():
            loss_ref[...] = jnp.zeros_like(loss_ref)
            dw_ref[...] = jnp.zeros_like(dw_ref)

        x = h_ref[...]
        rstd = lax.rsqrt(jnp.mean(x * x, axis=-1, keepdims=True) + EPS)
        xhat = x * rstd
        err = xhat * w_ref[...] - t_ref[...]
        rows = jnp.sum(err * err, axis=-1, keepdims=True)
        loss_ref[...] += (0.5 / D) * jnp.sum(rows, axis=0, keepdims=True)
        dy = err * (1.0 / D)
        dxh = dy * w_ref[...]
        dh_ref[...] = rstd * (dxh - xhat * jnp.mean(dxh * xhat, axis=-1, keepdims=True))
        dw_ref[...] += jnp.sum(dy * xhat, axis=0, keepdims=True)

    row = pl.BlockSpec((tm, D), lambda i: (i, 0))
    vec = pl.BlockSpec((1, D), lambda i: (0, 0))
    return pl.pallas_call(
        body, name=name, grid=(T // tm,), in_specs=[row, row, vec],
        out_specs=(row, pl.BlockSpec((1, 1), lambda i: (0, 0)), vec),
        out_shape=(jax.ShapeDtypeStruct((T, D), F32), jax.ShapeDtypeStruct((1, 1), F32),
                   jax.ShapeDtypeStruct((1, D), F32)),
        compiler_params=_params(1),
    )(h, target, w)


ANY = pl.BlockSpec(memory_space=pl.ANY)


def _place():
    return lax.axis_index("x"), lax.axis_index("y"), lax.axis_index("c")


def _gather_sems(n):
    return [pltpu.SemaphoreType.DMA((3 * n,))] * 4 + [pltpu.SemaphoreType.DMA((n,))]


def _gather_copies(rows, ins, outs, sems):
    ici_send, ici_recv, d2d_send, d2d_recv, local_sems = sems
    x, y, c = _place()
    k_me = 2 * x + y
    plan = []
    for t in range(len(rows)):
        half = rows[t] // 2
        mine = pl.ds(pl.multiple_of(c * half, 8), half)
        own = pltpu.make_async_copy(ins[t], outs[t].at[k_me], local_sems.at[t])
        sent, passed = [], []
        for j, (px, py) in enumerate([(1 - x, y), (x, 1 - y), (1 - x, 1 - y)]):
            landed = outs[t].at[2 * px + py, mine]
            sent.append(pltpu.make_async_remote_copy(
                src_ref=ins[t].at[mine], dst_ref=outs[t].at[k_me, mine], send_sem=ici_send.at[3 * t + j],
                recv_sem=ici_recv.at[3 * t + j], device_id=(px, py, c), device_id_type=MESH))
            passed.append(pltpu.make_async_remote_copy(
                src_ref=landed, dst_ref=landed, send_sem=d2d_send.at[3 * t + j], recv_sem=d2d_recv.at[3 * t + j],
                device_id=(x, y, 1 - c), device_id_type=MESH))
        plan.append((own, sent, passed))
    return plan


def _gather_start(rows, ins, outs, sems):
    for own, sent, _ in _gather_copies(rows, ins, outs, sems):
        own.start()
        for cp in sent:
            cp.start()


def _gather_finish(rows, ins, outs, sems):
    plan = _gather_copies(rows, ins, outs, sems)
    for _, sent, passed in plan:
        for cp, fwd in zip(sent, passed):
            cp.wait_recv()
            fwd.start()
    for own, sent, passed in plan:
        own.wait()
        for cp, fwd in zip(sent, passed):
            cp.wait_send()
            fwd.wait()


CONV_TM = 512
CONV_TC = 512
CONV_RB = 32


def _conv_specs(T, tm, sw, col0):
    hb = tm // HALO
    last = T // HALO - 1
    main = pl.BlockSpec((tm, sw), lambda j, i: (i, col0 + j))
    prev = pl.BlockSpec((HALO, sw), lambda j, i: (jnp.maximum(i * hb - 1, 0), col0 + j))
    nxt = pl.BlockSpec((HALO, sw), lambda j, i: (jnp.minimum((i + 1) * hb, last), col0 + j))
    return main, prev, nxt


def _conv_input(blk, glu, tc):
    x = blk.astype(F32)
    if glu:
        return x[:, :tc] * _sigmoid(x[:, tc:])
    return x


def _fill_padded(pad_ref, main, prev, nxt, first, last, tm):
    pad_ref[0:HALO, :] = jnp.where(first, 0.0, prev)
    pad_ref[HALO:HALO + tm, :] = main
    pad_ref[HALO + tm:HALO + tm + HALO, :] = jnp.where(last, 0.0, nxt)


SH_ROWS = 24


def _tap_plan(offsets):
    plan = [(o % 8, o - o % 8) for o in offsets]
    return plan, sorted({b for b, _ in plan if b})


def _fill_shifted(sh_ref, pad_ref, shifts, tm):
    for b in shifts:
        sh_ref[b] = pad_ref[b:b + tm + SH_ROWS, :]


def _tap_rows(pad_ref, sh_ref, b, start, rows):
    return pad_ref[start:start + rows, :] if b == 0 else sh_ref[b, start:start + rows, :]


def dwconv_fwd(src, w, b, *, width, glu, silu, col0, name, side=()):
    T = src.shape[0]
    C = w.shape[1]
    tm, tc = min(CONV_TM, T), CONV_TC
    sw = 2 * tc if glu else tc
    n_i = T // tm
    p = (width - 1) // 2
    rb = CONV_RB
    plan, shifts = _tap_plan([HALO - p + k for k in range(width)])

    n_side = len(side)
    side_rows = [t.shape[0] for t in side]

    def body(*refs):
        m_ref, p_ref, n_ref, w_ref, b_ref = refs[:5]
        side_in = refs[5:5 + n_side]
        o_ref = refs[5 + n_side]
        side_out = refs[6 + n_side:6 + 2 * n_side]
        pad_ref, sh_ref = refs[6 + 2 * n_side:8 + 2 * n_side]
        sems = refs[8 + 2 * n_side:]
        i = pl.program_id(1)
        j = pl.program_id(0)
        if n_side:
            @pl.when(jnp.logical_and(i == 0, j == 0))
            def _():
                _gather_start(side_rows, side_in, side_out, sems)

        _fill_padded(pad_ref, _conv_input(m_ref[...], glu, tc), _conv_input(p_ref[...], glu, tc),
                     _conv_input(n_ref[...], glu, tc), i == 0, i == n_i - 1, tm)
        _fill_shifted(sh_ref, pad_ref, shifts, tm)
        for r in range(tm // rb):
            acc = jnp.zeros((rb, tc), F32)
            for k, (sb, start) in enumerate(plan):
                acc = acc + _tap_rows(pad_ref, sh_ref, sb, start + r * rb, rb) * w_ref[k:k + 1, :]
            acc = acc + b_ref[...]
            if silu:
                acc = acc * _sigmoid(acc)
            o_ref[r * rb:(r + 1) * rb, :] = acc.astype(BF16)

        if n_side:
            @pl.when(jnp.logical_and(i == n_i - 1, j == C // tc - 1))
            def _():
                _gather_finish(side_rows, side_in, side_out, sems)

    main, prev, nxt = _conv_specs(T, tm, sw, col0)
    out = pl.pallas_call(
        body, name=name, grid=(C // tc, n_i),
        in_specs=[main, prev, nxt, pl.BlockSpec((w.shape[0], tc), lambda j, i: (0, j)),
                  pl.BlockSpec((1, tc), lambda j, i: (0, j))] + [ANY] * n_side,
        out_specs=tuple([pl.BlockSpec((tm, tc), lambda j, i: (i, j))] + [ANY] * n_side),
        out_shape=tuple([jax.ShapeDtypeStruct((T, C), BF16)]
                        + [jax.ShapeDtypeStruct((N_CHIPS,) + t.shape, t.dtype) for t in side]),
        scratch_shapes=[pltpu.VMEM((tm + 2 * HALO, tc), F32), pltpu.VMEM((8, tm + SH_ROWS, tc), F32)]
        + (_gather_sems(n_side) if n_side else []),
        compiler_params=_params(2),
    )(src, src, src, w, b, *side)
    return (out[0], list(out[1:])) if n_side else out[0]


def dwconv_bwd(dout, src, w, b, dsrc, *, width, glu, silu, col0, dcol0, name):
    T = src.shape[0]
    C = w.shape[1]
    kp = w.shape[0]
    tm, tc = min(CONV_TM, T), CONV_TC
    sw = 2 * tc if glu else tc
    n_i = T // tm
    p = (width - 1) // 2
    rb = CONV_RB
    edge = 8
    assert p <= edge or not silu
    plan, shifts = _tap_plan([HALO - p + k for k in range(width)])
    dplan, dshifts = _tap_plan([HALO + p - k for k in range(width)])

    def body(dm_ref, dp_ref, dn_ref, m_ref, p_ref, n_ref, w_ref, b_ref, _, o_ref, dw_ref, db_ref, pad_ref, dpre_ref,
             sh_ref, dsh_ref, acc_ref):
        i = pl.program_id(1)

        @pl.when(i == 0)
        def _():
            dw_ref[...] = jnp.zeros_like(dw_ref)
            acc_ref[...] = jnp.zeros_like(acc_ref)

        first, last = i == 0, i == n_i - 1
        _fill_padded(pad_ref, _conv_input(m_ref[...], glu, tc), _conv_input(p_ref[...], glu, tc),
                     _conv_input(n_ref[...], glu, tc), first, last, tm)
        _fill_padded(dpre_ref, dm_ref[...].astype(F32), dp_ref[...].astype(F32), dn_ref[...].astype(F32),
                     first, last, tm)
        _fill_shifted(sh_ref, pad_ref, shifts, tm)
        if silu:
            for r0 in range(HALO - edge, HALO + tm + edge, HALO):
                pre = jnp.zeros((HALO, tc), F32)
                for k, (sb, start) in enumerate(plan):
                    pre = pre + _tap_rows(pad_ref, sh_ref, sb, start + r0 - HALO, HALO) * w_ref[k:k + 1, :]
                pre = pre + b_ref[...]
                s = _sigmoid(pre)
                dpre_ref[r0:r0 + HALO, :] = dpre_ref[r0:r0 + HALO, :] * (s * (1.0 + pre * (1.0 - s)))
        _fill_shifted(dsh_ref, dpre_ref, dshifts, tm)

        for r in range(tm // rb):
            acc = jnp.zeros((rb, tc), F32)
            for k, (sb, start) in enumerate(dplan):
                acc = acc + _tap_rows(dpre_ref, dsh_ref, sb, start + r * rb, rb) * w_ref[k:k + 1, :]
            if glu:
                blk = m_ref[r * rb:(r + 1) * rb, :].astype(F32)
                v, s = blk[:, :tc], _sigmoid(blk[:, tc:])
                o_ref[r * rb:(r + 1) * rb, :tc] = (acc * s).astype(BF16)
                o_ref[r * rb:(r + 1) * rb, tc:] = (acc * v * s * (1.0 - s)).astype(BF16)
            else:
                o_ref[r * rb:(r + 1) * rb, :] = acc.astype(BF16)

        for r in range(tm // rb):
            dblk = dpre_ref[HALO + r * rb:HALO + (r + 1) * rb, :]
            for k, (sb, start) in enumerate(plan):
                prod = dblk * _tap_rows(pad_ref, sh_ref, sb, start + r * rb, rb)
                acc_ref[k] += jnp.sum(prod.reshape(rb // 8, 8, tc), axis=0)
            acc_ref[kp] += jnp.sum(dblk.reshape(rb // 8, 8, tc), axis=0)

        @pl.when(last)
        def _():
            for k in range(width):
                dw_ref[k:k + 1, :] = jnp.sum(acc_ref[k], axis=0, keepdims=True)
            db_ref[...] = jnp.sum(acc_ref[kp], axis=0, keepdims=True)

    dmain_s, dprev_s, dnext_s = _conv_specs(T, tm, tc, 0)
    main, prev, nxt = _conv_specs(T, tm, sw, col0)
    wspec = pl.BlockSpec((kp, tc), lambda j, i: (0, j))
    bspec = pl.BlockSpec((1, tc), lambda j, i: (0, j))
    return pl.pallas_call(
        body, name=name, grid=(C // tc, n_i),
        in_specs=[dmain_s, dprev_s, dnext_s, main, prev, nxt, wspec, bspec, pl.BlockSpec(memory_space=pl.ANY)],
        out_specs=(pl.BlockSpec((tm, sw), lambda j, i: (i, dcol0 + j)), wspec, bspec),
        out_shape=(jax.ShapeDtypeStruct(dsrc.shape, dsrc.dtype), jax.ShapeDtypeStruct((kp, C), F32),
                   jax.ShapeDtypeStruct((1, C), F32)),
        input_output_aliases={8: 0},
        scratch_shapes=[pltpu.VMEM((tm + 2 * HALO, tc), F32), pltpu.VMEM((tm + 2 * HALO, tc), F32),
                        pltpu.VMEM((8, tm + SH_ROWS, tc), F32), pltpu.VMEM((8, tm + SH_ROWS, tc), F32),
                        pltpu.VMEM((kp + 1, 8, tc), F32)],
        compiler_params=_params(2),
    )(dout, dout, dout, src, src, src, w, b, dsrc)


def _silu_grad(x, s):
    return s * (1.0 + x * (1.0 - s))


STRIP = 16
LCH = 512


def _strips(tm, fn):
    def step(s, carry):
        fn(pl.ds(pl.multiple_of(s * STRIP, STRIP), STRIP))
        return carry

    lax.fori_loop(0, tm // STRIP, step, 0, unroll=8)


def _chunks(e):
    return [slice(k, k + LCH) for k in range(0, e, LCH)]


def _row_sum(parts):
    acc = parts[0]
    for p in parts[1:]:
        acc = acc + p
    return jnp.sum(acc, axis=-1, keepdims=True)


def _fold8(x):
    return jnp.sum(x.reshape(STRIP // 8, 8, x.shape[-1]), axis=0)


def _ln_stats(u_ref, r, cks, e):
    mu = _row_sum([u_ref[r, ck].astype(F32) for ck in cks]) * (1.0 / e)
    var = _row_sum([jnp.square(u_ref[r, ck].astype(F32) - mu) for ck in cks]) * (1.0 / e)
    return mu, lax.rsqrt(var + EPS)


def conf_ln_fwd(u2, proj, ln_w, ln_b, *, name):
    T, E = u2.shape
    zc = proj.shape[1] // E - 1
    tm = _pick(T, (256, 128))
    cks = _chunks(E)

    def body(u_ref, z_ref, w_ref, b_ref, o_ref):
        def strip(r):
            mu, rstd = _ln_stats(u_ref, r, cks, E)
            for ck in cks:
                u3 = (u_ref[r, ck].astype(F32) - mu) * rstd * w_ref[:, ck] + b_ref[:, ck]
                z = z_ref[r, ck].astype(F32)
                o_ref[r, ck] = (u3 * _sigmoid(u3) * z * _sigmoid(z)).astype(BF16)

        _strips(tm, strip)

    row = pl.BlockSpec((tm, E), lambda i: (i, 0))
    vec = pl.BlockSpec((1, E), lambda i: (0, 0))
    return pl.pallas_call(
        body, name=name, grid=(T // tm,),
        in_specs=[row, pl.BlockSpec((tm, E), lambda i: (i, zc)), vec, vec], out_specs=row,
        out_shape=jax.ShapeDtypeStruct((T, E), BF16), compiler_params=_params(1),
    )(u2, proj, ln_w, ln_b)


def conf_ln_bwd(du4, u2, proj, ln_w, ln_b, *, name):
    T, E = u2.shape
    ncol = proj.shape[1] // E
    zc = ncol - 1
    tm = _pick(T, (256, 128))
    n_i = T // tm
    cks = _chunks(E)

    def body(d_ref, u_ref, z_ref, w_ref, b_ref, du_ref, dz_ref, dw_ref, db_ref, dxh_ref, accw_ref, accb_ref):
        i = pl.program_id(0)

        @pl.when(i == 0)
        def _():
            accw_ref[...] = jnp.zeros_like(accw_ref)
            accb_ref[...] = jnp.zeros_like(accb_ref)

        def strip(r):
            mu, rstd = _ln_stats(u_ref, r, cks, E)
            s1, s2 = [], []
            for ck in cks:
                xhat = (u_ref[r, ck].astype(F32) - mu) * rstd
                u3 = xhat * w_ref[:, ck] + b_ref[:, ck]
                z = z_ref[r, ck].astype(F32)
                s3, sz = _sigmoid(u3), _sigmoid(z)
                d4 = d_ref[r, ck].astype(F32)
                du3 = d4 * (z * sz) * _silu_grad(u3, s3)
                dz_ref[r, ck] = (d4 * (u3 * s3) * _silu_grad(z, sz)).astype(BF16)
                accw_ref[:, ck] += _fold8(du3 * xhat)
                accb_ref[:, ck] += _fold8(du3)
                dxh = du3 * w_ref[:, ck]
                dxh_ref[:, ck] = dxh
                s1.append(dxh)
                s2.append(dxh * xhat)
            m1, m2 = _row_sum(s1) * (1.0 / E), _row_sum(s2) * (1.0 / E)
            for ck in cks:
                xhat = (u_ref[r, ck].astype(F32) - mu) * rstd
                du_ref[r, ck] = (rstd * (dxh_ref[:, ck] - m1 - xhat * m2)).astype(BF16)

        _strips(tm, strip)

        @pl.when(i == n_i - 1)
        def _():
            dw_ref[...] = jnp.sum(accw_ref[...], axis=0, keepdims=True)
            db_ref[...] = jnp.sum(accb_ref[...], axis=0, keepdims=True)

    row = pl.BlockSpec((tm, E), lambda i: (i, 0))
    zrow = pl.BlockSpec((tm, E), lambda i: (i, zc))
    vec = pl.BlockSpec((1, E), lambda i: (0, 0))
    return pl.pallas_call(
        body, name=name, grid=(n_i,), in_specs=[row, row, zrow, vec, vec], out_specs=(row, zrow, vec, vec),
        out_shape=(jax.ShapeDtypeStruct((T, E), BF16), jax.ShapeDtypeStruct(proj.shape, BF16),
                   jax.ShapeDtypeStruct((1, E), F32), jax.ShapeDtypeStruct((1, E), F32)),
        scratch_shapes=[pltpu.VMEM((STRIP, E), F32), pltpu.VMEM((8, E), F32), pltpu.VMEM((8, E), F32)],
        compiler_params=_params(1),
    )(du4, u2, proj, ln_w, ln_b)


def _gated(y_ref, z_ref, r, ck):
    z = z_ref[r, ck].astype(F32)
    sz = _sigmoid(z)
    yv = y_ref[r, ck].astype(F32)
    return z, sz, yv, yv * (z * sz)


def ssd_gate_fwd(y, zx, norm_w, *, name):
    T, E = y.shape
    tm = _pick(T, (256, 128))
    cks = _chunks(E)

    def body(y_ref, z_ref, w_ref, o_ref, yz_ref):
        def strip(r):
            sq = []
            for ck in cks:
                yz = _gated(y_ref, z_ref, r, ck)[3]
                yz_ref[:, ck] = yz
                sq.append(yz * yz)
            rstd = lax.rsqrt(_row_sum(sq) * (1.0 / E) + EPS)
            for ck in cks:
                o_ref[r, ck] = (yz_ref[:, ck] * rstd * w_ref[:, ck]).astype(BF16)

        _strips(tm, strip)

    row = pl.BlockSpec((tm, E), lambda i: (i, 0))
    vec = pl.BlockSpec((1, E), lambda i: (0, 0))
    return pl.pallas_call(
        body, name=name, grid=(T // tm,), in_specs=[row, row, vec], out_specs=row,
        out_shape=jax.ShapeDtypeStruct((T, E), BF16), scratch_shapes=[pltpu.VMEM((STRIP, E), F32)],
        compiler_params=_params(1),
    )(y, zx, norm_w)


def ssd_gate_bwd(dyn, y, zx, norm_w, *, name):
    T, E = y.shape
    tm = _pick(T, (256, 128))
    n_i = T // tm
    cks = _chunks(E)

    def body(d_ref, y_ref, z_ref, w_ref, dy_ref, dz_ref, dw_ref, yz_ref, accw_ref):
        i = pl.program_id(0)

        @pl.when(i == 0)
        def _():
            accw_ref[...] = jnp.zeros_like(accw_ref)

        def strip(r):
            sq = []
            for ck in cks:
                yz = _gated(y_ref, z_ref, r, ck)[3]
                yz_ref[:, ck] = yz
                sq.append(yz * yz)
            rstd = lax.rsqrt(_row_sum(sq) * (1.0 / E) + EPS)
            s2 = []
            for ck in cks:
                yhat = yz_ref[:, ck] * rstd
                d = d_ref[r, ck].astype(F32)
                accw_ref[:, ck] += _fold8(d * yhat)
                s2.append(d * w_ref[:, ck] * yhat)
            m2 = _row_sum(s2) * (1.0 / E)
            for ck in cks:
                z, sz, yv, _ = _gated(y_ref, z_ref, r, ck)
                dyz = rstd * (d_ref[r, ck].astype(F32) * w_ref[:, ck] - yz_ref[:, ck] * rstd * m2)
                dy_ref[r, ck] = (dyz * (z * sz)).astype(BF16)
                dz_ref[r, ck] = (dyz * yv * _silu_grad(z, sz)).astype(BF16)

        _strips(tm, strip)

        @pl.when(i == n_i - 1)
        def _():
            dw_ref[...] = jnp.sum(accw_ref[...], axis=0, keepdims=True)

    row = pl.BlockSpec((tm, E), lambda i: (i, 0))
    vec = pl.BlockSpec((1, E), lambda i: (0, 0))
    return pl.pallas_call(
        body, name=name, grid=(n_i,), in_specs=[row, row, row, vec], out_specs=(row, row, vec),
        out_shape=(jax.ShapeDtypeStruct((T, E), BF16), jax.ShapeDtypeStruct(zx.shape, BF16),
                   jax.ShapeDtypeStruct((1, E), F32)),
        scratch_shapes=[pltpu.VMEM((STRIP, E), F32), pltpu.VMEM((8, E), F32)],
        compiler_params=_params(1),
    )(dyn, y, zx, norm_w)


def _cumsum_mm(mask, a):
    hi = a.astype(BF16)
    r1 = a - hi.astype(F32)
    mid = r1.astype(BF16)
    lo = (r1 - mid.astype(F32)).astype(BF16)
    out = _dot(jnp.where(mask, 1.0, 0.0).astype(BF16), jnp.concatenate([hi, mid, lo], axis=1))
    return out[:, :LANES] + out[:, LANES:2 * LANES] + out[:, 2 * LANES:]


def _chunk_terms(xcb, dt_raw, bias, alog, rev):
    L = CHUNK
    xs = xcb[:, :GW].astype(F32)
    Bm = xcb[:, GW:GW + D_STATE]
    Cm = xcb[:, GW + D_STATE:]
    pre = dt_raw + bias
    dt = _softplus(pre)
    A = -jnp.exp(alog)
    row = lax.broadcasted_iota(jnp.int32, (L, L), 0)
    col = lax.broadcasted_iota(jnp.int32, (L, L), 1)
    mask = (col >= row) if rev else (col <= row)
    mask_t = (col <= row) if rev else (col >= row)
    cs = _cumsum_mm(mask, dt * A)
    tot = cs[0:1, :] if rev else cs[L - 1:L, :]
    return xs, Bm, Cm, pre, dt, A, mask, mask_t, cs, cs.T, tot


def _decay(cs, cs_t, ln, mask):
    d = cs[:, ln:ln + 1] - cs_t[ln:ln + 1, :]
    return jnp.where(mask, jnp.exp(jnp.where(mask, d, 0.0)), 0.0)


def _pair(v, ln0, lo):
    return jnp.where(lo[:v.shape[0]], v[:, ln0:ln0 + 1], v[:, ln0 + 1:ln0 + 2])


def _scan_specs(nc, rev_order):
    ci = (lambda c: nc - 1 - c) if rev_order else (lambda c: c)
    xc = pl.BlockSpec((CHUNK, GROUPS * XCG), lambda c: (ci(c), 0))
    dt = pl.BlockSpec((CHUNK, GROUPS * LANES), lambda c: (ci(c), 0))
    vec = pl.BlockSpec((1, GROUPS * LANES), lambda c: (0, 0))
    wide = pl.BlockSpec((CHUNK, GROUPS * GW), lambda c: (ci(c), 0))
    wvec = pl.BlockSpec((1, GROUPS * GW), lambda c: (0, 0))
    st = pl.BlockSpec((1, D_STATE, GROUPS * GW), lambda c: (ci(c), 0, 0))
    return xc, dt, vec, wide, wvec, st


def _cols(ref, g, width):
    return ref.at[:, pl.ds(g * width, width)]


def _interleave(stages):
    for _ in itertools.zip_longest(*stages):
        pass


def _head_expand(r):
    row = lax.broadcasted_iota(jnp.int32, (LANES, GW), 0)
    col = lax.broadcasted_iota(jnp.int32, (LANES, GW), 1)
    first = (row - r * HPG) * HEADDIM
    return jnp.where(jnp.logical_and(col >= first, col < first + HEADDIM), 1.0, 0.0).astype(BF16)


def _head_collect(r):
    row = lax.broadcasted_iota(jnp.int32, (GW, LANES), 0)
    first = (lax.broadcasted_iota(jnp.int32, (GW, LANES), 1) - r * HPG) * HEADDIM
    return jnp.where(jnp.logical_and(row >= first, row < first + HEADDIM), 1.0, 0.0).astype(BF16)


def _expand(parts, sel):
    n = parts[0].shape[0]
    out = _dot(jnp.concatenate(parts, axis=0).astype(BF16), sel)
    return [out[i * n:(i + 1) * n] for i in range(len(parts))]


def ssd_scan_fwd(xc, dt4, bias4, alog4, *, rev, name, prev=None, dvec=None):
    T = xc.shape[0]
    nc = T // CHUNK
    E = GROUPS * GW
    r = 1 if rev else 0
    skip = prev is not None

    def one_group(sel, xc_ref, dt_ref, bias_ref, alog_ref, prev_ref, dvec_ref, y_ref, st_ref, s_ref):
        xs, Bm, Cm, _, dt, _, mask, _, cs, cs_t, tot = _chunk_terms(xc_ref[...], dt_ref[...], bias_ref[...],
                                                                   alog_ref[...], rev)
        yield
        dtx, ex, dx = _expand([dt, jnp.exp(cs), jnp.exp(tot - cs)], sel)
        et = jnp.exp(tot)
        cb = _dot_nt(Cm, Bm)
        yield
        sb = s_ref[...].astype(BF16)
        st_ref[...] = sb
        xp_all = xs * dtx
        y_off = _dot(Cm, sb) * ex
        lo = lax.broadcasted_iota(jnp.int32, (CHUNK, LANES), 1) < HEADDIM
        et_parts = []
        for p in range(HPG // 2):
            yield
            ln0 = r * HPG + 2 * p
            sl = slice(p * LANES, (p + 1) * LANES)
            xp = xp_all[:, sl]
            mcat = jnp.concatenate([cb * _decay(cs, cs_t, ln0, mask), cb * _decay(cs, cs_t, ln0 + 1, mask)],
                                   axis=1).astype(BF16)
            xbd = jnp.concatenate([jnp.where(lo, xp, 0.0), jnp.where(lo, 0.0, xp)], axis=0).astype(BF16)
            yp = _dot(mcat, xbd) + y_off[:, sl]
            if skip:
                yp = yp + prev_ref[:, sl].astype(F32) + xs[:, sl] * dvec_ref[:, sl]
            y_ref[:, sl] = yp.astype(BF16)
            et_parts.append(_pair(et, ln0, lo))
        yield
        s_ref[...] = s_ref[...] * jnp.concatenate(et_parts, axis=1) + _dot_tn(Bm, (xp_all * dx).astype(BF16))

    def body(*refs):
        xc_ref, dt_ref, bias_ref, alog_ref = refs[:4]
        prev_ref, dvec_ref = (refs[4], refs[5]) if skip else (None, None)
        y_ref, st_ref, s_ref = refs[-3:]

        @pl.when(pl.program_id(0) == 0)
        def _():
            s_ref[...] = jnp.zeros_like(s_ref)

        sel = _head_expand(r)
        _interleave([
            one_group(sel, _cols(xc_ref, g, XCG), _cols(dt_ref, g, LANES), _cols(bias_ref, g, LANES),
                      _cols(alog_ref, g, LANES), _cols(prev_ref, g, GW) if skip else None,
                      _cols(dvec_ref, g, GW) if skip else None, _cols(y_ref, g, GW),
                      st_ref.at[0, :, pl.ds(g * GW, GW)], _cols(s_ref, g, GW)) for g in range(GROUPS)])

    s_xc, s_dt, s_vec, s_wide, s_wvec, s_st = _scan_specs(nc, rev)
    in_specs = [s_xc, s_dt, s_vec, s_vec]
    args = [xc, dt4, bias4, alog4]
    if skip:
        in_specs += [s_wide, s_wvec]
        args += [prev, dvec]
    return pl.pallas_call(
        body, name=name, grid=(nc,), in_specs=in_specs, out_specs=(s_wide, s_st),
        out_shape=(jax.ShapeDtypeStruct((T, E), BF16), jax.ShapeDtypeStruct((nc, D_STATE, E), BF16)),
        scratch_shapes=[pltpu.VMEM((D_STATE, E), F32)], compiler_params=_params(1),
    )(*args)


def ssd_scan_bwd(xc, dt4, bias4, alog4, dy, states, *, rev, name, prev=None, dvec=None):
    T = xc.shape[0]
    nc = T // CHUNK
    E = GROUPS * GW
    L = CHUNK
    r = 1 if rev else 0
    skip = prev is not None

    def one_group(sel, sel_t, xc_ref, dt_ref, bias_ref, alog_ref, dy_ref, st_ref, pdxc_ref, pddt_ref, dvec_ref,
                  dxc_ref, ddt_ref, dalog_ref, dbias_ref, dd_ref, g_ref):
        xs, Bm, Cm, pre, dt, A, mask, mask_t, cs, cs_t, tot = _chunk_terms(
            xc_ref[...], dt_ref[...], bias_ref[...], alog_ref[...], rev)
        yield
        dtx, ex, dx = _expand([dt, jnp.exp(cs), jnp.exp(tot - cs)], sel)
        et = jnp.exp(tot)
        cb = _dot_nt(Cm, Bm)
        yield
        s_in = st_ref[...]
        dy_all = dy_ref[...].astype(F32)
        g_f = g_ref[...]
        g_b = g_f.astype(BF16)
        xp_all = xs * dtx
        dye_all = dy_all * ex
        bgd = _dot(Bm, g_b) * dx
        lane = lax.broadcasted_iota(jnp.int32, (L, LANES), 1)
        lo = lane < HEADDIM
        dcb = jnp.zeros((L, L), F32)
        yd_parts, dxd_parts, et_parts = [], [], []
        for p in range(HPG // 2):
            yield
            ln0 = r * HPG + 2 * p
            sl = slice(p * LANES, (p + 1) * LANES)
            xp, dy_p = xp_all[:, sl], dy_all[:, sl]
            lam0, lam1 = _decay(cs, cs_t, ln0, mask), _decay(cs, cs_t, ln0 + 1, mask)
            m0, m1 = (cb * lam0).astype(BF16), (cb * lam1).astype(BF16)
            dybd = jnp.concatenate([jnp.where(lo, dy_p, 0.0), jnp.where(lo, 0.0, dy_p)], axis=0).astype(BF16)
            xbd = jnp.concatenate([jnp.where(lo, xp, 0.0), jnp.where(lo, 0.0, xp)], axis=0).astype(BF16)
            dm = _dot_nt(dybd, xp.astype(BF16))
            dcb = dcb + dm[:L] * lam0 + dm[L:] * lam1
            yd_parts.append(_dot(jnp.concatenate([m0, m1], axis=1), xbd))
            dxd_parts.append(_dot_tn(jnp.concatenate([m0, m1], axis=0), dybd))
            et_parts.append(_pair(et, ln0, lo))
        yield
        y_diag = jnp.concatenate(yd_parts, axis=1)
        dx_diag = jnp.concatenate(dxd_parts, axis=1)
        etx = jnp.concatenate(et_parts, axis=1)
        dxt = dx_diag + bgd
        w2 = xp_all * bgd
        dy_r, xp_r = dy_all.astype(BF16).astype(F32), xp_all.astype(BF16).astype(F32)
        u = dye_all * _dot(Cm, s_in) + dy_r * y_diag - xp_r * dx_diag - w2
        dxx = dxt * xs
        tail = jnp.broadcast_to(jnp.sum(w2, axis=0, keepdims=True)
                                + jnp.sum(g_f * s_in.astype(F32), axis=0, keepdims=True) * etx, (8, GW))
        u_hi, t_hi = u.astype(BF16), tail.astype(BF16)
        red = _dot(jnp.concatenate([u_hi, (u - u_hi.astype(F32)).astype(BF16), dxx.astype(BF16), t_hi,
                                    (tail - t_hi.astype(F32)).astype(BF16)], axis=0), sel_t)
        yield
        dcs = red[:L] + red[L:2 * L]
        ddt = red[2 * L:3 * L]
        dtot = red[3 * L:3 * L + 1] + red[3 * L + 8:3 * L + 9]
        dxs = dxt * dtx
        if skip:
            dxs = dxs + dy_all * dvec_ref[...] + pdxc_ref[:, :GW].astype(F32)
            dd_ref[...] += jnp.sum(dy_all * xs, axis=0, keepdims=True)
        dxc_ref[:, :GW] = dxs.astype(BF16)
        dye_b = dye_all.astype(BF16)
        xd = (xp_all * dx).astype(BF16)
        dcb_b = dcb.astype(BF16)
        d_b = _dot_nt(xd, g_b) + _dot_tn(dcb_b, Cm)
        d_c = _dot_nt(dye_b, s_in) + _dot(dcb_b, Bm)
        if skip:
            d_b = d_b + pdxc_ref[:, GW:GW + D_STATE].astype(F32)
            d_c = d_c + pdxc_ref[:, GW + D_STATE:].astype(F32)
        dxc_ref[:, GW:GW + D_STATE] = d_b.astype(BF16)
        dxc_ref[:, GW + D_STATE:] = d_c.astype(BF16)
        yield
        g_ref[...] = g_f * etx + _dot_tn(Cm, dye_b)
        rowi = lax.broadcasted_iota(jnp.int32, (L, LANES), 0)
        da = _cumsum_mm(mask_t, dcs + jnp.where(rowi == (0 if rev else L - 1), dtot, 0.0))
        keep = jnp.logical_and(lane >= r * HPG, lane < (r + 1) * HPG)
        ddr = jnp.where(keep, (da * A + ddt) * _sigmoid(pre), 0.0)
        dbias_ref[...] += jnp.sum(ddr, axis=0, keepdims=True)
        dalog_ref[...] += jnp.sum(jnp.where(keep, da * dt * A, 0.0), axis=0, keepdims=True)
        if skip:
            ddr = ddr + pddt_ref[...]
        ddt_ref[...] = ddr

    def body(*refs):
        xc_ref, dt_ref, bias_ref, alog_ref, dy_ref, st_ref = refs[:6]
        pdxc_ref, pddt_ref, dvec_ref = refs[6:9] if skip else (None, None, None)
        pos = 9 if skip else 6
        dxc_ref, ddt_ref, dalog_ref, dbias_ref = refs[pos:pos + 4]
        dd_ref = refs[pos + 4] if skip else None
        g_ref = refs[-1]

        @pl.when(pl.program_id(0) == 0)
        def _():
            g_ref[...] = jnp.zeros_like(g_ref)
            dalog_ref[...] = jnp.zeros_like(dalog_ref)
            dbias_ref[...] = jnp.zeros_like(dbias_ref)
            if skip:
                dd_ref[...] = jnp.zeros_like(dd_ref)

        sel, sel_t = _head_expand(r), _head_collect(r)
        _interleave([
            one_group(sel, sel_t, _cols(xc_ref, g, XCG), _cols(dt_ref, g, LANES), _cols(bias_ref, g, LANES),
                      _cols(alog_ref, g, LANES), _cols(dy_ref, g, GW), st_ref.at[0, :, pl.ds(g * GW, GW)],
                      _cols(pdxc_ref, g, XCG) if skip else None, _cols(pddt_ref, g, LANES) if skip else None,
                      _cols(dvec_ref, g, GW) if skip else None, _cols(dxc_ref, g, XCG), _cols(ddt_ref, g, LANES),
                      _cols(dalog_ref, g, LANES), _cols(dbias_ref, g, LANES),
                      _cols(dd_ref, g, GW) if skip else None, _cols(g_ref, g, GW)) for g in range(GROUPS)])

    s_xc, s_dt, s_vec, s_wide, s_wvec, s_st = _scan_specs(nc, not rev)
    in_specs = [s_xc, s_dt, s_vec, s_vec, s_wide, s_st]
    args = [xc, dt4, bias4, alog4, dy, states]
    out_specs = [s_xc, s_dt, s_vec, s_vec]
    out_shape = [jax.ShapeDtypeStruct((T, GROUPS * XCG), BF16), jax.ShapeDtypeStruct((T, GROUPS * LANES), F32),
                 jax.ShapeDtypeStruct((1, GROUPS * LANES), F32), jax.ShapeDtypeStruct((1, GROUPS * LANES), F32)]
    if skip:
        in_specs += [s_xc, s_dt, s_wvec]
        args += [prev[0], prev[1], dvec]
        out_specs.append(s_wvec)
        out_shape.append(jax.ShapeDtypeStruct((1, E), F32))
    return pl.pallas_call(
        body, name=name, grid=(nc,), in_specs=in_specs, out_specs=tuple(out_specs),
        out_shape=tuple(out_shape), scratch_shapes=[pltpu.VMEM((D_STATE, E), F32)], compiler_params=_params(1),
    )(*args)


def _conf_cols(w):
    e = w.shape[-1] // 3
    lead = w.shape[:-1]
    vg = w[..., :2 * e].reshape(*lead, 2, e // CONV_TC, CONV_TC)
    vg = jnp.swapaxes(vg, -3, -2).reshape(*lead, 2 * e)
    return jnp.concatenate([vg, w[..., 2 * e:]], axis=-1)


def _conf_cols_inv(w):
    e = w.shape[-1] // 3
    lead = w.shape[:-1]
    vg = w[..., :2 * e].reshape(*lead, e // CONV_TC, 2, CONV_TC)
    vg = jnp.swapaxes(vg, -3, -2).reshape(*lead, 2 * e)
    return jnp.concatenate([vg, w[..., 2 * e:]], axis=-1)


def _xbc_cols(w):
    lead = w.shape[:-1]
    e = GROUPS * GW
    gn = GROUPS * D_STATE
    parts = [w[..., :e].reshape(*lead, GROUPS, GW), w[..., e:e + gn].reshape(*lead, GROUPS, D_STATE),
             w[..., e + gn:].reshape(*lead, GROUPS, D_STATE)]
    return jnp.concatenate(parts, axis=-1).reshape(*lead, GROUPS * XCG)


def _xbc_cols_inv(w):
    lead = w.shape[:-1]
    g = w.reshape(*lead, GROUPS, XCG)
    parts = [g[..., :GW].reshape(*lead, GROUPS * GW), g[..., GW:GW + D_STATE].reshape(*lead, GROUPS * D_STATE),
             g[..., GW + D_STATE:].reshape(*lead, GROUPS * D_STATE)]
    return jnp.concatenate(parts, axis=-1)


def _dt_cols(w):
    lead = w.shape[:-1]
    t = jnp.swapaxes(w.reshape(*lead, 2, GROUPS, HPG), -3, -2).reshape(*lead, GROUPS, 2 * HPG)
    pad = [(0, 0)] * (t.ndim - 1) + [(0, LANES - 2 * HPG)]
    return jnp.pad(t, pad).reshape(*lead, GROUPS * LANES)


def _dt_cols_inv(w):
    lead = w.shape[:-1]
    t = w.reshape(*lead, GROUPS, LANES)[..., :2 * HPG].reshape(*lead, GROUPS, 2, HPG)
    return jnp.swapaxes(t, -3, -2).reshape(*lead, 2 * HEADS)


def _pad_rows(w, rows):
    return jnp.pad(w, ((0, rows - w.shape[0]), (0, 0)))


def conf_weights(w_in, dw_w, dw_b, ln_w, ln_b):
    w_in_p = _conf_cols(w_in)
    return dict(w_in=w_in_p, w_in_t=w_in_p.T, dw_w=_pad_rows(dw_w, 32), dw_b=dw_b.reshape(1, -1),
                ln_w=ln_w.reshape(1, -1), ln_b=ln_b.reshape(1, -1))


def _xbc_rows(w):
    e, gn, c = GROUPS * GW, GROUPS * D_STATE, w.shape[1]
    parts = [w[:e].reshape(GROUPS, GW, c), w[e:e + gn].reshape(GROUPS, D_STATE, c),
             w[e + gn:].reshape(GROUPS, D_STATE, c)]
    return jnp.concatenate(parts, axis=1).reshape(GROUPS * XCG, c)


def _xbc_rows_inv(w):
    c = w.shape[1]
    g = w.reshape(GROUPS, XCG, c)
    parts = [g[:, :GW].reshape(GROUPS * GW, c), g[:, GW:GW + D_STATE].reshape(GROUPS * D_STATE, c),
             g[:, GW + D_STATE:].reshape(GROUPS * D_STATE, c)]
    return jnp.concatenate(parts, axis=0)


def _dt_rows(w):
    c = w.shape[1]
    t = jnp.swapaxes(w.reshape(2, GROUPS, HPG, c), 0, 1).reshape(GROUPS, 2 * HPG, c)
    return jnp.pad(t, ((0, 0), (0, LANES - 2 * HPG), (0, 0))).reshape(GROUPS * LANES, c)


def _dt_rows_inv(w):
    c = w.shape[1]
    t = w.reshape(GROUPS, LANES, c)[:, :2 * HPG].reshape(GROUPS, 2, HPG, c)
    return jnp.swapaxes(t, 0, 1).reshape(2 * HEADS, c)


def ssd_weights(w_in_t, conv_w, conv_b, dt_bias, a_log, d_skip, norm_w, w_out):
    e = GROUPS * GW
    xbc = e + 2 * GROUPS * D_STATE
    w_zx_t = jnp.concatenate([w_in_t[:e], _xbc_rows(w_in_t[e:e + xbc])], axis=0)
    return dict(w_zx_t=w_zx_t, w_dt_t=_dt_rows(w_in_t[e + xbc:]), w_out=w_out, w_out_t=w_out.T,
                conv_w=_pad_rows(_xbc_cols(conv_w), 8), conv_b=_xbc_cols(conv_b.reshape(1, -1)),
                bias4=_dt_cols(dt_bias.reshape(1, -1)), alog4=_dt_cols(a_log.reshape(1, -1)),
                dvec=jnp.repeat(d_skip, HEADDIM).reshape(1, -1), norm_w=norm_w.reshape(1, -1))


def conf_layer_fwd(h, nw, p, tag, side=(), w_out=None):
    hn = rmsnorm_fwd(h, nw, name=f"{tag}_norm")
    proj = mm_nn(hn, p["w_in"], out_dtype=BF16, name=f"{tag}_proj")
    u2 = dwconv_fwd(proj, p["dw_w"], p["dw_b"], width=31, glu=True, silu=False, col0=0, name=f"{tag}_conv", side=side)
    if side:
        u2, gathered = u2
        w_out = w_out(gathered)
    p.update(w_out=w_out, w_out_t=w_out.T)
    u4 = conf_ln_fwd(u2, proj, p["ln_w"], p["ln_b"], name=f"{tag}_ln")
    h2 = mm_nn(u4, p["w_out"], out_dtype=F32, res=h, name=f"{tag}_out")
    return h2, (h, hn, proj, u2, u4)


def conf_layer_bwd(dh, saved, nw, p, tag):
    h, hn, proj, u2, u4 = saved
    du4 = mm_nn(dh, p["w_out_t"], out_dtype=BF16, name=f"{tag}_d_u4")
    dw_out = mm_tn(u4, dh, name=f"{tag}_dw_out")
    du2, dproj, dln_w, dln_b = conf_ln_bwd(du4, u2, proj, p["ln_w"], p["ln_b"], name=f"{tag}_d_ln")
    dproj, ddw_w, ddw_b = dwconv_bwd(du2, proj, p["dw_w"], p["dw_b"], dproj, width=31, glu=True, silu=False,
                                     col0=0, dcol0=0, name=f"{tag}_d_conv")
    dh_prev, dnw = mm_nn(dproj, p["w_in_t"], out_dtype=F32, norm_bwd=(h, nw, dh), name=f"{tag}_d_hn")
    dw_in = mm_tn(hn, dproj, name=f"{tag}_dw_in")
    grads = dict(w_in=_conf_cols_inv(dw_in), dw_w=ddw_w[:31], dw_b=ddw_b[0], ln_w=dln_w[0], ln_b=dln_b[0],
                 w_out=dw_out, norm=dnw[0])
    return dh_prev, grads


def ssd_layer_fwd(h, nw, p, tag):
    e = GROUPS * GW
    hn = rmsnorm_fwd(h, nw, name=f"{tag}_norm")
    zx = mm_nn(hn, p["w_zx_t"], out_dtype=BF16, b_rows_are_n=True, name=f"{tag}_proj")
    dt4 = mm_nn(hn, p["w_dt_t"], out_dtype=F32, b_rows_are_n=True, name=f"{tag}_proj_dt")
    xc = dwconv_fwd(zx, p["conv_w"], p["conv_b"], width=5, glu=False, silu=True, col0=e // CONV_TC, name=f"{tag}_conv")
    y0, st0 = ssd_scan_fwd(xc, dt4, p["bias4"], p["alog4"], rev=False, name=f"{tag}_scan_f")
    y, st1 = ssd_scan_fwd(xc, dt4, p["bias4"], p["alog4"], rev=True, prev=y0, dvec=p["dvec"], name=f"{tag}_scan_b")
    yn = ssd_gate_fwd(y, zx, p["norm_w"], name=f"{tag}_gate")
    h2 = mm_nn(yn, p["w_out"], out_dtype=F32, res=h, name=f"{tag}_out")
    return h2, (h, hn, zx, dt4, xc, st0, st1, y, yn)


def ssd_layer_bwd(dh, saved, nw, p, tag):
    e = GROUPS * GW
    h, hn, zx, dt4, xc, st0, st1, y, yn = saved
    dyn = mm_nn(dh, p["w_out_t"], out_dtype=BF16, name=f"{tag}_d_yn")
    dw_out = mm_tn(yn, dh, name=f"{tag}_dw_out")
    dy, dzx, dnorm_w = ssd_gate_bwd(dyn, y, zx, p["norm_w"], name=f"{tag}_d_gate")
    dxc0, ddt0, dalog0, dbias0 = ssd_scan_bwd(xc, dt4, p["bias4"], p["alog4"], dy, st0, rev=False,
                                              name=f"{tag}_d_scan_f")
    dxc, ddt4, dalog1, dbias1, ddvec = ssd_scan_bwd(xc, dt4, p["bias4"], p["alog4"], dy, st1, rev=True,
                                                    prev=(dxc0, ddt0), dvec=p["dvec"], name=f"{tag}_d_scan_b")
    dzx, dconv_w, dconv_b = dwconv_bwd(dxc, zx, p["conv_w"], p["conv_b"], dzx, width=5, glu=False, silu=True,
                                       col0=e // CONV_TC, dcol0=e // CONV_TC, name=f"{tag}_d_conv")
    dh_prev, dnw = mm_nn(dzx, p["w_zx_t"], out_dtype=F32, a2=ddt4, b2=p["w_dt_t"], norm_bwd=(h, nw, dh),
                         name=f"{tag}_d_hn")
    dw_zx_t = mm_tn(dzx, hn, name=f"{tag}_dw_zx")
    dw_dt_t = mm_tn(ddt4, hn, name=f"{tag}_dw_dt")
    dw_in_t = jnp.concatenate([dw_zx_t[:e], _xbc_rows_inv(dw_zx_t[e:]), _dt_rows_inv(dw_dt_t)], axis=0)
    grads = dict(w_in_t=dw_in_t, conv_w=_xbc_cols_inv(dconv_w[:5]), conv_b=_xbc_cols_inv(dconv_b)[0],
                 dt_bias=_dt_cols_inv(dbias0 + dbias1).reshape(2, HEADS),
                 a_log=_dt_cols_inv(dalog0 + dalog1).reshape(2, HEADS),
                 d_skip=jnp.sum(ddvec.reshape(HEADS, HEADDIM), axis=-1), norm_w=dnorm_w[0], w_out=dw_out,
                 norm=dnw[0])
    return dh_prev, grads


def gather_chips(bufs, *, name):
    n = len(bufs)
    rows = [b.shape[0] for b in bufs]

    def body(*refs):
        ins, outs, sems = refs[:n], refs[n:2 * n], refs[2 * n:]
        _gather_start(rows, ins, outs, sems)
        _gather_finish(rows, ins, outs, sems)

    return pl.pallas_call(
        body, name=name, in_specs=[ANY] * n, out_specs=tuple([ANY] * n),
        out_shape=tuple(jax.ShapeDtypeStruct((N_CHIPS,) + b.shape, b.dtype) for b in bufs),
        scratch_shapes=_gather_sems(n),
    )(*bufs)


def swap_other_half(g2, *, name):
    def body(g_ref, o_ref, send_sem, recv_sem):
        x, y, c = _place()
        cp = pltpu.make_async_remote_copy(src_ref=g_ref.at[1 - c], dst_ref=o_ref, send_sem=send_sem, recv_sem=recv_sem,
                                          device_id=(x, y, 1 - c), device_id_type=MESH)
        cp.start()
        cp.wait()

    return pl.pallas_call(
        body, name=name, in_specs=[ANY], out_specs=ANY, out_shape=jax.ShapeDtypeStruct(g2.shape[1:], g2.dtype),
        scratch_shapes=[pltpu.SemaphoreType.DMA, pltpu.SemaphoreType.DMA],
    )(g2)


def exchange_chips(p, *, name):
    def body(p_ref, o_ref, send_sems, recv_sems, local_sem):
        x, y, c = _place()
        k_me = 2 * x + y
        own = pltpu.make_async_copy(p_ref.at[k_me], o_ref.at[k_me], local_sem)
        own.start()
        copies = [own]
        for j, (px, py) in enumerate([(1 - x, y), (x, 1 - y), (1 - x, 1 - y)]):
            cp = pltpu.make_async_remote_copy(
                src_ref=p_ref.at[2 * px + py], dst_ref=o_ref.at[k_me], send_sem=send_sems.at[j],
                recv_sem=recv_sems.at[j], device_id=(px, py, c), device_id_type=MESH)
            cp.start()
            copies.append(cp)
        for cp in copies:
            cp.wait()

    return pl.pallas_call(
        body, name=name, in_specs=[ANY], out_specs=ANY, out_shape=jax.ShapeDtypeStruct(p.shape, p.dtype),
        scratch_shapes=[pltpu.SemaphoreType.DMA((3,)), pltpu.SemaphoreType.DMA((3,)), pltpu.SemaphoreType.DMA],
    )(p)


def share_half(full, *, name):
    def body(_, f_ref, send_sem, recv_sem):
        x, y, c = _place()
        cp = pltpu.make_async_remote_copy(src_ref=f_ref.at[c], dst_ref=f_ref.at[c], send_sem=send_sem,
                                          recv_sem=recv_sem, device_id=(x, y, 1 - c), device_id_type=MESH)
        cp.start()
        cp.wait()

    return pl.pallas_call(
        body, name=name, in_specs=[ANY], out_specs=ANY, out_shape=jax.ShapeDtypeStruct(full.shape, full.dtype),
        input_output_aliases={0: 0},
        scratch_shapes=[pltpu.SemaphoreType.DMA, pltpu.SemaphoreType.DMA],
    )(full)


def gather_all(v, *, name):
    def body(v_ref, o_ref, send_sems, recv_sems, local_sem):
        x, y, c = _place()
        me = 4 * x + 2 * y + c
        own = pltpu.make_async_copy(v_ref, o_ref.at[me], local_sem)
        own.start()
        copies = [own]
        idx = 0
        for fx in (0, 1):
            for fy in (0, 1):
                for fc in (0, 1):
                    if not (fx or fy or fc):
                        continue
                    peer = (1 - x if fx else x, 1 - y if fy else y, 1 - c if fc else c)
                    cp = pltpu.make_async_remote_copy(src_ref=v_ref, dst_ref=o_ref.at[me], send_sem=send_sems.at[idx],
                                                      recv_sem=recv_sems.at[idx], device_id=peer, device_id_type=MESH)
                    cp.start()
                    copies.append(cp)
                    idx += 1
        for cp in copies:
            cp.wait()

    return pl.pallas_call(
        body, name=name, in_specs=[ANY], out_specs=ANY, out_shape=jax.ShapeDtypeStruct((N_DEV,) + v.shape, v.dtype),
        scratch_shapes=[pltpu.SemaphoreType.DMA((N_DEV - 1,)), pltpu.SemaphoreType.DMA((N_DEV - 1,)),
                        pltpu.SemaphoreType.DMA],
    )(v)


RED_TR = 432


def pair_sum(g2, recv, cidx, *, name):
    _, K, R, C = g2.shape
    tr = _pick(R, (RED_TR, 8))

    def body(c_ref, a_ref, b_ref, o_ref):
        o_ref[...] = (a_ref[0] + b_ref[...]).astype(BF16)

    blk = pl.BlockSpec((1, tr, C), lambda k, i, c: (k, i, 0))
    return pl.pallas_call(
        body, name=name,
        grid_spec=pltpu.PrefetchScalarGridSpec(
            num_scalar_prefetch=1, grid=(K, R // tr),
            in_specs=[pl.BlockSpec((1, 1, tr, C), lambda k, i, c: (c[0], k, i, 0)), blk], out_specs=blk),
        out_shape=jax.ShapeDtypeStruct((K, R, C), BF16), compiler_params=_params(2),
    )(cidx, g2, recv)


def sum_lead(a, *, name, slot=None, nslots=1):
    K, R, C = a.shape
    tr = _pick(R, (RED_TR, 8))

    def body(s_ref, a_ref, o_ref):
        acc = a_ref[0].astype(F32)
        for k in range(1, K):
            acc = acc + a_ref[k].astype(F32)
        o_ref[0] = acc

    if slot is None:
        slot = jnp.zeros((1,), jnp.int32)
    return pl.pallas_call(
        body, name=name,
        grid_spec=pltpu.PrefetchScalarGridSpec(
            num_scalar_prefetch=1, grid=(R // tr,),
            in_specs=[pl.BlockSpec((K, tr, C), lambda i, s: (0, i, 0))],
            out_specs=pl.BlockSpec((1, tr, C), lambda i, s: (s[0], i, 0))),
        out_shape=jax.ShapeDtypeStruct((nslots, R, C), F32), compiler_params=_params(1),
    )(slot, a)


def adamw(g, w, m, v, *, name):
    R, C = w.shape
    tr = _pick(R, (256, 128, 64, 32, 16, 8))

    def body(g_ref, w_ref, m_ref, v_ref, d_ref, nm_ref, nv_ref):
        gv = g_ref[...]
        m_new = ADAM_B1 * m_ref[...] + (1.0 - ADAM_B1) * gv
        v_new = ADAM_B2 * v_ref[...] + (1.0 - ADAM_B2) * (gv * gv)
        m_hat = m_new / (1.0 - ADAM_B1 ** ADAM_STEP)
        v_hat = v_new / (1.0 - ADAM_B2 ** ADAM_STEP)
        d_ref[...] = -ADAM_LR * (m_hat / (jnp.sqrt(v_hat) + ADAM_EPS) + ADAM_WD * w_ref[...])
        nm_ref[...] = m_new
        nv_ref[...] = v_new

    blk = pl.BlockSpec((tr, C), lambda i: (i, 0))
    sds = jax.ShapeDtypeStruct((R, C), F32)
    return pl.pallas_call(
        body, name=name, grid=(R // tr,), in_specs=[blk] * 4, out_specs=(blk,) * 3, out_shape=(sds,) * 3,
        compiler_params=_params(1),
    )(g, w, m, v)


WEIGHTS = ("norm_w", "final_norm_w", "cm_w_in", "cm_dw_w", "cm_dw_b", "cm_ln_w", "cm_ln_b", "cm_w_out", "ssd_w_in",
           "ssd_conv_w", "ssd_conv_b", "ssd_dt_bias", "ssd_A_log", "ssd_D", "ssd_norm_w", "ssd_w_out")
BIG = (("cm_w_in", 2), ("cm_w_out", 1), ("ssd_w_in", 1), ("ssd_w_out", 1))
TRANSPOSED = ("ssd_w_in",)
SMALL_SHARDED = (("cm_dw_w", 2), ("ssd_conv_w", 2), ("ssd_conv_b", 1), ("ssd_norm_w", 1))
REPLICATED = ("norm_w", "final_norm_w", "cm_dw_b", "cm_ln_w", "cm_ln_b", "ssd_dt_bias", "ssd_A_log", "ssd_D")
ROW = 1024


def _to_shards(g, axis):
    n = g.shape[axis]
    s = g.reshape(g.shape[:axis] + (N_CHIPS, n // N_CHIPS) + g.shape[axis + 1:])
    return jnp.moveaxis(s, axis, 0).reshape(N_CHIPS, -1)


def _from_shards(x4, local_shape, axis):
    local_shape = tuple(local_shape)
    s = jnp.moveaxis(x4.reshape((N_CHIPS,) + local_shape), 0, axis)
    return s.reshape(local_shape[:axis] + (N_CHIPS * local_shape[axis],) + local_shape[axis + 1:])


def _flat_pad(parts, multiple):
    n = sum(p.size for p in parts)
    fill = [jnp.zeros(((-n) % multiple,), parts[0].dtype)] if n % multiple else []
    return jnp.concatenate([p.reshape(-1) for p in parts] + fill)


def _split(flat, like, names):
    out, off = {}, 0
    for n in names:
        out[n] = flat[off:off + like[n].size].reshape(like[n].shape)
        off += like[n].size
    return out


def kernel(x, norm_w, final_norm_w, cm_w_in, cm_dw_w, cm_dw_b, cm_ln_w, cm_ln_b, cm_w_out, ssd_w_in, ssd_conv_w, ssd_conv_b, ssd_dt_bias, ssd_A_log, ssd_D, ssd_norm_w, ssd_w_out, loss_target, m_norm_w, m_final_norm_w, m_cm_w_in, m_cm_dw_w, m_cm_dw_b, m_cm_ln_w, m_cm_ln_b, m_cm_w_out, m_ssd_w_in, m_ssd_conv_w, m_ssd_conv_b, m_ssd_dt_bias, m_ssd_A_log, m_ssd_D, m_ssd_norm_w, m_ssd_w_out, v_norm_w, v_final_norm_w, v_cm_w_in, v_cm_dw_w, v_cm_dw_b, v_cm_ln_w, v_cm_ln_b, v_cm_w_out, v_ssd_w_in, v_ssd_conv_w, v_ssd_conv_b, v_ssd_dt_bias, v_ssd_A_log, v_ssd_D, v_ssd_norm_w, v_ssd_w_out):
    a = dict(locals())
    w = {n: a[n] for n in WEIGHTS}
    m = {n: a["m_" + n] for n in WEIGHTS}
    v = {n: a["v_" + n] for n in WEIGHTS}
    _, T, D = x.shape
    cidx = lax.axis_index("c").astype(jnp.int32).reshape(1)
    big_names = [n for n, _ in BIG]
    small_names = [n for n, _ in SMALL_SHARDED]

    wx = {n: (jnp.swapaxes(w[n], 1, 2) if n in TRANSPOSED else w[n]) for n in big_names + small_names}
    big = _flat_pad([wx[n] for n in big_names], 16 * ROW).astype(BF16).reshape(-1, ROW)
    small = _flat_pad([wx[n] for n in small_names], 8 * ROW).reshape(-1, ROW)
    first_name, first_axis = BIG[0]
    first_shape = wx[first_name].shape[1:]
    n_first = math.prod(first_shape) // ROW
    g_first, g_small = gather_chips([big[:n_first], small], name="gather_weights")
    g_small = g_small.reshape(N_CHIPS, -1)
    full, off = {}, 0
    for n, ax in SMALL_SHARDED:
        full[n] = _from_shards(g_small[:, off:off + wx[n].size], wx[n].shape, ax)
        off += wx[n].size
    w_in_0 = _from_shards(g_first.reshape(N_CHIPS, -1), first_shape, first_axis - 1)
    n_layers = norm_w.shape[0]
    lw = [None] * n_layers
    lw[0] = conf_weights(w_in_0, full["cm_dw_w"][0], cm_dw_b[0], cm_ln_w[0], cm_ln_b[0])

    def unpack_rest(gathered):
        g_big = jnp.concatenate([g_first, gathered[0]], axis=1).reshape(N_CHIPS, -1)
        off = 0
        for n, ax in BIG:
            full[n] = _from_shards(g_big[:, off:off + wx[n].size], wx[n].shape, ax)
            off += wx[n].size
        for i in range(1, n_layers):
            j = i // 2
            if i % 2 == 0:
                lw[i] = conf_weights(full["cm_w_in"][j], full["cm_dw_w"][j], cm_dw_b[j], cm_ln_w[j], cm_ln_b[j])
            else:
                lw[i] = ssd_weights(full["ssd_w_in"][j], full["ssd_conv_w"][j], full["ssd_conv_b"][j], ssd_dt_bias[j],
                                    ssd_A_log[j], ssd_D[j], full["ssd_norm_w"][j], full["ssd_w_out"][j])
        return full["cm_w_out"][0]

    h = x[0]
    saved = []
    for i in range(n_layers):
        nw_i = norm_w[i].reshape(1, -1)
        if i == 0:
            h, s = conf_layer_fwd(h, nw_i, lw[0], "l0", side=(big[n_first:],), w_out=unpack_rest)
        elif i % 2 == 0:
            h, s = conf_layer_fwd(h, nw_i, lw[i], f"l{i}", w_out=full["cm_w_out"][i // 2])
        else:
            h, s = ssd_layer_fwd(h, nw_i, lw[i], f"l{i}")
        saved.append(s)
    dh, loss_local, d_final = loss_head(h, loss_target[0], final_norm_w.reshape(1, -1), name="loss_head")
    lg = [None] * n_layers
    for i in reversed(range(n_layers)):
        bwd = conf_layer_bwd if i % 2 == 0 else ssd_layer_bwd
        dh, lg[i] = bwd(dh, saved[i], norm_w[i].reshape(1, -1), lw[i], f"l{i}")
    conf_g, ssd_g = lg[0::2], lg[1::2]
    local = {
        "norm_w": jnp.stack([g["norm"] for g in lg]), "final_norm_w": d_final[0],
        "cm_w_in": jnp.stack([g["w_in"] for g in conf_g]), "cm_dw_w": jnp.stack([g["dw_w"] for g in conf_g]),
        "cm_dw_b": jnp.stack([g["dw_b"] for g in conf_g]), "cm_ln_w": jnp.stack([g["ln_w"] for g in conf_g]),
        "cm_ln_b": jnp.stack([g["ln_b"] for g in conf_g]), "cm_w_out": jnp.stack([g["w_out"] for g in conf_g]),
        "ssd_w_in": jnp.stack([g["w_in_t"] for g in ssd_g]), "ssd_conv_w": jnp.stack([g["conv_w"] for g in ssd_g]),
        "ssd_conv_b": jnp.stack([g["conv_b"] for g in ssd_g]), "ssd_dt_bias": jnp.stack([g["dt_bias"] for g in ssd_g]),
        "ssd_A_log": jnp.stack([g["a_log"] for g in ssd_g]), "ssd_D": jnp.stack([g["d_skip"] for g in ssd_g]),
        "ssd_norm_w": jnp.stack([g["norm_w"] for g in ssd_g]), "ssd_w_out": jnp.stack([g["w_out"] for g in ssd_g]),
    }

    shards = [_to_shards(local[n], ax) for n, ax in BIG + SMALL_SHARDED]
    fill = (-sum(t.shape[1] for t in shards)) % (2 * RED_TR * ROW)
    flat4 = jnp.concatenate(shards + [jnp.zeros((N_CHIPS, fill), F32)], axis=1)
    g2 = jnp.swapaxes(flat4.reshape(N_CHIPS, 2, -1, ROW), 0, 1)
    theirs = swap_other_half(g2, name="grad_pair_swap")
    part = pair_sum(g2, theirs, cidx, name="grad_pair_sum")
    got = exchange_chips(part, name="grad_chip_exchange")
    half = sum_lead(got, slot=cidx, nslots=2, name="grad_chip_sum")
    shard_flat = share_half(half, name="grad_pair_share").reshape(-1)
    grads = _split(shard_flat, wx, big_names + small_names)
    for n in TRANSPOSED:
        grads[n] = jnp.swapaxes(grads[n], 1, 2)

    rep = _flat_pad([local[n] for n in REPLICATED], 8 * LANES).reshape(-1, LANES)
    rep_sum = sum_lead(gather_all(rep, name="grad_small_gather"), name="grad_small_sum")
    grads.update(_split(rep_sum.reshape(-1), w, REPLICATED))

    delta, new_m, new_v = {}, {}, {}
    for n in big_names:
        two_d = (-1, w[n].shape[-1])
        d_, m_, v_ = adamw(grads[n].reshape(two_d), w[n].reshape(two_d), m[n].reshape(two_d), v[n].reshape(two_d),
                           name="adamw_" + n)
        delta[n], new_m[n], new_v[n] = d_.reshape(w[n].shape), m_.reshape(w[n].shape), v_.reshape(w[n].shape)
    rest = list(REPLICATED) + small_names
    packed = [_flat_pad([t[n] for n in rest], 8 * LANES).reshape(-1, LANES) for t in (grads, w, m, v)]
    for out, res in zip((delta, new_m, new_v), adamw(*packed, name="adamw_small")):
        out.update(_split(res.reshape(-1), w, rest))

    loss = lax.psum(loss_local[0, 0], ("x", "y", "c"))
    return (loss, dh.reshape(x.shape), *[grads[n] for n in WEIGHTS], *[delta[n] for n in WEIGHTS],
            *[new_m[n] for n in WEIGHTS], *[new_v[n] for n in WEIGHTS])
```

```python
import itertools
import math

import jax
import jax.numpy as jnp
from jax import lax
from jax.experimental import pallas as pl
from jax.experimental.pallas import tpu as pltpu

F32 = jnp.float32
BF16 = jnp.bfloat16
MESH = pl.DeviceIdType.MESH

EPS = 1e-5
HEADDIM = 64
HEADS = 32
GROUPS = 4
HPG = HEADS // GROUPS
D_STATE = 128
CHUNK = 128
GW = HPG * HEADDIM
XCG = GW + 2 * D_STATE
HALO = 16
LANES = 128
N_CHIPS = 4
N_DEV = 8

ADAM_LR = 0.001
ADAM_B1 = 0.9
ADAM_B2 = 0.999
ADAM_EPS = 1e-08
ADAM_WD = 0.01
ADAM_STEP = 10

VMEM_LIMIT = 52 * 1024 * 1024


def _params(n_axes):
    return pltpu.CompilerParams(dimension_semantics=("arbitrary",) * n_axes, vmem_limit_bytes=VMEM_LIMIT)


def _sigmoid(x):
    return 1.0 / (1.0 + jnp.exp(-x))


def _softplus(x):
    return jnp.maximum(x, 0.0) + jnp.log(1.0 + jnp.exp(-jnp.abs(x)))


def _dot(a, b):
    return jnp.dot(a, b, preferred_element_type=F32)


def _dot_nt(a, b):
    return lax.dot_general(a, b, (((1,), (1,)), ((), ())), preferred_element_type=F32)


def _dot_tn(a, b):
    return lax.dot_general(a, b, (((0,), (0,)), ((), ())), preferred_element_type=F32)


def _pick(n, pref):
    for t in pref:
        if n % t == 0:
            return t
    return n


def mm_nn(a, b, *, out_dtype, name, res=None, a2=None, b2=None, b_rows_are_n=False, norm_bwd=None):
    M, K = a.shape
    N = b.shape[0] if b_rows_are_n else b.shape[1]
    has2, has_res, has_nb = a2 is not None, res is not None, norm_bwd is not None
    tm = _pick(M, (1024, 512, 256, 128))
    tn = N if has_nb else _pick(N, (1024, 512, 256, 128))
    tk = _pick(K, (2048, 1024, 512, 256, 128) if a.dtype == BF16 and not has_nb else (1024, 512, 256, 128))
    nk = K // tk

    def body(*refs):
        a_ref, b_ref = refs[0], refs[1]
        pos = 2
        if has2:
            a2_ref, b2_ref = refs[pos], refs[pos + 1]
            pos += 2
        if has_res:
            r_ref = refs[pos]
            pos += 1
        if has_nb:
            h_ref, w_ref, dh_ref = refs[pos:pos + 3]
            pos += 3
        o_ref = refs[pos]
        acc_ref = refs[-1]
        k = pl.program_id(2)
        first_rows = pl.program_id(0) == 0

        @pl.when(k == 0)
        def _():
            if has2:
                acc_ref[...] = _dot(a2_ref[...].astype(BF16), b2_ref[...])
            else:
                acc_ref[...] = jnp.zeros_like(acc_ref)

        acc_ref[...] += (_dot_nt if b_rows_are_n else _dot)(a_ref[...].astype(BF16), b_ref[...])

        @pl.when(k == nk - 1)
        def _():
            r = acc_ref[...]
            if has_res:
                r = r + r_ref[...]
            if has_nb:
                dw_ref = refs[pos + 1]

                @pl.when(first_rows)
                def _():
                    dw_ref[...] = jnp.zeros_like(dw_ref)

                x = h_ref[...]
                rstd = lax.rsqrt(jnp.mean(x * x, axis=-1, keepdims=True) + EPS)
                xhat = x * rstd
                dxh = r * w_ref[...]
                dw_ref[...] += jnp.sum(r * xhat, axis=0, keepdims=True)
                r = dh_ref[...] + rstd * (dxh - xhat * jnp.mean(dxh * xhat, axis=-1, keepdims=True))
            o_ref[...] = r.astype(out_dtype)

    b_spec = pl.BlockSpec((tn, tk), lambda i, j, k: (j, k)) if b_rows_are_n else pl.BlockSpec((tk, tn), lambda i, j, k: (k, j))
    in_specs = [pl.BlockSpec((tm, tk), lambda i, j, k: (i, k)), b_spec]
    args = [a, b]
    if has2:
        k2 = a2.shape[1]
        in_specs += [pl.BlockSpec((tm, k2), lambda i, j, k: (i, 0)), pl.BlockSpec((k2, tn), lambda i, j, k: (0, j))]
        args += [a2, b2]
    tile = pl.BlockSpec((tm, tn), lambda i, j, k: (i, j))
    if has_res:
        in_specs.append(tile)
        args.append(res)
    out_specs, out_shape = tile, jax.ShapeDtypeStruct((M, N), out_dtype)
    if has_nb:
        vec = pl.BlockSpec((1, N), lambda i, j, k: (0, 0))
        in_specs += [tile, vec, tile]
        args += list(norm_bwd)
        out_specs, out_shape = (tile, vec), (out_shape, jax.ShapeDtypeStruct((1, N), F32))
    return pl.pallas_call(
        body, name=name, grid=(M // tm, N // tn, nk), in_specs=in_specs, out_specs=out_specs, out_shape=out_shape,
        scratch_shapes=[pltpu.VMEM((tm, tn), F32)], compiler_params=_params(3),
    )(*args)


def mm_tn(a, b, *, name):
    T, M = a.shape
    N = b.shape[1]
    tm = _pick(M, (1024, 512, 256, 128))
    tn = _pick(N, (1024, 512, 256, 128))
    tt = _pick(T, (2048, 1024, 512, 256, 128))

    def body(a_ref, b_ref, o_ref):
        @pl.when(pl.program_id(2) == 0)
        def _():
            o_ref[...] = jnp.zeros_like(o_ref)

        o_ref[...] += _dot_tn(a_ref[...].astype(BF16), b_ref[...].astype(BF16))

    return pl.pallas_call(
        body, name=name, grid=(M // tm, N // tn, T // tt),
        in_specs=[pl.BlockSpec((tt, tm), lambda i, j, t: (t, i)), pl.BlockSpec((tt, tn), lambda i, j, t: (t, j))],
        out_specs=pl.BlockSpec((tm, tn), lambda i, j, t: (i, j)),
        out_shape=jax.ShapeDtypeStruct((M, N), F32), compiler_params=_params(3),
    )(a, b)


def rmsnorm_fwd(h, w, *, name):
    T, D = h.shape
    tm = _pick(T, (512, 256, 128))

    def body(h_ref, w_ref, o_ref):
        x = h_ref[...]
        rstd = lax.rsqrt(jnp.mean(x * x, axis=-1, keepdims=True) + EPS)
        o_ref[...] = (x * rstd * w_ref[...]).astype(BF16)

    return pl.pallas_call(
        body, name=name, grid=(T // tm,),
        in_specs=[pl.BlockSpec((tm, D), lambda i: (i, 0)), pl.BlockSpec((1, D), lambda i: (0, 0))],
        out_specs=pl.BlockSpec((tm, D), lambda i: (i, 0)),
        out_shape=jax.ShapeDtypeStruct((T, D), BF16), compiler_params=_params(1),
    )(h, w)


def loss_head(h, target, w, *, name):
    T, D = h.shape
    tm = _pick(T, (512, 256, 128))

    def body(h_ref, t_ref, w_ref, dh_ref, loss_ref, dw_ref):
        @pl.when(pl.program_id(0) == 0)
        def _():
            loss_ref[...] = jnp.zeros_like(loss_ref)
            dw_ref[...] = jnp.zeros_like(dw_ref)

        x = h_ref[...]
        rstd = lax.rsqrt(jnp.mean(x * x, axis=-1, keepdims=True) + EPS)
        xhat = x * rstd
        err = xhat * w_ref[...] - t_ref[...]
        rows = jnp.sum(err * err, axis=-1, keepdims=True)
        loss_ref[...] += (0.5 / D) * jnp.sum(rows, axis=0, keepdims=True)
        dy = err * (1.0 / D)
        dxh = dy * w_ref[...]
        dh_ref[...] = rstd * (dxh - xhat * jnp.mean(dxh * xhat, axis=-1, keepdims=True))
        dw_ref[...] += jnp.sum(dy * xhat, axis=0, keepdims=True)

    row = pl.BlockSpec((tm, D), lambda i: (i, 0))
    vec = pl.BlockSpec((1, D), lambda i: (0, 0))
    return pl.pallas_call(
        body, name=name, grid=(T // tm,), in_specs=[row, row, vec],
        out_specs=(row, pl.BlockSpec((1, 1), lambda i: (0, 0)), vec),
        out_shape=(jax.ShapeDtypeStruct((T, D), F32), jax.ShapeDtypeStruct((1, 1), F32),
                   jax.ShapeDtypeStruct((1, D), F32)),
        compiler_params=_params(1),
    )(h, target, w)


ANY = pl.BlockSpec(memory_space=pl.ANY)


def _place():
    return lax.axis_index("x"), lax.axis_index("y"), lax.axis_index("c")


def _gather_sems(n):
    return [pltpu.SemaphoreType.DMA((3 * n,))] * 4 + [pltpu.SemaphoreType.DMA((n,))]


def _gather_copies(rows, ins, outs, sems):
    ici_send, ici_recv, d2d_send, d2d_recv, local_sems = sems
    x, y, c = _place()
    k_me = 2 * x + y
    plan = []
    for t in range(len(rows)):
        half = rows[t] // 2
        mine = pl.ds(pl.multiple_of(c * half, 8), half)
        own = pltpu.make_async_copy(ins[t], outs[t].at[k_me], local_sems.at[t])
        sent, passed = [], []
        for j, (px, py) in enumerate([(1 - x, y), (x, 1 - y), (1 - x, 1 - y)]):
            landed = outs[t].at[2 * px + py, mine]
            sent.append(pltpu.make_async_remote_copy(
                src_ref=ins[t].at[mine], dst_ref=outs[t].at[k_me, mine], send_sem=ici_send.at[3 * t + j],
                recv_sem=ici_recv.at[3 * t + j], device_id=(px, py, c), device_id_type=MESH))
            passed.append(pltpu.make_async_remote_copy(
                src_ref=landed, dst_ref=landed, send_sem=d2d_send.at[3 * t + j], recv_sem=d2d_recv.at[3 * t + j],
                device_id=(x, y, 1 - c), device_id_type=MESH))
        plan.append((own, sent, passed))
    return plan


def _gather_start(rows, ins, outs, sems):
    for own, sent, _ in _gather_copies(rows, ins, outs, sems):
        own.start()
        for cp in sent:
            cp.start()


def _gather_finish(rows, ins, outs, sems):
    plan = _gather_copies(rows, ins, outs, sems)
    for _, sent, passed in plan:
        for cp, fwd in zip(sent, passed):
            cp.wait_recv()
            fwd.start()
    for own, sent, passed in plan:
        own.wait()
        for cp, fwd in zip(sent, passed):
            cp.wait_send()
            fwd.wait()


CONV_TM = 512
CONV_TC = 512
CONV_RB = 16


def _conv_specs(T, tm, sw, col0):
    hb = tm // HALO
    last = T // HALO - 1
    main = pl.BlockSpec((tm, sw), lambda j, i: (i, col0 + j))
    prev = pl.BlockSpec((HALO, sw), lambda j, i: (jnp.maximum(i * hb - 1, 0), col0 + j))
    nxt = pl.BlockSpec((HALO, sw), lambda j, i: (jnp.minimum((i + 1) * hb, last), col0 + j))
    return main, prev, nxt


def _conv_input(blk, glu, tc):
    x = blk.astype(F32)
    if glu:
        return x[:, :tc] * _sigmoid(x[:, tc:])
    return x


def _fill_padded(pad_ref, main, prev, nxt, first, last, tm):
    pad_ref[0:HALO, :] = jnp.where(first, 0.0, prev)
    pad_ref[HALO:HALO + tm, :] = main
    pad_ref[HALO + tm:HALO + tm + HALO, :] = jnp.where(last, 0.0, nxt)


SH_ROWS = 24


def _tap_plan(offsets):
    plan = [(o % 8, o - o % 8) for o in offsets]
    return plan, sorted({b for b, _ in plan if b})


def _fill_shifted(sh_ref, pad_ref, shifts, tm):
    for b in shifts:
        sh_ref[b] = pad_ref[b:b + tm + SH_ROWS, :]


def _tap_rows(pad_ref, sh_ref, b, start, rows):
    return pad_ref[start:start + rows, :] if b == 0 else sh_ref[b, start:start + rows, :]


def dwconv_fwd(src, w, b, *, width, glu, silu, col0, name, side=()):
    T = src.shape[0]
    C = w.shape[1]
    tm, tc = min(CONV_TM, T), CONV_TC
    sw = 2 * tc if glu else tc
    n_i = T // tm
    p = (width - 1) // 2
    rb = CONV_RB
    plan, shifts = _tap_plan([HALO - p + k for k in range(width)])

    n_side = len(side)
    side_rows = [t.shape[0] for t in side]

    def body(*refs):
        m_ref, p_ref, n_ref, w_ref, b_ref = refs[:5]
        side_in = refs[5:5 + n_side]
        o_ref = refs[5 + n_side]
        side_out = refs[6 + n_side:6 + 2 * n_side]
        pad_ref, sh_ref = refs[6 + 2 * n_side:8 + 2 * n_side]
        sems = refs[8 + 2 * n_side:]
        i = pl.program_id(1)
        j = pl.program_id(0)
        if n_side:
            @pl.when(jnp.logical_and(i == 0, j == 0))
            def _():
                _gather_start(side_rows, side_in, side_out, sems)

        _fill_padded(pad_ref, _conv_input(m_ref[...], glu, tc), _conv_input(p_ref[...], glu, tc),
                     _conv_input(n_ref[...], glu, tc), i == 0, i == n_i - 1, tm)
        _fill_shifted(sh_ref, pad_ref, shifts, tm)
        for r in range(tm // rb):
            acc = jnp.zeros((rb, tc), F32)
            for k, (sb, start) in enumerate(plan):
                acc = acc + _tap_rows(pad_ref, sh_ref, sb, start + r * rb, rb) * w_ref[k:k + 1, :]
            acc = acc + b_ref[...]
            if silu:
                acc = acc * _sigmoid(acc)
            o_ref[r * rb:(r + 1) * rb, :] = acc.astype(BF16)

        if n_side:
            @pl.when(jnp.logical_and(i == n_i - 1, j == C // tc - 1))
            def _():
                _gather_finish(side_rows, side_in, side_out, sems)

    main, prev, nxt = _conv_specs(T, tm, sw, col0)
    out = pl.pallas_call(
        body, name=name, grid=(C // tc, n_i),
        in_specs=[main, prev, nxt, pl.BlockSpec((w.shape[0], tc), lambda j, i: (0, j)),
                  pl.BlockSpec((1, tc), lambda j, i: (0, j))] + [ANY] * n_side,
        out_specs=tuple([pl.BlockSpec((tm, tc), lambda j, i: (i, j))] + [ANY] * n_side),
        out_shape=tuple([jax.ShapeDtypeStruct((T, C), BF16)]
                        + [jax.ShapeDtypeStruct((N_CHIPS,) + t.shape, t.dtype) for t in side]),
        scratch_shapes=[pltpu.VMEM((tm + 2 * HALO, tc), F32), pltpu.VMEM((8, tm + SH_ROWS, tc), F32)]
        + (_gather_sems(n_side) if n_side else []),
        compiler_params=_params(2),
    )(src, src, src, w, b, *side)
    return (out[0], list(out[1:])) if n_side else out[0]


def dwconv_bwd(dout, src, w, b, dsrc, *, width, glu, silu, col0, dcol0, name):
    T = src.shape[0]
    C = w.shape[1]
    kp = w.shape[0]
    tm, tc = min(CONV_TM, T), CONV_TC
    sw = 2 * tc if glu else tc
    n_i = T // tm
    p = (width - 1) // 2
    rb = CONV_RB
    edge = 8
    assert p <= edge or not silu
    plan, shifts = _tap_plan([HALO - p + k for k in range(width)])
    dplan, dshifts = _tap_plan([HALO + p - k for k in range(width)])

    def body(dm_ref, dp_ref, dn_ref, m_ref, p_ref, n_ref, w_ref, b_ref, _, o_ref, dw_ref, db_ref, pad_ref, dpre_ref,
             sh_ref, dsh_ref, acc_ref):
        i = pl.program_id(1)

        @pl.when(i == 0)
        def _():
            dw_ref[...] = jnp.zeros_like(dw_ref)
            acc_ref[...] = jnp.zeros_like(acc_ref)

        first, last = i == 0, i == n_i - 1
        _fill_padded(pad_ref, _conv_input(m_ref[...], glu, tc), _conv_input(p_ref[...], glu, tc),
                     _conv_input(n_ref[...], glu, tc), first, last, tm)
        _fill_padded(dpre_ref, dm_ref[...].astype(F32), dp_ref[...].astype(F32), dn_ref[...].astype(F32),
                     first, last, tm)
        _fill_shifted(sh_ref, pad_ref, shifts, tm)
        if silu:
            for r0 in range(HALO - edge, HALO + tm + edge, HALO):
                pre = jnp.zeros((HALO, tc), F32)
                for k, (sb, start) in enumerate(plan):
                    pre = pre + _tap_rows(pad_ref, sh_ref, sb, start + r0 - HALO, HALO) * w_ref[k:k + 1, :]
                pre = pre + b_ref[...]
                s = _sigmoid(pre)
                dpre_ref[r0:r0 + HALO, :] = dpre_ref[r0:r0 + HALO, :] * (s * (1.0 + pre * (1.0 - s)))
        _fill_shifted(dsh_ref, dpre_ref, dshifts, tm)

        for r in range(tm // rb):
            acc = jnp.zeros((rb, tc), F32)
            for k, (sb, start) in enumerate(dplan):
                acc = acc + _tap_rows(dpre_ref, dsh_ref, sb, start + r * rb, rb) * w_ref[k:k + 1, :]
            if glu:
                blk = m_ref[r * rb:(r + 1) * rb, :].astype(F32)
                v, s = blk[:, :tc], _sigmoid(blk[:, tc:])
                o_ref[r * rb:(r + 1) * rb, :tc] = (acc * s).astype(BF16)
                o_ref[r * rb:(r + 1) * rb, tc:] = (acc * v * s * (1.0 - s)).astype(BF16)
            else:
                o_ref[r * rb:(r + 1) * rb, :] = acc.astype(BF16)

        for r in range(tm // rb):
            dblk = dpre_ref[HALO + r * rb:HALO + (r + 1) * rb, :]
            for k, (sb, start) in enumerate(plan):
                prod = dblk * _tap_rows(pad_ref, sh_ref, sb, start + r * rb, rb)
                acc_ref[k] += jnp.sum(prod.reshape(rb // 8, 8, tc), axis=0)
            acc_ref[kp] += jnp.sum(dblk.reshape(rb // 8, 8, tc), axis=0)

        @pl.when(last)
        def _():
            for k in range(width):
                dw_ref[k:k + 1, :] = jnp.sum(acc_ref[k], axis=0, keepdims=True)
            db_ref[...] = jnp.sum(acc_ref[kp], axis=0, keepdims=True)

    dmain_s, dprev_s, dnext_s = _conv_specs(T, tm, tc, 0)
    main, prev, nxt = _conv_specs(T, tm, sw, col0)
    wspec = pl.BlockSpec((kp, tc), lambda j, i: (0, j))
    bspec = pl.BlockSpec((1, tc), lambda j, i: (0, j))
    return pl.pallas_call(
        body, name=name, grid=(C // tc, n_i),
        in_specs=[dmain_s, dprev_s, dnext_s, main, prev, nxt, wspec, bspec, pl.BlockSpec(memory_space=pl.ANY)],
        out_specs=(pl.BlockSpec((tm, sw), lambda j, i: (i, dcol0 + j)), wspec, bspec),
        out_shape=(jax.ShapeDtypeStruct(dsrc.shape, dsrc.dtype), jax.ShapeDtypeStruct((kp, C), F32),
                   jax.ShapeDtypeStruct((1, C), F32)),
        input_output_aliases={8: 0},
        scratch_shapes=[pltpu.VMEM((tm + 2 * HALO, tc), F32), pltpu.VMEM((tm + 2 * HALO, tc), F32),
                        pltpu.VMEM((8, tm + SH_ROWS, tc), F32), pltpu.VMEM((8, tm + SH_ROWS, tc), F32),
                        pltpu.VMEM((kp + 1, 8, tc), F32)],
        compiler_params=_params(2),
    )(dout, dout, dout, src, src, src, w, b, dsrc)


def _silu_grad(x, s):
    return s * (1.0 + x * (1.0 - s))


STRIP = 16
LCH = 512


def _strips(tm, fn):
    def step(s, carry):
        fn(pl.ds(pl.multiple_of(s * STRIP, STRIP), STRIP))
        return carry

    lax.fori_loop(0, tm // STRIP, step, 0, unroll=8)


def _chunks(e):
    return [slice(k, k + LCH) for k in range(0, e, LCH)]


def _row_sum(parts):
    acc = parts[0]
    for p in parts[1:]:
        acc = acc + p
    return jnp.sum(acc, axis=-1, keepdims=True)


def _fold8(x):
    return jnp.sum(x.reshape(STRIP // 8, 8, x.shape[-1]), axis=0)


def _ln_stats(u_ref, r, cks, e):
    mu = _row_sum([u_ref[r, ck].astype(F32) for ck in cks]) * (1.0 / e)
    var = _row_sum([jnp.square(u_ref[r, ck].astype(F32) - mu) for ck in cks]) * (1.0 / e)
    return mu, lax.rsqrt(var + EPS)


def conf_ln_fwd(u2, proj, ln_w, ln_b, *, name):
    T, E = u2.shape
    zc = proj.shape[1] // E - 1
    tm = _pick(T, (256, 128))
    cks = _chunks(E)

    def body(u_ref, z_ref, w_ref, b_ref, o_ref):
        def strip(r):
            mu, rstd = _ln_stats(u_ref, r, cks, E)
            for ck in cks:
                u3 = (u_ref[r, ck].astype(F32) - mu) * rstd * w_ref[:, ck] + b_ref[:, ck]
                z = z_ref[r, ck].astype(F32)
                o_ref[r, ck] = (u3 * _sigmoid(u3) * z * _sigmoid(z)).astype(BF16)

        _strips(tm, strip)

    row = pl.BlockSpec((tm, E), lambda i: (i, 0))
    vec = pl.BlockSpec((1, E), lambda i: (0, 0))
    return pl.pallas_call(
        body, name=name, grid=(T // tm,),
        in_specs=[row, pl.BlockSpec((tm, E), lambda i: (i, zc)), vec, vec], out_specs=row,
        out_shape=jax.ShapeDtypeStruct((T, E), BF16), compiler_params=_params(1),
    )(u2, proj, ln_w, ln_b)


def conf_ln_bwd(du4, u2, proj, ln_w, ln_b, *, name):
    T, E = u2.shape
    ncol = proj.shape[1] // E
    zc = ncol - 1
    tm = _pick(T, (256, 128))
    n_i = T // tm
    cks = _chunks(E)

    def body(d_ref, u_ref, z_ref, w_ref, b_ref, du_ref, dz_ref, dw_ref, db_ref, dxh_ref, accw_ref, accb_ref):
        i = pl.program_id(0)

        @pl.when(i == 0)
        def _():
            accw_ref[...] = jnp.zeros_like(accw_ref)
            accb_ref[...] = jnp.zeros_like(accb_ref)

        def strip(r):
            mu, rstd = _ln_stats(u_ref, r, cks, E)
            s1, s2 = [], []
            for ck in cks:
                xhat = (u_ref[r, ck].astype(F32) - mu) * rstd
                u3 = xhat * w_ref[:, ck] + b_ref[:, ck]
                z = z_ref[r, ck].astype(F32)
                s3, sz = _sigmoid(u3), _sigmoid(z)
                d4 = d_ref[r, ck].astype(F32)
                du3 = d4 * (z * sz) * _silu_grad(u3, s3)
                dz_ref[r, ck] = (d4 * (u3 * s3) * _silu_grad(z, sz)).astype(BF16)
                accw_ref[:, ck] += _fold8(du3 * xhat)
                accb_ref[:, ck] += _fold8(du3)
                dxh = du3 * w_ref[:, ck]
                dxh_ref[:, ck] = dxh
                s1.append(dxh)
                s2.append(dxh * xhat)
            m1, m2 = _row_sum(s1) * (1.0 / E), _row_sum(s2) * (1.0 / E)
            for ck in cks:
                xhat = (u_ref[r, ck].astype(F32) - mu) * rstd
                du_ref[r, ck] = (rstd * (dxh_ref[:, ck] - m1 - xhat * m2)).astype(BF16)

        _strips(tm, strip)

        @pl.when(i == n_i - 1)
        def _():
            dw_ref[...] = jnp.sum(accw_ref[...], axis=0, keepdims=True)
            db_ref[...] = jnp.sum(accb_ref[...], axis=0, keepdims=True)

    row = pl.BlockSpec((tm, E), lambda i: (i, 0))
    zrow = pl.BlockSpec((tm, E), lambda i: (i, zc))
    vec = pl.BlockSpec((1, E), lambda i: (0, 0))
    return pl.pallas_call(
        body, name=name, grid=(n_i,), in_specs=[row, row, zrow, vec, vec], out_specs=(row, zrow, vec, vec),
        out_shape=(jax.ShapeDtypeStruct((T, E), BF16), jax.ShapeDtypeStruct(proj.shape, BF16),
                   jax.ShapeDtypeStruct((1, E), F32), jax.ShapeDtypeStruct((1, E), F32)),
        scratch_shapes=[pltpu.VMEM((STRIP, E), F32), pltpu.VMEM((8, E), F32), pltpu.VMEM((8, E), F32)],
        compiler_params=_params(1),
    )(du4, u2, proj, ln_w, ln_b)


def _gated(y_ref, z_ref, r, ck):
    z = z_ref[r, ck].astype(F32)
    sz = _sigmoid(z)
    yv = y_ref[r, ck].astype(F32)
    return z, sz, yv, yv * (z * sz)


def ssd_gate_fwd(y, zx, norm_w, *, name):
    T, E = y.shape
    tm = _pick(T, (256, 128))
    cks = _chunks(E)

    def body(y_ref, z_ref, w_ref, o_ref, yz_ref):
        def strip(r):
            sq = []
            for ck in cks:
                yz = _gated(y_ref, z_ref, r, ck)[3]
                yz_ref[:, ck] = yz
                sq.append(yz * yz)
            rstd = lax.rsqrt(_row_sum(sq) * (1.0 / E) + EPS)
            for ck in cks:
                o_ref[r, ck] = (yz_ref[:, ck] * rstd * w_ref[:, ck]).astype(BF16)

        _strips(tm, strip)

    row = pl.BlockSpec((tm, E), lambda i: (i, 0))
    vec = pl.BlockSpec((1, E), lambda i: (0, 0))
    return pl.pallas_call(
        body, name=name, grid=(T // tm,), in_specs=[row, row, vec], out_specs=row,
        out_shape=jax.ShapeDtypeStruct((T, E), BF16), scratch_shapes=[pltpu.VMEM((STRIP, E), F32)],
        compiler_params=_params(1),
    )(y, zx, norm_w)


def ssd_gate_bwd(dyn, y, zx, norm_w, *, name):
    T, E = y.shape
    tm = _pick(T, (256, 128))
    n_i = T // tm
    cks = _chunks(E)

    def body(d_ref, y_ref, z_ref, w_ref, dy_ref, dz_ref, dw_ref, yz_ref, accw_ref):
        i = pl.program_id(0)

        @pl.when(i == 0)
        def _():
            accw_ref[...] = jnp.zeros_like(accw_ref)

        def strip(r):
            sq = []
            for ck in cks:
                yz = _gated(y_ref, z_ref, r, ck)[3]
                yz_ref[:, ck] = yz
                sq.append(yz * yz)
            rstd = lax.rsqrt(_row_sum(sq) * (1.0 / E) + EPS)
            s2 = []
            for ck in cks:
                yhat = yz_ref[:, ck] * rstd
                d = d_ref[r, ck].astype(F32)
                accw_ref[:, ck] += _fold8(d * yhat)
                s2.append(d * w_ref[:, ck] * yhat)
            m2 = _row_sum(s2) * (1.0 / E)
            for ck in cks:
                z, sz, yv, _ = _gated(y_ref, z_ref, r, ck)
                dyz = rstd * (d_ref[r, ck].astype(F32) * w_ref[:, ck] - yz_ref[:, ck] * rstd * m2)
                dy_ref[r, ck] = (dyz * (z * sz)).astype(BF16)
                dz_ref[r, ck] = (dyz * yv * _silu_grad(z, sz)).astype(BF16)

        _strips(tm, strip)

        @pl.when(i == n_i - 1)
        def _():
            dw_ref[...] = jnp.sum(accw_ref[...], axis=0, keepdims=True)

    row = pl.BlockSpec((tm, E), lambda i: (i, 0))
    vec = pl.BlockSpec((1, E), lambda i: (0, 0))
    return pl.pallas_call(
        body, name=name, grid=(n_i,), in_specs=[row, row, row, vec], out_specs=(row, row, vec),
        out_shape=(jax.ShapeDtypeStruct((T, E), BF16), jax.ShapeDtypeStruct(zx.shape, BF16),
                   jax.ShapeDtypeStruct((1, E), F32)),
        scratch_shapes=[pltpu.VMEM((STRIP, E), F32), pltpu.VMEM((8, E), F32)],
        compiler_params=_params(1),
    )(dyn, y, zx, norm_w)


def _cumsum_mm(mask, a):
    hi = a.astype(BF16)
    r1 = a - hi.astype(F32)
    mid = r1.astype(BF16)
    lo = (r1 - mid.astype(F32)).astype(BF16)
    out = _dot(jnp.where(mask, 1.0, 0.0).astype(BF16), jnp.concatenate([hi, mid, lo], axis=1))
    return out[:, :LANES] + out[:, LANES:2 * LANES] + out[:, 2 * LANES:]


def _chunk_terms(xcb, dt_raw, bias, alog, rev):
    L = CHUNK
    xs = xcb[:, :GW].astype(F32)
    Bm = xcb[:, GW:GW + D_STATE]
    Cm = xcb[:, GW + D_STATE:]
    pre = dt_raw + bias
    dt = _softplus(pre)
    A = -jnp.exp(alog)
    row = lax.broadcasted_iota(jnp.int32, (L, L), 0)
    col = lax.broadcasted_iota(jnp.int32, (L, L), 1)
    mask = (col >= row) if rev else (col <= row)
    mask_t = (col <= row) if rev else (col >= row)
    cs = _cumsum_mm(mask, dt * A)
    tot = cs[0:1, :] if rev else cs[L - 1:L, :]
    return xs, Bm, Cm, pre, dt, A, mask, mask_t, cs, cs.T, tot


def _decay(cs, cs_t, ln, mask):
    d = cs[:, ln:ln + 1] - cs_t[ln:ln + 1, :]
    return jnp.where(mask, jnp.exp(jnp.where(mask, d, 0.0)), 0.0)


def _pair(v, ln0, lo):
    return jnp.where(lo[:v.shape[0]], v[:, ln0:ln0 + 1], v[:, ln0 + 1:ln0 + 2])


def _scan_specs(nc, rev_order):
    ci = (lambda c: nc - 1 - c) if rev_order else (lambda c: c)
    xc = pl.BlockSpec((CHUNK, GROUPS * XCG), lambda c: (ci(c), 0))
    dt = pl.BlockSpec((CHUNK, GROUPS * LANES), lambda c: (ci(c), 0))
    vec = pl.BlockSpec((1, GROUPS * LANES), lambda c: (0, 0))
    wide = pl.BlockSpec((CHUNK, GROUPS * GW), lambda c: (ci(c), 0))
    wvec = pl.BlockSpec((1, GROUPS * GW), lambda c: (0, 0))
    st = pl.BlockSpec((1, D_STATE, GROUPS * GW), lambda c: (ci(c), 0, 0))
    return xc, dt, vec, wide, wvec, st


def _cols(ref, g, width):
    return ref.at[:, pl.ds(g * width, width)]


def _interleave(stages):
    for _ in itertools.zip_longest(*stages):
        pass


def _head_expand(r):
    row = lax.broadcasted_iota(jnp.int32, (LANES, GW), 0)
    col = lax.broadcasted_iota(jnp.int32, (LANES, GW), 1)
    first = (row - r * HPG) * HEADDIM
    return jnp.where(jnp.logical_and(col >= first, col < first + HEADDIM), 1.0, 0.0).astype(BF16)


def _head_collect(r):
    row = lax.broadcasted_iota(jnp.int32, (GW, LANES), 0)
    first = (lax.broadcasted_iota(jnp.int32, (GW, LANES), 1) - r * HPG) * HEADDIM
    return jnp.where(jnp.logical_and(row >= first, row < first + HEADDIM), 1.0, 0.0).astype(BF16)


def _expand(parts, sel):
    n = parts[0].shape[0]
    out = _dot(jnp.concatenate(parts, axis=0).astype(BF16), sel)
    return [out[i * n:(i + 1) * n] for i in range(len(parts))]


def ssd_scan_fwd(xc, dt4, bias4, alog4, *, rev, name, prev=None, dvec=None):
    T = xc.shape[0]
    nc = T // CHUNK
    E = GROUPS * GW
    r = 1 if rev else 0
    skip = prev is not None

    def one_group(sel, xc_ref, dt_ref, bias_ref, alog_ref, prev_ref, dvec_ref, y_ref, st_ref, s_ref):
        xs, Bm, Cm, _, dt, _, mask, _, cs, cs_t, tot = _chunk_terms(xc_ref[...], dt_ref[...], bias_ref[...],
                                                                   alog_ref[...], rev)
        yield
        dtx, ex, dx = _expand([dt, jnp.exp(cs), jnp.exp(tot - cs)], sel)
        et = jnp.exp(tot)
        cb = _dot_nt(Cm, Bm)
        yield
        sb = s_ref[...].astype(BF16)
        st_ref[...] = sb
        xp_all = xs * dtx
        y_off = _dot(Cm, sb) * ex
        lo = lax.broadcasted_iota(jnp.int32, (CHUNK, LANES), 1) < HEADDIM
        et_parts = []
        for p in range(HPG // 2):
            yield
            ln0 = r * HPG + 2 * p
            sl = slice(p * LANES, (p + 1) * LANES)
            xp = xp_all[:, sl]
            mcat = jnp.concatenate([cb * _decay(cs, cs_t, ln0, mask), cb * _decay(cs, cs_t, ln0 + 1, mask)],
                                   axis=1).astype(BF16)
            xbd = jnp.concatenate([jnp.where(lo, xp, 0.0), jnp.where(lo, 0.0, xp)], axis=0).astype(BF16)
            yp = _dot(mcat, xbd) + y_off[:, sl]
            if skip:
                yp = yp + prev_ref[:, sl].astype(F32) + xs[:, sl] * dvec_ref[:, sl]
            y_ref[:, sl] = yp.astype(BF16)
            et_parts.append(_pair(et, ln0, lo))
        yield
        s_ref[...] = s_ref[...] * jnp.concatenate(et_parts, axis=1) + _dot_tn(Bm, (xp_all * dx).astype(BF16))

    def body(*refs):
        xc_ref, dt_ref, bias_ref, alog_ref = refs[:4]
        prev_ref, dvec_ref = (refs[4], refs[5]) if skip else (None, None)
        y_ref, st_ref, s_ref = refs[-3:]

        @pl.when(pl.program_id(0) == 0)
        def _():
            s_ref[...] = jnp.zeros_like(s_ref)

        sel = _head_expand(r)
        _interleave([
            one_group(sel, _cols(xc_ref, g, XCG), _cols(dt_ref, g, LANES), _cols(bias_ref, g, LANES),
                      _cols(alog_ref, g, LANES), _cols(prev_ref, g, GW) if skip else None,
                      _cols(dvec_ref, g, GW) if skip else None, _cols(y_ref, g, GW),
                      st_ref.at[0, :, pl.ds(g * GW, GW)], _cols(s_ref, g, GW)) for g in range(GROUPS)])

    s_xc, s_dt, s_vec, s_wide, s_wvec, s_st = _scan_specs(nc, rev)
    in_specs = [s_xc, s_dt, s_vec, s_vec]
    args = [xc, dt4, bias4, alog4]
    if skip:
        in_specs += [s_wide, s_wvec]
        args += [prev, dvec]
    return pl.pallas_call(
        body, name=name, grid=(nc,), in_specs=in_specs, out_specs=(s_wide, s_st),
        out_shape=(jax.ShapeDtypeStruct((T, E), BF16), jax.ShapeDtypeStruct((nc, D_STATE, E), BF16)),
        scratch_shapes=[pltpu.VMEM((D_STATE, E), F32)], compiler_params=_params(1),
    )(*args)


def ssd_scan_bwd(xc, dt4, bias4, alog4, dy, states, *, rev, name, prev=None, dvec=None):
    T = xc.shape[0]
    nc = T // CHUNK
    E = GROUPS * GW
    L = CHUNK
    r = 1 if rev else 0
    skip = prev is not None

    def one_group(sel, sel_t, xc_ref, dt_ref, bias_ref, alog_ref, dy_ref, st_ref, pdxc_ref, pddt_ref, dvec_ref,
                  dxc_ref, ddt_ref, dalog_ref, dbias_ref, dd_ref, g_ref):
        xs, Bm, Cm, pre, dt, A, mask, mask_t, cs, cs_t, tot = _chunk_terms(
            xc_ref[...], dt_ref[...], bias_ref[...], alog_ref[...], rev)
        yield
        dtx, ex, dx = _expand([dt, jnp.exp(cs), jnp.exp(tot - cs)], sel)
        et = jnp.exp(tot)
        cb = _dot_nt(Cm, Bm)
        yield
        s_in = st_ref[...]
        dy_all = dy_ref[...].astype(F32)
        g_f = g_ref[...]
        g_b = g_f.astype(BF16)
        xp_all = xs * dtx
        dye_all = dy_all * ex
        bgd = _dot(Bm, g_b) * dx
        lane = lax.broadcasted_iota(jnp.int32, (L, LANES), 1)
        lo = lane < HEADDIM
        dcb = jnp.zeros((L, L), F32)
        yd_parts, dxd_parts, et_parts = [], [], []
        for p in range(HPG // 2):
            yield
            ln0 = r * HPG + 2 * p
            sl = slice(p * LANES, (p + 1) * LANES)
            xp, dy_p = xp_all[:, sl], dy_all[:, sl]
            lam0, lam1 = _decay(cs, cs_t, ln0, mask), _decay(cs, cs_t, ln0 + 1, mask)
            m0, m1 = (cb * lam0).astype(BF16), (cb * lam1).astype(BF16)
            dybd = jnp.concatenate([jnp.where(lo, dy_p, 0.0), jnp.where(lo, 0.0, dy_p)], axis=0).astype(BF16)
            xbd = jnp.concatenate([jnp.where(lo, xp, 0.0), jnp.where(lo, 0.0, xp)], axis=0).astype(BF16)
            dm = _dot_nt(dybd, xp.astype(BF16))
            dcb = dcb + dm[:L] * lam0 + dm[L:] * lam1
            yd_parts.append(_dot(jnp.concatenate([m0, m1], axis=1), xbd))
            dxd_parts.append(_dot_tn(jnp.concatenate([m0, m1], axis=0), dybd))
            et_parts.append(_pair(et, ln0, lo))
        yield
        y_diag = jnp.concatenate(yd_parts, axis=1)
        dx_diag = jnp.concatenate(dxd_parts, axis=1)
        etx = jnp.concatenate(et_parts, axis=1)
        dxt = dx_diag + bgd
        w2 = xp_all * bgd
        dy_r, xp_r = dy_all.astype(BF16).astype(F32), xp_all.astype(BF16).astype(F32)
        u = dye_all * _dot(Cm, s_in) + dy_r * y_diag - xp_r * dx_diag - w2
        dxx = dxt * xs
        tail = jnp.broadcast_to(jnp.sum(w2, axis=0, keepdims=True)
                                + jnp.sum(g_f * s_in.astype(F32), axis=0, keepdims=True) * etx, (8, GW))
        u_hi, t_hi = u.astype(BF16), tail.astype(BF16)
        red = _dot(jnp.concatenate([u_hi, (u - u_hi.astype(F32)).astype(BF16), dxx.astype(BF16), t_hi,
                                    (tail - t_hi.astype(F32)).astype(BF16)], axis=0), sel_t)
        yield
        dcs = red[:L] + red[L:2 * L]
        ddt = red[2 * L:3 * L]
        dtot = red[3 * L:3 * L + 1] + red[3 * L + 8:3 * L + 9]
        dxs = dxt * dtx
        if skip:
            dxs = dxs + dy_all * dvec_ref[...] + pdxc_ref[:, :GW].astype(F32)
            dd_ref[...] += jnp.sum(dy_all * xs, axis=0, keepdims=True)
        dxc_ref[:, :GW] = dxs.astype(BF16)
        dye_b = dye_all.astype(BF16)
        xd = (xp_all * dx).astype(BF16)
        dcb_b = dcb.astype(BF16)
        d_b = _dot_nt(xd, g_b) + _dot_tn(dcb_b, Cm)
        d_c = _dot_nt(dye_b, s_in) + _dot(dcb_b, Bm)
        if skip:
            d_b = d_b + pdxc_ref[:, GW:GW + D_STATE].astype(F32)
            d_c = d_c + pdxc_ref[:, GW + D_STATE:].astype(F32)
        dxc_ref[:, GW:GW + D_STATE] = d_b.astype(BF16)
        dxc_ref[:, GW + D_STATE:] = d_c.astype(BF16)
        yield
        g_ref[...] = g_f * etx + _dot_tn(Cm, dye_b)
        rowi = lax.broadcasted_iota(jnp.int32, (L, LANES), 0)
        da = _cumsum_mm(mask_t, dcs + jnp.where(rowi == (0 if rev else L - 1), dtot, 0.0))
        keep = jnp.logical_and(lane >= r * HPG, lane < (r + 1) * HPG)
        ddr = jnp.where(keep, (da * A + ddt) * _sigmoid(pre), 0.0)
        dbias_ref[...] += jnp.sum(ddr, axis=0, keepdims=True)
        dalog_ref[...] += jnp.sum(jnp.where(keep, da * dt * A, 0.0), axis=0, keepdims=True)
        if skip:
            ddr = ddr + pddt_ref[...]
        ddt_ref[...] = ddr

    def body(*refs):
        xc_ref, dt_ref, bias_ref, alog_ref, dy_ref, st_ref = refs[:6]
        pdxc_ref, pddt_ref, dvec_ref = refs[6:9] if skip else (None, None, None)
        pos = 9 if skip else 6
        dxc_ref, ddt_ref, dalog_ref, dbias_ref = refs[pos:pos + 4]
        dd_ref = refs[pos + 4] if skip else None
        g_ref = refs[-1]

        @pl.when(pl.program_id(0) == 0)
        def _():
            g_ref[...] = jnp.zeros_like(g_ref)
            dalog_ref[...] = jnp.zeros_like(dalog_ref)
            dbias_ref[...] = jnp.zeros_like(dbias_ref)
            if skip:
                dd_ref[...] = jnp.zeros_like(dd_ref)

        sel, sel_t = _head_expand(r), _head_collect(r)
        _interleave([
            one_group(sel, sel_t, _cols(xc_ref, g, XCG), _cols(dt_ref, g, LANES), _cols(bias_ref, g, LANES),
                      _cols(alog_ref, g, LANES), _cols(dy_ref, g, GW), st_ref.at[0, :, pl.ds(g * GW, GW)],
                      _cols(pdxc_ref, g, XCG) if skip else None, _cols(pddt_ref, g, LANES) if skip else None,
                      _cols(dvec_ref, g, GW) if skip else None, _cols(dxc_ref, g, XCG), _cols(ddt_ref, g, LANES),
                      _cols(dalog_ref, g, LANES), _cols(dbias_ref, g, LANES),
                      _cols(dd_ref, g, GW) if skip else None, _cols(g_ref, g, GW)) for g in range(GROUPS)])

    s_xc, s_dt, s_vec, s_wide, s_wvec, s_st = _scan_specs(nc, not rev)
    in_specs = [s_xc, s_dt, s_vec, s_vec, s_wide, s_st]
    args = [xc, dt4, bias4, alog4, dy, states]
    out_specs = [s_xc, s_dt, s_vec, s_vec]
    out_shape = [jax.ShapeDtypeStruct((T, GROUPS * XCG), BF16), jax.ShapeDtypeStruct((T, GROUPS * LANES), F32),
                 jax.ShapeDtypeStruct((1, GROUPS * LANES), F32), jax.ShapeDtypeStruct((1, GROUPS * LANES), F32)]
    if skip:
        in_specs += [s_xc, s_dt, s_wvec]
        args += [prev[0], prev[1], dvec]
        out_specs.append(s_wvec)
        out_shape.append(jax.ShapeDtypeStruct((1, E), F32))
    return pl.pallas_call(
        body, name=name, grid=(nc,), in_specs=in_specs, out_specs=tuple(out_specs),
        out_shape=tuple(out_shape), scratch_shapes=[pltpu.VMEM((D_STATE, E), F32)], compiler_params=_params(1),
    )(*args)


def _conf_cols(w):
    e = w.shape[-1] // 3
    lead = w.shape[:-1]
    vg = w[..., :2 * e].reshape(*lead, 2, e // CONV_TC, CONV_TC)
    vg = jnp.swapaxes(vg, -3, -2).reshape(*lead, 2 * e)
    return jnp.concatenate([vg, w[..., 2 * e:]], axis=-1)


def _conf_cols_inv(w):
    e = w.shape[-1] // 3
    lead = w.shape[:-1]
    vg = w[..., :2 * e].reshape(*lead, e // CONV_TC, 2, CONV_TC)
    vg = jnp.swapaxes(vg, -3, -2).reshape(*lead, 2 * e)
    return jnp.concatenate([vg, w[..., 2 * e:]], axis=-1)


def _xbc_cols(w):
    lead = w.shape[:-1]
    e = GROUPS * GW
    gn = GROUPS * D_STATE
    parts = [w[..., :e].reshape(*lead, GROUPS, GW), w[..., e:e + gn].reshape(*lead, GROUPS, D_STATE),
             w[..., e + gn:].reshape(*lead, GROUPS, D_STATE)]
    return jnp.concatenate(parts, axis=-1).reshape(*lead, GROUPS * XCG)


def _xbc_cols_inv(w):
    lead = w.shape[:-1]
    g = w.reshape(*lead, GROUPS, XCG)
    parts = [g[..., :GW].reshape(*lead, GROUPS * GW), g[..., GW:GW + D_STATE].reshape(*lead, GROUPS * D_STATE),
             g[..., GW + D_STATE:].reshape(*lead, GROUPS * D_STATE)]
    return jnp.concatenate(parts, axis=-1)


def _dt_cols(w):
    lead = w.shape[:-1]
    t = jnp.swapaxes(w.reshape(*lead, 2, GROUPS, HPG), -3, -2).reshape(*lead, GROUPS, 2 * HPG)
    pad = [(0, 0)] * (t.ndim - 1) + [(0, LANES - 2 * HPG)]
    return jnp.pad(t, pad).reshape(*lead, GROUPS * LANES)


def _dt_cols_inv(w):
    lead = w.shape[:-1]
    t = w.reshape(*lead, GROUPS, LANES)[..., :2 * HPG].reshape(*lead, GROUPS, 2, HPG)
    return jnp.swapaxes(t, -3, -2).reshape(*lead, 2 * HEADS)


def _pad_rows(w, rows):
    return jnp.pad(w, ((0, rows - w.shape[0]), (0, 0)))


def conf_weights(w_in, dw_w, dw_b, ln_w, ln_b):
    w_in_p = _conf_cols(w_in)
    return dict(w_in=w_in_p, dw_w=_pad_rows(dw_w, 32), dw_b=dw_b.reshape(1, -1),
                ln_w=ln_w.reshape(1, -1), ln_b=ln_b.reshape(1, -1))


def _xbc_rows(w):
    e, gn, c = GROUPS * GW, GROUPS * D_STATE, w.shape[1]
    parts = [w[:e].reshape(GROUPS, GW, c), w[e:e + gn].reshape(GROUPS, D_STATE, c),
             w[e + gn:].reshape(GROUPS, D_STATE, c)]
    return jnp.concatenate(parts, axis=1).reshape(GROUPS * XCG, c)


def _xbc_rows_inv(w):
    c = w.shape[1]
    g = w.reshape(GROUPS, XCG, c)
    parts = [g[:, :GW].reshape(GROUPS * GW, c), g[:, GW:GW + D_STATE].reshape(GROUPS * D_STATE, c),
             g[:, GW + D_STATE:].reshape(GROUPS * D_STATE, c)]
    return jnp.concatenate(parts, axis=0)


def _dt_rows(w):
    c = w.shape[1]
    t = jnp.swapaxes(w.reshape(2, GROUPS, HPG, c), 0, 1).reshape(GROUPS, 2 * HPG, c)
    return jnp.pad(t, ((0, 0), (0, LANES - 2 * HPG), (0, 0))).reshape(GROUPS * LANES, c)


def _dt_rows_inv(w):
    c = w.shape[1]
    t = w.reshape(GROUPS, LANES, c)[:, :2 * HPG].reshape(GROUPS, 2, HPG, c)
    return jnp.swapaxes(t, 0, 1).reshape(2 * HEADS, c)


def ssd_weights(w_in_t, conv_w, conv_b, dt_bias, a_log, d_skip, norm_w, w_out):
    e = GROUPS * GW
    xbc = e + 2 * GROUPS * D_STATE
    w_zx_t = jnp.concatenate([w_in_t[:e], _xbc_rows(w_in_t[e:e + xbc])], axis=0)
    return dict(w_zx_t=w_zx_t, w_dt_t=_dt_rows(w_in_t[e + xbc:]), w_out=w_out,
                conv_w=_pad_rows(_xbc_cols(conv_w), 8), conv_b=_xbc_cols(conv_b.reshape(1, -1)),
                bias4=_dt_cols(dt_bias.reshape(1, -1)), alog4=_dt_cols(a_log.reshape(1, -1)),
                dvec=jnp.repeat(d_skip, HEADDIM).reshape(1, -1), norm_w=norm_w.reshape(1, -1))


def conf_layer_fwd(h, nw, p, tag, side=(), w_out=None):
    hn = rmsnorm_fwd(h, nw, name=f"{tag}_norm")
    proj = mm_nn(hn, p["w_in"], out_dtype=BF16, name=f"{tag}_proj")
    u2 = dwconv_fwd(proj, p["dw_w"], p["dw_b"], width=31, glu=True, silu=False, col0=0, name=f"{tag}_conv", side=side)
    if side:
        u2, gathered = u2
        w_out = w_out(gathered)
    p.update(w_out=w_out)
    u4 = conf_ln_fwd(u2, proj, p["ln_w"], p["ln_b"], name=f"{tag}_ln")
    h2 = mm_nn(u4, p["w_out"], out_dtype=F32, res=h, name=f"{tag}_out")
    return h2, (h, hn, proj, u2, u4)


def conf_layer_bwd(dh, saved, nw, p, tag):
    h, hn, proj, u2, u4 = saved
    du4 = mm_nn(dh, p["w_out"], out_dtype=BF16, b_rows_are_n=True, name=f"{tag}_d_u4")
    dw_out = mm_tn(u4, dh, name=f"{tag}_dw_out")
    du2, dproj, dln_w, dln_b = conf_ln_bwd(du4, u2, proj, p["ln_w"], p["ln_b"], name=f"{tag}_d_ln")
    dproj, ddw_w, ddw_b = dwconv_bwd(du2, proj, p["dw_w"], p["dw_b"], dproj, width=31, glu=True, silu=False,
                                     col0=0, dcol0=0, name=f"{tag}_d_conv")
    dh_prev, dnw = mm_nn(dproj, p["w_in"], out_dtype=F32, b_rows_are_n=True, norm_bwd=(h, nw, dh),
                         name=f"{tag}_d_hn")
    dw_in = mm_tn(hn, dproj, name=f"{tag}_dw_in")
    grads = dict(w_in=_conf_cols_inv(dw_in), dw_w=ddw_w[:31], dw_b=ddw_b[0], ln_w=dln_w[0], ln_b=dln_b[0],
                 w_out=dw_out, norm=dnw[0])
    return dh_prev, grads


def ssd_layer_fwd(h, nw, p, tag):
    e = GROUPS * GW
    hn = rmsnorm_fwd(h, nw, name=f"{tag}_norm")
    zx = mm_nn(hn, p["w_zx_t"], out_dtype=BF16, b_rows_are_n=True, name=f"{tag}_proj")
    dt4 = mm_nn(hn, p["w_dt_t"], out_dtype=F32, b_rows_are_n=True, name=f"{tag}_proj_dt")
    xc = dwconv_fwd(zx, p["conv_w"], p["conv_b"], width=5, glu=False, silu=True, col0=e // CONV_TC, name=f"{tag}_conv")
    y0, st0 = ssd_scan_fwd(xc, dt4, p["bias4"], p["alog4"], rev=False, name=f"{tag}_scan_f")
    y, st1 = ssd_scan_fwd(xc, dt4, p["bias4"], p["alog4"], rev=True, prev=y0, dvec=p["dvec"], name=f"{tag}_scan_b")
    yn = ssd_gate_fwd(y, zx, p["norm_w"], name=f"{tag}_gate")
    h2 = mm_nn(yn, p["w_out"], out_dtype=F32, res=h, name=f"{tag}_out")
    return h2, (h, hn, zx, dt4, xc, st0, st1, y, yn)


def ssd_layer_bwd(dh, saved, nw, p, tag):
    e = GROUPS * GW
    h, hn, zx, dt4, xc, st0, st1, y, yn = saved
    dyn = mm_nn(dh, p["w_out"], out_dtype=BF16, b_rows_are_n=True, name=f"{tag}_d_yn")
    dw_out = mm_tn(yn, dh, name=f"{tag}_dw_out")
    dy, dzx, dnorm_w = ssd_gate_bwd(dyn, y, zx, p["norm_w"], name=f"{tag}_d_gate")
    dxc0, ddt0, dalog0, dbias0 = ssd_scan_bwd(xc, dt4, p["bias4"], p["alog4"], dy, st0, rev=False,
                                              name=f"{tag}_d_scan_f")
    dxc, ddt4, dalog1, dbias1, ddvec = ssd_scan_bwd(xc, dt4, p["bias4"], p["alog4"], dy, st1, rev=True,
                                                    prev=(dxc0, ddt0), dvec=p["dvec"], name=f"{tag}_d_scan_b")
    dzx, dconv_w, dconv_b = dwconv_bwd(dxc, zx, p["conv_w"], p["conv_b"], dzx, width=5, glu=False, silu=True,
                                       col0=e // CONV_TC, dcol0=e // CONV_TC, name=f"{tag}_d_conv")
    dh_prev, dnw = mm_nn(dzx, p["w_zx_t"], out_dtype=F32, a2=ddt4, b2=p["w_dt_t"], norm_bwd=(h, nw, dh),
                         name=f"{tag}_d_hn")
    dw_zx_t = mm_tn(dzx, hn, name=f"{tag}_dw_zx")
    dw_dt_t = mm_tn(ddt4, hn, name=f"{tag}_dw_dt")
    dw_in_t = jnp.concatenate([dw_zx_t[:e], _xbc_rows_inv(dw_zx_t[e:]), _dt_rows_inv(dw_dt_t)], axis=0)
    grads = dict(w_in_t=dw_in_t, conv_w=_xbc_cols_inv(dconv_w[:5]), conv_b=_xbc_cols_inv(dconv_b)[0],
                 dt_bias=_dt_cols_inv(dbias0 + dbias1).reshape(2, HEADS),
                 a_log=_dt_cols_inv(dalog0 + dalog1).reshape(2, HEADS),
                 d_skip=jnp.sum(ddvec.reshape(HEADS, HEADDIM), axis=-1), norm_w=dnorm_w[0], w_out=dw_out,
                 norm=dnw[0])
    return dh_prev, grads


def gather_chips(bufs, *, name):
    n = len(bufs)
    rows = [b.shape[0] for b in bufs]

    def body(*refs):
        ins, outs, sems = refs[:n], refs[n:2 * n], refs[2 * n:]
        _gather_start(rows, ins, outs, sems)
        _gather_finish(rows, ins, outs, sems)

    return pl.pallas_call(
        body, name=name, in_specs=[ANY] * n, out_specs=tuple([ANY] * n),
        out_shape=tuple(jax.ShapeDtypeStruct((N_CHIPS,) + b.shape, b.dtype) for b in bufs),
        scratch_shapes=_gather_sems(n),
    )(*bufs)


def swap_other_half(g2, *, name):
    def body(g_ref, o_ref, send_sem, recv_sem):
        x, y, c = _place()
        cp = pltpu.make_async_remote_copy(src_ref=g_ref.at[1 - c], dst_ref=o_ref, send_sem=send_sem, recv_sem=recv_sem,
                                          device_id=(x, y, 1 - c), device_id_type=MESH)
        cp.start()
        cp.wait()

    return pl.pallas_call(
        body, name=name, in_specs=[ANY], out_specs=ANY, out_shape=jax.ShapeDtypeStruct(g2.shape[1:], g2.dtype),
        scratch_shapes=[pltpu.SemaphoreType.DMA, pltpu.SemaphoreType.DMA],
    )(g2)


def exchange_chips(p, *, name):
    def body(p_ref, o_ref, send_sems, recv_sems, local_sem):
        x, y, c = _place()
        k_me = 2 * x + y
        own = pltpu.make_async_copy(p_ref.at[k_me], o_ref.at[k_me], local_sem)
        own.start()
        copies = [own]
        for j, (px, py) in enumerate([(1 - x, y), (x, 1 - y), (1 - x, 1 - y)]):
            cp = pltpu.make_async_remote_copy(
                src_ref=p_ref.at[2 * px + py], dst_ref=o_ref.at[k_me], send_sem=send_sems.at[j],
                recv_sem=recv_sems.at[j], device_id=(px, py, c), device_id_type=MESH)
            cp.start()
            copies.append(cp)
        for cp in copies:
            cp.wait()

    return pl.pallas_call(
        body, name=name, in_specs=[ANY], out_specs=ANY, out_shape=jax.ShapeDtypeStruct(p.shape, p.dtype),
        scratch_shapes=[pltpu.SemaphoreType.DMA((3,)), pltpu.SemaphoreType.DMA((3,)), pltpu.SemaphoreType.DMA],
    )(p)


def share_half(full, *, name):
    def body(_, f_ref, send_sem, recv_sem):
        x, y, c = _place()
        cp = pltpu.make_async_remote_copy(src_ref=f_ref.at[c], dst_ref=f_ref.at[c], send_sem=send_sem,
                                          recv_sem=recv_sem, device_id=(x, y, 1 - c), device_id_type=MESH)
        cp.start()
        cp.wait()

    return pl.pallas_call(
        body, name=name, in_specs=[ANY], out_specs=ANY, out_shape=jax.ShapeDtypeStruct(full.shape, full.dtype),
        input_output_aliases={0: 0},
        scratch_shapes=[pltpu.SemaphoreType.DMA, pltpu.SemaphoreType.DMA],
    )(full)


def gather_all(v, *, name):
    def body(v_ref, o_ref, send_sems, recv_sems, local_sem):
        x, y, c = _place()
        me = 4 * x + 2 * y + c
        own = pltpu.make_async_copy(v_ref, o_ref.at[me], local_sem)
        own.start()
        copies = [own]
        idx = 0
        for fx in (0, 1):
            for fy in (0, 1):
                for fc in (0, 1):
                    if not (fx or fy or fc):
                        continue
                    peer = (1 - x if fx else x, 1 - y if fy else y, 1 - c if fc else c)
                    cp = pltpu.make_async_remote_copy(src_ref=v_ref, dst_ref=o_ref.at[me], send_sem=send_sems.at[idx],
                                                      recv_sem=recv_sems.at[idx], device_id=peer, device_id_type=MESH)
                    cp.start()
                    copies.append(cp)
                    idx += 1
        for cp in copies:
            cp.wait()

    return pl.pallas_call(
        body, name=name, in_specs=[ANY], out_specs=ANY, out_shape=jax.ShapeDtypeStruct((N_DEV,) + v.shape, v.dtype),
        scratch_shapes=[pltpu.SemaphoreType.DMA((N_DEV - 1,)), pltpu.SemaphoreType.DMA((N_DEV - 1,)),
                        pltpu.SemaphoreType.DMA],
    )(v)


RED_TR = 432


def pair_sum(g2, recv, cidx, *, name):
    _, K, R, C = g2.shape
    tr = _pick(R, (RED_TR, 8))

    def body(c_ref, a_ref, b_ref, o_ref):
        o_ref[...] = (a_ref[0] + b_ref[...]).astype(BF16)

    blk = pl.BlockSpec((1, tr, C), lambda k, i, c: (k, i, 0))
    return pl.pallas_call(
        body, name=name,
        grid_spec=pltpu.PrefetchScalarGridSpec(
            num_scalar_prefetch=1, grid=(K, R // tr),
            in_specs=[pl.BlockSpec((1, 1, tr, C), lambda k, i, c: (c[0], k, i, 0)), blk], out_specs=blk),
        out_shape=jax.ShapeDtypeStruct((K, R, C), BF16), compiler_params=_params(2),
    )(cidx, g2, recv)


def sum_lead(a, *, name, slot=None, nslots=1):
    K, R, C = a.shape
    tr = _pick(R, (RED_TR, 8))

    def body(s_ref, a_ref, o_ref):
        acc = a_ref[0].astype(F32)
        for k in range(1, K):
            acc = acc + a_ref[k].astype(F32)
        o_ref[0] = acc

    if slot is None:
        slot = jnp.zeros((1,), jnp.int32)
    return pl.pallas_call(
        body, name=name,
        grid_spec=pltpu.PrefetchScalarGridSpec(
            num_scalar_prefetch=1, grid=(R // tr,),
            in_specs=[pl.BlockSpec((K, tr, C), lambda i, s: (0, i, 0))],
            out_specs=pl.BlockSpec((1, tr, C), lambda i, s: (s[0], i, 0))),
        out_shape=jax.ShapeDtypeStruct((nslots, R, C), F32), compiler_params=_params(1),
    )(slot, a)


def adamw(g, w, m, v, *, name):
    R, C = w.shape
    tr = _pick(R, (256, 128, 64, 32, 16, 8))

    def body(g_ref, w_ref, m_ref, v_ref, d_ref, nm_ref, nv_ref):
        gv = g_ref[...]
        m_new = ADAM_B1 * m_ref[...] + (1.0 - ADAM_B1) * gv
        v_new = ADAM_B2 * v_ref[...] + (1.0 - ADAM_B2) * (gv * gv)
        m_hat = m_new / (1.0 - ADAM_B1 ** ADAM_STEP)
        v_hat = v_new / (1.0 - ADAM_B2 ** ADAM_STEP)
        d_ref[...] = -ADAM_LR * (m_hat / (jnp.sqrt(v_hat) + ADAM_EPS) + ADAM_WD * w_ref[...])
        nm_ref[...] = m_new
        nv_ref[...] = v_new

    blk = pl.BlockSpec((tr, C), lambda i: (i, 0))
    sds = jax.ShapeDtypeStruct((R, C), F32)
    return pl.pallas_call(
        body, name=name, grid=(R // tr,), in_specs=[blk] * 4, out_specs=(blk,) * 3, out_shape=(sds,) * 3,
        compiler_params=_params(1),
    )(g, w, m, v)


WEIGHTS = ("norm_w", "final_norm_w", "cm_w_in", "cm_dw_w", "cm_dw_b", "cm_ln_w", "cm_ln_b", "cm_w_out", "ssd_w_in",
           "ssd_conv_w", "ssd_conv_b", "ssd_dt_bias", "ssd_A_log", "ssd_D", "ssd_norm_w", "ssd_w_out")
BIG = (("cm_w_in", 2), ("cm_w_out", 1), ("ssd_w_in", 1), ("ssd_w_out", 1))
TRANSPOSED = ("ssd_w_in",)
SMALL_SHARDED = (("cm_dw_w", 2), ("ssd_conv_w", 2), ("ssd_conv_b", 1), ("ssd_norm_w", 1))
REPLICATED = ("norm_w", "final_norm_w", "cm_dw_b", "cm_ln_w", "cm_ln_b", "ssd_dt_bias", "ssd_A_log", "ssd_D")
ROW = 1024


def _to_shards(g, axis):
    n = g.shape[axis]
    s = g.reshape(g.shape[:axis] + (N_CHIPS, n // N_CHIPS) + g.shape[axis + 1:])
    return jnp.moveaxis(s, axis, 0).reshape(N_CHIPS, -1)


def _from_shards(x4, local_shape, axis):
    local_shape = tuple(local_shape)
    s = jnp.moveaxis(x4.reshape((N_CHIPS,) + local_shape), 0, axis)
    return s.reshape(local_shape[:axis] + (N_CHIPS * local_shape[axis],) + local_shape[axis + 1:])


def _flat_pad(parts, multiple):
    n = sum(p.size for p in parts)
    fill = [jnp.zeros(((-n) % multiple,), parts[0].dtype)] if n % multiple else []
    return jnp.concatenate([p.reshape(-1) for p in parts] + fill)


def _split(flat, like, names):
    out, off = {}, 0
    for n in names:
        out[n] = flat[off:off + like[n].size].reshape(like[n].shape)
        off += like[n].size
    return out


def kernel(x, norm_w, final_norm_w, cm_w_in, cm_dw_w, cm_dw_b, cm_ln_w, cm_ln_b, cm_w_out, ssd_w_in, ssd_conv_w, ssd_conv_b, ssd_dt_bias, ssd_A_log, ssd_D, ssd_norm_w, ssd_w_out, loss_target, m_norm_w, m_final_norm_w, m_cm_w_in, m_cm_dw_w, m_cm_dw_b, m_cm_ln_w, m_cm_ln_b, m_cm_w_out, m_ssd_w_in, m_ssd_conv_w, m_ssd_conv_b, m_ssd_dt_bias, m_ssd_A_log, m_ssd_D, m_ssd_norm_w, m_ssd_w_out, v_norm_w, v_final_norm_w, v_cm_w_in, v_cm_dw_w, v_cm_dw_b, v_cm_ln_w, v_cm_ln_b, v_cm_w_out, v_ssd_w_in, v_ssd_conv_w, v_ssd_conv_b, v_ssd_dt_bias, v_ssd_A_log, v_ssd_D, v_ssd_norm_w, v_ssd_w_out):
    a = dict(locals())
    w = {n: a[n] for n in WEIGHTS}
    m = {n: a["m_" + n] for n in WEIGHTS}
    v = {n: a["v_" + n] for n in WEIGHTS}
    _, T, D = x.shape
    cidx = lax.axis_index("c").astype(jnp.int32).reshape(1)
    big_names = [n for n, _ in BIG]
    small_names = [n for n, _ in SMALL_SHARDED]

    wx = {n: (jnp.swapaxes(w[n], 1, 2) if n in TRANSPOSED else w[n]) for n in big_names + small_names}
    big = _flat_pad([wx[n] for n in big_names], 16 * ROW).astype(BF16).reshape(-1, ROW)
    small = _flat_pad([wx[n] for n in small_names], 8 * ROW).reshape(-1, ROW)
    first_name, first_axis = BIG[0]
    first_shape = wx[first_name].shape[1:]
    n_first = math.prod(first_shape) // ROW
    g_first, g_small = gather_chips([big[:n_first], small], name="gather_weights")
    g_small = g_small.reshape(N_CHIPS, -1)
    full, off = {}, 0
    for n, ax in SMALL_SHARDED:
        full[n] = _from_shards(g_small[:, off:off + wx[n].size], wx[n].shape, ax)
        off += wx[n].size
    w_in_0 = _from_shards(g_first.reshape(N_CHIPS, -1), first_shape, first_axis - 1)
    n_layers = norm_w.shape[0]
    lw = [None] * n_layers
    lw[0] = conf_weights(w_in_0, full["cm_dw_w"][0], cm_dw_b[0], cm_ln_w[0], cm_ln_b[0])

    def unpack_rest(gathered):
        g_big = jnp.concatenate([g_first, gathered[0]], axis=1).reshape(N_CHIPS, -1)
        off = 0
        for n, ax in BIG:
            full[n] = _from_shards(g_big[:, off:off + wx[n].size], wx[n].shape, ax)
            off += wx[n].size
        for i in range(1, n_layers):
            j = i // 2
            if i % 2 == 0:
                lw[i] = conf_weights(full["cm_w_in"][j], full["cm_dw_w"][j], cm_dw_b[j], cm_ln_w[j], cm_ln_b[j])
            else:
                lw[i] = ssd_weights(full["ssd_w_in"][j], full["ssd_conv_w"][j], full["ssd_conv_b"][j], ssd_dt_bias[j],
                                    ssd_A_log[j], ssd_D[j], full["ssd_norm_w"][j], full["ssd_w_out"][j])
        return full["cm_w_out"][0]

    h = x[0]
    saved = []
    for i in range(n_layers):
        nw_i = norm_w[i].reshape(1, -1)
        if i == 0:
            h, s = conf_layer_fwd(h, nw_i, lw[0], "l0", side=(big[n_first:],), w_out=unpack_rest)
        elif i % 2 == 0:
            h, s = conf_layer_fwd(h, nw_i, lw[i], f"l{i}", w_out=full["cm_w_out"][i // 2])
        else:
            h, s = ssd_layer_fwd(h, nw_i, lw[i], f"l{i}")
        saved.append(s)
    dh, loss_local, d_final = loss_head(h, loss_target[0], final_norm_w.reshape(1, -1), name="loss_head")
    lg = [None] * n_layers
    for i in reversed(range(n_layers)):
        bwd = conf_layer_bwd if i % 2 == 0 else ssd_layer_bwd
        dh, lg[i] = bwd(dh, saved[i], norm_w[i].reshape(1, -1), lw[i], f"l{i}")
    conf_g, ssd_g = lg[0::2], lg[1::2]
    local = {
        "norm_w": jnp.stack([g["norm"] for g in lg]), "final_norm_w": d_final[0],
        "cm_w_in": jnp.stack([g["w_in"] for g in conf_g]), "cm_dw_w": jnp.stack([g["dw_w"] for g in conf_g]),
        "cm_dw_b": jnp.stack([g["dw_b"] for g in conf_g]), "cm_ln_w": jnp.stack([g["ln_w"] for g in conf_g]),
        "cm_ln_b": jnp.stack([g["ln_b"] for g in conf_g]), "cm_w_out": jnp.stack([g["w_out"] for g in conf_g]),
        "ssd_w_in": jnp.stack([g["w_in_t"] for g in ssd_g]), "ssd_conv_w": jnp.stack([g["conv_w"] for g in ssd_g]),
        "ssd_conv_b": jnp.stack([g["conv_b"] for g in ssd_g]), "ssd_dt_bias": jnp.stack([g["dt_bias"] for g in ssd_g]),
        "ssd_A_log": jnp.stack([g["a_log"] for g in ssd_g]), "ssd_D": jnp.stack([g["d_skip"] for g in ssd_g]),
        "ssd_norm_w": jnp.stack([g["norm_w"] for g in ssd_g]), "ssd_w_out": jnp.stack([g["w_out"] for g in ssd_g]),
    }

    shards = [_to_shards(local[n], ax) for n, ax in BIG + SMALL_SHARDED]
    fill = (-sum(t.shape[1] for t in shards)) % (2 * RED_TR * ROW)
    flat4 = jnp.concatenate(shards + [jnp.zeros((N_CHIPS, fill), F32)], axis=1)
    g2 = jnp.swapaxes(flat4.reshape(N_CHIPS, 2, -1, ROW), 0, 1)
    theirs = swap_other_half(g2, name="grad_pair_swap")
    part = pair_sum(g2, theirs, cidx, name="grad_pair_sum")
    got = exchange_chips(part, name="grad_chip_exchange")
    half = sum_lead(got, slot=cidx, nslots=2, name="grad_chip_sum")
    shard_flat = share_half(half, name="grad_pair_share").reshape(-1)
    grads = _split(shard_flat, wx, big_names + small_names)
    for n in TRANSPOSED:
        grads[n] = jnp.swapaxes(grads[n], 1, 2)

    rep = _flat_pad([local[n] for n in REPLICATED], 8 * LANES).reshape(-1, LANES)
    rep_sum = sum_lead(gather_all(rep, name="grad_small_gather"), name="grad_small_sum")
    grads.update(_split(rep_sum.reshape(-1), w, REPLICATED))

    delta, new_m, new_v = {}, {}, {}
    for n in big_names:
        two_d = (-1, w[n].shape[-1])
        d_, m_, v_ = adamw(grads[n].reshape(two_d), w[n].reshape(two_d), m[n].reshape(two_d), v[n].reshape(two_d),
                           name="adamw_" + n)
        delta[n], new_m[n], new_v[n] = d_.reshape(w[n].shape), m_.reshape(w[n].shape), v_.reshape(w[n].shape)
    rest = list(REPLICATED) + small_names
    packed = [_flat_pad([t[n] for n in rest], 8 * LANES).reshape(-1, LANES) for t in (grads, w, m, v)]
    for out, res in zip((delta, new_m, new_v), adamw(*packed, name="adamw_small")):
        out.update(_split(res.reshape(-1), w, rest))

    loss = lax.psum(loss_local[0, 0], ("x", "y", "c"))
    return (loss, dh.reshape(x.shape), *[grads[n] for n in WEIGHTS], *[delta[n] for n in WEIGHTS],
            *[new_m[n] for n in WEIGHTS], *[new_v[n] for n in WEIGHTS])
```

```python
import itertools
import math

import jax
import jax.numpy as jnp
from jax import lax
from jax.experimental import pallas as pl
from jax.experimental.pallas import tpu as pltpu

F32 = jnp.float32
BF16 = jnp.bfloat16
MESH = pl.DeviceIdType.MESH

EPS = 1e-5
HEADDIM = 64
HEADS = 32
GROUPS = 4
HPG = HEADS // GROUPS
D_STATE = 128
CHUNK = 128
GW = HPG * HEADDIM
XCG = GW + 2 * D_STATE
HALO = 16
LANES = 128
N_CHIPS = 4
N_DEV = 8

ADAM_LR = 0.001
ADAM_B1 = 0.9
ADAM_B2 = 0.999
ADAM_EPS = 1e-08
ADAM_WD = 0.01
ADAM_STEP = 10

VMEM_LIMIT = 52 * 1024 * 1024


def _params(n_axes):
    return pltpu.CompilerParams(dimension_semantics=("arbitrary",) * n_axes, vmem_limit_bytes=VMEM_LIMIT)


def _sigmoid(x):
    return 1.0 / (1.0 + jnp.exp(-x))


def _softplus(x):
    return jnp.maximum(x, 0.0) + jnp.log(1.0 + jnp.exp(-jnp.abs(x)))


def _dot(a, b):
    return jnp.dot(a, b, preferred_element_type=F32)


def _dot_nt(a, b):
    return lax.dot_general(a, b, (((1,), (1,)), ((), ())), preferred_element_type=F32)


def _dot_tn(a, b):
    return lax.dot_general(a, b, (((0,), (0,)), ((), ())), preferred_element_type=F32)


def _pick(n, pref):
    for t in pref:
        if n % t == 0:
            return t
    return n


def mm_nn(a, b, *, out_dtype, name, res=None, a2=None, b2=None, b_rows_are_n=False, norm_bwd=None):
    M, K = a.shape
    N = b.shape[0] if b_rows_are_n else b.shape[1]
    has2, has_res, has_nb = a2 is not None, res is not None, norm_bwd is not None
    tm = _pick(M, (1024, 512, 256, 128))
    tn = N if has_nb else _pick(N, (1024, 512, 256, 128))
    tk = _pick(K, (2048, 1024, 512, 256, 128) if a.dtype == BF16 and not has_nb else (1024, 512, 256, 128))
    nk = K // tk

    def body(*refs):
        a_ref, b_ref = refs[0], refs[1]
        pos = 2
        if has2:
            a2_ref, b2_ref = refs[pos], refs[pos + 1]
            pos += 2
        if has_res:
            r_ref = refs[pos]
            pos += 1
        if has_nb:
            h_ref, w_ref, dh_ref = refs[pos:pos + 3]
            pos += 3
        o_ref = refs[pos]
        acc_ref = refs[-1]
        k = pl.program_id(2)
        first_rows = pl.program_id(0) == 0

        @pl.when(k == 0)
        def _():
            if has2:
                acc_ref[...] = _dot(a2_ref[...].astype(BF16), b2_ref[...])
            else:
                acc_ref[...] = jnp.zeros_like(acc_ref)

        acc_ref[...] += (_dot_nt if b_rows_are_n else _dot)(a_ref[...].astype(BF16), b_ref[...])

        @pl.when(k == nk - 1)
        def _():
            r = acc_ref[...]
            if has_res:
                r = r + r_ref[...]
            if has_nb:
                dw_ref = refs[pos + 1]

                @pl.when(first_rows)
                def _():
                    dw_ref[...] = jnp.zeros_like(dw_ref)

                x = h_ref[...]
                rstd = lax.rsqrt(jnp.mean(x * x, axis=-1, keepdims=True) + EPS)
                xhat = x * rstd
                dxh = r * w_ref[...]
                dw_ref[...] += jnp.sum(r * xhat, axis=0, keepdims=True)
                r = dh_ref[...] + rstd * (dxh - xhat * jnp.mean(dxh * xhat, axis=-1, keepdims=True))
            o_ref[...] = r.astype(out_dtype)

    b_spec = pl.BlockSpec((tn, tk), lambda i, j, k: (j, k)) if b_rows_are_n else pl.BlockSpec((tk, tn), lambda i, j, k: (k, j))
    in_specs = [pl.BlockSpec((tm, tk), lambda i, j, k: (i, k)), b_spec]
    args = [a, b]
    if has2:
        k2 = a2.shape[1]
        in_specs += [pl.BlockSpec((tm, k2), lambda i, j, k: (i, 0)), pl.BlockSpec((k2, tn), lambda i, j, k: (0, j))]
        args += [a2, b2]
    tile = pl.BlockSpec((tm, tn), lambda i, j, k: (i, j))
    if has_res:
        in_specs.append(tile)
        args.append(res)
    out_specs, out_shape = tile, jax.ShapeDtypeStruct((M, N), out_dtype)
    if has_nb:
        vec = pl.BlockSpec((1, N), lambda i, j, k: (0, 0))
        in_specs += [tile, vec, tile]
        args += list(norm_bwd)
        out_specs, out_shape = (tile, vec), (out_shape, jax.ShapeDtypeStruct((1, N), F32))
    return pl.pallas_call(
        body, name=name, grid=(M // tm, N // tn, nk), in_specs=in_specs, out_specs=out_specs, out_shape=out_shape,
        scratch_shapes=[pltpu.VMEM((tm, tn), F32)], compiler_params=_params(3),
    )(*args)


def mm_tn(a, b, *, name):
    T, M = a.shape
    N = b.shape[1]
    tm = _pick(M, (1024, 512, 256, 128))
    tn = _pick(N, (1024, 512, 256, 128))
    tt = _pick(T, (2048, 1024, 512, 256, 128))

    def body(a_ref, b_ref, o_ref):
        @pl.when(pl.program_id(2) == 0)
        def _():
            o_ref[...] = jnp.zeros_like(o_ref)

        o_ref[...] += _dot_tn(a_ref[...].astype(BF16), b_ref[...].astype(BF16))

    return pl.pallas_call(
        body, name=name, grid=(M // tm, N // tn, T // tt),
        in_specs=[pl.BlockSpec((tt, tm), lambda i, j, t: (t, i)), pl.BlockSpec((tt, tn), lambda i, j, t: (t, j))],
        out_specs=pl.BlockSpec((tm, tn), lambda i, j, t: (i, j)),
        out_shape=jax.ShapeDtypeStruct((M, N), F32), compiler_params=_params(3),
    )(a, b)


def rmsnorm_fwd(h, w, *, name):
    T, D = h.shape
    tm = _pick(T, (512, 256, 128))
    cks = _chunks(D)

    def body(h_ref, w_ref, o_ref):
        def strip(r):
            rstd = lax.rsqrt(_row_sum([jnp.square(h_ref[r, ck]) for ck in cks]) * (1.0 / D) + EPS)
            for ck in cks:
                o_ref[r, ck] = (h_ref[r, ck] * rstd * w_ref[:, ck]).astype(BF16)

        _strips(tm, strip)

    return pl.pallas_call(
        body, name=name, grid=(T // tm,),
        in_specs=[pl.BlockSpec((tm, D), lambda i: (i, 0)), pl.BlockSpec((1, D), lambda i: (0, 0))],
        out_specs=pl.BlockSpec((tm, D), lambda i: (i, 0)),
        out_shape=jax.ShapeDtypeStruct((T, D), BF16), compiler_params=_params(1),
    )(h, w)


def loss_head(h, target, w, *, name):
    T, D = h.shape
    tm = _pick(T, (512, 256, 128))
    n_i = T // tm
    cks = _chunks(D)

    def body(h_ref, t_ref, w_ref, dh_ref, loss_ref, dw_ref, accl_ref, accw_ref):
        i = pl.program_id(0)

        @pl.when(i == 0)
        def _():
            accl_ref[...] = jnp.zeros_like(accl_ref)
            accw_ref[...] = jnp.zeros_like(accw_ref)

        def strip(r):
            rstd = lax.rsqrt(_row_sum([jnp.square(h_ref[r, ck]) for ck in cks]) * (1.0 / D) + EPS)
            s2 = []
            for ck in cks:
                xhat = h_ref[r, ck] * rstd
                err = xhat * w_ref[:, ck] - t_ref[r, ck]
                accl_ref[:, ck] += _fold8(err * err)
                dy = err * (1.0 / D)
                accw_ref[:, ck] += _fold8(dy * xhat)
                s2.append(dy * w_ref[:, ck] * xhat)
            m2 = _row_sum(s2) * (1.0 / D)
            for ck in cks:
                xhat = h_ref[r, ck] * rstd
                dxh = (xhat * w_ref[:, ck] - t_ref[r, ck]) * (1.0 / D) * w_ref[:, ck]
                dh_ref[r, ck] = rstd * (dxh - xhat * m2)

        _strips(tm, strip)

        @pl.when(i == n_i - 1)
        def _():
            rows = jnp.sum(accl_ref[...], axis=-1, keepdims=True)
            loss_ref[...] = (0.5 / D) * jnp.sum(rows, axis=0, keepdims=True)
            dw_ref[...] = jnp.sum(accw_ref[...], axis=0, keepdims=True)

    row = pl.BlockSpec((tm, D), lambda i: (i, 0))
    vec = pl.BlockSpec((1, D), lambda i: (0, 0))
    return pl.pallas_call(
        body, name=name, grid=(n_i,), in_specs=[row, row, vec],
        out_specs=(row, pl.BlockSpec((1, 1), lambda i: (0, 0)), vec),
        out_shape=(jax.ShapeDtypeStruct((T, D), F32), jax.ShapeDtypeStruct((1, 1), F32),
                   jax.ShapeDtypeStruct((1, D), F32)),
        scratch_shapes=[pltpu.VMEM((8, D), F32), pltpu.VMEM((8, D), F32)],
        compiler_params=_params(1),
    )(h, target, w)


ANY = pl.BlockSpec(memory_space=pl.ANY)


def _place():
    return lax.axis_index("x"), lax.axis_index("y"), lax.axis_index("c")


def _gather_sems(n):
    return [pltpu.SemaphoreType.DMA((3 * n,))] * 4 + [pltpu.SemaphoreType.DMA((n,))]


def _gather_copies(rows, ins, outs, sems):
    ici_send, ici_recv, d2d_send, d2d_recv, local_sems = sems
    x, y, c = _place()
    k_me = 2 * x + y
    plan = []
    for t in range(len(rows)):
        half = rows[t] // 2
        mine = pl.ds(pl.multiple_of(c * half, 8), half)
        own = pltpu.make_async_copy(ins[t], outs[t].at[k_me], local_sems.at[t])
        sent, passed = [], []
        for j, (px, py) in enumerate([(1 - x, y), (x, 1 - y), (1 - x, 1 - y)]):
            landed = outs[t].at[2 * px + py, mine]
            sent.append(pltpu.make_async_remote_copy(
                src_ref=ins[t].at[mine], dst_ref=outs[t].at[k_me, mine], send_sem=ici_send.at[3 * t + j],
                recv_sem=ici_recv.at[3 * t + j], device_id=(px, py, c), device_id_type=MESH))
            passed.append(pltpu.make_async_remote_copy(
                src_ref=landed, dst_ref=landed, send_sem=d2d_send.at[3 * t + j], recv_sem=d2d_recv.at[3 * t + j],
                device_id=(x, y, 1 - c), device_id_type=MESH))
        plan.append((own, sent, passed))
    return plan


def _gather_start(rows, ins, outs, sems):
    for own, sent, _ in _gather_copies(rows, ins, outs, sems):
        own.start()
        for cp in sent:
            cp.start()


def _gather_finish(rows, ins, outs, sems):
    plan = _gather_copies(rows, ins, outs, sems)
    for _, sent, passed in plan:
        for cp, fwd in zip(sent, passed):
            cp.wait_recv()
            fwd.start()
    for own, sent, passed in plan:
        own.wait()
        for cp, fwd in zip(sent, passed):
            cp.wait_send()
            fwd.wait()


CONV_TM = 512
CONV_TC = 512
CONV_RB = 16


def _conv_specs(T, tm, sw, col0):
    hb = tm // HALO
    last = T // HALO - 1
    main = pl.BlockSpec((tm, sw), lambda j, i: (i, col0 + j))
    prev = pl.BlockSpec((HALO, sw), lambda j, i: (jnp.maximum(i * hb - 1, 0), col0 + j))
    nxt = pl.BlockSpec((HALO, sw), lambda j, i: (jnp.minimum((i + 1) * hb, last), col0 + j))
    return main, prev, nxt


def _conv_input(blk, glu, tc):
    x = blk.astype(F32)
    if glu:
        return x[:, :tc] * _sigmoid(x[:, tc:])
    return x


def _fill_padded(pad_ref, main, prev, nxt, first, last, tm):
    pad_ref[0:HALO, :] = jnp.where(first, 0.0, prev)
    pad_ref[HALO:HALO + tm, :] = main
    pad_ref[HALO + tm:HALO + tm + HALO, :] = jnp.where(last, 0.0, nxt)


SH_ROWS = 24


def _tap_plan(offsets):
    plan = [(o % 8, o - o % 8) for o in offsets]
    return plan, sorted({b for b, _ in plan if b})


def _fill_shifted(sh_ref, pad_ref, shifts, tm):
    for b in shifts:
        sh_ref[b] = pad_ref[b:b + tm + SH_ROWS, :]


def _tap_rows(pad_ref, sh_ref, b, start, rows):
    return pad_ref[start:start + rows, :] if b == 0 else sh_ref[b, start:start + rows, :]


def dwconv_fwd(src, w, b, *, width, glu, silu, col0, name, side=()):
    T = src.shape[0]
    C = w.shape[1]
    tm, tc = min(CONV_TM, T), CONV_TC
    sw = 2 * tc if glu else tc
    n_i = T // tm
    p = (width - 1) // 2
    rb = CONV_RB
    plan, shifts = _tap_plan([HALO - p + k for k in range(width)])

    n_side = len(side)
    side_rows = [t.shape[0] for t in side]

    def body(*refs):
        m_ref, p_ref, n_ref, w_ref, b_ref = refs[:5]
        side_in = refs[5:5 + n_side]
        o_ref = refs[5 + n_side]
        side_out = refs[6 + n_side:6 + 2 * n_side]
        pad_ref, sh_ref = refs[6 + 2 * n_side:8 + 2 * n_side]
        sems = refs[8 + 2 * n_side:]
        i = pl.program_id(1)
        j = pl.program_id(0)
        if n_side:
            @pl.when(jnp.logical_and(i == 0, j == 0))
            def _():
                _gather_start(side_rows, side_in, side_out, sems)

        _fill_padded(pad_ref, _conv_input(m_ref[...], glu, tc), _conv_input(p_ref[...], glu, tc),
                     _conv_input(n_ref[...], glu, tc), i == 0, i == n_i - 1, tm)
        _fill_shifted(sh_ref, pad_ref, shifts, tm)
        for r in range(tm // rb):
            acc = jnp.zeros((rb, tc), F32)
            for k, (sb, start) in enumerate(plan):
                acc = acc + _tap_rows(pad_ref, sh_ref, sb, start + r * rb, rb) * w_ref[k:k + 1, :]
            acc = acc + b_ref[...]
            if silu:
                acc = acc * _sigmoid(acc)
            o_ref[r * rb:(r + 1) * rb, :] = acc.astype(BF16)

        if n_side:
            @pl.when(jnp.logical_and(i == n_i - 1, j == C // tc - 1))
            def _():
                _gather_finish(side_rows, side_in, side_out, sems)

    main, prev, nxt = _conv_specs(T, tm, sw, col0)
    out = pl.pallas_call(
        body, name=name, grid=(C // tc, n_i),
        in_specs=[main, prev, nxt, pl.BlockSpec((w.shape[0], tc), lambda j, i: (0, j)),
                  pl.BlockSpec((1, tc), lambda j, i: (0, j))] + [ANY] * n_side,
        out_specs=tuple([pl.BlockSpec((tm, tc), lambda j, i: (i, j))] + [ANY] * n_side),
        out_shape=tuple([jax.ShapeDtypeStruct((T, C), BF16)]
                        + [jax.ShapeDtypeStruct((N_CHIPS,) + t.shape, t.dtype) for t in side]),
        scratch_shapes=[pltpu.VMEM((tm + 2 * HALO, tc), F32), pltpu.VMEM((8, tm + SH_ROWS, tc), F32)]
        + (_gather_sems(n_side) if n_side else []),
        compiler_params=_params(2),
    )(src, src, src, w, b, *side)
    return (out[0], list(out[1:])) if n_side else out[0]


def dwconv_bwd(dout, src, w, b, dsrc, *, width, glu, silu, col0, dcol0, name):
    T = src.shape[0]
    C = w.shape[1]
    kp = w.shape[0]
    tm, tc = min(CONV_TM, T), CONV_TC
    sw = 2 * tc if glu else tc
    n_i = T // tm
    p = (width - 1) // 2
    rb = CONV_RB
    edge = 8
    assert p <= edge or not silu
    plan, shifts = _tap_plan([HALO - p + k for k in range(width)])
    dplan, dshifts = _tap_plan([HALO + p - k for k in range(width)])

    def body(dm_ref, dp_ref, dn_ref, m_ref, p_ref, n_ref, w_ref, b_ref, _, o_ref, dw_ref, db_ref, pad_ref, dpre_ref,
             sh_ref, dsh_ref, acc_ref):
        i = pl.program_id(1)

        @pl.when(i == 0)
        def _():
            dw_ref[...] = jnp.zeros_like(dw_ref)
            acc_ref[...] = jnp.zeros_like(acc_ref)

        first, last = i == 0, i == n_i - 1
        _fill_padded(pad_ref, _conv_input(m_ref[...], glu, tc), _conv_input(p_ref[...], glu, tc),
                     _conv_input(n_ref[...], glu, tc), first, last, tm)
        _fill_padded(dpre_ref, dm_ref[...].astype(F32), dp_ref[...].astype(F32), dn_ref[...].astype(F32),
                     first, last, tm)
        _fill_shifted(sh_ref, pad_ref, shifts, tm)
        if silu:
            for r0 in range(HALO - edge, HALO + tm + edge, HALO):
                pre = jnp.zeros((HALO, tc), F32)
                for k, (sb, start) in enumerate(plan):
                    pre = pre + _tap_rows(pad_ref, sh_ref, sb, start + r0 - HALO, HALO) * w_ref[k:k + 1, :]
                pre = pre + b_ref[...]
                s = _sigmoid(pre)
                dpre_ref[r0:r0 + HALO, :] = dpre_ref[r0:r0 + HALO, :] * (s * (1.0 + pre * (1.0 - s)))
        _fill_shifted(dsh_ref, dpre_ref, dshifts, tm)

        for r in range(tm // rb):
            acc = jnp.zeros((rb, tc), F32)
            for k, (sb, start) in enumerate(dplan):
                acc = acc + _tap_rows(dpre_ref, dsh_ref, sb, start + r * rb, rb) * w_ref[k:k + 1, :]
            if glu:
                blk = m_ref[r * rb:(r + 1) * rb, :].astype(F32)
                v, s = blk[:, :tc], _sigmoid(blk[:, tc:])
                o_ref[r * rb:(r + 1) * rb, :tc] = (acc * s).astype(BF16)
                o_ref[r * rb:(r + 1) * rb, tc:] = (acc * v * s * (1.0 - s)).astype(BF16)
            else:
                o_ref[r * rb:(r + 1) * rb, :] = acc.astype(BF16)

        for r in range(tm // rb):
            dblk = dpre_ref[HALO + r * rb:HALO + (r + 1) * rb, :]
            for k, (sb, start) in enumerate(plan):
                prod = dblk * _tap_rows(pad_ref, sh_ref, sb, start + r * rb, rb)
                acc_ref[k] += jnp.sum(prod.reshape(rb // 8, 8, tc), axis=0)
            acc_ref[kp] += jnp.sum(dblk.reshape(rb // 8, 8, tc), axis=0)

        @pl.when(last)
        def _():
            for k in range(width):
                dw_ref[k:k + 1, :] = jnp.sum(acc_ref[k], axis=0, keepdims=True)
            db_ref[...] = jnp.sum(acc_ref[kp], axis=0, keepdims=True)

    dmain_s, dprev_s, dnext_s = _conv_specs(T, tm, tc, 0)
    main, prev, nxt = _conv_specs(T, tm, sw, col0)
    wspec = pl.BlockSpec((kp, tc), lambda j, i: (0, j))
    bspec = pl.BlockSpec((1, tc), lambda j, i: (0, j))
    return pl.pallas_call(
        body, name=name, grid=(C // tc, n_i),
        in_specs=[dmain_s, dprev_s, dnext_s, main, prev, nxt, wspec, bspec, pl.BlockSpec(memory_space=pl.ANY)],
        out_specs=(pl.BlockSpec((tm, sw), lambda j, i: (i, dcol0 + j)), wspec, bspec),
        out_shape=(jax.ShapeDtypeStruct(dsrc.shape, dsrc.dtype), jax.ShapeDtypeStruct((kp, C), F32),
                   jax.ShapeDtypeStruct((1, C), F32)),
        input_output_aliases={8: 0},
        scratch_shapes=[pltpu.VMEM((tm + 2 * HALO, tc), F32), pltpu.VMEM((tm + 2 * HALO, tc), F32),
                        pltpu.VMEM((8, tm + SH_ROWS, tc), F32), pltpu.VMEM((8, tm + SH_ROWS, tc), F32),
                        pltpu.VMEM((kp + 1, 8, tc), F32)],
        compiler_params=_params(2),
    )(dout, dout, dout, src, src, src, w, b, dsrc)


def _silu_grad(x, s):
    return s * (1.0 + x * (1.0 - s))


STRIP = 16
LCH = 512


def _strips(tm, fn):
    def step(s, carry):
        fn(pl.ds(pl.multiple_of(s * STRIP, STRIP), STRIP))
        return carry

    lax.fori_loop(0, tm // STRIP, step, 0, unroll=8)


def _chunks(e):
    return [slice(k, k + LCH) for k in range(0, e, LCH)]


def _row_sum(parts):
    acc = parts[0]
    for p in parts[1:]:
        acc = acc + p
    return jnp.sum(acc, axis=-1, keepdims=True)


def _fold8(x):
    return jnp.sum(x.reshape(STRIP // 8, 8, x.shape[-1]), axis=0)


def _ln_stats(u_ref, r, cks, e):
    mu = _row_sum([u_ref[r, ck].astype(F32) for ck in cks]) * (1.0 / e)
    var = _row_sum([jnp.square(u_ref[r, ck].astype(F32) - mu) for ck in cks]) * (1.0 / e)
    return mu, lax.rsqrt(var + EPS)


def conf_ln_fwd(u2, proj, ln_w, ln_b, *, name):
    T, E = u2.shape
    zc = proj.shape[1] // E - 1
    tm = _pick(T, (256, 128))
    cks = _chunks(E)

    def body(u_ref, z_ref, w_ref, b_ref, o_ref):
        def strip(r):
            mu, rstd = _ln_stats(u_ref, r, cks, E)
            for ck in cks:
                u3 = (u_ref[r, ck].astype(F32) - mu) * rstd * w_ref[:, ck] + b_ref[:, ck]
                z = z_ref[r, ck].astype(F32)
                o_ref[r, ck] = (u3 * _sigmoid(u3) * z * _sigmoid(z)).astype(BF16)

        _strips(tm, strip)

    row = pl.BlockSpec((tm, E), lambda i: (i, 0))
    vec = pl.BlockSpec((1, E), lambda i: (0, 0))
    return pl.pallas_call(
        body, name=name, grid=(T // tm,),
        in_specs=[row, pl.BlockSpec((tm, E), lambda i: (i, zc)), vec, vec], out_specs=row,
        out_shape=jax.ShapeDtypeStruct((T, E), BF16), compiler_params=_params(1),
    )(u2, proj, ln_w, ln_b)


def conf_ln_bwd(du4, u2, proj, ln_w, ln_b, *, name):
    T, E = u2.shape
    ncol = proj.shape[1] // E
    zc = ncol - 1
    tm = _pick(T, (256, 128))
    n_i = T // tm
    cks = _chunks(E)

    def body(d_ref, u_ref, z_ref, w_ref, b_ref, du_ref, dz_ref, dw_ref, db_ref, dxh_ref, accw_ref, accb_ref):
        i = pl.program_id(0)

        @pl.when(i == 0)
        def _():
            accw_ref[...] = jnp.zeros_like(accw_ref)
            accb_ref[...] = jnp.zeros_like(accb_ref)

        def strip(r):
            mu, rstd = _ln_stats(u_ref, r, cks, E)
            s1, s2 = [], []
            for ck in cks:
                xhat = (u_ref[r, ck].astype(F32) - mu) * rstd
                u3 = xhat * w_ref[:, ck] + b_ref[:, ck]
                z = z_ref[r, ck].astype(F32)
                s3, sz = _sigmoid(u3), _sigmoid(z)
                d4 = d_ref[r, ck].astype(F32)
                du3 = d4 * (z * sz) * _silu_grad(u3, s3)
                dz_ref[r, ck] = (d4 * (u3 * s3) * _silu_grad(z, sz)).astype(BF16)
                accw_ref[:, ck] += _fold8(du3 * xhat)
                accb_ref[:, ck] += _fold8(du3)
                dxh = du3 * w_ref[:, ck]
                dxh_ref[:, ck] = dxh
                s1.append(dxh)
                s2.append(dxh * xhat)
            m1, m2 = _row_sum(s1) * (1.0 / E), _row_sum(s2) * (1.0 / E)
            for ck in cks:
                xhat = (u_ref[r, ck].astype(F32) - mu) * rstd
                du_ref[r, ck] = (rstd * (dxh_ref[:, ck] - m1 - xhat * m2)).astype(BF16)

        _strips(tm, strip)

        @pl.when(i == n_i - 1)
        def _():
            dw_ref[...] = jnp.sum(accw_ref[...], axis=0, keepdims=True)
            db_ref[...] = jnp.sum(accb_ref[...], axis=0, keepdims=True)

    row = pl.BlockSpec((tm, E), lambda i: (i, 0))
    zrow = pl.BlockSpec((tm, E), lambda i: (i, zc))
    vec = pl.BlockSpec((1, E), lambda i: (0, 0))
    return pl.pallas_call(
        body, name=name, grid=(n_i,), in_specs=[row, row, zrow, vec, vec], out_specs=(row, zrow, vec, vec),
        out_shape=(jax.ShapeDtypeStruct((T, E), BF16), jax.ShapeDtypeStruct(proj.shape, BF16),
                   jax.ShapeDtypeStruct((1, E), F32), jax.ShapeDtypeStruct((1, E), F32)),
        scratch_shapes=[pltpu.VMEM((STRIP, E), F32), pltpu.VMEM((8, E), F32), pltpu.VMEM((8, E), F32)],
        compiler_params=_params(1),
    )(du4, u2, proj, ln_w, ln_b)


def _gated(y_ref, z_ref, r, ck):
    z = z_ref[r, ck].astype(F32)
    sz = _sigmoid(z)
    yv = y_ref[r, ck].astype(F32)
    return z, sz, yv, yv * (z * sz)


def ssd_gate_fwd(y, zx, norm_w, *, name):
    T, E = y.shape
    tm = _pick(T, (256, 128))
    cks = _chunks(E)

    def body(y_ref, z_ref, w_ref, o_ref, yz_ref):
        def strip(r):
            sq = []
            for ck in cks:
                yz = _gated(y_ref, z_ref, r, ck)[3]
                yz_ref[:, ck] = yz
                sq.append(yz * yz)
            rstd = lax.rsqrt(_row_sum(sq) * (1.0 / E) + EPS)
            for ck in cks:
                o_ref[r, ck] = (yz_ref[:, ck] * rstd * w_ref[:, ck]).astype(BF16)

        _strips(tm, strip)

    row = pl.BlockSpec((tm, E), lambda i: (i, 0))
    vec = pl.BlockSpec((1, E), lambda i: (0, 0))
    return pl.pallas_call(
        body, name=name, grid=(T // tm,), in_specs=[row, row, vec], out_specs=row,
        out_shape=jax.ShapeDtypeStruct((T, E), BF16), scratch_shapes=[pltpu.VMEM((STRIP, E), F32)],
        compiler_params=_params(1),
    )(y, zx, norm_w)


def ssd_gate_bwd(dyn, y, zx, norm_w, *, name):
    T, E = y.shape
    tm = _pick(T, (256, 128))
    n_i = T // tm
    cks = _chunks(E)

    def body(d_ref, y_ref, z_ref, w_ref, dy_ref, dz_ref, dw_ref, yz_ref, accw_ref):
        i = pl.program_id(0)

        @pl.when(i == 0)
        def _():
            accw_ref[...] = jnp.zeros_like(accw_ref)

        def strip(r):
            sq = []
            for ck in cks:
                yz = _gated(y_ref, z_ref, r, ck)[3]
                yz_ref[:, ck] = yz
                sq.append(yz * yz)
            rstd = lax.rsqrt(_row_sum(sq) * (1.0 / E) + EPS)
            s2 = []
            for ck in cks:
                yhat = yz_ref[:, ck] * rstd
                d = d_ref[r, ck].astype(F32)
                accw_ref[:, ck] += _fold8(d * yhat)
                s2.append(d * w_ref[:, ck] * yhat)
            m2 = _row_sum(s2) * (1.0 / E)
            for ck in cks:
                z, sz, yv, _ = _gated(y_ref, z_ref, r, ck)
                dyz = rstd * (d_ref[r, ck].astype(F32) * w_ref[:, ck] - yz_ref[:, ck] * rstd * m2)
                dy_ref[r, ck] = (dyz * (z * sz)).astype(BF16)
                dz_ref[r, ck] = (dyz * yv * _silu_grad(z, sz)).astype(BF16)

        _strips(tm, strip)

        @pl.when(i == n_i - 1)
        def _():
            dw_ref[...] = jnp.sum(accw_ref[...], axis=0, keepdims=True)

    row = pl.BlockSpec((tm, E), lambda i: (i, 0))
    vec = pl.BlockSpec((1, E), lambda i: (0, 0))
    return pl.pallas_call(
        body, name=name, grid=(n_i,), in_specs=[row, row, row, vec], out_specs=(row, row, vec),
        out_shape=(jax.ShapeDtypeStruct((T, E), BF16), jax.ShapeDtypeStruct(zx.shape, BF16),
                   jax.ShapeDtypeStruct((1, E), F32)),
        scratch_shapes=[pltpu.VMEM((STRIP, E), F32), pltpu.VMEM((8, E), F32)],
        compiler_params=_params(1),
    )(dyn, y, zx, norm_w)


def _cumsum_mm(mask, a):
    hi = a.astype(BF16)
    r1 = a - hi.astype(F32)
    mid = r1.astype(BF16)
    lo = (r1 - mid.astype(F32)).astype(BF16)
    out = _dot(jnp.where(mask, 1.0, 0.0).astype(BF16), jnp.concatenate([hi, mid, lo], axis=1))
    return out[:, :LANES] + out[:, LANES:2 * LANES] + out[:, 2 * LANES:]


def _chunk_terms(xcb, dt_raw, bias, alog, rev):
    L = CHUNK
    xs = xcb[:, :GW].astype(F32)
    Bm = xcb[:, GW:GW + D_STATE]
    Cm = xcb[:, GW + D_STATE:]
    pre = dt_raw + bias
    dt = _softplus(pre)
    A = -jnp.exp(alog)
    row = lax.broadcasted_iota(jnp.int32, (L, L), 0)
    col = lax.broadcasted_iota(jnp.int32, (L, L), 1)
    mask = (col >= row) if rev else (col <= row)
    mask_t = (col <= row) if rev else (col >= row)
    cs = _cumsum_mm(mask, dt * A)
    tot = cs[0:1, :] if rev else cs[L - 1:L, :]
    return xs, Bm, Cm, pre, dt, A, mask, mask_t, cs, cs.T, tot


def _decay(cs, cs_t, ln, mask):
    d = cs[:, ln:ln + 1] - cs_t[ln:ln + 1, :]
    return jnp.where(mask, jnp.exp(jnp.where(mask, d, 0.0)), 0.0)


def _pair(v, ln0, lo):
    return jnp.where(lo[:v.shape[0]], v[:, ln0:ln0 + 1], v[:, ln0 + 1:ln0 + 2])


def _scan_specs(nc, rev_order):
    ci = (lambda c: nc - 1 - c) if rev_order else (lambda c: c)
    xc = pl.BlockSpec((CHUNK, GROUPS * XCG), lambda c: (ci(c), 0))
    dt = pl.BlockSpec((CHUNK, GROUPS * LANES), lambda c: (ci(c), 0))
    vec = pl.BlockSpec((1, GROUPS * LANES), lambda c: (0, 0))
    wide = pl.BlockSpec((CHUNK, GROUPS * GW), lambda c: (ci(c), 0))
    wvec = pl.BlockSpec((1, GROUPS * GW), lambda c: (0, 0))
    st = pl.BlockSpec((1, D_STATE, GROUPS * GW), lambda c: (ci(c), 0, 0))
    return xc, dt, vec, wide, wvec, st


def _cols(ref, g, width):
    return ref.at[:, pl.ds(g * width, width)]


def _interleave(stages):
    for _ in itertools.zip_longest(*stages):
        pass


def _head_expand(r):
    row = lax.broadcasted_iota(jnp.int32, (LANES, GW), 0)
    col = lax.broadcasted_iota(jnp.int32, (LANES, GW), 1)
    first = (row - r * HPG) * HEADDIM
    return jnp.where(jnp.logical_and(col >= first, col < first + HEADDIM), 1.0, 0.0).astype(BF16)


def _head_collect(r):
    row = lax.broadcasted_iota(jnp.int32, (GW, LANES), 0)
    first = (lax.broadcasted_iota(jnp.int32, (GW, LANES), 1) - r * HPG) * HEADDIM
    return jnp.where(jnp.logical_and(row >= first, row < first + HEADDIM), 1.0, 0.0).astype(BF16)


def _expand(parts, sel):
    n = parts[0].shape[0]
    out = _dot(jnp.concatenate(parts, axis=0).astype(BF16), sel)
    return [out[i * n:(i + 1) * n] for i in range(len(parts))]


def ssd_scan_fwd(xc, dt4, bias4, alog4, *, rev, name, prev=None, dvec=None):
    T = xc.shape[0]
    nc = T // CHUNK
    E = GROUPS * GW
    r = 1 if rev else 0
    skip = prev is not None

    def one_group(sel, xc_ref, dt_ref, bias_ref, alog_ref, prev_ref, dvec_ref, y_ref, st_ref, s_ref):
        xs, Bm, Cm, _, dt, _, mask, _, cs, cs_t, tot = _chunk_terms(xc_ref[...], dt_ref[...], bias_ref[...],
                                                                   alog_ref[...], rev)
        yield
        dtx, ex, dx = _expand([dt, jnp.exp(cs), jnp.exp(tot - cs)], sel)
        et = jnp.exp(tot)
        cb = _dot_nt(Cm, Bm)
        yield
        sb = s_ref[...].astype(BF16)
        st_ref[...] = sb
        xp_all = xs * dtx
        y_off = _dot(Cm, sb) * ex
        lo = lax.broadcasted_iota(jnp.int32, (CHUNK, LANES), 1) < HEADDIM
        et_parts = []
        for p in range(HPG // 2):
            yield
            ln0 = r * HPG + 2 * p
            sl = slice(p * LANES, (p + 1) * LANES)
            xp = xp_all[:, sl]
            mcat = jnp.concatenate([cb * _decay(cs, cs_t, ln0, mask), cb * _decay(cs, cs_t, ln0 + 1, mask)],
                                   axis=1).astype(BF16)
            xbd = jnp.concatenate([jnp.where(lo, xp, 0.0), jnp.where(lo, 0.0, xp)], axis=0).astype(BF16)
            yp = _dot(mcat, xbd) + y_off[:, sl]
            if skip:
                yp = yp + prev_ref[:, sl].astype(F32) + xs[:, sl] * dvec_ref[:, sl]
            y_ref[:, sl] = yp.astype(BF16)
            et_parts.append(_pair(et, ln0, lo))
        yield
        s_ref[...] = s_ref[...] * jnp.concatenate(et_parts, axis=1) + _dot_tn(Bm, (xp_all * dx).astype(BF16))

    def body(*refs):
        xc_ref, dt_ref, bias_ref, alog_ref = refs[:4]
        prev_ref, dvec_ref = (refs[4], refs[5]) if skip else (None, None)
        y_ref, st_ref, s_ref = refs[-3:]

        @pl.when(pl.program_id(0) == 0)
        def _():
            s_ref[...] = jnp.zeros_like(s_ref)

        sel = _head_expand(r)
        _interleave([
            one_group(sel, _cols(xc_ref, g, XCG), _cols(dt_ref, g, LANES), _cols(bias_ref, g, LANES),
                      _cols(alog_ref, g, LANES), _cols(prev_ref, g, GW) if skip else None,
                      _cols(dvec_ref, g, GW) if skip else None, _cols(y_ref, g, GW),
                      st_ref.at[0, :, pl.ds(g * GW, GW)], _cols(s_ref, g, GW)) for g in range(GROUPS)])

    s_xc, s_dt, s_vec, s_wide, s_wvec, s_st = _scan_specs(nc, rev)
    in_specs = [s_xc, s_dt, s_vec, s_vec]
    args = [xc, dt4, bias4, alog4]
    if skip:
        in_specs += [s_wide, s_wvec]
        args += [prev, dvec]
    return pl.pallas_call(
        body, name=name, grid=(nc,), in_specs=in_specs, out_specs=(s_wide, s_st),
        out_shape=(jax.ShapeDtypeStruct((T, E), BF16), jax.ShapeDtypeStruct((nc, D_STATE, E), BF16)),
        scratch_shapes=[pltpu.VMEM((D_STATE, E), F32)], compiler_params=_params(1),
    )(*args)


def ssd_scan_bwd(xc, dt4, bias4, alog4, dy, states, *, rev, name, prev=None, dvec=None):
    T = xc.shape[0]
    nc = T // CHUNK
    E = GROUPS * GW
    L = CHUNK
    r = 1 if rev else 0
    skip = prev is not None

    def one_group(sel, sel_t, xc_ref, dt_ref, bias_ref, alog_ref, dy_ref, st_ref, pdxc_ref, pddt_ref, dvec_ref,
                  dxc_ref, ddt_ref, dalog_ref, dbias_ref, dd_ref, g_ref):
        xs, Bm, Cm, pre, dt, A, mask, mask_t, cs, cs_t, tot = _chunk_terms(
            xc_ref[...], dt_ref[...], bias_ref[...], alog_ref[...], rev)
        yield
        dtx, ex, dx = _expand([dt, jnp.exp(cs), jnp.exp(tot - cs)], sel)
        et = jnp.exp(tot)
        cb = _dot_nt(Cm, Bm)
        yield
        s_in = st_ref[...]
        dy_all = dy_ref[...].astype(F32)
        g_f = g_ref[...]
        g_b = g_f.astype(BF16)
        xp_all = xs * dtx
        dye_all = dy_all * ex
        bgd = _dot(Bm, g_b) * dx
        lane = lax.broadcasted_iota(jnp.int32, (L, LANES), 1)
        lo = lane < HEADDIM
        dcb = jnp.zeros((L, L), F32)
        yd_parts, dxd_parts, et_parts = [], [], []
        for p in range(HPG // 2):
            yield
            ln0 = r * HPG + 2 * p
            sl = slice(p * LANES, (p + 1) * LANES)
            xp, dy_p = xp_all[:, sl], dy_all[:, sl]
            lam0, lam1 = _decay(cs, cs_t, ln0, mask), _decay(cs, cs_t, ln0 + 1, mask)
            m0, m1 = (cb * lam0).astype(BF16), (cb * lam1).astype(BF16)
            dybd = jnp.concatenate([jnp.where(lo, dy_p, 0.0), jnp.where(lo, 0.0, dy_p)], axis=0).astype(BF16)
            xbd = jnp.concatenate([jnp.where(lo, xp, 0.0), jnp.where(lo, 0.0, xp)], axis=0).astype(BF16)
            dm = _dot_nt(dybd, xp.astype(BF16))
            dcb = dcb + dm[:L] * lam0 + dm[L:] * lam1
            yd_parts.append(_dot(jnp.concatenate([m0, m1], axis=1), xbd))
            dxd_parts.append(_dot_tn(jnp.concatenate([m0, m1], axis=0), dybd))
            et_parts.append(_pair(et, ln0, lo))
        yield
        y_diag = jnp.concatenate(yd_parts, axis=1)
        dx_diag = jnp.concatenate(dxd_parts, axis=1)
        etx = jnp.concatenate(et_parts, axis=1)
        dxt = dx_diag + bgd
        w2 = xp_all * bgd
        dy_r, xp_r = dy_all.astype(BF16).astype(F32), xp_all.astype(BF16).astype(F32)
        u = dye_all * _dot(Cm, s_in) + dy_r * y_diag - xp_r * dx_diag - w2
        dxx = dxt * xs
        tail = jnp.broadcast_to(jnp.sum(w2, axis=0, keepdims=True)
                                + jnp.sum(g_f * s_in.astype(F32), axis=0, keepdims=True) * etx, (8, GW))
        u_hi, t_hi = u.astype(BF16), tail.astype(BF16)
        red = _dot(jnp.concatenate([u_hi, (u - u_hi.astype(F32)).astype(BF16), dxx.astype(BF16), t_hi,
                                    (tail - t_hi.astype(F32)).astype(BF16)], axis=0), sel_t)
        yield
        dcs = red[:L] + red[L:2 * L]
        ddt = red[2 * L:3 * L]
        dtot = red[3 * L:3 * L + 1] + red[3 * L + 8:3 * L + 9]
        dxs = dxt * dtx
        if skip:
            dxs = dxs + dy_all * dvec_ref[...] + pdxc_ref[:, :GW].astype(F32)
            dd_ref[...] += jnp.sum(dy_all * xs, axis=0, keepdims=True)
        dxc_ref[:, :GW] = dxs.astype(BF16)
        dye_b = dye_all.astype(BF16)
        xd = (xp_all * dx).astype(BF16)
        dcb_b = dcb.astype(BF16)
        d_b = _dot_nt(xd, g_b) + _dot_tn(dcb_b, Cm)
        d_c = _dot_nt(dye_b, s_in) + _dot(dcb_b, Bm)
        if skip:
            d_b = d_b + pdxc_ref[:, GW:GW + D_STATE].astype(F32)
            d_c = d_c + pdxc_ref[:, GW + D_STATE:].astype(F32)
        dxc_ref[:, GW:GW + D_STATE] = d_b.astype(BF16)
        dxc_ref[:, GW + D_STATE:] = d_c.astype(BF16)
        yield
        g_ref[...] = g_f * etx + _dot_tn(Cm, dye_b)
        rowi = lax.broadcasted_iota(jnp.int32, (L, LANES), 0)
        da = _cumsum_mm(mask_t, dcs + jnp.where(rowi == (0 if rev else L - 1), dtot, 0.0))
        keep = jnp.logical_and(lane >= r * HPG, lane < (r + 1) * HPG)
        ddr = jnp.where(keep, (da * A + ddt) * _sigmoid(pre), 0.0)
        dbias_ref[...] += jnp.sum(ddr, axis=0, keepdims=True)
        dalog_ref[...] += jnp.sum(jnp.where(keep, da * dt * A, 0.0), axis=0, keepdims=True)
        if skip:
            ddr = ddr + pddt_ref[...]
        ddt_ref[...] = ddr

    def body(*refs):
        xc_ref, dt_ref, bias_ref, alog_ref, dy_ref, st_ref = refs[:6]
        pdxc_ref, pddt_ref, dvec_ref = refs[6:9] if skip else (None, None, None)
        pos = 9 if skip else 6
        dxc_ref, ddt_ref, dalog_ref, dbias_ref = refs[pos:pos + 4]
        dd_ref = refs[pos + 4] if skip else None
        g_ref = refs[-1]

        @pl.when(pl.program_id(0) == 0)
        def _():
            g_ref[...] = jnp.zeros_like(g_ref)
            dalog_ref[...] = jnp.zeros_like(dalog_ref)
            dbias_ref[...] = jnp.zeros_like(dbias_ref)
            if skip:
                dd_ref[...] = jnp.zeros_like(dd_ref)

        sel, sel_t = _head_expand(r), _head_collect(r)
        _interleave([
            one_group(sel, sel_t, _cols(xc_ref, g, XCG), _cols(dt_ref, g, LANES), _cols(bias_ref, g, LANES),
                      _cols(alog_ref, g, LANES), _cols(dy_ref, g, GW), st_ref.at[0, :, pl.ds(g * GW, GW)],
                      _cols(pdxc_ref, g, XCG) if skip else None, _cols(pddt_ref, g, LANES) if skip else None,
                      _cols(dvec_ref, g, GW) if skip else None, _cols(dxc_ref, g, XCG), _cols(ddt_ref, g, LANES),
                      _cols(dalog_ref, g, LANES), _cols(dbias_ref, g, LANES),
                      _cols(dd_ref, g, GW) if skip else None, _cols(g_ref, g, GW)) for g in range(GROUPS)])

    s_xc, s_dt, s_vec, s_wide, s_wvec, s_st = _scan_specs(nc, not rev)
    in_specs = [s_xc, s_dt, s_vec, s_vec, s_wide, s_st]
    args = [xc, dt4, bias4, alog4, dy, states]
    out_specs = [s_xc, s_dt, s_vec, s_vec]
    out_shape = [jax.ShapeDtypeStruct((T, GROUPS * XCG), BF16), jax.ShapeDtypeStruct((T, GROUPS * LANES), F32),
                 jax.ShapeDtypeStruct((1, GROUPS * LANES), F32), jax.ShapeDtypeStruct((1, GROUPS * LANES), F32)]
    if skip:
        in_specs += [s_xc, s_dt, s_wvec]
        args += [prev[0], prev[1], dvec]
        out_specs.append(s_wvec)
        out_shape.append(jax.ShapeDtypeStruct((1, E), F32))
    return pl.pallas_call(
        body, name=name, grid=(nc,), in_specs=in_specs, out_specs=tuple(out_specs),
        out_shape=tuple(out_shape), scratch_shapes=[pltpu.VMEM((D_STATE, E), F32)], compiler_params=_params(1),
    )(*args)


def _conf_cols(w):
    e = w.shape[-1] // 3
    lead = w.shape[:-1]
    vg = w[..., :2 * e].reshape(*lead, 2, e // CONV_TC, CONV_TC)
    vg = jnp.swapaxes(vg, -3, -2).reshape(*lead, 2 * e)
    return jnp.concatenate([vg, w[..., 2 * e:]], axis=-1)


def _conf_cols_inv(w):
    e = w.shape[-1] // 3
    lead = w.shape[:-1]
    vg = w[..., :2 * e].reshape(*lead, e // CONV_TC, 2, CONV_TC)
    vg = jnp.swapaxes(vg, -3, -2).reshape(*lead, 2 * e)
    return jnp.concatenate([vg, w[..., 2 * e:]], axis=-1)


def _xbc_cols(w):
    lead = w.shape[:-1]
    e = GROUPS * GW
    gn = GROUPS * D_STATE
    parts = [w[..., :e].reshape(*lead, GROUPS, GW), w[..., e:e + gn].reshape(*lead, GROUPS, D_STATE),
             w[..., e + gn:].reshape(*lead, GROUPS, D_STATE)]
    return jnp.concatenate(parts, axis=-1).reshape(*lead, GROUPS * XCG)


def _xbc_cols_inv(w):
    lead = w.shape[:-1]
    g = w.reshape(*lead, GROUPS, XCG)
    parts = [g[..., :GW].reshape(*lead, GROUPS * GW), g[..., GW:GW + D_STATE].reshape(*lead, GROUPS * D_STATE),
             g[..., GW + D_STATE:].reshape(*lead, GROUPS * D_STATE)]
    return jnp.concatenate(parts, axis=-1)


def _dt_cols(w):
    lead = w.shape[:-1]
    t = jnp.swapaxes(w.reshape(*lead, 2, GROUPS, HPG), -3, -2).reshape(*lead, GROUPS, 2 * HPG)
    pad = [(0, 0)] * (t.ndim - 1) + [(0, LANES - 2 * HPG)]
    return jnp.pad(t, pad).reshape(*lead, GROUPS * LANES)


def _dt_cols_inv(w):
    lead = w.shape[:-1]
    t = w.reshape(*lead, GROUPS, LANES)[..., :2 * HPG].reshape(*lead, GROUPS, 2, HPG)
    return jnp.swapaxes(t, -3, -2).reshape(*lead, 2 * HEADS)


def _pad_rows(w, rows):
    return jnp.pad(w, ((0, rows - w.shape[0]), (0, 0)))


def conf_weights(w_in, dw_w, dw_b, ln_w, ln_b):
    w_in_p = _conf_cols(w_in)
    return dict(w_in=w_in_p, dw_w=_pad_rows(dw_w, 32), dw_b=dw_b.reshape(1, -1),
                ln_w=ln_w.reshape(1, -1), ln_b=ln_b.reshape(1, -1))


def _xbc_rows(w):
    e, gn, c = GROUPS * GW, GROUPS * D_STATE, w.shape[1]
    parts = [w[:e].reshape(GROUPS, GW, c), w[e:e + gn].reshape(GROUPS, D_STATE, c),
             w[e + gn:].reshape(GROUPS, D_STATE, c)]
    return jnp.concatenate(parts, axis=1).reshape(GROUPS * XCG, c)


def _xbc_rows_inv(w):
    c = w.shape[1]
    g = w.reshape(GROUPS, XCG, c)
    parts = [g[:, :GW].reshape(GROUPS * GW, c), g[:, GW:GW + D_STATE].reshape(GROUPS * D_STATE, c),
             g[:, GW + D_STATE:].reshape(GROUPS * D_STATE, c)]
    return jnp.concatenate(parts, axis=0)


def _dt_rows(w):
    c = w.shape[1]
    t = jnp.swapaxes(w.reshape(2, GROUPS, HPG, c), 0, 1).reshape(GROUPS, 2 * HPG, c)
    return jnp.pad(t, ((0, 0), (0, LANES - 2 * HPG), (0, 0))).reshape(GROUPS * LANES, c)


def _dt_rows_inv(w):
    c = w.shape[1]
    t = w.reshape(GROUPS, LANES, c)[:, :2 * HPG].reshape(GROUPS, 2, HPG, c)
    return jnp.swapaxes(t, 0, 1).reshape(2 * HEADS, c)


def ssd_weights(w_in_t, conv_w, conv_b, dt_bias, a_log, d_skip, norm_w, w_out):
    e = GROUPS * GW
    xbc = e + 2 * GROUPS * D_STATE
    w_zx_t = jnp.concatenate([w_in_t[:e], _xbc_rows(w_in_t[e:e + xbc])], axis=0)
    return dict(w_zx_t=w_zx_t, w_dt_t=_dt_rows(w_in_t[e + xbc:]), w_out=w_out,
                conv_w=_pad_rows(_xbc_cols(conv_w), 8), conv_b=_xbc_cols(conv_b.reshape(1, -1)),
                bias4=_dt_cols(dt_bias.reshape(1, -1)), alog4=_dt_cols(a_log.reshape(1, -1)),
                dvec=jnp.repeat(d_skip, HEADDIM).reshape(1, -1), norm_w=norm_w.reshape(1, -1))


def conf_layer_fwd(h, nw, p, tag, side=(), w_out=None):
    hn = rmsnorm_fwd(h, nw, name=f"{tag}_norm")
    proj = mm_nn(hn, p["w_in"], out_dtype=BF16, name=f"{tag}_proj")
    u2 = dwconv_fwd(proj, p["dw_w"], p["dw_b"], width=31, glu=True, silu=False, col0=0, name=f"{tag}_conv", side=side)
    if side:
        u2, gathered = u2
        w_out = w_out(gathered)
    p.update(w_out=w_out)
    u4 = conf_ln_fwd(u2, proj, p["ln_w"], p["ln_b"], name=f"{tag}_ln")
    h2 = mm_nn(u4, p["w_out"], out_dtype=F32, res=h, name=f"{tag}_out")
    return h2, (h, hn, proj, u2, u4)


def conf_layer_bwd(dh, saved, nw, p, tag):
    h, hn, proj, u2, u4 = saved
    du4 = mm_nn(dh, p["w_out"], out_dtype=BF16, b_rows_are_n=True, name=f"{tag}_d_u4")
    dw_out = mm_tn(u4, dh, name=f"{tag}_dw_out")
    du2, dproj, dln_w, dln_b = conf_ln_bwd(du4, u2, proj, p["ln_w"], p["ln_b"], name=f"{tag}_d_ln")
    dproj, ddw_w, ddw_b = dwconv_bwd(du2, proj, p["dw_w"], p["dw_b"], dproj, width=31, glu=True, silu=False,
                                     col0=0, dcol0=0, name=f"{tag}_d_conv")
    dh_prev, dnw = mm_nn(dproj, p["w_in"], out_dtype=F32, b_rows_are_n=True, norm_bwd=(h, nw, dh),
                         name=f"{tag}_d_hn")
    dw_in = mm_tn(hn, dproj, name=f"{tag}_dw_in")
    grads = dict(w_in=_conf_cols_inv(dw_in), dw_w=ddw_w[:31], dw_b=ddw_b[0], ln_w=dln_w[0], ln_b=dln_b[0],
                 w_out=dw_out, norm=dnw[0])
    return dh_prev, grads


def ssd_layer_fwd(h, nw, p, tag):
    e = GROUPS * GW
    hn = rmsnorm_fwd(h, nw, name=f"{tag}_norm")
    zx = mm_nn(hn, p["w_zx_t"], out_dtype=BF16, b_rows_are_n=True, name=f"{tag}_proj")
    dt4 = mm_nn(hn, p["w_dt_t"], out_dtype=F32, b_rows_are_n=True, name=f"{tag}_proj_dt")
    xc = dwconv_fwd(zx, p["conv_w"], p["conv_b"], width=5, glu=False, silu=True, col0=e // CONV_TC, name=f"{tag}_conv")
    y0, st0 = ssd_scan_fwd(xc, dt4, p["bias4"], p["alog4"], rev=False, name=f"{tag}_scan_f")
    y, st1 = ssd_scan_fwd(xc, dt4, p["bias4"], p["alog4"], rev=True, prev=y0, dvec=p["dvec"], name=f"{tag}_scan_b")
    yn = ssd_gate_fwd(y, zx, p["norm_w"], name=f"{tag}_gate")
    h2 = mm_nn(yn, p["w_out"], out_dtype=F32, res=h, name=f"{tag}_out")
    return h2, (h, hn, zx, dt4, xc, st0, st1, y, yn)


def ssd_layer_bwd(dh, saved, nw, p, tag):
    e = GROUPS * GW
    h, hn, zx, dt4, xc, st0, st1, y, yn = saved
    dyn = mm_nn(dh, p["w_out"], out_dtype=BF16, b_rows_are_n=True, name=f"{tag}_d_yn")
    dw_out = mm_tn(yn, dh, name=f"{tag}_dw_out")
    dy, dzx, dnorm_w = ssd_gate_bwd(dyn, y, zx, p["norm_w"], name=f"{tag}_d_gate")
    dxc0, ddt0, dalog0, dbias0 = ssd_scan_bwd(xc, dt4, p["bias4"], p["alog4"], dy, st0, rev=False,
                                              name=f"{tag}_d_scan_f")
    dxc, ddt4, dalog1, dbias1, ddvec = ssd_scan_bwd(xc, dt4, p["bias4"], p["alog4"], dy, st1, rev=True,
                                                    prev=(dxc0, ddt0), dvec=p["dvec"], name=f"{tag}_d_scan_b")
    dzx, dconv_w, dconv_b = dwconv_bwd(dxc, zx, p["conv_w"], p["conv_b"], dzx, width=5, glu=False, silu=True,
                                       col0=e // CONV_TC, dcol0=e // CONV_TC, name=f"{tag}_d_conv")
    dh_prev, dnw = mm_nn(dzx, p["w_zx_t"], out_dtype=F32, a2=ddt4, b2=p["w_dt_t"], norm_bwd=(h, nw, dh),
                         name=f"{tag}_d_hn")
    dw_zx_t = mm_tn(dzx, hn, name=f"{tag}_dw_zx")
    dw_dt_t = mm_tn(ddt4, hn, name=f"{tag}_dw_dt")
    dw_in_t = jnp.concatenate([dw_zx_t[:e], _xbc_rows_inv(dw_zx_t[e:]), _dt_rows_inv(dw_dt_t)], axis=0)
    grads = dict(w_in_t=dw_in_t, conv_w=_xbc_cols_inv(dconv_w[:5]), conv_b=_xbc_cols_inv(dconv_b)[0],
                 dt_bias=_dt_cols_inv(dbias0 + dbias1).reshape(2, HEADS),
                 a_log=_dt_cols_inv(dalog0 + dalog1).reshape(2, HEADS),
                 d_skip=jnp.sum(ddvec.reshape(HEADS, HEADDIM), axis=-1), norm_w=dnorm_w[0], w_out=dw_out,
                 norm=dnw[0])
    return dh_prev, grads


def gather_chips(bufs, *, name):
    n = len(bufs)
    rows = [b.shape[0] for b in bufs]

    def body(*refs):
        ins, outs, sems = refs[:n], refs[n:2 * n], refs[2 * n:]
        _gather_start(rows, ins, outs, sems)
        _gather_finish(rows, ins, outs, sems)

    return pl.pallas_call(
        body, name=name, in_specs=[ANY] * n, out_specs=tuple([ANY] * n),
        out_shape=tuple(jax.ShapeDtypeStruct((N_CHIPS,) + b.shape, b.dtype) for b in bufs),
        scratch_shapes=_gather_sems(n),
    )(*bufs)


def swap_other_half(g2, *, name):
    def body(g_ref, o_ref, send_sem, recv_sem):
        x, y, c = _place()
        cp = pltpu.make_async_remote_copy(src_ref=g_ref.at[1 - c], dst_ref=o_ref, send_sem=send_sem, recv_sem=recv_sem,
                                          device_id=(x, y, 1 - c), device_id_type=MESH)
        cp.start()
        cp.wait()

    return pl.pallas_call(
        body, name=name, in_specs=[ANY], out_specs=ANY, out_shape=jax.ShapeDtypeStruct(g2.shape[1:], g2.dtype),
        scratch_shapes=[pltpu.SemaphoreType.DMA, pltpu.SemaphoreType.DMA],
    )(g2)


def exchange_chips(p, *, name):
    def body(p_ref, o_ref, send_sems, recv_sems, local_sem):
        x, y, c = _place()
        k_me = 2 * x + y
        own = pltpu.make_async_copy(p_ref.at[k_me], o_ref.at[k_me], local_sem)
        own.start()
        copies = [own]
        for j, (px, py) in enumerate([(1 - x, y), (x, 1 - y), (1 - x, 1 - y)]):
            cp = pltpu.make_async_remote_copy(
                src_ref=p_ref.at[2 * px + py], dst_ref=o_ref.at[k_me], send_sem=send_sems.at[j],
                recv_sem=recv_sems.at[j], device_id=(px, py, c), device_id_type=MESH)
            cp.start()
            copies.append(cp)
        for cp in copies:
            cp.wait()

    return pl.pallas_call(
        body, name=name, in_specs=[ANY], out_specs=ANY, out_shape=jax.ShapeDtypeStruct(p.shape, p.dtype),
        scratch_shapes=[pltpu.SemaphoreType.DMA((3,)), pltpu.SemaphoreType.DMA((3,)), pltpu.SemaphoreType.DMA],
    )(p)


def share_half(full, *, name):
    def body(_, f_ref, send_sem, recv_sem):
        x, y, c = _place()
        cp = pltpu.make_async_remote_copy(src_ref=f_ref.at[c], dst_ref=f_ref.at[c], send_sem=send_sem,
                                          recv_sem=recv_sem, device_id=(x, y, 1 - c), device_id_type=MESH)
        cp.start()
        cp.wait()

    return pl.pallas_call(
        body, name=name, in_specs=[ANY], out_specs=ANY, out_shape=jax.ShapeDtypeStruct(full.shape, full.dtype),
        input_output_aliases={0: 0},
        scratch_shapes=[pltpu.SemaphoreType.DMA, pltpu.SemaphoreType.DMA],
    )(full)


def gather_all(v, *, name):
    def body(v_ref, o_ref, send_sems, recv_sems, local_sem):
        x, y, c = _place()
        me = 4 * x + 2 * y + c
        own = pltpu.make_async_copy(v_ref, o_ref.at[me], local_sem)
        own.start()
        copies = [own]
        idx = 0
        for fx in (0, 1):
            for fy in (0, 1):
                for fc in (0, 1):
                    if not (fx or fy or fc):
                        continue
                    peer = (1 - x if fx else x, 1 - y if fy else y, 1 - c if fc else c)
                    cp = pltpu.make_async_remote_copy(src_ref=v_ref, dst_ref=o_ref.at[me], send_sem=send_sems.at[idx],
                                                      recv_sem=recv_sems.at[idx], device_id=peer, device_id_type=MESH)
                    cp.start()
                    copies.append(cp)
                    idx += 1
        for cp in copies:
            cp.wait()

    return pl.pallas_call(
        body, name=name, in_specs=[ANY], out_specs=ANY, out_shape=jax.ShapeDtypeStruct((N_DEV,) + v.shape, v.dtype),
        scratch_shapes=[pltpu.SemaphoreType.DMA((N_DEV - 1,)), pltpu.SemaphoreType.DMA((N_DEV - 1,)),
                        pltpu.SemaphoreType.DMA],
    )(v)


RED_TR = 432


def pair_sum(g2, recv, cidx, *, name):
    _, K, R, C = g2.shape
    tr = _pick(R, (RED_TR, 8))

    def body(c_ref, a_ref, b_ref, o_ref):
        o_ref[...] = (a_ref[0] + b_ref[...]).astype(BF16)

    blk = pl.BlockSpec((1, tr, C), lambda k, i, c: (k, i, 0))
    return pl.pallas_call(
        body, name=name,
        grid_spec=pltpu.PrefetchScalarGridSpec(
            num_scalar_prefetch=1, grid=(K, R // tr),
            in_specs=[pl.BlockSpec((1, 1, tr, C), lambda k, i, c: (c[0], k, i, 0)), blk], out_specs=blk),
        out_shape=jax.ShapeDtypeStruct((K, R, C), BF16), compiler_params=_params(2),
    )(cidx, g2, recv)


def sum_lead(a, *, name, slot=None, nslots=1):
    K, R, C = a.shape
    tr = _pick(R, (RED_TR, 8))

    def body(s_ref, a_ref, o_ref):
        acc = a_ref[0].astype(F32)
        for k in range(1, K):
            acc = acc + a_ref[k].astype(F32)
        o_ref[0] = acc

    if slot is None:
        slot = jnp.zeros((1,), jnp.int32)
    return pl.pallas_call(
        body, name=name,
        grid_spec=pltpu.PrefetchScalarGridSpec(
            num_scalar_prefetch=1, grid=(R // tr,),
            in_specs=[pl.BlockSpec((K, tr, C), lambda i, s: (0, i, 0))],
            out_specs=pl.BlockSpec((1, tr, C), lambda i, s: (s[0], i, 0))),
        out_shape=jax.ShapeDtypeStruct((nslots, R, C), F32), compiler_params=_params(1),
    )(slot, a)


def adamw(g, w, m, v, *, name):
    R, C = w.shape
    tr = _pick(R, (256, 128, 64, 32, 16, 8))

    def body(g_ref, w_ref, m_ref, v_ref, d_ref, nm_ref, nv_ref):
        gv = g_ref[...]
        m_new = ADAM_B1 * m_ref[...] + (1.0 - ADAM_B1) * gv
        v_new = ADAM_B2 * v_ref[...] + (1.0 - ADAM_B2) * (gv * gv)
        m_hat = m_new / (1.0 - ADAM_B1 ** ADAM_STEP)
        v_hat = v_new / (1.0 - ADAM_B2 ** ADAM_STEP)
        d_ref[...] = -ADAM_LR * (m_hat / (jnp.sqrt(v_hat) + ADAM_EPS) + ADAM_WD * w_ref[...])
        nm_ref[...] = m_new
        nv_ref[...] = v_new

    blk = pl.BlockSpec((tr, C), lambda i: (i, 0))
    sds = jax.ShapeDtypeStruct((R, C), F32)
    return pl.pallas_call(
        body, name=name, grid=(R // tr,), in_specs=[blk] * 4, out_specs=(blk,) * 3, out_shape=(sds,) * 3,
        compiler_params=_params(1),
    )(g, w, m, v)


WEIGHTS = ("norm_w", "final_norm_w", "cm_w_in", "cm_dw_w", "cm_dw_b", "cm_ln_w", "cm_ln_b", "cm_w_out", "ssd_w_in",
           "ssd_conv_w", "ssd_conv_b", "ssd_dt_bias", "ssd_A_log", "ssd_D", "ssd_norm_w", "ssd_w_out")
BIG = (("cm_w_in", 2), ("cm_w_out", 1), ("ssd_w_in", 1), ("ssd_w_out", 1))
TRANSPOSED = ("ssd_w_in",)
SMALL_SHARDED = (("cm_dw_w", 2), ("ssd_conv_w", 2), ("ssd_conv_b", 1), ("ssd_norm_w", 1))
REPLICATED = ("norm_w", "final_norm_w", "cm_dw_b", "cm_ln_w", "cm_ln_b", "ssd_dt_bias", "ssd_A_log", "ssd_D")
ROW = 1024


def _to_shards(g, axis):
    n = g.shape[axis]
    s = g.reshape(g.shape[:axis] + (N_CHIPS, n // N_CHIPS) + g.shape[axis + 1:])
    return jnp.moveaxis(s, axis, 0).reshape(N_CHIPS, -1)


def _from_shards(x4, local_shape, axis):
    local_shape = tuple(local_shape)
    s = jnp.moveaxis(x4.reshape((N_CHIPS,) + local_shape), 0, axis)
    return s.reshape(local_shape[:axis] + (N_CHIPS * local_shape[axis],) + local_shape[axis + 1:])


def _flat_pad(parts, multiple):
    n = sum(p.size for p in parts)
    fill = [jnp.zeros(((-n) % multiple,), parts[0].dtype)] if n % multiple else []
    return jnp.concatenate([p.reshape(-1) for p in parts] + fill)


def _split(flat, like, names):
    out, off = {}, 0
    for n in names:
        out[n] = flat[off:off + like[n].size].reshape(like[n].shape)
        off += like[n].size
    return out


def kernel(x, norm_w, final_norm_w, cm_w_in, cm_dw_w, cm_dw_b, cm_ln_w, cm_ln_b, cm_w_out, ssd_w_in, ssd_conv_w, ssd_conv_b, ssd_dt_bias, ssd_A_log, ssd_D, ssd_norm_w, ssd_w_out, loss_target, m_norm_w, m_final_norm_w, m_cm_w_in, m_cm_dw_w, m_cm_dw_b, m_cm_ln_w, m_cm_ln_b, m_cm_w_out, m_ssd_w_in, m_ssd_conv_w, m_ssd_conv_b, m_ssd_dt_bias, m_ssd_A_log, m_ssd_D, m_ssd_norm_w, m_ssd_w_out, v_norm_w, v_final_norm_w, v_cm_w_in, v_cm_dw_w, v_cm_dw_b, v_cm_ln_w, v_cm_ln_b, v_cm_w_out, v_ssd_w_in, v_ssd_conv_w, v_ssd_conv_b, v_ssd_dt_bias, v_ssd_A_log, v_ssd_D, v_ssd_norm_w, v_ssd_w_out):
    a = dict(locals())
    w = {n: a[n] for n in WEIGHTS}
    m = {n: a["m_" + n] for n in WEIGHTS}
    v = {n: a["v_" + n] for n in WEIGHTS}
    _, T, D = x.shape
    cidx = lax.axis_index("c").astype(jnp.int32).reshape(1)
    big_names = [n for n, _ in BIG]
    small_names = [n for n, _ in SMALL_SHARDED]

    wx = {n: (jnp.swapaxes(w[n], 1, 2) if n in TRANSPOSED else w[n]) for n in big_names + small_names}
    big = _flat_pad([wx[n] for n in big_names], 16 * ROW).astype(BF16).reshape(-1, ROW)
    small = _flat_pad([wx[n] for n in small_names], 8 * ROW).reshape(-1, ROW)
    first_name, first_axis = BIG[0]
    first_shape = wx[first_name].shape[1:]
    n_first = math.prod(first_shape) // ROW
    g_first, g_small = gather_chips([big[:n_first], small], name="gather_weights")
    g_small = g_small.reshape(N_CHIPS, -1)
    full, off = {}, 0
    for n, ax in SMALL_SHARDED:
        full[n] = _from_shards(g_small[:, off:off + wx[n].size], wx[n].shape, ax)
        off += wx[n].size
    w_in_0 = _from_shards(g_first.reshape(N_CHIPS, -1), first_shape, first_axis - 1)
    n_layers = norm_w.shape[0]
    lw = [None] * n_layers
    lw[0] = conf_weights(w_in_0, full["cm_dw_w"][0], cm_dw_b[0], cm_ln_w[0], cm_ln_b[0])

    def unpack_rest(gathered):
        g_big = jnp.concatenate([g_first, gathered[0]], axis=1).reshape(N_CHIPS, -1)
        off = 0
        for n, ax in BIG:
            full[n] = _from_shards(g_big[:, off:off + wx[n].size], wx[n].shape, ax)
            off += wx[n].size
        for i in range(1, n_layers):
            j = i // 2
            if i % 2 == 0:
                lw[i] = conf_weights(full["cm_w_in"][j], full["cm_dw_w"][j], cm_dw_b[j], cm_ln_w[j], cm_ln_b[j])
            else:
                lw[i] = ssd_weights(full["ssd_w_in"][j], full["ssd_conv_w"][j], full["ssd_conv_b"][j], ssd_dt_bias[j],
                                    ssd_A_log[j], ssd_D[j], full["ssd_norm_w"][j], full["ssd_w_out"][j])
        return full["cm_w_out"][0]

    h = x[0]
    saved = []
    for i in range(n_layers):
        nw_i = norm_w[i].reshape(1, -1)
        if i == 0:
            h, s = conf_layer_fwd(h, nw_i, lw[0], "l0", side=(big[n_first:],), w_out=unpack_rest)
        elif i % 2 == 0:
            h, s = conf_layer_fwd(h, nw_i, lw[i], f"l{i}", w_out=full["cm_w_out"][i // 2])
        else:
            h, s = ssd_layer_fwd(h, nw_i, lw[i], f"l{i}")
        saved.append(s)
    dh, loss_local, d_final = loss_head(h, loss_target[0], final_norm_w.reshape(1, -1), name="loss_head")
    lg = [None] * n_layers
    for i in reversed(range(n_layers)):
        bwd = conf_layer_bwd if i % 2 == 0 else ssd_layer_bwd
        dh, lg[i] = bwd(dh, saved[i], norm_w[i].reshape(1, -1), lw[i], f"l{i}")
    conf_g, ssd_g = lg[0::2], lg[1::2]
    local = {
        "norm_w": jnp.stack([g["norm"] for g in lg]), "final_norm_w": d_final[0],
        "cm_w_in": jnp.stack([g["w_in"] for g in conf_g]), "cm_dw_w": jnp.stack([g["dw_w"] for g in conf_g]),
        "cm_dw_b": jnp.stack([g["dw_b"] for g in conf_g]), "cm_ln_w": jnp.stack([g["ln_w"] for g in conf_g]),
        "cm_ln_b": jnp.stack([g["ln_b"] for g in conf_g]), "cm_w_out": jnp.stack([g["w_out"] for g in conf_g]),
        "ssd_w_in": jnp.stack([g["w_in_t"] for g in ssd_g]), "ssd_conv_w": jnp.stack([g["conv_w"] for g in ssd_g]),
        "ssd_conv_b": jnp.stack([g["conv_b"] for g in ssd_g]), "ssd_dt_bias": jnp.stack([g["dt_bias"] for g in ssd_g]),
        "ssd_A_log": jnp.stack([g["a_log"] for g in ssd_g]), "ssd_D": jnp.stack([g["d_skip"] for g in ssd_g]),
        "ssd_norm_w": jnp.stack([g["norm_w"] for g in ssd_g]), "ssd_w_out": jnp.stack([g["w_out"] for g in ssd_g]),
    }

    shards = [_to_shards(local[n], ax) for n, ax in BIG + SMALL_SHARDED]
    fill = (-sum(t.shape[1] for t in shards)) % (2 * RED_TR * ROW)
    flat4 = jnp.concatenate(shards + [jnp.zeros((N_CHIPS, fill), F32)], axis=1)
    g2 = jnp.swapaxes(flat4.reshape(N_CHIPS, 2, -1, ROW), 0, 1)
    theirs = swap_other_half(g2, name="grad_pair_swap")
    part = pair_sum(g2, theirs, cidx, name="grad_pair_sum")
    got = exchange_chips(part, name="grad_chip_exchange")
    half = sum_lead(got, slot=cidx, nslots=2, name="grad_chip_sum")
    shard_flat = share_half(half, name="grad_pair_share").reshape(-1)
    grads = _split(shard_flat, wx, big_names + small_names)
    for n in TRANSPOSED:
        grads[n] = jnp.swapaxes(grads[n], 1, 2)

    rep = _flat_pad([local[n] for n in REPLICATED], 8 * LANES).reshape(-1, LANES)
    rep_sum = sum_lead(gather_all(rep, name="grad_small_gather"), name="grad_small_sum")
    grads.update(_split(rep_sum.reshape(-1), w, REPLICATED))

    delta, new_m, new_v = {}, {}, {}
    for n in big_names:
        two_d = (-1, w[n].shape[-1])
        d_, m_, v_ = adamw(grads[n].reshape(two_d), w[n].reshape(two_d), m[n].reshape(two_d), v[n].reshape(two_d),
                           name="adamw_" + n)
        delta[n], new_m[n], new_v[n] = d_.reshape(w[n].shape), m_.reshape(w[n].shape), v_.reshape(w[n].shape)
    rest = list(REPLICATED) + small_names
    packed = [_flat_pad([t[n] for n in rest], 8 * LANES).reshape(-1, LANES) for t in (grads, w, m, v)]
    for out, res in zip((delta, new_m, new_v), adamw(*packed, name="adamw_small")):
        out.update(_split(res.reshape(-1), w, rest))

    loss = lax.psum(loss_local[0, 0], ("x", "y", "c"))
    return (loss, dh.reshape(x.shape), *[grads[n] for n in WEIGHTS], *[delta[n] for n in WEIGHTS],
            *[new_m[n] for n in WEIGHTS], *[new_v[n] for n in WEIGHTS])
```

```python
import itertools
import math

import jax
import jax.numpy as jnp
from jax import lax
from jax.experimental import pallas as pl
from jax.experimental.pallas import tpu as pltpu

F32 = jnp.float32
BF16 = jnp.bfloat16
MESH = pl.DeviceIdType.MESH

EPS = 1e-5
HEADDIM = 64
HEADS = 32
GROUPS = 4
HPG = HEADS // GROUPS
D_STATE = 128
CHUNK = 128
GW = HPG * HEADDIM
XCG = GW + 2 * D_STATE
HALO = 16
LANES = 128
N_CHIPS = 4
N_DEV = 8

ADAM_LR = 0.001
ADAM_B1 = 0.9
ADAM_B2 = 0.999
ADAM_EPS = 1e-08
ADAM_WD = 0.01
ADAM_STEP = 10

VMEM_LIMIT = 52 * 1024 * 1024


def _params(n_axes):
    return pltpu.CompilerParams(dimension_semantics=("arbitrary",) * n_axes, vmem_limit_bytes=VMEM_LIMIT)


def _sigmoid(x):
    return 1.0 / (1.0 + jnp.exp(-x))


def _softplus(x):
    return jnp.maximum(x, 0.0) + jnp.log(1.0 + jnp.exp(-jnp.abs(x)))


def _dot(a, b):
    return jnp.dot(a, b, preferred_element_type=F32)


def _dot_nt(a, b):
    return lax.dot_general(a, b, (((1,), (1,)), ((), ())), preferred_element_type=F32)


def _dot_tn(a, b):
    return lax.dot_general(a, b, (((0,), (0,)), ((), ())), preferred_element_type=F32)


def _pick(n, pref):
    for t in pref:
        if n % t == 0:
            return t
    return n


def mm_nn(a, b, *, out_dtype, name, res=None, a2=None, b2=None, b_rows_are_n=False, norm_bwd=None):
    M, K = a.shape
    N = b.shape[0] if b_rows_are_n else b.shape[1]
    has2, has_res, has_nb = a2 is not None, res is not None, norm_bwd is not None
    tm = _pick(M, (1024, 512, 256, 128))
    tn = N if has_nb else _pick(N, (1024, 512, 256, 128))
    tk = _pick(K, (2048, 1024, 512, 256, 128) if a.dtype == BF16 and not has_nb else (1024, 512, 256, 128))
    nk = K // tk

    def body(*refs):
        a_ref, b_ref = refs[0], refs[1]
        pos = 2
        if has2:
            a2_ref, b2_ref = refs[pos], refs[pos + 1]
            pos += 2
        if has_res:
            r_ref = refs[pos]
            pos += 1
        if has_nb:
            h_ref, w_ref, dh_ref = refs[pos:pos + 3]
            pos += 3
        o_ref = refs[pos]
        acc_ref = refs[-1]
        k = pl.program_id(2)
        first_rows = pl.program_id(0) == 0

        @pl.when(k == 0)
        def _():
            if has2:
                acc_ref[...] = _dot(a2_ref[...].astype(BF16), b2_ref[...])
            else:
                acc_ref[...] = jnp.zeros_like(acc_ref)

        acc_ref[...] += (_dot_nt if b_rows_are_n else _dot)(a_ref[...].astype(BF16), b_ref[...])

        @pl.when(k == nk - 1)
        def _():
            r = acc_ref[...]
            if has_res:
                r = r + r_ref[...]
            if has_nb:
                dw_ref = refs[pos + 1]

                @pl.when(first_rows)
                def _():
                    dw_ref[...] = jnp.zeros_like(dw_ref)

                x = h_ref[...]
                rstd = lax.rsqrt(jnp.mean(x * x, axis=-1, keepdims=True) + EPS)
                xhat = x * rstd
                dxh = r * w_ref[...]
                dw_ref[...] += jnp.sum(r * xhat, axis=0, keepdims=True)
                r = dh_ref[...] + rstd * (dxh - xhat * jnp.mean(dxh * xhat, axis=-1, keepdims=True))
            o_ref[...] = r.astype(out_dtype)

    b_spec = pl.BlockSpec((tn, tk), lambda i, j, k: (j, k)) if b_rows_are_n else pl.BlockSpec((tk, tn), lambda i, j, k: (k, j))
    in_specs = [pl.BlockSpec((tm, tk), lambda i, j, k: (i, k)), b_spec]
    args = [a, b]
    if has2:
        k2 = a2.shape[1]
        in_specs += [pl.BlockSpec((tm, k2), lambda i, j, k: (i, 0)), pl.BlockSpec((k2, tn), lambda i, j, k: (0, j))]
        args += [a2, b2]
    tile = pl.BlockSpec((tm, tn), lambda i, j, k: (i, j))
    if has_res:
        in_specs.append(tile)
        args.append(res)
    out_specs, out_shape = tile, jax.ShapeDtypeStruct((M, N), out_dtype)
    if has_nb:
        vec = pl.BlockSpec((1, N), lambda i, j, k: (0, 0))
        in_specs += [tile, vec, tile]
        args += list(norm_bwd)
        out_specs, out_shape = (tile, vec), (out_shape, jax.ShapeDtypeStruct((1, N), F32))
    return pl.pallas_call(
        body, name=name, grid=(M // tm, N // tn, nk), in_specs=in_specs, out_specs=out_specs, out_shape=out_shape,
        scratch_shapes=[pltpu.VMEM((tm, tn), F32)], compiler_params=_params(3),
    )(*args)


def mm_tn(a, b, *, name):
    T, M = a.shape
    N = b.shape[1]
    tm = _pick(M, (1024, 512, 256, 128))
    tn = _pick(N, (1024, 512, 256, 128))
    tt = _pick(T, (2048, 1024, 512, 256, 128))

    def body(a_ref, b_ref, o_ref):
        @pl.when(pl.program_id(2) == 0)
        def _():
            o_ref[...] = jnp.zeros_like(o_ref)

        o_ref[...] += _dot_tn(a_ref[...].astype(BF16), b_ref[...].astype(BF16))

    return pl.pallas_call(
        body, name=name, grid=(M // tm, N // tn, T // tt),
        in_specs=[pl.BlockSpec((tt, tm), lambda i, j, t: (t, i)), pl.BlockSpec((tt, tn), lambda i, j, t: (t, j))],
        out_specs=pl.BlockSpec((tm, tn), lambda i, j, t: (i, j)),
        out_shape=jax.ShapeDtypeStruct((M, N), F32), compiler_params=_params(3),
    )(a, b)


def rmsnorm_fwd(h, w, *, name, side=()):
    T, D = h.shape
    tm = _pick(T, (512, 256, 128))
    n_i = T // tm
    n_side = len(side)
    side_rows = [t.shape[0] for t in side]

    def body(*refs):
        h_ref, w_ref = refs[:2]
        side_in = refs[2:2 + n_side]
        o_ref = refs[2 + n_side]
        side_out = refs[3 + n_side:3 + 2 * n_side]
        sems = refs[3 + 2 * n_side:]
        i = pl.program_id(0)
        if n_side:
            @pl.when(i == 0)
            def _():
                _gather_start(side_rows, side_in, side_out, sems)

        x = h_ref[...]
        rstd = lax.rsqrt(jnp.mean(x * x, axis=-1, keepdims=True) + EPS)
        o_ref[...] = (x * rstd * w_ref[...]).astype(BF16)
        if n_side:
            @pl.when(i == n_i - 1)
            def _():
                _gather_finish(side_rows, side_in, side_out, sems)

    out = pl.pallas_call(
        body, name=name, grid=(n_i,),
        in_specs=[pl.BlockSpec((tm, D), lambda i: (i, 0)), pl.BlockSpec((1, D), lambda i: (0, 0))] + [ANY] * n_side,
        out_specs=tuple([pl.BlockSpec((tm, D), lambda i: (i, 0))] + [ANY] * n_side),
        out_shape=tuple([jax.ShapeDtypeStruct((T, D), BF16)]
                        + [jax.ShapeDtypeStruct((N_CHIPS,) + t.shape, t.dtype) for t in side]),
        scratch_shapes=_gather_sems(n_side) if n_side else [], compiler_params=_params(1),
    )(h, w, *side)
    return (out[0], list(out[1:])) if n_side else out[0]


def loss_head(h, target, w, *, name):
    T, D = h.shape
    tm = _pick(T, (512, 256, 128))

    def body(h_ref, t_ref, w_ref, dh_ref, loss_ref, dw_ref):
        @pl.when(pl.program_id(0) == 0)
        def _():
            loss_ref[...] = jnp.zeros_like(loss_ref)
            dw_ref[...] = jnp.zeros_like(dw_ref)

        x = h_ref[...]
        rstd = lax.rsqrt(jnp.mean(x * x, axis=-1, keepdims=True) + EPS)
        xhat = x * rstd
        err = xhat * w_ref[...] - t_ref[...]
        rows = jnp.sum(err * err, axis=-1, keepdims=True)
        loss_ref[...] += (0.5 / D) * jnp.sum(rows, axis=0, keepdims=True)
        dy = err * (1.0 / D)
        dxh = dy * w_ref[...]
        dh_ref[...] = rstd * (dxh - xhat * jnp.mean(dxh * xhat, axis=-1, keepdims=True))
        dw_ref[...] += jnp.sum(dy * xhat, axis=0, keepdims=True)

    row = pl.BlockSpec((tm, D), lambda i: (i, 0))
    vec = pl.BlockSpec((1, D), lambda i: (0, 0))
    return pl.pallas_call(
        body, name=name, grid=(T // tm,), in_specs=[row, row, vec],
        out_specs=(row, pl.BlockSpec((1, 1), lambda i: (0, 0)), vec),
        out_shape=(jax.ShapeDtypeStruct((T, D), F32), jax.ShapeDtypeStruct((1, 1), F32),
                   jax.ShapeDtypeStruct((1, D), F32)),
        compiler_params=_params(1),
    )(h, target, w)


ANY = pl.BlockSpec(memory_space=pl.ANY)


def _place():
    return lax.axis_index("x"), lax.axis_index("y"), lax.axis_index("c")


def _gather_sems(n):
    return [pltpu.SemaphoreType.DMA((3 * n,))] * 4 + [pltpu.SemaphoreType.DMA((n,))]


def _gather_copies(rows, ins, outs, sems):
    ici_send, ici_recv, d2d_send, d2d_recv, local_sems = sems
    x, y, c = _place()
    k_me = 2 * x + y
    plan = []
    for t in range(len(rows)):
        half = rows[t] // 2
        mine = pl.ds(pl.multiple_of(c * half, 8), half)
        own = pltpu.make_async_copy(ins[t], outs[t].at[k_me], local_sems.at[t])
        sent, passed = [], []
        for j, (px, py) in enumerate([(1 - x, y), (x, 1 - y), (1 - x, 1 - y)]):
            landed = outs[t].at[2 * px + py, mine]
            sent.append(pltpu.make_async_remote_copy(
                src_ref=ins[t].at[mine], dst_ref=outs[t].at[k_me, mine], send_sem=ici_send.at[3 * t + j],
                recv_sem=ici_recv.at[3 * t + j], device_id=(px, py, c), device_id_type=MESH))
            passed.append(pltpu.make_async_remote_copy(
                src_ref=landed, dst_ref=landed, send_sem=d2d_send.at[3 * t + j], recv_sem=d2d_recv.at[3 * t + j],
                device_id=(x, y, 1 - c), device_id_type=MESH))
        plan.append((own, sent, passed))
    return plan


def _gather_start(rows, ins, outs, sems):
    for own, sent, _ in _gather_copies(rows, ins, outs, sems):
        own.start()
        for cp in sent:
            cp.start()


def _gather_finish(rows, ins, outs, sems):
    plan = _gather_copies(rows, ins, outs, sems)
    for _, sent, passed in plan:
        for cp, fwd in zip(sent, passed):
            cp.wait_recv()
            fwd.start()
    for own, sent, passed in plan:
        own.wait()
        for cp, fwd in zip(sent, passed):
            cp.wait_send()
            fwd.wait()


CONV_TM = 512
CONV_TC = 512
CONV_RB = 16


def _conv_specs(T, tm, sw, col0):
    hb = tm // HALO
    last = T // HALO - 1
    main = pl.BlockSpec((tm, sw), lambda j, i: (i, col0 + j))
    prev = pl.BlockSpec((HALO, sw), lambda j, i: (jnp.maximum(i * hb - 1, 0), col0 + j))
    nxt = pl.BlockSpec((HALO, sw), lambda j, i: (jnp.minimum((i + 1) * hb, last), col0 + j))
    return main, prev, nxt


def _conv_input(blk, glu, tc):
    x = blk.astype(F32)
    if glu:
        return x[:, :tc] * _sigmoid(x[:, tc:])
    return x


def _fill_padded(pad_ref, main, prev, nxt, first, last, tm):
    pad_ref[0:HALO, :] = jnp.where(first, 0.0, prev)
    pad_ref[HALO:HALO + tm, :] = main
    pad_ref[HALO + tm:HALO + tm + HALO, :] = jnp.where(last, 0.0, nxt)


SH_ROWS = 24


def _tap_plan(offsets):
    plan = [(o % 8, o - o % 8) for o in offsets]
    return plan, sorted({b for b, _ in plan if b})


def _fill_shifted(sh_ref, pad_ref, shifts, tm):
    for b in shifts:
        sh_ref[b] = pad_ref[b:b + tm + SH_ROWS, :]


def _tap_rows(pad_ref, sh_ref, b, start, rows):
    return pad_ref[start:start + rows, :] if b == 0 else sh_ref[b, start:start + rows, :]


def dwconv_fwd(src, w, b, *, width, glu, silu, col0, name, side=()):
    T = src.shape[0]
    C = w.shape[1]
    tm, tc = min(CONV_TM, T), CONV_TC
    sw = 2 * tc if glu else tc
    n_i = T // tm
    p = (width - 1) // 2
    rb = CONV_RB
    plan, shifts = _tap_plan([HALO - p + k for k in range(width)])

    n_side = len(side)
    side_rows = [t.shape[0] for t in side]

    def body(*refs):
        m_ref, p_ref, n_ref, w_ref, b_ref = refs[:5]
        side_in = refs[5:5 + n_side]
        o_ref = refs[5 + n_side]
        side_out = refs[6 + n_side:6 + 2 * n_side]
        pad_ref, sh_ref = refs[6 + 2 * n_side:8 + 2 * n_side]
        sems = refs[8 + 2 * n_side:]
        i = pl.program_id(1)
        j = pl.program_id(0)
        if n_side:
            @pl.when(jnp.logical_and(i == 0, j == 0))
            def _():
                _gather_start(side_rows, side_in, side_out, sems)

        _fill_padded(pad_ref, _conv_input(m_ref[...], glu, tc), _conv_input(p_ref[...], glu, tc),
                     _conv_input(n_ref[...], glu, tc), i == 0, i == n_i - 1, tm)
        _fill_shifted(sh_ref, pad_ref, shifts, tm)
        for r in range(tm // rb):
            acc = jnp.zeros((rb, tc), F32)
            for k, (sb, start) in enumerate(plan):
                acc = acc + _tap_rows(pad_ref, sh_ref, sb, start + r * rb, rb) * w_ref[k:k + 1, :]
            acc = acc + b_ref[...]
            if silu:
                acc = acc * _sigmoid(acc)
            o_ref[r * rb:(r + 1) * rb, :] = acc.astype(BF16)

        if n_side:
            @pl.when(jnp.logical_and(i == n_i - 1, j == C // tc - 1))
            def _():
                _gather_finish(side_rows, side_in, side_out, sems)

    main, prev, nxt = _conv_specs(T, tm, sw, col0)
    out = pl.pallas_call(
        body, name=name, grid=(C // tc, n_i),
        in_specs=[main, prev, nxt, pl.BlockSpec((w.shape[0], tc), lambda j, i: (0, j)),
                  pl.BlockSpec((1, tc), lambda j, i: (0, j))] + [ANY] * n_side,
        out_specs=tuple([pl.BlockSpec((tm, tc), lambda j, i: (i, j))] + [ANY] * n_side),
        out_shape=tuple([jax.ShapeDtypeStruct((T, C), BF16)]
                        + [jax.ShapeDtypeStruct((N_CHIPS,) + t.shape, t.dtype) for t in side]),
        scratch_shapes=[pltpu.VMEM((tm + 2 * HALO, tc), F32), pltpu.VMEM((8, tm + SH_ROWS, tc), F32)]
        + (_gather_sems(n_side) if n_side else []),
        compiler_params=_params(2),
    )(src, src, src, w, b, *side)
    return (out[0], list(out[1:])) if n_side else out[0]


def dwconv_bwd(dout, src, w, b, dsrc, *, width, glu, silu, col0, dcol0, name):
    T = src.shape[0]
    C = w.shape[1]
    kp = w.shape[0]
    tm, tc = min(CONV_TM, T), CONV_TC
    sw = 2 * tc if glu else tc
    n_i = T // tm
    p = (width - 1) // 2
    rb = CONV_RB
    edge = 8
    assert p <= edge or not silu
    plan, shifts = _tap_plan([HALO - p + k for k in range(width)])
    dplan, dshifts = _tap_plan([HALO + p - k for k in range(width)])

    def body(dm_ref, dp_ref, dn_ref, m_ref, p_ref, n_ref, w_ref, b_ref, _, o_ref, dw_ref, db_ref, pad_ref, dpre_ref,
             sh_ref, dsh_ref, acc_ref):
        i = pl.program_id(1)

        @pl.when(i == 0)
        def _():
            dw_ref[...] = jnp.zeros_like(dw_ref)
            acc_ref[...] = jnp.zeros_like(acc_ref)

        first, last = i == 0, i == n_i - 1
        _fill_padded(pad_ref, _conv_input(m_ref[...], glu, tc), _conv_input(p_ref[...], glu, tc),
                     _conv_input(n_ref[...], glu, tc), first, last, tm)
        _fill_padded(dpre_ref, dm_ref[...].astype(F32), dp_ref[...].astype(F32), dn_ref[...].astype(F32),
                     first, last, tm)
        _fill_shifted(sh_ref, pad_ref, shifts, tm)
        if silu:
            for r0 in range(HALO - edge, HALO + tm + edge, HALO):
                pre = jnp.zeros((HALO, tc), F32)
                for k, (sb, start) in enumerate(plan):
                    pre = pre + _tap_rows(pad_ref, sh_ref, sb, start + r0 - HALO, HALO) * w_ref[k:k + 1, :]
                pre = pre + b_ref[...]
                s = _sigmoid(pre)
                dpre_ref[r0:r0 + HALO, :] = dpre_ref[r0:r0 + HALO, :] * (s * (1.0 + pre * (1.0 - s)))
        _fill_shifted(dsh_ref, dpre_ref, dshifts, tm)

        for r in range(tm // rb):
            acc = jnp.zeros((rb, tc), F32)
            for k, (sb, start) in enumerate(dplan):
                acc = acc + _tap_rows(dpre_ref, dsh_ref, sb, start + r * rb, rb) * w_ref[k:k + 1, :]
            if glu:
                blk = m_ref[r * rb:(r + 1) * rb, :].astype(F32)
                v, s = blk[:, :tc], _sigmoid(blk[:, tc:])
                o_ref[r * rb:(r + 1) * rb, :tc] = (acc * s).astype(BF16)
                o_ref[r * rb:(r + 1) * rb, tc:] = (acc * v * s * (1.0 - s)).astype(BF16)
            else:
                o_ref[r * rb:(r + 1) * rb, :] = acc.astype(BF16)

        for r in range(tm // rb):
            dblk = dpre_ref[HALO + r * rb:HALO + (r + 1) * rb, :]
            for k, (sb, start) in enumerate(plan):
                prod = dblk * _tap_rows(pad_ref, sh_ref, sb, start + r * rb, rb)
                acc_ref[k] += jnp.sum(prod.reshape(rb // 8, 8, tc), axis=0)
            acc_ref[kp] += jnp.sum(dblk.reshape(rb // 8, 8, tc), axis=0)

        @pl.when(last)
        def _():
            for k in range(width):
                dw_ref[k:k + 1, :] = jnp.sum(acc_ref[k], axis=0, keepdims=True)
            db_ref[...] = jnp.sum(acc_ref[kp], axis=0, keepdims=True)

    dmain_s, dprev_s, dnext_s = _conv_specs(T, tm, tc, 0)
    main, prev, nxt = _conv_specs(T, tm, sw, col0)
    wspec = pl.BlockSpec((kp, tc), lambda j, i: (0, j))
    bspec = pl.BlockSpec((1, tc), lambda j, i: (0, j))
    return pl.pallas_call(
        body, name=name, grid=(C // tc, n_i),
        in_specs=[dmain_s, dprev_s, dnext_s, main, prev, nxt, wspec, bspec, pl.BlockSpec(memory_space=pl.ANY)],
        out_specs=(pl.BlockSpec((tm, sw), lambda j, i: (i, dcol0 + j)), wspec, bspec),
        out_shape=(jax.ShapeDtypeStruct(dsrc.shape, dsrc.dtype), jax.ShapeDtypeStruct((kp, C), F32),
                   jax.ShapeDtypeStruct((1, C), F32)),
        input_output_aliases={8: 0},
        scratch_shapes=[pltpu.VMEM((tm + 2 * HALO, tc), F32), pltpu.VMEM((tm + 2 * HALO, tc), F32),
                        pltpu.VMEM((8, tm + SH_ROWS, tc), F32), pltpu.VMEM((8, tm + SH_ROWS, tc), F32),
                        pltpu.VMEM((kp + 1, 8, tc), F32)],
        compiler_params=_params(2),
    )(dout, dout, dout, src, src, src, w, b, dsrc)


def _silu_grad(x, s):
    return s * (1.0 + x * (1.0 - s))


STRIP = 16
LCH = 512


def _strips(tm, fn):
    def step(s, carry):
        fn(pl.ds(pl.multiple_of(s * STRIP, STRIP), STRIP))
        return carry

    lax.fori_loop(0, tm // STRIP, step, 0, unroll=8)


def _chunks(e):
    return [slice(k, k + LCH) for k in range(0, e, LCH)]


def _row_sum(parts):
    acc = parts[0]
    for p in parts[1:]:
        acc = acc + p
    return jnp.sum(acc, axis=-1, keepdims=True)


def _fold8(x):
    return jnp.sum(x.reshape(STRIP // 8, 8, x.shape[-1]), axis=0)


def _ln_stats(u_ref, r, cks, e):
    mu = _row_sum([u_ref[r, ck].astype(F32) for ck in cks]) * (1.0 / e)
    var = _row_sum([jnp.square(u_ref[r, ck].astype(F32) - mu) for ck in cks]) * (1.0 / e)
    return mu, lax.rsqrt(var + EPS)


def conf_ln_fwd(u2, proj, ln_w, ln_b, *, name):
    T, E = u2.shape
    zc = proj.shape[1] // E - 1
    tm = _pick(T, (256, 128))
    cks = _chunks(E)

    def body(u_ref, z_ref, w_ref, b_ref, o_ref):
        def strip(r):
            mu, rstd = _ln_stats(u_ref, r, cks, E)
            for ck in cks:
                u3 = (u_ref[r, ck].astype(F32) - mu) * rstd * w_ref[:, ck] + b_ref[:, ck]
                z = z_ref[r, ck].astype(F32)
                o_ref[r, ck] = (u3 * _sigmoid(u3) * z * _sigmoid(z)).astype(BF16)

        _strips(tm, strip)

    row = pl.BlockSpec((tm, E), lambda i: (i, 0))
    vec = pl.BlockSpec((1, E), lambda i: (0, 0))
    return pl.pallas_call(
        body, name=name, grid=(T // tm,),
        in_specs=[row, pl.BlockSpec((tm, E), lambda i: (i, zc)), vec, vec], out_specs=row,
        out_shape=jax.ShapeDtypeStruct((T, E), BF16), compiler_params=_params(1),
    )(u2, proj, ln_w, ln_b)


def conf_ln_bwd(du4, u2, proj, ln_w, ln_b, *, name):
    T, E = u2.shape
    ncol = proj.shape[1] // E
    zc = ncol - 1
    tm = _pick(T, (256, 128))
    n_i = T // tm
    cks = _chunks(E)

    def body(d_ref, u_ref, z_ref, w_ref, b_ref, du_ref, dz_ref, dw_ref, db_ref, dxh_ref, accw_ref, accb_ref):
        i = pl.program_id(0)

        @pl.when(i == 0)
        def _():
            accw_ref[...] = jnp.zeros_like(accw_ref)
            accb_ref[...] = jnp.zeros_like(accb_ref)

        def strip(r):
            mu, rstd = _ln_stats(u_ref, r, cks, E)
            s1, s2 = [], []
            for ck in cks:
                xhat = (u_ref[r, ck].astype(F32) - mu) * rstd
                u3 = xhat * w_ref[:, ck] + b_ref[:, ck]
                z = z_ref[r, ck].astype(F32)
                s3, sz = _sigmoid(u3), _sigmoid(z)
                d4 = d_ref[r, ck].astype(F32)
                du3 = d4 * (z * sz) * _silu_grad(u3, s3)
                dz_ref[r, ck] = (d4 * (u3 * s3) * _silu_grad(z, sz)).astype(BF16)
                accw_ref[:, ck] += _fold8(du3 * xhat)
                accb_ref[:, ck] += _fold8(du3)
                dxh = du3 * w_ref[:, ck]
                dxh_ref[:, ck] = dxh
                s1.append(dxh)
                s2.append(dxh * xhat)
            m1, m2 = _row_sum(s1) * (1.0 / E), _row_sum(s2) * (1.0 / E)
            for ck in cks:
                xhat = (u_ref[r, ck].astype(F32) - mu) * rstd
                du_ref[r, ck] = (rstd * (dxh_ref[:, ck] - m1 - xhat * m2)).astype(BF16)

        _strips(tm, strip)

        @pl.when(i == n_i - 1)
        def _():
            dw_ref[...] = jnp.sum(accw_ref[...], axis=0, keepdims=True)
            db_ref[...] = jnp.sum(accb_ref[...], axis=0, keepdims=True)

    row = pl.BlockSpec((tm, E), lambda i: (i, 0))
    zrow = pl.BlockSpec((tm, E), lambda i: (i, zc))
    vec = pl.BlockSpec((1, E), lambda i: (0, 0))
    return pl.pallas_call(
        body, name=name, grid=(n_i,), in_specs=[row, row, zrow, vec, vec], out_specs=(row, zrow, vec, vec),
        out_shape=(jax.ShapeDtypeStruct((T, E), BF16), jax.ShapeDtypeStruct(proj.shape, BF16),
                   jax.ShapeDtypeStruct((1, E), F32), jax.ShapeDtypeStruct((1, E), F32)),
        scratch_shapes=[pltpu.VMEM((STRIP, E), F32), pltpu.VMEM((8, E), F32), pltpu.VMEM((8, E), F32)],
        compiler_params=_params(1),
    )(du4, u2, proj, ln_w, ln_b)


def _gated(y_ref, z_ref, r, ck):
    z = z_ref[r, ck].astype(F32)
    sz = _sigmoid(z)
    yv = y_ref[r, ck].astype(F32)
    return z, sz, yv, yv * (z * sz)


def ssd_gate_fwd(y, zx, norm_w, *, name):
    T, E = y.shape
    tm = _pick(T, (256, 128))
    cks = _chunks(E)

    def body(y_ref, z_ref, w_ref, o_ref, yz_ref):
        def strip(r):
            sq = []
            for ck in cks:
                yz = _gated(y_ref, z_ref, r, ck)[3]
                yz_ref[:, ck] = yz
                sq.append(yz * yz)
            rstd = lax.rsqrt(_row_sum(sq) * (1.0 / E) + EPS)
            for ck in cks:
                o_ref[r, ck] = (yz_ref[:, ck] * rstd * w_ref[:, ck]).astype(BF16)

        _strips(tm, strip)

    row = pl.BlockSpec((tm, E), lambda i: (i, 0))
    vec = pl.BlockSpec((1, E), lambda i: (0, 0))
    return pl.pallas_call(
        body, name=name, grid=(T // tm,), in_specs=[row, row, vec], out_specs=row,
        out_shape=jax.ShapeDtypeStruct((T, E), BF16), scratch_shapes=[pltpu.VMEM((STRIP, E), F32)],
        compiler_params=_params(1),
    )(y, zx, norm_w)


def ssd_gate_bwd(dyn, y, zx, norm_w, *, name):
    T, E = y.shape
    tm = _pick(T, (256, 128))
    n_i = T // tm
    cks = _chunks(E)

    def body(d_ref, y_ref, z_ref, w_ref, dy_ref, dz_ref, dw_ref, yz_ref, accw_ref):
        i = pl.program_id(0)

        @pl.when(i == 0)
        def _():
            accw_ref[...] = jnp.zeros_like(accw_ref)

        def strip(r):
            sq = []
            for ck in cks:
                yz = _gated(y_ref, z_ref, r, ck)[3]
                yz_ref[:, ck] = yz
                sq.append(yz * yz)
            rstd = lax.rsqrt(_row_sum(sq) * (1.0 / E) + EPS)
            s2 = []
            for ck in cks:
                yhat = yz_ref[:, ck] * rstd
                d = d_ref[r, ck].astype(F32)
                accw_ref[:, ck] += _fold8(d * yhat)
                s2.append(d * w_ref[:, ck] * yhat)
            m2 = _row_sum(s2) * (1.0 / E)
            for ck in cks:
                z, sz, yv, _ = _gated(y_ref, z_ref, r, ck)
                dyz = rstd * (d_ref[r, ck].astype(F32) * w_ref[:, ck] - yz_ref[:, ck] * rstd * m2)
                dy_ref[r, ck] = (dyz * (z * sz)).astype(BF16)
                dz_ref[r, ck] = (dyz * yv * _silu_grad(z, sz)).astype(BF16)

        _strips(tm, strip)

        @pl.when(i == n_i - 1)
        def _():
            dw_ref[...] = jnp.sum(accw_ref[...], axis=0, keepdims=True)

    row = pl.BlockSpec((tm, E), lambda i: (i, 0))
    vec = pl.BlockSpec((1, E), lambda i: (0, 0))
    return pl.pallas_call(
        body, name=name, grid=(n_i,), in_specs=[row, row, row, vec], out_specs=(row, row, vec),
        out_shape=(jax.ShapeDtypeStruct((T, E), BF16), jax.ShapeDtypeStruct(zx.shape, BF16),
                   jax.ShapeDtypeStruct((1, E), F32)),
        scratch_shapes=[pltpu.VMEM((STRIP, E), F32), pltpu.VMEM((8, E), F32)],
        compiler_params=_params(1),
    )(dyn, y, zx, norm_w)


def _cumsum_mm(mask, a):
    hi = a.astype(BF16)
    r1 = a - hi.astype(F32)
    mid = r1.astype(BF16)
    lo = (r1 - mid.astype(F32)).astype(BF16)
    out = _dot(jnp.where(mask, 1.0, 0.0).astype(BF16), jnp.concatenate([hi, mid, lo], axis=1))
    return out[:, :LANES] + out[:, LANES:2 * LANES] + out[:, 2 * LANES:]


def _chunk_terms(xcb, dt_raw, bias, alog, rev):
    L = CHUNK
    xs = xcb[:, :GW].astype(F32)
    Bm = xcb[:, GW:GW + D_STATE]
    Cm = xcb[:, GW + D_STATE:]
    pre = dt_raw + bias
    dt = _softplus(pre)
    A = -jnp.exp(alog)
    row = lax.broadcasted_iota(jnp.int32, (L, L), 0)
    col = lax.broadcasted_iota(jnp.int32, (L, L), 1)
    mask = (col >= row) if rev else (col <= row)
    mask_t = (col <= row) if rev else (col >= row)
    cs = _cumsum_mm(mask, dt * A)
    tot = cs[0:1, :] if rev else cs[L - 1:L, :]
    return xs, Bm, Cm, pre, dt, A, mask, mask_t, cs, cs.T, tot


def _decay(cs, cs_t, ln, mask):
    d = cs[:, ln:ln + 1] - cs_t[ln:ln + 1, :]
    return jnp.where(mask, jnp.exp(jnp.where(mask, d, 0.0)), 0.0)


def _pair(v, ln0, lo):
    return jnp.where(lo[:v.shape[0]], v[:, ln0:ln0 + 1], v[:, ln0 + 1:ln0 + 2])


def _scan_specs(nc, rev_order):
    ci = (lambda c: nc - 1 - c) if rev_order else (lambda c: c)
    xc = pl.BlockSpec((CHUNK, GROUPS * XCG), lambda c: (ci(c), 0))
    dt = pl.BlockSpec((CHUNK, GROUPS * LANES), lambda c: (ci(c), 0))
    vec = pl.BlockSpec((1, GROUPS * LANES), lambda c: (0, 0))
    wide = pl.BlockSpec((CHUNK, GROUPS * GW), lambda c: (ci(c), 0))
    wvec = pl.BlockSpec((1, GROUPS * GW), lambda c: (0, 0))
    st = pl.BlockSpec((1, D_STATE, GROUPS * GW), lambda c: (ci(c), 0, 0))
    return xc, dt, vec, wide, wvec, st


def _cols(ref, g, width):
    return ref.at[:, pl.ds(g * width, width)]


def _interleave(stages):
    for _ in itertools.zip_longest(*stages):
        pass


def _head_expand(r):
    row = lax.broadcasted_iota(jnp.int32, (LANES, GW), 0)
    col = lax.broadcasted_iota(jnp.int32, (LANES, GW), 1)
    first = (row - r * HPG) * HEADDIM
    return jnp.where(jnp.logical_and(col >= first, col < first + HEADDIM), 1.0, 0.0).astype(BF16)


def _head_collect(r):
    row = lax.broadcasted_iota(jnp.int32, (GW, LANES), 0)
    first = (lax.broadcasted_iota(jnp.int32, (GW, LANES), 1) - r * HPG) * HEADDIM
    return jnp.where(jnp.logical_and(row >= first, row < first + HEADDIM), 1.0, 0.0).astype(BF16)


def _expand(parts, sel):
    n = parts[0].shape[0]
    out = _dot(jnp.concatenate(parts, axis=0).astype(BF16), sel)
    return [out[i * n:(i + 1) * n] for i in range(len(parts))]


def ssd_scan_fwd(xc, dt4, bias4, alog4, *, rev, name, prev=None, dvec=None):
    T = xc.shape[0]
    nc = T // CHUNK
    E = GROUPS * GW
    r = 1 if rev else 0
    skip = prev is not None

    def one_group(sel, xc_ref, dt_ref, bias_ref, alog_ref, prev_ref, dvec_ref, y_ref, st_ref, s_ref):
        xs, Bm, Cm, _, dt, _, mask, _, cs, cs_t, tot = _chunk_terms(xc_ref[...], dt_ref[...], bias_ref[...],
                                                                   alog_ref[...], rev)
        yield
        dtx, ex, dx = _expand([dt, jnp.exp(cs), jnp.exp(tot - cs)], sel)
        et = jnp.exp(tot)
        cb = _dot_nt(Cm, Bm)
        yield
        sb = s_ref[...].astype(BF16)
        st_ref[...] = sb
        xp_all = xs * dtx
        y_off = _dot(Cm, sb) * ex
        lo = lax.broadcasted_iota(jnp.int32, (CHUNK, LANES), 1) < HEADDIM
        et_parts = []
        for p in range(HPG // 2):
            yield
            ln0 = r * HPG + 2 * p
            sl = slice(p * LANES, (p + 1) * LANES)
            xp = xp_all[:, sl]
            mcat = jnp.concatenate([cb * _decay(cs, cs_t, ln0, mask), cb * _decay(cs, cs_t, ln0 + 1, mask)],
                                   axis=1).astype(BF16)
            xbd = jnp.concatenate([jnp.where(lo, xp, 0.0), jnp.where(lo, 0.0, xp)], axis=0).astype(BF16)
            yp = _dot(mcat, xbd) + y_off[:, sl]
            if skip:
                yp = yp + prev_ref[:, sl].astype(F32) + xs[:, sl] * dvec_ref[:, sl]
            y_ref[:, sl] = yp.astype(BF16)
            et_parts.append(_pair(et, ln0, lo))
        yield
        s_ref[...] = s_ref[...] * jnp.concatenate(et_parts, axis=1) + _dot_tn(Bm, (xp_all * dx).astype(BF16))

    def body(*refs):
        xc_ref, dt_ref, bias_ref, alog_ref = refs[:4]
        prev_ref, dvec_ref = (refs[4], refs[5]) if skip else (None, None)
        y_ref, st_ref, s_ref = refs[-3:]

        @pl.when(pl.program_id(0) == 0)
        def _():
            s_ref[...] = jnp.zeros_like(s_ref)

        sel = _head_expand(r)
        _interleave([
            one_group(sel, _cols(xc_ref, g, XCG), _cols(dt_ref, g, LANES), _cols(bias_ref, g, LANES),
                      _cols(alog_ref, g, LANES), _cols(prev_ref, g, GW) if skip else None,
                      _cols(dvec_ref, g, GW) if skip else None, _cols(y_ref, g, GW),
                      st_ref.at[0, :, pl.ds(g * GW, GW)], _cols(s_ref, g, GW)) for g in range(GROUPS)])

    s_xc, s_dt, s_vec, s_wide, s_wvec, s_st = _scan_specs(nc, rev)
    in_specs = [s_xc, s_dt, s_vec, s_vec]
    args = [xc, dt4, bias4, alog4]
    if skip:
        in_specs += [s_wide, s_wvec]
        args += [prev, dvec]
    return pl.pallas_call(
        body, name=name, grid=(nc,), in_specs=in_specs, out_specs=(s_wide, s_st),
        out_shape=(jax.ShapeDtypeStruct((T, E), BF16), jax.ShapeDtypeStruct((nc, D_STATE, E), BF16)),
        scratch_shapes=[pltpu.VMEM((D_STATE, E), F32)], compiler_params=_params(1),
    )(*args)


def ssd_scan_bwd(xc, dt4, bias4, alog4, dy, states, *, rev, name, prev=None, dvec=None):
    T = xc.shape[0]
    nc = T // CHUNK
    E = GROUPS * GW
    L = CHUNK
    r = 1 if rev else 0
    skip = prev is not None

    def one_group(sel, sel_t, xc_ref, dt_ref, bias_ref, alog_ref, dy_ref, st_ref, pdxc_ref, pddt_ref, dvec_ref,
                  dxc_ref, ddt_ref, dalog_ref, dbias_ref, dd_ref, g_ref):
        xs, Bm, Cm, pre, dt, A, mask, mask_t, cs, cs_t, tot = _chunk_terms(
            xc_ref[...], dt_ref[...], bias_ref[...], alog_ref[...], rev)
        yield
        dtx, ex, dx = _expand([dt, jnp.exp(cs), jnp.exp(tot - cs)], sel)
        et = jnp.exp(tot)
        cb = _dot_nt(Cm, Bm)
        yield
        s_in = st_ref[...]
        dy_all = dy_ref[...].astype(F32)
        g_f = g_ref[...]
        g_b = g_f.astype(BF16)
        xp_all = xs * dtx
        dye_all = dy_all * ex
        bgd = _dot(Bm, g_b) * dx
        lane = lax.broadcasted_iota(jnp.int32, (L, LANES), 1)
        lo = lane < HEADDIM
        dcb = jnp.zeros((L, L), F32)
        yd_parts, dxd_parts, et_parts = [], [], []
        for p in range(HPG // 2):
            yield
            ln0 = r * HPG + 2 * p
            sl = slice(p * LANES, (p + 1) * LANES)
            xp, dy_p = xp_all[:, sl], dy_all[:, sl]
            lam0, lam1 = _decay(cs, cs_t, ln0, mask), _decay(cs, cs_t, ln0 + 1, mask)
            m0, m1 = (cb * lam0).astype(BF16), (cb * lam1).astype(BF16)
            dybd = jnp.concatenate([jnp.where(lo, dy_p, 0.0), jnp.where(lo, 0.0, dy_p)], axis=0).astype(BF16)
            xbd = jnp.concatenate([jnp.where(lo, xp, 0.0), jnp.where(lo, 0.0, xp)], axis=0).astype(BF16)
            dm = _dot_nt(dybd, xp.astype(BF16))
            dcb = dcb + dm[:L] * lam0 + dm[L:] * lam1
            yd_parts.append(_dot(jnp.concatenate([m0, m1], axis=1), xbd))
            dxd_parts.append(_dot_tn(jnp.concatenate([m0, m1], axis=0), dybd))
            et_parts.append(_pair(et, ln0, lo))
        yield
        y_diag = jnp.concatenate(yd_parts, axis=1)
        dx_diag = jnp.concatenate(dxd_parts, axis=1)
        etx = jnp.concatenate(et_parts, axis=1)
        dxt = dx_diag + bgd
        w2 = xp_all * bgd
        dy_r, xp_r = dy_all.astype(BF16).astype(F32), xp_all.astype(BF16).astype(F32)
        u = dye_all * _dot(Cm, s_in) + dy_r * y_diag - xp_r * dx_diag - w2
        dxx = dxt * xs
        tail = jnp.broadcast_to(jnp.sum(w2, axis=0, keepdims=True)
                                + jnp.sum(g_f * s_in.astype(F32), axis=0, keepdims=True) * etx, (8, GW))
        u_hi, t_hi = u.astype(BF16), tail.astype(BF16)
        red = _dot(jnp.concatenate([u_hi, (u - u_hi.astype(F32)).astype(BF16), dxx.astype(BF16), t_hi,
                                    (tail - t_hi.astype(F32)).astype(BF16)], axis=0), sel_t)
        yield
        dcs = red[:L] + red[L:2 * L]
        ddt = red[2 * L:3 * L]
        dtot = red[3 * L:3 * L + 1] + red[3 * L + 8:3 * L + 9]
        dxs = dxt * dtx
        if skip:
            dxs = dxs + dy_all * dvec_ref[...] + pdxc_ref[:, :GW].astype(F32)
            dd_ref[...] += jnp.sum(dy_all * xs, axis=0, keepdims=True)
        dxc_ref[:, :GW] = dxs.astype(BF16)
        dye_b = dye_all.astype(BF16)
        xd = (xp_all * dx).astype(BF16)
        dcb_b = dcb.astype(BF16)
        d_b = _dot_nt(xd, g_b) + _dot_tn(dcb_b, Cm)
        d_c = _dot_nt(dye_b, s_in) + _dot(dcb_b, Bm)
        if skip:
            d_b = d_b + pdxc_ref[:, GW:GW + D_STATE].astype(F32)
            d_c = d_c + pdxc_ref[:, GW + D_STATE:].astype(F32)
        dxc_ref[:, GW:GW + D_STATE] = d_b.astype(BF16)
        dxc_ref[:, GW + D_STATE:] = d_c.astype(BF16)
        yield
        g_ref[...] = g_f * etx + _dot_tn(Cm, dye_b)
        rowi = lax.broadcasted_iota(jnp.int32, (L, LANES), 0)
        da = _cumsum_mm(mask_t, dcs + jnp.where(rowi == (0 if rev else L - 1), dtot, 0.0))
        keep = jnp.logical_and(lane >= r * HPG, lane < (r + 1) * HPG)
        ddr = jnp.where(keep, (da * A + ddt) * _sigmoid(pre), 0.0)
        dbias_ref[...] += jnp.sum(ddr, axis=0, keepdims=True)
        dalog_ref[...] += jnp.sum(jnp.where(keep, da * dt * A, 0.0), axis=0, keepdims=True)
        if skip:
            ddr = ddr + pddt_ref[...]
        ddt_ref[...] = ddr

    def body(*refs):
        xc_ref, dt_ref, bias_ref, alog_ref, dy_ref, st_ref = refs[:6]
        pdxc_ref, pddt_ref, dvec_ref = refs[6:9] if skip else (None, None, None)
        pos = 9 if skip else 6
        dxc_ref, ddt_ref, dalog_ref, dbias_ref = refs[pos:pos + 4]
        dd_ref = refs[pos + 4] if skip else None
        g_ref = refs[-1]

        @pl.when(pl.program_id(0) == 0)
        def _():
            g_ref[...] = jnp.zeros_like(g_ref)
            dalog_ref[...] = jnp.zeros_like(dalog_ref)
            dbias_ref[...] = jnp.zeros_like(dbias_ref)
            if skip:
                dd_ref[...] = jnp.zeros_like(dd_ref)

        sel, sel_t = _head_expand(r), _head_collect(r)
        _interleave([
            one_group(sel, sel_t, _cols(xc_ref, g, XCG), _cols(dt_ref, g, LANES), _cols(bias_ref, g, LANES),
                      _cols(alog_ref, g, LANES), _cols(dy_ref, g, GW), st_ref.at[0, :, pl.ds(g * GW, GW)],
                      _cols(pdxc_ref, g, XCG) if skip else None, _cols(pddt_ref, g, LANES) if skip else None,
                      _cols(dvec_ref, g, GW) if skip else None, _cols(dxc_ref, g, XCG), _cols(ddt_ref, g, LANES),
                      _cols(dalog_ref, g, LANES), _cols(dbias_ref, g, LANES),
                      _cols(dd_ref, g, GW) if skip else None, _cols(g_ref, g, GW)) for g in range(GROUPS)])

    s_xc, s_dt, s_vec, s_wide, s_wvec, s_st = _scan_specs(nc, not rev)
    in_specs = [s_xc, s_dt, s_vec, s_vec, s_wide, s_st]
    args = [xc, dt4, bias4, alog4, dy, states]
    out_specs = [s_xc, s_dt, s_vec, s_vec]
    out_shape = [jax.ShapeDtypeStruct((T, GROUPS * XCG), BF16), jax.ShapeDtypeStruct((T, GROUPS * LANES), F32),
                 jax.ShapeDtypeStruct((1, GROUPS * LANES), F32), jax.ShapeDtypeStruct((1, GROUPS * LANES), F32)]
    if skip:
        in_specs += [s_xc, s_dt, s_wvec]
        args += [prev[0], prev[1], dvec]
        out_specs.append(s_wvec)
        out_shape.append(jax.ShapeDtypeStruct((1, E), F32))
    return pl.pallas_call(
        body, name=name, grid=(nc,), in_specs=in_specs, out_specs=tuple(out_specs),
        out_shape=tuple(out_shape), scratch_shapes=[pltpu.VMEM((D_STATE, E), F32)], compiler_params=_params(1),
    )(*args)


def _conf_cols(w):
    e = w.shape[-1] // 3
    lead = w.shape[:-1]
    vg = w[..., :2 * e].reshape(*lead, 2, e // CONV_TC, CONV_TC)
    vg = jnp.swapaxes(vg, -3, -2).reshape(*lead, 2 * e)
    return jnp.concatenate([vg, w[..., 2 * e:]], axis=-1)


def _conf_cols_inv(w):
    e = w.shape[-1] // 3
    lead = w.shape[:-1]
    vg = w[..., :2 * e].reshape(*lead, e // CONV_TC, 2, CONV_TC)
    vg = jnp.swapaxes(vg, -3, -2).reshape(*lead, 2 * e)
    return jnp.concatenate([vg, w[..., 2 * e:]], axis=-1)


def _xbc_cols(w):
    lead = w.shape[:-1]
    e = GROUPS * GW
    gn = GROUPS * D_STATE
    parts = [w[..., :e].reshape(*lead, GROUPS, GW), w[..., e:e + gn].reshape(*lead, GROUPS, D_STATE),
             w[..., e + gn:].reshape(*lead, GROUPS, D_STATE)]
    return jnp.concatenate(parts, axis=-1).reshape(*lead, GROUPS * XCG)


def _xbc_cols_inv(w):
    lead = w.shape[:-1]
    g = w.reshape(*lead, GROUPS, XCG)
    parts = [g[..., :GW].reshape(*lead, GROUPS * GW), g[..., GW:GW + D_STATE].reshape(*lead, GROUPS * D_STATE),
             g[..., GW + D_STATE:].reshape(*lead, GROUPS * D_STATE)]
    return jnp.concatenate(parts, axis=-1)


def _dt_cols(w):
    lead = w.shape[:-1]
    t = jnp.swapaxes(w.reshape(*lead, 2, GROUPS, HPG), -3, -2).reshape(*lead, GROUPS, 2 * HPG)
    pad = [(0, 0)] * (t.ndim - 1) + [(0, LANES - 2 * HPG)]
    return jnp.pad(t, pad).reshape(*lead, GROUPS * LANES)


def _dt_cols_inv(w):
    lead = w.shape[:-1]
    t = w.reshape(*lead, GROUPS, LANES)[..., :2 * HPG].reshape(*lead, GROUPS, 2, HPG)
    return jnp.swapaxes(t, -3, -2).reshape(*lead, 2 * HEADS)


def _pad_rows(w, rows):
    return jnp.pad(w, ((0, rows - w.shape[0]), (0, 0)))


def conf_weights(w_in, dw_w, dw_b, ln_w, ln_b):
    w_in_p = _conf_cols(w_in)
    return dict(w_in=w_in_p, dw_w=_pad_rows(dw_w, 32), dw_b=dw_b.reshape(1, -1),
                ln_w=ln_w.reshape(1, -1), ln_b=ln_b.reshape(1, -1))


def _xbc_rows(w):
    e, gn, c = GROUPS * GW, GROUPS * D_STATE, w.shape[1]
    parts = [w[:e].reshape(GROUPS, GW, c), w[e:e + gn].reshape(GROUPS, D_STATE, c),
             w[e + gn:].reshape(GROUPS, D_STATE, c)]
    return jnp.concatenate(parts, axis=1).reshape(GROUPS * XCG, c)


def _xbc_rows_inv(w):
    c = w.shape[1]
    g = w.reshape(GROUPS, XCG, c)
    parts = [g[:, :GW].reshape(GROUPS * GW, c), g[:, GW:GW + D_STATE].reshape(GROUPS * D_STATE, c),
             g[:, GW + D_STATE:].reshape(GROUPS * D_STATE, c)]
    return jnp.concatenate(parts, axis=0)


def _dt_rows(w):
    c = w.shape[1]
    t = jnp.swapaxes(w.reshape(2, GROUPS, HPG, c), 0, 1).reshape(GROUPS, 2 * HPG, c)
    return jnp.pad(t, ((0, 0), (0, LANES - 2 * HPG), (0, 0))).reshape(GROUPS * LANES, c)


def _dt_rows_inv(w):
    c = w.shape[1]
    t = w.reshape(GROUPS, LANES, c)[:, :2 * HPG].reshape(GROUPS, 2, HPG, c)
    return jnp.swapaxes(t, 0, 1).reshape(2 * HEADS, c)


def ssd_weights(w_in_t, conv_w, conv_b, dt_bias, a_log, d_skip, norm_w, w_out):
    e = GROUPS * GW
    xbc = e + 2 * GROUPS * D_STATE
    w_zx_t = jnp.concatenate([w_in_t[:e], _xbc_rows(w_in_t[e:e + xbc])], axis=0)
    return dict(w_zx_t=w_zx_t, w_dt_t=_dt_rows(w_in_t[e + xbc:]), w_out=w_out,
                conv_w=_pad_rows(_xbc_cols(conv_w), 8), conv_b=_xbc_cols(conv_b.reshape(1, -1)),
                bias4=_dt_cols(dt_bias.reshape(1, -1)), alog4=_dt_cols(a_log.reshape(1, -1)),
                dvec=jnp.repeat(d_skip, HEADDIM).reshape(1, -1), norm_w=norm_w.reshape(1, -1))


def conf_layer_fwd(h, nw, p, tag, side=(), w_out=None, hn=None):
    if hn is None:
        hn = rmsnorm_fwd(h, nw, name=f"{tag}_norm")
    proj = mm_nn(hn, p["w_in"], out_dtype=BF16, name=f"{tag}_proj")
    u2 = dwconv_fwd(proj, p["dw_w"], p["dw_b"], width=31, glu=True, silu=False, col0=0, name=f"{tag}_conv", side=side)
    if side:
        u2, gathered = u2
        w_out = w_out(gathered)
    p.update(w_out=w_out)
    u4 = conf_ln_fwd(u2, proj, p["ln_w"], p["ln_b"], name=f"{tag}_ln")
    h2 = mm_nn(u4, p["w_out"], out_dtype=F32, res=h, name=f"{tag}_out")
    return h2, (h, hn, proj, u2, u4)


def conf_layer_bwd(dh, saved, nw, p, tag):
    h, hn, proj, u2, u4 = saved
    du4 = mm_nn(dh, p["w_out"], out_dtype=BF16, b_rows_are_n=True, name=f"{tag}_d_u4")
    dw_out = mm_tn(u4, dh, name=f"{tag}_dw_out")
    du2, dproj, dln_w, dln_b = conf_ln_bwd(du4, u2, proj, p["ln_w"], p["ln_b"], name=f"{tag}_d_ln")
    dproj, ddw_w, ddw_b = dwconv_bwd(du2, proj, p["dw_w"], p["dw_b"], dproj, width=31, glu=True, silu=False,
                                     col0=0, dcol0=0, name=f"{tag}_d_conv")
    dh_prev, dnw = mm_nn(dproj, p["w_in"], out_dtype=F32, b_rows_are_n=True, norm_bwd=(h, nw, dh),
                         name=f"{tag}_d_hn")
    dw_in = mm_tn(hn, dproj, name=f"{tag}_dw_in")
    grads = dict(w_in=_conf_cols_inv(dw_in), dw_w=ddw_w[:31], dw_b=ddw_b[0], ln_w=dln_w[0], ln_b=dln_b[0],
                 w_out=dw_out, norm=dnw[0])
    return dh_prev, grads


def ssd_layer_fwd(h, nw, p, tag):
    e = GROUPS * GW
    hn = rmsnorm_fwd(h, nw, name=f"{tag}_norm")
    zx = mm_nn(hn, p["w_zx_t"], out_dtype=BF16, b_rows_are_n=True, name=f"{tag}_proj")
    dt4 = mm_nn(hn, p["w_dt_t"], out_dtype=F32, b_rows_are_n=True, name=f"{tag}_proj_dt")
    xc = dwconv_fwd(zx, p["conv_w"], p["conv_b"], width=5, glu=False, silu=True, col0=e // CONV_TC, name=f"{tag}_conv")
    y0, st0 = ssd_scan_fwd(xc, dt4, p["bias4"], p["alog4"], rev=False, name=f"{tag}_scan_f")
    y, st1 = ssd_scan_fwd(xc, dt4, p["bias4"], p["alog4"], rev=True, prev=y0, dvec=p["dvec"], name=f"{tag}_scan_b")
    yn = ssd_gate_fwd(y, zx, p["norm_w"], name=f"{tag}_gate")
    h2 = mm_nn(yn, p["w_out"], out_dtype=F32, res=h, name=f"{tag}_out")
    return h2, (h, hn, zx, dt4, xc, st0, st1, y, yn)


def ssd_layer_bwd(dh, saved, nw, p, tag):
    e = GROUPS * GW
    h, hn, zx, dt4, xc, st0, st1, y, yn = saved
    dyn = mm_nn(dh, p["w_out"], out_dtype=BF16, b_rows_are_n=True, name=f"{tag}_d_yn")
    dw_out = mm_tn(yn, dh, name=f"{tag}_dw_out")
    dy, dzx, dnorm_w = ssd_gate_bwd(dyn, y, zx, p["norm_w"], name=f"{tag}_d_gate")
    dxc0, ddt0, dalog0, dbias0 = ssd_scan_bwd(xc, dt4, p["bias4"], p["alog4"], dy, st0, rev=False,
                                              name=f"{tag}_d_scan_f")
    dxc, ddt4, dalog1, dbias1, ddvec = ssd_scan_bwd(xc, dt4, p["bias4"], p["alog4"], dy, st1, rev=True,
                                                    prev=(dxc0, ddt0), dvec=p["dvec"], name=f"{tag}_d_scan_b")
    dzx, dconv_w, dconv_b = dwconv_bwd(dxc, zx, p["conv_w"], p["conv_b"], dzx, width=5, glu=False, silu=True,
                                       col0=e // CONV_TC, dcol0=e // CONV_TC, name=f"{tag}_d_conv")
    dh_prev, dnw = mm_nn(dzx, p["w_zx_t"], out_dtype=F32, a2=ddt4, b2=p["w_dt_t"], norm_bwd=(h, nw, dh),
                         name=f"{tag}_d_hn")
    dw_zx_t = mm_tn(dzx, hn, name=f"{tag}_dw_zx")
    dw_dt_t = mm_tn(ddt4, hn, name=f"{tag}_dw_dt")
    dw_in_t = jnp.concatenate([dw_zx_t[:e], _xbc_rows_inv(dw_zx_t[e:]), _dt_rows_inv(dw_dt_t)], axis=0)
    grads = dict(w_in_t=dw_in_t, conv_w=_xbc_cols_inv(dconv_w[:5]), conv_b=_xbc_cols_inv(dconv_b)[0],
                 dt_bias=_dt_cols_inv(dbias0 + dbias1).reshape(2, HEADS),
                 a_log=_dt_cols_inv(dalog0 + dalog1).reshape(2, HEADS),
                 d_skip=jnp.sum(ddvec.reshape(HEADS, HEADDIM), axis=-1), norm_w=dnorm_w[0], w_out=dw_out,
                 norm=dnw[0])
    return dh_prev, grads


def swap_other_half(g2, *, name):
    def body(g_ref, o_ref, send_sem, recv_sem):
        x, y, c = _place()
        cp = pltpu.make_async_remote_copy(src_ref=g_ref.at[1 - c], dst_ref=o_ref, send_sem=send_sem, recv_sem=recv_sem,
                                          device_id=(x, y, 1 - c), device_id_type=MESH)
        cp.start()
        cp.wait()

    return pl.pallas_call(
        body, name=name, in_specs=[ANY], out_specs=ANY, out_shape=jax.ShapeDtypeStruct(g2.shape[1:], g2.dtype),
        scratch_shapes=[pltpu.SemaphoreType.DMA, pltpu.SemaphoreType.DMA],
    )(g2)


def exchange_chips(p, *, name):
    def body(p_ref, o_ref, send_sems, recv_sems, local_sem):
        x, y, c = _place()
        k_me = 2 * x + y
        own = pltpu.make_async_copy(p_ref.at[k_me], o_ref.at[k_me], local_sem)
        own.start()
        copies = [own]
        for j, (px, py) in enumerate([(1 - x, y), (x, 1 - y), (1 - x, 1 - y)]):
            cp = pltpu.make_async_remote_copy(
                src_ref=p_ref.at[2 * px + py], dst_ref=o_ref.at[k_me], send_sem=send_sems.at[j],
                recv_sem=recv_sems.at[j], device_id=(px, py, c), device_id_type=MESH)
            cp.start()
            copies.append(cp)
        for cp in copies:
            cp.wait()

    return pl.pallas_call(
        body, name=name, in_specs=[ANY], out_specs=ANY, out_shape=jax.ShapeDtypeStruct(p.shape, p.dtype),
        scratch_shapes=[pltpu.SemaphoreType.DMA((3,)), pltpu.SemaphoreType.DMA((3,)), pltpu.SemaphoreType.DMA],
    )(p)


def share_half(full, *, name):
    def body(_, f_ref, send_sem, recv_sem):
        x, y, c = _place()
        cp = pltpu.make_async_remote_copy(src_ref=f_ref.at[c], dst_ref=f_ref.at[c], send_sem=send_sem,
                                          recv_sem=recv_sem, device_id=(x, y, 1 - c), device_id_type=MESH)
        cp.start()
        cp.wait()

    return pl.pallas_call(
        body, name=name, in_specs=[ANY], out_specs=ANY, out_shape=jax.ShapeDtypeStruct(full.shape, full.dtype),
        input_output_aliases={0: 0},
        scratch_shapes=[pltpu.SemaphoreType.DMA, pltpu.SemaphoreType.DMA],
    )(full)


def gather_all(v, *, name):
    def body(v_ref, o_ref, send_sems, recv_sems, local_sem):
        x, y, c = _place()
        me = 4 * x + 2 * y + c
        own = pltpu.make_async_copy(v_ref, o_ref.at[me], local_sem)
        own.start()
        copies = [own]
        idx = 0
        for fx in (0, 1):
            for fy in (0, 1):
                for fc in (0, 1):
                    if not (fx or fy or fc):
                        continue
                    peer = (1 - x if fx else x, 1 - y if fy else y, 1 - c if fc else c)
                    cp = pltpu.make_async_remote_copy(src_ref=v_ref, dst_ref=o_ref.at[me], send_sem=send_sems.at[idx],
                                                      recv_sem=recv_sems.at[idx], device_id=peer, device_id_type=MESH)
                    cp.start()
                    copies.append(cp)
                    idx += 1
        for cp in copies:
            cp.wait()

    return pl.pallas_call(
        body, name=name, in_specs=[ANY], out_specs=ANY, out_shape=jax.ShapeDtypeStruct((N_DEV,) + v.shape, v.dtype),
        scratch_shapes=[pltpu.SemaphoreType.DMA((N_DEV - 1,)), pltpu.SemaphoreType.DMA((N_DEV - 1,)),
                        pltpu.SemaphoreType.DMA],
    )(v)


RED_TR = 432


def pair_sum(g2, recv, cidx, *, name):
    _, K, R, C = g2.shape
    tr = _pick(R, (RED_TR, 8))

    def body(c_ref, a_ref, b_ref, o_ref):
        o_ref[...] = (a_ref[0] + b_ref[...]).astype(BF16)

    blk = pl.BlockSpec((1, tr, C), lambda k, i, c: (k, i, 0))
    return pl.pallas_call(
        body, name=name,
        grid_spec=pltpu.PrefetchScalarGridSpec(
            num_scalar_prefetch=1, grid=(K, R // tr),
            in_specs=[pl.BlockSpec((1, 1, tr, C), lambda k, i, c: (c[0], k, i, 0)), blk], out_specs=blk),
        out_shape=jax.ShapeDtypeStruct((K, R, C), BF16), compiler_params=_params(2),
    )(cidx, g2, recv)


def sum_lead(a, *, name, slot=None, nslots=1):
    K, R, C = a.shape
    tr = _pick(R, (RED_TR, 8))

    def body(s_ref, a_ref, o_ref):
        acc = a_ref[0].astype(F32)
        for k in range(1, K):
            acc = acc + a_ref[k].astype(F32)
        o_ref[0] = acc

    if slot is None:
        slot = jnp.zeros((1,), jnp.int32)
    return pl.pallas_call(
        body, name=name,
        grid_spec=pltpu.PrefetchScalarGridSpec(
            num_scalar_prefetch=1, grid=(R // tr,),
            in_specs=[pl.BlockSpec((K, tr, C), lambda i, s: (0, i, 0))],
            out_specs=pl.BlockSpec((1, tr, C), lambda i, s: (s[0], i, 0))),
        out_shape=jax.ShapeDtypeStruct((nslots, R, C), F32), compiler_params=_params(1),
    )(slot, a)


def adamw(g, w, m, v, *, name):
    R, C = w.shape
    tr = _pick(R, (256, 128, 64, 32, 16, 8))

    def body(g_ref, w_ref, m_ref, v_ref, d_ref, nm_ref, nv_ref):
        gv = g_ref[...]
        m_new = ADAM_B1 * m_ref[...] + (1.0 - ADAM_B1) * gv
        v_new = ADAM_B2 * v_ref[...] + (1.0 - ADAM_B2) * (gv * gv)
        m_hat = m_new / (1.0 - ADAM_B1 ** ADAM_STEP)
        v_hat = v_new / (1.0 - ADAM_B2 ** ADAM_STEP)
        d_ref[...] = -ADAM_LR * (m_hat / (jnp.sqrt(v_hat) + ADAM_EPS) + ADAM_WD * w_ref[...])
        nm_ref[...] = m_new
        nv_ref[...] = v_new

    blk = pl.BlockSpec((tr, C), lambda i: (i, 0))
    sds = jax.ShapeDtypeStruct((R, C), F32)
    return pl.pallas_call(
        body, name=name, grid=(R // tr,), in_specs=[blk] * 4, out_specs=(blk,) * 3, out_shape=(sds,) * 3,
        compiler_params=_params(1),
    )(g, w, m, v)


WEIGHTS = ("norm_w", "final_norm_w", "cm_w_in", "cm_dw_w", "cm_dw_b", "cm_ln_w", "cm_ln_b", "cm_w_out", "ssd_w_in",
           "ssd_conv_w", "ssd_conv_b", "ssd_dt_bias", "ssd_A_log", "ssd_D", "ssd_norm_w", "ssd_w_out")
BIG = (("cm_w_in", 2), ("cm_w_out", 1), ("ssd_w_in", 1), ("ssd_w_out", 1))
TRANSPOSED = ("ssd_w_in",)
SMALL_SHARDED = (("cm_dw_w", 2), ("ssd_conv_w", 2), ("ssd_conv_b", 1), ("ssd_norm_w", 1))
REPLICATED = ("norm_w", "final_norm_w", "cm_dw_b", "cm_ln_w", "cm_ln_b", "ssd_dt_bias", "ssd_A_log", "ssd_D")
ROW = 1024


def _to_shards(g, axis):
    n = g.shape[axis]
    s = g.reshape(g.shape[:axis] + (N_CHIPS, n // N_CHIPS) + g.shape[axis + 1:])
    return jnp.moveaxis(s, axis, 0).reshape(N_CHIPS, -1)


def _from_shards(x4, local_shape, axis):
    local_shape = tuple(local_shape)
    s = jnp.moveaxis(x4.reshape((N_CHIPS,) + local_shape), 0, axis)
    return s.reshape(local_shape[:axis] + (N_CHIPS * local_shape[axis],) + local_shape[axis + 1:])


def _flat_pad(parts, multiple):
    n = sum(p.size for p in parts)
    fill = [jnp.zeros(((-n) % multiple,), parts[0].dtype)] if n % multiple else []
    return jnp.concatenate([p.reshape(-1) for p in parts] + fill)


def _split(flat, like, names):
    out, off = {}, 0
    for n in names:
        out[n] = flat[off:off + like[n].size].reshape(like[n].shape)
        off += like[n].size
    return out


def kernel(x, norm_w, final_norm_w, cm_w_in, cm_dw_w, cm_dw_b, cm_ln_w, cm_ln_b, cm_w_out, ssd_w_in, ssd_conv_w, ssd_conv_b, ssd_dt_bias, ssd_A_log, ssd_D, ssd_norm_w, ssd_w_out, loss_target, m_norm_w, m_final_norm_w, m_cm_w_in, m_cm_dw_w, m_cm_dw_b, m_cm_ln_w, m_cm_ln_b, m_cm_w_out, m_ssd_w_in, m_ssd_conv_w, m_ssd_conv_b, m_ssd_dt_bias, m_ssd_A_log, m_ssd_D, m_ssd_norm_w, m_ssd_w_out, v_norm_w, v_final_norm_w, v_cm_w_in, v_cm_dw_w, v_cm_dw_b, v_cm_ln_w, v_cm_ln_b, v_cm_w_out, v_ssd_w_in, v_ssd_conv_w, v_ssd_conv_b, v_ssd_dt_bias, v_ssd_A_log, v_ssd_D, v_ssd_norm_w, v_ssd_w_out):
    a = dict(locals())
    w = {n: a[n] for n in WEIGHTS}
    m = {n: a["m_" + n] for n in WEIGHTS}
    v = {n: a["v_" + n] for n in WEIGHTS}
    _, T, D = x.shape
    cidx = lax.axis_index("c").astype(jnp.int32).reshape(1)
    big_names = [n for n, _ in BIG]
    small_names = [n for n, _ in SMALL_SHARDED]

    wx = {n: (jnp.swapaxes(w[n], 1, 2) if n in TRANSPOSED else w[n]) for n in big_names + small_names}
    big = _flat_pad([wx[n] for n in big_names], 16 * ROW).astype(BF16).reshape(-1, ROW)
    small = _flat_pad([wx[n] for n in small_names], 8 * ROW).reshape(-1, ROW)
    first_name, first_axis = BIG[0]
    first_shape = wx[first_name].shape[1:]
    n_first = math.prod(first_shape) // ROW
    hn_0, (g_first, g_small) = rmsnorm_fwd(x[0], norm_w[0].reshape(1, -1), name="l0_norm",
                                           side=(big[:n_first], small))
    g_small = g_small.reshape(N_CHIPS, -1)
    full, off = {}, 0
    for n, ax in SMALL_SHARDED:
        full[n] = _from_shards(g_small[:, off:off + wx[n].size], wx[n].shape, ax)
        off += wx[n].size
    w_in_0 = _from_shards(g_first.reshape(N_CHIPS, -1), first_shape, first_axis - 1)
    n_layers = norm_w.shape[0]
    lw = [None] * n_layers
    lw[0] = conf_weights(w_in_0, full["cm_dw_w"][0], cm_dw_b[0], cm_ln_w[0], cm_ln_b[0])

    def unpack_rest(gathered):
        g_big = jnp.concatenate([g_first, gathered[0]], axis=1).reshape(N_CHIPS, -1)
        off = 0
        for n, ax in BIG:
            full[n] = _from_shards(g_big[:, off:off + wx[n].size], wx[n].shape, ax)
            off += wx[n].size
        for i in range(1, n_layers):
            j = i // 2
            if i % 2 == 0:
                lw[i] = conf_weights(full["cm_w_in"][j], full["cm_dw_w"][j], cm_dw_b[j], cm_ln_w[j], cm_ln_b[j])
            else:
                lw[i] = ssd_weights(full["ssd_w_in"][j], full["ssd_conv_w"][j], full["ssd_conv_b"][j], ssd_dt_bias[j],
                                    ssd_A_log[j], ssd_D[j], full["ssd_norm_w"][j], full["ssd_w_out"][j])
        return full["cm_w_out"][0]

    h = x[0]
    saved = []
    for i in range(n_layers):
        nw_i = norm_w[i].reshape(1, -1)
        if i == 0:
            h, s = conf_layer_fwd(h, nw_i, lw[0], "l0", side=(big[n_first:],), w_out=unpack_rest, hn=hn_0)
        elif i % 2 == 0:
            h, s = conf_layer_fwd(h, nw_i, lw[i], f"l{i}", w_out=full["cm_w_out"][i // 2])
        else:
            h, s = ssd_layer_fwd(h, nw_i, lw[i], f"l{i}")
        saved.append(s)
    dh, loss_local, d_final = loss_head(h, loss_target[0], final_norm_w.reshape(1, -1), name="loss_head")
    lg = [None] * n_layers
    for i in reversed(range(n_layers)):
        bwd = conf_layer_bwd if i % 2 == 0 else ssd_layer_bwd
        dh, lg[i] = bwd(dh, saved[i], norm_w[i].reshape(1, -1), lw[i], f"l{i}")
    conf_g, ssd_g = lg[0::2], lg[1::2]
    local = {
        "norm_w": jnp.stack([g["norm"] for g in lg]), "final_norm_w": d_final[0],
        "cm_w_in": jnp.stack([g["w_in"] for g in conf_g]), "cm_dw_w": jnp.stack([g["dw_w"] for g in conf_g]),
        "cm_dw_b": jnp.stack([g["dw_b"] for g in conf_g]), "cm_ln_w": jnp.stack([g["ln_w"] for g in conf_g]),
        "cm_ln_b": jnp.stack([g["ln_b"] for g in conf_g]), "cm_w_out": jnp.stack([g["w_out"] for g in conf_g]),
        "ssd_w_in": jnp.stack([g["w_in_t"] for g in ssd_g]), "ssd_conv_w": jnp.stack([g["conv_w"] for g in ssd_g]),
        "ssd_conv_b": jnp.stack([g["conv_b"] for g in ssd_g]), "ssd_dt_bias": jnp.stack([g["dt_bias"] for g in ssd_g]),
        "ssd_A_log": jnp.stack([g["a_log"] for g in ssd_g]), "ssd_D": jnp.stack([g["d_skip"] for g in ssd_g]),
        "ssd_norm_w": jnp.stack([g["norm_w"] for g in ssd_g]), "ssd_w_out": jnp.stack([g["w_out"] for g in ssd_g]),
    }

    shards = [_to_shards(local[n], ax) for n, ax in BIG + SMALL_SHARDED]
    fill = (-sum(t.shape[1] for t in shards)) % (2 * RED_TR * ROW)
    flat4 = jnp.concatenate(shards + [jnp.zeros((N_CHIPS, fill), F32)], axis=1)
    g2 = jnp.swapaxes(flat4.reshape(N_CHIPS, 2, -1, ROW), 0, 1)
    theirs = swap_other_half(g2, name="grad_pair_swap")
    part = pair_sum(g2, theirs, cidx, name="grad_pair_sum")
    got = exchange_chips(part, name="grad_chip_exchange")
    half = sum_lead(got, slot=cidx, nslots=2, name="grad_chip_sum")
    shard_flat = share_half(half, name="grad_pair_share").reshape(-1)
    grads = _split(shard_flat, wx, big_names + small_names)
    for n in TRANSPOSED:
        grads[n] = jnp.swapaxes(grads[n], 1, 2)

    rep = _flat_pad([local[n] for n in REPLICATED], 8 * LANES).reshape(-1, LANES)
    rep_sum = sum_lead(gather_all(rep, name="grad_small_gather"), name="grad_small_sum")
    grads.update(_split(rep_sum.reshape(-1), w, REPLICATED))

    delta, new_m, new_v = {}, {}, {}
    for n in big_names:
        two_d = (-1, w[n].shape[-1])
        d_, m_, v_ = adamw(grads[n].reshape(two_d), w[n].reshape(two_d), m[n].reshape(two_d), v[n].reshape(two_d),
                           name="adamw_" + n)
        delta[n], new_m[n], new_v[n] = d_.reshape(w[n].shape), m_.reshape(w[n].shape), v_.reshape(w[n].shape)
    rest = list(REPLICATED) + small_names
    packed = [_flat_pad([t[n] for n in rest], 8 * LANES).reshape(-1, LANES) for t in (grads, w, m, v)]
    for out, res in zip((delta, new_m, new_v), adamw(*packed, name="adamw_small")):
        out.update(_split(res.reshape(-1), w, rest))

    loss = lax.psum(loss_local[0, 0], ("x", "y", "c"))
    return (loss, dh.reshape(x.shape), *[grads[n] for n in WEIGHTS], *[delta[n] for n in WEIGHTS],
            *[new_m[n] for n in WEIGHTS], *[new_v[n] for n in WEIGHTS])
```
